```python
import math
import jax, jax.numpy as jnp
from jax import lax
import numpy as np

D_MODEL = 2048
BATCH = 1
SEQ = 8192
DEPTH = 4

N_MIXERS = 3
ATTN_HEADS = 16
HEAD_DIM = 128
DILATION_PATTERNS = ((128, 1), (512, 4), (2048, 16))
N_DIL_GROUPS = len(DILATION_PATTERNS)
Q_BLOCK = 128
ROPE_THETA = 10000.0
CONV_KERNEL = 31
SHORT_CONV_KERNEL = 3
N_EXPERTS = 16
N_EXPERT_GROUPS = 4
EXPERTS_PER_GROUP = N_EXPERTS // N_EXPERT_GROUPS
TOP_K = 2
D_EXPERT = 1408
ROW_BLOCK = 128
ALPHA = (2 * DEPTH) ** 0.25
BETA = (8 * DEPTH) ** -0.25
LN_EPS = 1e-5
NEG_INF = -1e30
N_ATTN_LAYERS = len(range(0, DEPTH, N_MIXERS))
N_CONV_LAYERS = len(range(1, DEPTH, N_MIXERS))
N_SC_LAYERS = len(range(2, DEPTH, N_MIXERS))

kernel_name = 'hybrid_dilattn_conformer_shortconv_groupmoe_deepnorm_adaln'


def layer_norm(x, g, b):
    xf = x.astype(jnp.float32)
    mu = xf.mean(-1, keepdims=True)
    var = jnp.square(xf - mu).mean(-1, keepdims=True)
    y = (xf - mu) * lax.rsqrt(var + LN_EPS) * g.astype(jnp.float32) + b.astype(jnp.float32)
    return y.astype(x.dtype)


def apply_rope(x, positions):
    half = HEAD_DIM // 2
    inv_freq = ROPE_THETA ** (-jnp.arange(half, dtype=jnp.float32) / half)
    ang = positions.astype(jnp.float32)[:, :, None] * inv_freq
    cos = jnp.cos(ang)[:, :, None, None, :]
    sin = jnp.sin(ang)[:, :, None, None, :]
    xf = x.astype(jnp.float32)
    x1, x2 = xf[..., :half], xf[..., half:]
    return jnp.concatenate([x1 * cos - x2 * sin, x2 * cos + x1 * sin], axis=-1).astype(x.dtype)


def strided_window_attention(q, k, v, window, dilation):
    B, S, H, Dh = q.shape
    L = S // dilation
    span = window // dilation
    nb = -(-L // Q_BLOCK)
    Lp = nb * Q_BLOCK

    def gather(a):
        a = a.reshape(B, L, dilation, H, Dh).transpose(0, 2, 1, 3, 4)
        a = jnp.pad(a, ((0, 0), (0, 0), (0, Lp - L), (0, 0), (0, 0)))
        return a.reshape(B, dilation, nb, Q_BLOCK, H, Dh)

    def with_prev(a):
        prev = jnp.pad(a, ((0, 0), (0, 0), (1, 0), (0, 0), (0, 0), (0, 0)))[:, :, :-1]
        return jnp.concatenate([prev, a], axis=3)

    qb = gather(q)
    kk = with_prev(gather(k))
    vv = with_prev(gather(v))
    s = jnp.einsum('brnqhd,brnkhd->brnhqk', qb, kk,
                   preferred_element_type=jnp.float32) * (1.0 / math.sqrt(Dh))
    qi = jnp.arange(Q_BLOCK)[:, None]
    kj = jnp.arange(2 * Q_BLOCK)[None, :]
    dist = Q_BLOCK + qi - kj
    in_band = (dist >= 0) & (dist <= span)
    first = (jnp.arange(nb) == 0)[:, None, None]
    valid = in_band[None] & ~(first & (kj < Q_BLOCK)[None])
    s = jnp.where(valid[None, None, :, None], s, NEG_INF)
    m = s.max(-1, keepdims=True)
    p = jnp.exp(s - m)
    den = p.sum(-1)
    o = jnp.einsum('brnhqk,brnkhd->brnqhd', p.astype(vv.dtype), vv,
                   preferred_element_type=jnp.float32)
    o = o / jnp.swapaxes(den, -1, -2)[..., None]
    lse = jnp.swapaxes(m[..., 0] + jnp.log(den), -1, -2)

    def ungather(a):
        a = a.reshape((B, dilation, Lp) + a.shape[4:])[:, :, :L]
        a = jnp.moveaxis(a, 1, 2)
        return a.reshape((B, S) + a.shape[3:])

    return ungather(o), ungather(lse)


def dilated_attention(u, positions, w_qkv, w_o):
    B, S, _ = u.shape
    qkv = (u @ w_qkv).reshape(B, S, N_DIL_GROUPS, 3, ATTN_HEADS, HEAD_DIM)
    q = apply_rope(qkv[:, :, :, 0], positions)
    k = apply_rope(qkv[:, :, :, 1], positions)
    v = qkv[:, :, :, 2]
    outs, lses = [], []
    for g, (window, dilation) in enumerate(DILATION_PATTERNS):
        o_g, l_g = strided_window_attention(q[:, :, g], k[:, :, g], v[:, :, g], window, dilation)
        outs.append(o_g)
        lses.append(l_g)
    w = jax.nn.softmax(jnp.stack(lses), axis=0)
    o = jnp.einsum('gbsh,gbshd->bshd', w, jnp.stack(outs)).astype(u.dtype)
    return o.reshape(B, S, ATTN_HEADS * HEAD_DIM) @ w_o


def causal_depthwise_conv(x, w):
    K = w.shape[0]
    return lax.conv_general_dilated(x, w[:, None, :].astype(x.dtype), window_strides=(1,),
                                    padding=((K - 1, 0),),
                                    dimension_numbers=('NWC', 'WIO', 'NWC'),
                                    feature_group_count=x.shape[-1])


def conformer_conv(u, w_pw1, w_dw, ln_g, ln_b, w_pw2):
    a, gate = jnp.split(u @ w_pw1, 2, axis=-1)
    h = causal_depthwise_conv(a * jax.nn.sigmoid(gate), w_dw)
    h = jax.nn.silu(layer_norm(h, ln_g, ln_b))
    return h @ w_pw2


def short_gated_conv(u, w_in, w_conv, w_out):
    b_gate, c_gate, h = jnp.split(u @ w_in, 3, axis=-1)
    z = causal_depthwise_conv(c_gate * h, w_conv)
    return (b_gate * z) @ w_out


def moe_ffn(u, router_w, router_b, w_gate, w_up, w_down):
    B, S, D = u.shape
    N = B * S
    xf = u.reshape(N, D)
    logits = (xf @ router_w).astype(jnp.float32) + router_b.astype(jnp.float32)
    probs = jax.nn.softmax(logits, axis=-1)
    pg = probs.reshape(N, N_EXPERT_GROUPS, EXPERTS_PER_GROUP)
    top_p, top_i = lax.top_k(pg, TOP_K)
    gsel = jnp.argmax(top_p.sum(-1), axis=-1)
    sel_p = jnp.take_along_axis(top_p, gsel[:, None, None], axis=1)[:, 0]
    sel_i = jnp.take_along_axis(top_i, gsel[:, None, None], axis=1)[:, 0]
    expert = gsel[:, None] * EXPERTS_PER_GROUP + sel_i
    gates = sel_p / sel_p.sum(-1, keepdims=True)

    A = N * TOP_K
    e_flat = expert.reshape(A).astype(jnp.int32)
    order = jnp.argsort(e_flat)
    e_sorted = e_flat[order]
    counts = jnp.bincount(e_flat, length=N_EXPERTS)
    padded = (counts + ROW_BLOCK - 1) // ROW_BLOCK * ROW_BLOCK
    pad_end = jnp.cumsum(padded)
    pad_start = pad_end - padded
    start = jnp.cumsum(counts) - counts
    dest_sorted = pad_start[e_sorted] + (jnp.arange(A) - start[e_sorted])
    dest = jnp.zeros((A,), jnp.int32).at[order].set(dest_sorted.astype(jnp.int32))
    P = A + N_EXPERTS * ROW_BLOCK
    P = -(-P // ROW_BLOCK) * ROW_BLOCK
    nb = P // ROW_BLOCK
    row_token = jnp.full((P,), N, jnp.int32).at[dest].set(jnp.arange(A, dtype=jnp.int32) // TOP_K)
    rows = jnp.concatenate([xf, jnp.zeros((1, D), xf.dtype)], axis=0)[row_token]
    block_expert = jnp.minimum(
        jnp.searchsorted(pad_end, jnp.arange(nb) * ROW_BLOCK, side='right'), N_EXPERTS - 1)

    def expert_block(args):
        xb, e = args
        h = jax.nn.silu(xb @ w_gate[e]) * (xb @ w_up[e])
        return h @ w_down[e]

    out_rows = lax.map(expert_block, (rows.reshape(nb, ROW_BLOCK, D), block_expert)).reshape(P, D)
    y = jnp.einsum('nkd,nk->nd', out_rows[dest].reshape(N, TOP_K, D), gates.astype(u.dtype))
    return y.reshape(B, S, D)


def setup_inputs(seed: int = 0) -> dict:
    key = jax.random.key(seed)
    ks = jax.random.split(key, 24)
    f32 = jnp.float32

    def nrm(k, shape, scale):
        return jax.random.normal(k, shape, f32) * scale

    D, E, F = D_MODEL, N_EXPERTS, D_EXPERT
    HD = ATTN_HEADS * HEAD_DIM
    return {
        'x': nrm(ks[0], (BATCH, SEQ, D), 1.0),
        'c': nrm(ks[1], (BATCH, D), 1.0),
        'positions': jnp.arange(SEQ, dtype=jnp.int32)[None, :]
                     + jax.random.randint(ks[2], (BATCH, 1), 0, 1024, jnp.int32),
        'ada_w': nrm(ks[3], (DEPTH, D, 6 * D), 0.3 * D ** -0.5),
        'ada_b': nrm(ks[4], (DEPTH, 6 * D), 0.01),
        'ln_g': 1.0 + nrm(ks[5], (DEPTH, 2, D), 0.01),
        'ln_b': nrm(ks[6], (DEPTH, 2, D), 0.01),
        'attn_w_qkv': nrm(ks[7], (N_ATTN_LAYERS, D, N_DIL_GROUPS * 3 * HD), D ** -0.5),
        'attn_w_o': nrm(ks[8], (N_ATTN_LAYERS, HD, D), BETA * HD ** -0.5),
        'conv_w_pw1': nrm(ks[9], (N_CONV_LAYERS, D, 2 * D), D ** -0.5),
        'conv_w_dw': nrm(ks[10], (N_CONV_LAYERS, CONV_KERNEL, D), CONV_KERNEL ** -0.5),
        'conv_ln_g': 1.0 + nrm(ks[11], (N_CONV_LAYERS, D), 0.01),
        'conv_ln_b': nrm(ks[12], (N_CONV_LAYERS, D), 0.01),
        'conv_w_pw2': nrm(ks[13], (N_CONV_LAYERS, D, D), BETA * D ** -0.5),
        'sc_w_in': nrm(ks[14], (N_SC_LAYERS, D, 3 * D), D ** -0.5),
        'sc_w_conv': nrm(ks[15], (N_SC_LAYERS, SHORT_CONV_KERNEL, D), SHORT_CONV_KERNEL ** -0.5),
        'sc_w_out': nrm(ks[16], (N_SC_LAYERS, D, D), BETA * D ** -0.5),
        'router_w': nrm(ks[17], (D, E), D ** -0.5),
        'router_b': nrm(ks[18], (E,), 0.01),
        'moe_w_gate': nrm(ks[19], (DEPTH, E, D, F), D ** -0.5),
        'moe_w_up': nrm(ks[20], (DEPTH, E, D, F), D ** -0.5),
        'moe_w_down': nrm(ks[21], (DEPTH, E, F, D), BETA * F ** -0.5),
    }


def reference(x, c, positions, ada_w, ada_b, ln_g, ln_b, attn_w_qkv, attn_w_o,
              conv_w_pw1, conv_w_dw, conv_ln_g, conv_ln_b, conv_w_pw2,
              sc_w_in, sc_w_conv, sc_w_out, router_w, router_b,
              moe_w_gate, moe_w_up, moe_w_down):
    c_act = jax.nn.silu(c)
    for i in range(DEPTH):
        mod = (c_act @ ada_w[i] + ada_b[i])[:, None, :]
        sh1, sc1, g1, sh2, sc2, g2 = jnp.split(mod, 6, axis=-1)
        u = x * (1.0 + sc1) + sh1
        m, j = i % N_MIXERS, i // N_MIXERS
        if m == 0:
            y = dilated_attention(u, positions, attn_w_qkv[j], attn_w_o[j])
        elif m == 1:
            y = conformer_conv(u, conv_w_pw1[j], conv_w_dw[j], conv_ln_g[j], conv_ln_b[j], conv_w_pw2[j])
        else:
            y = short_gated_conv(u, sc_w_in[j], sc_w_conv[j], sc_w_out[j])
        x = layer_norm(ALPHA * x + (1.0 + g1) * y, ln_g[i, 0], ln_b[i, 0])
        u = x * (1.0 + sc2) + sh2
        y = moe_ffn(u, router_w, router_b, moe_w_gate[i], moe_w_up[i], moe_w_down[i])
        x = layer_norm(ALPHA * x + (1.0 + g2) * y, ln_g[i, 1], ln_b[i, 1])
    return x
```

```python
import functools
import math

import jax
import jax.numpy as jnp
from jax import lax
from jax.experimental import pallas as pl
from jax.experimental.pallas import tpu as pltpu

F32 = jnp.float32
BF16 = jnp.bfloat16

D_MODEL = 2048
DEPTH = 4
N_MIXERS = 3
ATTN_HEADS = 16
HEAD_DIM = 128
HD = ATTN_HEADS * HEAD_DIM
DILATIONS = (1, 4, 16)
Q_BLOCK = 128
ROPE_THETA = 10000.0
CONV_KERNEL = 31
SHORT_CONV_KERNEL = 3
N_EXPERTS = 16
N_EXPERT_GROUPS = 4
EXPERTS_PER_GROUP = 4
TOP_K = 2
D_EXPERT = 1408
ALPHA = (2 * DEPTH) ** 0.25
LN_EPS = 1e-5
NEG_INF = -1e30

LANES_V7X = 128
VMEM_LIMIT_V7X = 56 * 1024 * 1024
TM_PROJ = 512
TN_PROJ = 1024
TN_GLU = 512
TM_OUT = 256
TM_ROUTE = 512
TM_MOE = 256
TM_COMB = 256
TN_ADA = 1024
HALO_CONV = 32
HALO_SC = 8
CONV_ROWS = 32
CONV_COLS = 512


def _cparams(sem):
    return pltpu.CompilerParams(dimension_semantics=sem, vmem_limit_bytes=VMEM_LIMIT_V7X)


def _ln_rows(z, g, b):
    mu = jnp.mean(z, axis=-1, keepdims=True)
    zc = z - mu
    var = jnp.mean(zc * zc, axis=-1, keepdims=True)
    return zc * lax.rsqrt(var + LN_EPS) * g + b


def _deepnorm(x, y, gate, g, b):
    return _ln_rows(ALPHA * x + (1.0 + gate) * y, g, b)


def _row_spec(tm, width):
    return pl.BlockSpec((tm, width), lambda i: (i, 0))


def _vec_spec(width):
    return pl.BlockSpec((1, width), lambda i: (0, 0))


def _adaln_kernel(c_ref, w_ref, b_ref, o_ref):
    c = c_ref[...]
    ca = c * jax.nn.sigmoid(c)
    o_ref[...] = jnp.sum(w_ref[...] * ca, axis=0, keepdims=True) + b_ref[...]


def _adaln(c_col, ada_w, ada_b):
    depth, d, n = ada_w.shape
    return pl.pallas_call(
        _adaln_kernel,
        grid=(depth, n // TN_ADA),
        in_specs=[pl.BlockSpec((d, 1), lambda l, j: (0, 0)),
                  pl.BlockSpec((None, d, TN_ADA), lambda l, j: (l, 0, j)),
                  pl.BlockSpec((None, 1, TN_ADA), lambda l, j: (l, 0, j))],
        out_specs=pl.BlockSpec((None, 1, TN_ADA), lambda l, j: (l, 0, j)),
        out_shape=jax.ShapeDtypeStruct((depth, 1, n), F32),
        compiler_params=_cparams(("arbitrary", "arbitrary")),
        name="adaln",
    )(c_col, ada_w, ada_b.reshape(depth, 1, n))


def _rope_kernel(pos_ref, freq_ref, cos_ref, sin_ref):
    ang = pos_ref[...].astype(F32) * freq_ref[...]
    lane = lax.broadcasted_iota(jnp.int32, ang.shape, 1)
    s = jnp.sin(ang)
    cos_ref[...] = jnp.cos(ang)
    sin_ref[...] = jnp.where(lane < HEAD_DIM // 2, -s, s)


def _rope_tables(pos_col, freq_row):
    s = pos_col.shape[0]
    tm = 1024
    spec = pl.BlockSpec((tm, HEAD_DIM), lambda i: (i, 0))
    return pl.pallas_call(
        _rope_kernel,
        grid=(s // tm,),
        in_specs=[pl.BlockSpec((tm, 1), lambda i: (i, 0)), _vec_spec(HEAD_DIM)],
        out_specs=[spec, spec],
        out_shape=[jax.ShapeDtypeStruct((s, HEAD_DIM), F32)] * 2,
        compiler_params=_cparams(("arbitrary",)),
        name="rope_tables",
    )(pos_col, freq_row)


def _qkv_kernel(x_ref, sc_ref, sh_ref, w_ref, cos_ref, sin_ref, o_ref, u_scr, acc_scr, *, d, n_rope):
    j = pl.program_id(1)
    nh, tm, _ = acc_scr.shape

    @pl.when(j == 0)
    def _():
        u_scr[...] = (x_ref[...] * (1.0 + sc_ref[...]) + sh_ref[...]).astype(BF16)

    acc = jnp.dot(u_scr[...], w_ref[...], preferred_element_type=F32)

    @pl.when(j < n_rope)
    def _():
        cos = cos_ref[...]
        sin = sin_ref[...]
        for h in range(nh):
            a = acc[:, h * HEAD_DIM:(h + 1) * HEAD_DIM]
            acc_scr[h] = a * cos + pltpu.roll(a, HEAD_DIM // 2, 1) * sin

    @pl.when(j >= n_rope)
    def _():
        for h in range(nh):
            acc_scr[h] = acc[:, h * HEAD_DIM:(h + 1) * HEAD_DIM]

    for r in range(d):
        for h in range(nh):
            rows = acc_scr[h, pl.ds(r, tm // d, stride=d), :] if d > 1 else acc_scr[h]
            o_ref[r, :, h * HEAD_DIM:(h + 1) * HEAD_DIM] = rows.astype(BF16)


def _qkv_proj(x, sc, sh, w_bf, cos, sin, group):
    s = x.shape[0]
    d = DILATIONS[group]
    tm, tn = TM_PROJ, TN_PROJ
    ncol = 3 * HD
    col0 = group * (ncol // tn)
    return pl.pallas_call(
        functools.partial(_qkv_kernel, d=d, n_rope=2 * HD // tn),
        grid=(s // tm, ncol // tn),
        in_specs=[pl.BlockSpec((tm, D_MODEL), lambda i, j: (i, 0)),
                  pl.BlockSpec((1, D_MODEL), lambda i, j: (0, 0)),
                  pl.BlockSpec((1, D_MODEL), lambda i, j: (0, 0)),
                  pl.BlockSpec((D_MODEL, tn), lambda i, j: (0, col0 + j)),
                  pl.BlockSpec((tm, HEAD_DIM), lambda i, j: (i, 0)),
                  pl.BlockSpec((tm, HEAD_DIM), lambda i, j: (i, 0))],
        out_specs=pl.BlockSpec((d, tm // d, tn), lambda i, j: (0, i, j)),
        out_shape=jax.ShapeDtypeStruct((d, s // d, ncol), BF16),
        scratch_shapes=[pltpu.VMEM((tm, D_MODEL), BF16), pltpu.VMEM((tn // HEAD_DIM, tm, HEAD_DIM), F32)],
        compiler_params=_cparams(("arbitrary", "arbitrary")),
        name=f"qkv_proj_g{group}",
    )(x, sc, sh, w_bf, cos, sin)


def _attn_kernel(q_ref, kc_ref, vc_ref, kp_ref, vp_ref, o_ref, lse_ref, *, nblk):
    b = pl.program_id(0)
    not_first = (b % nblk) != 0
    row = lax.broadcasted_iota(jnp.int32, (Q_BLOCK, Q_BLOCK), 0)
    col = lax.broadcasted_iota(jnp.int32, (Q_BLOCK, Q_BLOCK), 1)
    mask_c = col <= row
    mask_p = jnp.logical_and(col >= row, not_first)
    scale = 1.0 / math.sqrt(HEAD_DIM)
    nt = (((1,), (1,)), ((), ()))
    for h in range(ATTN_HEADS):
        sl = slice(h * HEAD_DIM, (h + 1) * HEAD_DIM)
        q = q_ref[:, sl]
        s_c = lax.dot_general(q, kc_ref[:, sl], nt, preferred_element_type=F32) * scale
        s_p = lax.dot_general(q, kp_ref[:, sl], nt, preferred_element_type=F32) * scale
        s_c = jnp.where(mask_c, s_c, NEG_INF)
        s_p = jnp.where(mask_p, s_p, NEG_INF)
        m = jnp.max(jnp.maximum(s_c, s_p), axis=1, keepdims=True)
        p_c = jnp.exp(s_c - m)
        p_p = jnp.exp(s_p - m)
        den = jnp.sum(p_c + p_p, axis=1, keepdims=True)
        o = (jnp.dot(p_c.astype(BF16), vc_ref[:, sl], preferred_element_type=F32)
             + jnp.dot(p_p.astype(BF16), vp_ref[:, sl], preferred_element_type=F32))
        o_ref[:, sl] = (o / den).astype(BF16)
        lse_ref[:, h:h + 1] = m + jnp.log(den)


def _attention(qkv, group):
    d, l, ncol = qkv.shape
    s = d * l
    flat = qkv.reshape(s, ncol)
    nblk = l // Q_BLOCK
    cur = lambda c: pl.BlockSpec((Q_BLOCK, HD), lambda b: (b, c))
    prev = lambda c: pl.BlockSpec((Q_BLOCK, HD), lambda b: (jnp.maximum(b - 1, 0), c))
    return pl.pallas_call(
        functools.partial(_attn_kernel, nblk=nblk),
        grid=(s // Q_BLOCK,),
        in_specs=[cur(0), cur(1), cur(2), prev(1), prev(2)],
        out_specs=[pl.BlockSpec((Q_BLOCK, HD), lambda b: (b, 0)),
                   pl.BlockSpec((Q_BLOCK, ATTN_HEADS), lambda b: (b, 0))],
        out_shape=[jax.ShapeDtypeStruct((s, HD), BF16), jax.ShapeDtypeStruct((s, ATTN_HEADS), F32)],
        compiler_params=_cparams(("arbitrary",)),
        name=f"dil_attn_g{group}",
    )(flat, flat, flat, flat, flat)


def _attn_out_kernel(o0_ref, o1_ref, o2_ref, l0_ref, l1_ref, l2_ref, w_ref, x_ref, gate_ref, g_ref, b_ref,
                     out_ref, o_scr, a_scr):
    tm = x_ref.shape[0]
    for gi, o_ref in enumerate((o1_ref, o2_ref)):
        d = DILATIONS[gi + 1]
        for r in range(d):
            for h in range(ATTN_HEADS):
                o_scr[gi, h, pl.ds(r, tm // d, stride=d), :] = (
                    o_ref[r, :, h * HEAD_DIM:(h + 1) * HEAD_DIM].astype(F32))
    l0 = l0_ref[...]
    l1 = l1_ref[...]
    l2 = l2_ref[...]
    m = jnp.maximum(jnp.maximum(l0, l1), l2)
    e0 = jnp.exp(l0 - m)
    e1 = jnp.exp(l1 - m)
    e2 = jnp.exp(l2 - m)
    z = e0 + e1 + e2
    w0 = e0 / z
    w1 = e1 / z
    w2 = e2 / z
    for h in range(ATTN_HEADS):
        sl = slice(h * HEAD_DIM, (h + 1) * HEAD_DIM)
        a = (w0[:, h:h + 1] * o0_ref[:, sl].astype(F32)
             + w1[:, h:h + 1] * o_scr[0, h]
             + w2[:, h:h + 1] * o_scr[1, h])
        a_scr[:, sl] = a.astype(BF16)
    y = jnp.dot(a_scr[...], w_ref[...], preferred_element_type=F32)
    out_ref[...] = _deepnorm(x_ref[...], y, gate_ref[...], g_ref[...], b_ref[...])


def _attn_out(outs, lses, w_bf, x, gate, g, b):
    s = x.shape[0]
    tm = TM_OUT
    d1, d2 = DILATIONS[1], DILATIONS[2]
    o1 = outs[1].reshape(d1, s // d1, HD)
    o2 = outs[2].reshape(d2, s // d2, HD)
    l1 = lses[1].reshape(d1, s // d1, ATTN_HEADS).transpose(1, 0, 2).reshape(s, ATTN_HEADS)
    l2 = lses[2].reshape(d2, s // d2, ATTN_HEADS).transpose(1, 0, 2).reshape(s, ATTN_HEADS)
    perm = lambda d, w: pl.BlockSpec((d, tm // d, w), lambda i: (0, i, 0))
    return pl.pallas_call(
        _attn_out_kernel,
        grid=(s // tm,),
        in_specs=[_row_spec(tm, HD), perm(d1, HD), perm(d2, HD),
                  _row_spec(tm, ATTN_HEADS), _row_spec(tm, ATTN_HEADS), _row_spec(tm, ATTN_HEADS),
                  pl.BlockSpec((HD, D_MODEL), lambda i: (0, 0)),
                  _row_spec(tm, D_MODEL), _vec_spec(D_MODEL), _vec_spec(D_MODEL), _vec_spec(D_MODEL)],
        out_specs=_row_spec(tm, D_MODEL),
        out_shape=jax.ShapeDtypeStruct((s, D_MODEL), F32),
        scratch_shapes=[pltpu.VMEM((2, ATTN_HEADS, tm, HEAD_DIM), F32), pltpu.VMEM((tm, HD), BF16)],
        compiler_params=_cparams(("arbitrary",)),
        name="attn_out",
    )(outs[0], o1, o2, lses[0], l1, l2, w_bf, x, gate, g, b)


def _modulate_once(x_ref, sc_ref, sh_ref, u_scr):
    @pl.when(pl.program_id(1) == 0)
    def _():
        u_scr[...] = (x_ref[...] * (1.0 + sc_ref[...]) + sh_ref[...]).astype(BF16)


def _glu_in_kernel(x_ref, sc_ref, sh_ref, wa_ref, wg_ref, o_ref, u_scr):
    _modulate_once(x_ref, sc_ref, sh_ref, u_scr)
    u = u_scr[...]
    a = jnp.dot(u, wa_ref[...], preferred_element_type=F32)
    gt = jnp.dot(u, wg_ref[...], preferred_element_type=F32)
    o_ref[...] = a * jax.nn.sigmoid(gt)


def _sc_in_kernel(x_ref, sc_ref, sh_ref, wb_ref, wc_ref, wh_ref, b_ref, ch_ref, u_scr):
    _modulate_once(x_ref, sc_ref, sh_ref, u_scr)
    u = u_scr[...]
    b_ref[...] = jnp.dot(u, wb_ref[...], preferred_element_type=F32)
    c = jnp.dot(u, wc_ref[...], preferred_element_type=F32)
    h = jnp.dot(u, wh_ref[...], preferred_element_type=F32)
    ch_ref[...] = c * h


def _gated_in(kernel, n_parts, n_out, x, sc, sh, w_bf, name):
    s = x.shape[0]
    tm, tn = TM_PROJ, TN_GLU
    nj = D_MODEL // tn
    w_specs = [pl.BlockSpec((D_MODEL, tn), functools.partial(lambda i, j, p: (0, p * nj + j), p=p))
               for p in range(n_parts)]
    out_spec = pl.BlockSpec((tm, tn), lambda i, j: (i, j))
    out_shape = jax.ShapeDtypeStruct((s, D_MODEL), F32)
    return pl.pallas_call(
        kernel,
        grid=(s // tm, nj),
        in_specs=[pl.BlockSpec((tm, D_MODEL), lambda i, j: (i, 0)),
                  pl.BlockSpec((1, D_MODEL), lambda i, j: (0, 0)),
                  pl.BlockSpec((1, D_MODEL), lambda i, j: (0, 0))] + w_specs,
        out_specs=[out_spec] * n_out if n_out > 1 else out_spec,
        out_shape=[out_shape] * n_out if n_out > 1 else out_shape,
        scratch_shapes=[pltpu.VMEM((tm, D_MODEL), BF16)],
        compiler_params=_cparams(("arbitrary", "arbitrary")),
        name=name,
    )(x, sc, sh, *([w_bf] * n_parts))


def _fill_halo(hal_scr, prev_ref, cur_ref, halo):
    is_first = pl.program_id(0) == 0
    hal_scr[pl.ds(0, halo), :] = jnp.where(is_first, 0.0, prev_ref[...])
    hal_scr[pl.ds(halo, cur_ref.shape[0]), :] = cur_ref[...]


def _dwconv(hal_scr, w_ref, dst_scr, halo, ksize, tm):
    off = halo - (ksize - 1)

    for r0 in range(0, tm, CONV_ROWS):
        for c0 in range(0, D_MODEL, CONV_COLS):
            cs = slice(c0, c0 + CONV_COLS)
            acc = jnp.zeros((CONV_ROWS, CONV_COLS), F32)
            for k in range(ksize):
                acc = acc + w_ref[k:k + 1, cs] * hal_scr[r0 + off + k:r0 + off + k + CONV_ROWS, cs]
            dst_scr[r0:r0 + CONV_ROWS, cs] = acc


def _conv_out_kernel(cur_ref, prev_ref, wdw_ref, cg_ref, cb_ref, w_ref, x_ref, gate_ref, g_ref, b_ref,
                     out_ref, hal_scr, h_scr):
    tm = x_ref.shape[0]
    _fill_halo(hal_scr, prev_ref, cur_ref, HALO_CONV)
    _dwconv(hal_scr, wdw_ref, h_scr, HALO_CONV, CONV_KERNEL, tm)
    hn = _ln_rows(h_scr[...], cg_ref[...], cb_ref[...])
    a = (hn * jax.nn.sigmoid(hn)).astype(BF16)
    y = jnp.dot(a, w_ref[...], preferred_element_type=F32)
    out_ref[...] = _deepnorm(x_ref[...], y, gate_ref[...], g_ref[...], b_ref[...])


def _sc_out_kernel(cur_ref, prev_ref, bg_ref, wdw_ref, w_ref, x_ref, gate_ref, g_ref, b_ref,
                   out_ref, hal_scr, h_scr):
    tm = x_ref.shape[0]
    _fill_halo(hal_scr, prev_ref, cur_ref, HALO_SC)
    _dwconv(hal_scr, wdw_ref, h_scr, HALO_SC, SHORT_CONV_KERNEL, tm)
    a = (bg_ref[...] * h_scr[...]).astype(BF16)
    y = jnp.dot(a, w_ref[...], preferred_element_type=F32)
    out_ref[...] = _deepnorm(x_ref[...], y, gate_ref[...], g_ref[...], b_ref[...])


def _halo_spec(tm, halo):
    per = tm // halo
    return pl.BlockSpec((halo, D_MODEL), lambda i: (jnp.maximum(i * per - 1, 0), 0))


def _conv_out(glu, w_dw, cg, cb, w_bf, x, gate, g, b):
    s = x.shape[0]
    tm = TM_OUT
    full = lambda shape: pl.BlockSpec(shape, lambda i: (0, 0))
    return pl.pallas_call(
        _conv_out_kernel,
        grid=(s // tm,),
        in_specs=[_row_spec(tm, D_MODEL), _halo_spec(tm, HALO_CONV), full((CONV_KERNEL, D_MODEL)),
                  _vec_spec(D_MODEL), _vec_spec(D_MODEL), full((D_MODEL, D_MODEL)),
                  _row_spec(tm, D_MODEL), _vec_spec(D_MODEL), _vec_spec(D_MODEL), _vec_spec(D_MODEL)],
        out_specs=_row_spec(tm, D_MODEL),
        out_shape=jax.ShapeDtypeStruct((s, D_MODEL), F32),
        scratch_shapes=[pltpu.VMEM((tm + HALO_CONV, D_MODEL), F32), pltpu.VMEM((tm, D_MODEL), F32)],
        compiler_params=_cparams(("arbitrary",)),
        name="conv_out",
    )(glu, glu, w_dw, cg, cb, w_bf, x, gate, g, b)


def _sc_out(ch, bgate, w_dw, w_bf, x, gate, g, b):
    s = x.shape[0]
    tm = TM_OUT
    full = lambda shape: pl.BlockSpec(shape, lambda i: (0, 0))
    return pl.pallas_call(
        _sc_out_kernel,
        grid=(s // tm,),
        in_specs=[_row_spec(tm, D_MODEL), _halo_spec(tm, HALO_SC), _row_spec(tm, D_MODEL),
                  full((SHORT_CONV_KERNEL, D_MODEL)), full((D_MODEL, D_MODEL)),
                  _row_spec(tm, D_MODEL), _vec_spec(D_MODEL), _vec_spec(D_MODEL), _vec_spec(D_MODEL)],
        out_specs=_row_spec(tm, D_MODEL),
        out_shape=jax.ShapeDtypeStruct((s, D_MODEL), F32),
        scratch_shapes=[pltpu.VMEM((tm + HALO_SC, D_MODEL), F32), pltpu.VMEM((tm, D_MODEL), F32)],
        compiler_params=_cparams(("arbitrary",)),
        name="sc_out",
    )(ch, ch, bgate, w_dw, w_bf, x, gate, g, b)


def _argmax4_first(v):
    i01 = jnp.where(v[1] > v[0], 1, 0)
    m01 = jnp.maximum(v[0], v[1])
    i23 = jnp.where(v[3] > v[2], 3, 2)
    m23 = jnp.maximum(v[2], v[3])
    return jnp.where(m23 > m01, i23, i01), jnp.maximum(m01, m23)


def _select4(idx, v):
    return jnp.where(idx == 0, v[0], jnp.where(idx == 1, v[1], jnp.where(idx == 2, v[2], v[3])))


def _router_kernel(x_ref, sc_ref, sh_ref, wt_ref, rb_ref, e_ref, gt_ref, rk_ref, cnt_ref, carry_scr):
    i = pl.program_id(0)
    tm = x_ref.shape[0]

    @pl.when(i == 0)
    def _():
        carry_scr[...] = jnp.zeros_like(carry_scr)

    u = x_ref[...] * (1.0 + sc_ref[...]) + sh_ref[...]
    logits = lax.dot_general(wt_ref[...], u, (((1,), (1,)), ((), ())), precision=lax.Precision.HIGHEST,
                             preferred_element_type=F32) + rb_ref[...]
    ex = jnp.exp(logits - jnp.max(logits, axis=0, keepdims=True))
    probs = ex / jnp.sum(ex, axis=0, keepdims=True)

    top1_i, top1_v, top2_i, top2_v, score = [], [], [], [], []
    for grp in range(N_EXPERT_GROUPS):
        p = [probs[grp * EXPERTS_PER_GROUP + j:grp * EXPERTS_PER_GROUP + j + 1, :]
             for j in range(EXPERTS_PER_GROUP)]
        i1, v1 = _argmax4_first(p)
        rest = [jnp.where(i1 == j, -1.0, p[j]) for j in range(EXPERTS_PER_GROUP)]
        i2, v2 = _argmax4_first(rest)
        top1_i.append(i1)
        top1_v.append(v1)
        top2_i.append(i2)
        top2_v.append(v2)
        score.append(v1 + v2)
    gsel, _ = _argmax4_first(score)
    p1 = _select4(gsel, top1_v)
    p2 = _select4(gsel, top2_v)
    e1 = gsel * EXPERTS_PER_GROUP + _select4(gsel, top1_i)
    e2 = gsel * EXPERTS_PER_GROUP + _select4(gsel, top2_i)
    psum = p1 + p2
    e_ref[0:1, :] = e1
    e_ref[1:2, :] = e2
    gt_ref[0:1, :] = p1 / psum
    gt_ref[1:2, :] = p2 / psum

    eid = lax.broadcasted_iota(jnp.int32, (N_EXPERTS, tm), 0)
    earlier = (lax.broadcasted_iota(jnp.int32, (tm, tm), 0)
               < lax.broadcasted_iota(jnp.int32, (tm, tm), 1)).astype(BF16)
    base = carry_scr[:, 0:1]
    for slot, e_sel in enumerate((e1, e2)):
        onehot = (eid == e_sel).astype(F32)
        before = jnp.dot(onehot.astype(BF16), earlier, preferred_element_type=F32)
        rank = jnp.sum(onehot * (before + base), axis=0, keepdims=True)
        rk_ref[slot:slot + 1, :] = rank.astype(jnp.int32)
        base = base + jnp.sum(onehot, axis=1, keepdims=True)
    carry_scr[...] = jnp.broadcast_to(base, carry_scr.shape)
    cnt_ref[...] = carry_scr[...]


def _router(x, sc, sh, rw_t, rb_col):
    s = x.shape[0]
    tm = TM_ROUTE
    slot_spec = pl.BlockSpec((TOP_K, tm), lambda i: (0, i))
    full = lambda shape: pl.BlockSpec(shape, lambda i: (0, 0))
    return pl.pallas_call(
        _router_kernel,
        grid=(s // tm,),
        in_specs=[_row_spec(tm, D_MODEL), _vec_spec(D_MODEL), _vec_spec(D_MODEL),
                  full((N_EXPERTS, D_MODEL)), full((N_EXPERTS, 1))],
        out_specs=[slot_spec, slot_spec, slot_spec, full((N_EXPERTS, LANES_V7X))],
        out_shape=[jax.ShapeDtypeStruct((TOP_K, s), jnp.int32), jax.ShapeDtypeStruct((TOP_K, s), F32),
                   jax.ShapeDtypeStruct((TOP_K, s), jnp.int32),
                   jax.ShapeDtypeStruct((N_EXPERTS, LANES_V7X), F32)],
        scratch_shapes=[pltpu.VMEM((N_EXPERTS, LANES_V7X), F32)],
        compiler_params=_cparams(("arbitrary",)),
        name="router",
    )(x, sc, sh, rw_t, rb_col)


def _row_gather(src_hbm, idx_ref, base, n, dst, sem):
    def body(r, carry):
        tok = idx_ref[base + r]
        pltpu.make_async_copy(src_hbm.at[pl.ds(tok, 1), :], dst.at[pl.ds(r, 1), :], sem).start()
        return carry

    lax.fori_loop(0, n, body, 0)


def _row_gather_wait(src_hbm, n, dst, sem):
    pltpu.make_async_copy(src_hbm.at[pl.ds(0, n), :], dst, sem).wait()


def _ffn_kernel(rt_ref, be_ref, nbu_ref, x_hbm, sc_ref, sh_ref, wg_ref, wu_ref, wd_ref, o_ref, xbuf, sem):
    del be_ref
    i = pl.program_id(0)
    nbu = nbu_ref[0]
    tm = xbuf.shape[1]

    @pl.when(i == 0)
    def _():
        _row_gather(x_hbm, rt_ref, 0, tm, xbuf.at[0], sem.at[0])

    @pl.when(i + 1 < nbu)
    def _():
        nxt = (i + 1) % 2
        _row_gather(x_hbm, rt_ref, (i + 1) * tm, tm, xbuf.at[nxt], sem.at[nxt])

    @pl.when(i < nbu)
    def _():
        slot = i % 2
        _row_gather_wait(x_hbm, tm, xbuf.at[slot], sem.at[slot])
        u = (xbuf[slot] * (1.0 + sc_ref[...]) + sh_ref[...]).astype(BF16)
        gt = jnp.dot(u, wg_ref[...], preferred_element_type=F32)
        up = jnp.dot(u, wu_ref[...], preferred_element_type=F32)
        h = (gt * jax.nn.sigmoid(gt) * up).astype(BF16)
        o_ref[...] = jnp.dot(h, wd_ref[...], preferred_element_type=F32)

    @pl.when(i >= nbu)
    def _():
        o_ref[...] = jnp.zeros_like(o_ref)


def _moe_ffn(x, sc, sh, wg_bf, wu_bf, wd_bf, layer, row_token, block_expert, nb_used):
    tm = TM_MOE
    p_rows = row_token.shape[0]
    nb = p_rows // tm

    def last_used(i, nbu):
        return jnp.minimum(i, nbu[0] - 1)

    w_in = pl.BlockSpec((None, None, D_MODEL, D_EXPERT),
                        lambda i, rt, be, nbu: (layer, be[last_used(i, nbu)], 0, 0))
    w_out = pl.BlockSpec((None, None, D_EXPERT, D_MODEL),
                         lambda i, rt, be, nbu: (layer, be[last_used(i, nbu)], 0, 0))
    vec = pl.BlockSpec((1, D_MODEL), lambda i, rt, be, nbu: (0, 0))
    return pl.pallas_call(
        _ffn_kernel,
        grid_spec=pltpu.PrefetchScalarGridSpec(
            num_scalar_prefetch=3,
            grid=(nb,),
            in_specs=[pl.BlockSpec(memory_space=pl.ANY), vec, vec, w_in, w_in, w_out],
            out_specs=pl.BlockSpec((tm, D_MODEL), lambda i, rt, be, nbu: (i, 0)),
            scratch_shapes=[pltpu.VMEM((2, tm, D_MODEL), F32), pltpu.SemaphoreType.DMA((2,))]),
        out_shape=jax.ShapeDtypeStruct((p_rows, D_MODEL), F32),
        compiler_params=_cparams(("arbitrary",)),
        name="moe_ffn",
    )(row_token, block_expert, nb_used, x, sc, sh, wg_bf, wu_bf, wd_bf)


def _combine_kernel(dest_ref, rows_hbm, gt_ref, x_ref, gate_ref, g_ref, b_ref, out_ref, ybuf, sem):
    i = pl.program_id(0)
    n = pl.num_programs(0)
    tm = x_ref.shape[0]
    s = tm * n

    def start(tile, slot):
        for k in range(TOP_K):
            _row_gather(rows_hbm, dest_ref, k * s + tile * tm, tm, ybuf.at[slot, k], sem.at[slot, k])

    @pl.when(i == 0)
    def _():
        start(0, 0)

    @pl.when(i + 1 < n)
    def _():
        start(i + 1, (i + 1) % 2)

    slot = i % 2
    for k in range(TOP_K):
        _row_gather_wait(rows_hbm, tm, ybuf.at[slot, k], sem.at[slot, k])
    gt = gt_ref[...]
    y = gt[:, 0:1] * ybuf[slot, 0] + gt[:, 1:2] * ybuf[slot, 1]
    out_ref[...] = _deepnorm(x_ref[...], y, gate_ref[...], g_ref[...], b_ref[...])


def _moe_combine(dest_flat, out_rows, gates_t, x, gate, g, b):
    s = x.shape[0]
    tm = TM_COMB
    vec = pl.BlockSpec((1, D_MODEL), lambda i, dst: (0, 0))
    return pl.pallas_call(
        _combine_kernel,
        grid_spec=pltpu.PrefetchScalarGridSpec(
            num_scalar_prefetch=1,
            grid=(s // tm,),
            in_specs=[pl.BlockSpec(memory_space=pl.ANY),
                      pl.BlockSpec((tm, TOP_K), lambda i, dst: (i, 0)),
                      pl.BlockSpec((tm, D_MODEL), lambda i, dst: (i, 0)), vec, vec, vec],
            out_specs=pl.BlockSpec((tm, D_MODEL), lambda i, dst: (i, 0)),
            scratch_shapes=[pltpu.VMEM((2, TOP_K, tm, D_MODEL), F32), pltpu.SemaphoreType.DMA((2, TOP_K))]),
        out_shape=jax.ShapeDtypeStruct((s, D_MODEL), F32),
        compiler_params=_cparams(("arbitrary",)),
        name="moe_combine",
    )(dest_flat, out_rows, gates_t, x, gate, g, b)


def _moe_layer(x, sc, sh, gate, g, b, rw_t, rb_col, wg_bf, wu_bf, wd_bf, layer):
    s = x.shape[0]
    tm = TM_MOE
    expert, gates, rank, cnt = _router(x, sc, sh, rw_t, rb_col)
    counts = cnt[:, 0].astype(jnp.int32)
    padded = (counts + tm - 1) // tm * tm
    pad_end = jnp.cumsum(padded)
    pad_start = pad_end - padded
    dest = (pad_start[expert] + rank).reshape(-1)
    p_rows = TOP_K * s + N_EXPERTS * tm
    token = jnp.tile(jnp.arange(s, dtype=jnp.int32), TOP_K)
    row_token = jnp.zeros((p_rows,), jnp.int32).at[dest].set(token)
    nb = p_rows // tm
    block_expert = jnp.minimum(jnp.searchsorted(pad_end, jnp.arange(nb, dtype=jnp.int32) * tm, side='right'),
                               N_EXPERTS - 1).astype(jnp.int32)
    nb_used = (pad_end[-1:] // tm).astype(jnp.int32)
    out_rows = _moe_ffn(x, sc, sh, wg_bf, wu_bf, wd_bf, layer, row_token, block_expert, nb_used)
    return _moe_combine(dest.astype(jnp.int32), out_rows, gates.T, x, gate, g, b)


def kernel(x, c, positions, ada_w, ada_b, ln_g, ln_b, attn_w_qkv, attn_w_o, conv_w_pw1, conv_w_dw, conv_ln_g,
           conv_ln_b, conv_w_pw2, sc_w_in, sc_w_conv, sc_w_out, router_w, router_b, moe_w_gate, moe_w_up,
           moe_w_down):
    batch, s, d = x.shape
    assert batch == 1 and d == D_MODEL and s % (DILATIONS[-1] * Q_BLOCK) == 0
    xs = x.reshape(s, d)

    mod = _adaln(c.reshape(d, 1), ada_w, ada_b)
    half = HEAD_DIM // 2
    inv_freq = ROPE_THETA ** (-jnp.arange(half, dtype=F32) / half)
    freq_row = jnp.concatenate([inv_freq, inv_freq]).reshape(1, HEAD_DIM)
    cos, sin = _rope_tables(positions.reshape(s, 1), freq_row)

    rw_t = router_w.T
    rb_col = router_b.reshape(N_EXPERTS, 1)
    wg_bf = moe_w_gate.astype(BF16)
    wu_bf = moe_w_up.astype(BF16)
    wd_bf = moe_w_down.astype(BF16)
    vec = lambda a: a.reshape(1, d)

    for i in range(DEPTH):
        sh1, sc1, g1, sh2, sc2, g2 = [mod[i, :, k * d:(k + 1) * d] for k in range(6)]
        m, j = i % N_MIXERS, i // N_MIXERS
        lg, lb = vec(ln_g[i, 0]), vec(ln_b[i, 0])
        if m == 0:
            w_qkv = attn_w_qkv[j].astype(BF16)
            outs, lses = [], []
            for grp in range(len(DILATIONS)):
                o_g, l_g = _attention(_qkv_proj(xs, sc1, sh1, w_qkv, cos, sin, grp), grp)
                outs.append(o_g)
                lses.append(l_g)
            xs = _attn_out(outs, lses, attn_w_o[j].astype(BF16), xs, g1, lg, lb)
        elif m == 1:
            glu = _gated_in(_glu_in_kernel, 2, 1, xs, sc1, sh1, conv_w_pw1[j].astype(BF16), "conv_in")
            xs = _conv_out(glu, conv_w_dw[j], vec(conv_ln_g[j]), vec(conv_ln_b[j]),
                           conv_w_pw2[j].astype(BF16), xs, g1, lg, lb)
        else:
            bgate, ch = _gated_in(_sc_in_kernel, 3, 2, xs, sc1, sh1, sc_w_in[j].astype(BF16), "sc_in")
            xs = _sc_out(ch, bgate, sc_w_conv[j], sc_w_out[j].astype(BF16), xs, g1, lg, lb)
        xs = _moe_layer(xs, sc2, sh2, g2, vec(ln_g[i, 1]), vec(ln_b[i, 1]), rw_t, rb_col,
                        wg_bf, wu_bf, wd_bf, i)
    return xs.reshape(batch, s, d)
```

```python
import functools
import math

import jax
import jax.numpy as jnp
from jax import lax
from jax.experimental import pallas as pl
from jax.experimental.pallas import tpu as pltpu

F32 = jnp.float32
BF16 = jnp.bfloat16

D_MODEL = 2048
DEPTH = 4
N_MIXERS = 3
ATTN_HEADS = 16
HEAD_DIM = 128
HD = ATTN_HEADS * HEAD_DIM
DILATIONS = (1, 4, 16)
Q_BLOCK = 128
ROPE_THETA = 10000.0
CONV_KERNEL = 31
SHORT_CONV_KERNEL = 3
N_EXPERTS = 16
N_EXPERT_GROUPS = 4
EXPERTS_PER_GROUP = 4
TOP_K = 2
D_EXPERT = 1408
ALPHA = (2 * DEPTH) ** 0.25
LN_EPS = 1e-5
NEG_INF = -1e30

LANES_V7X = 128
VMEM_LIMIT_V7X = 56 * 1024 * 1024
TM_PROJ = 512
TN_PROJ = 1024
TN_GLU = 512
TM_OUT = 256
TM_ROUTE = 512
TM_MOE = 256
TM_COMB = 256
TN_ADA = 1024
HALO_CONV = 32
HALO_SC = 8
CONV_ROWS = 32
CONV_COLS = 512


def _cparams(sem):
    return pltpu.CompilerParams(dimension_semantics=sem, vmem_limit_bytes=VMEM_LIMIT_V7X)


def _ln_rows(z, g, b):
    mu = jnp.mean(z, axis=-1, keepdims=True)
    zc = z - mu
    var = jnp.mean(zc * zc, axis=-1, keepdims=True)
    return zc * lax.rsqrt(var + LN_EPS) * g + b


def _deepnorm(x, y, gate, g, b):
    return _ln_rows(ALPHA * x + (1.0 + gate) * y, g, b)


def _row_spec(tm, width):
    return pl.BlockSpec((tm, width), lambda i: (i, 0))


def _vec_spec(width):
    return pl.BlockSpec((1, width), lambda i: (0, 0))


def _adaln_kernel(c_ref, w_ref, b_ref, o_ref):
    c = c_ref[...]
    ca = c * jax.nn.sigmoid(c)
    o_ref[...] = jnp.sum(w_ref[...] * ca, axis=0, keepdims=True) + b_ref[...]


def _adaln(c_col, ada_w, ada_b):
    depth, d, n = ada_w.shape
    return pl.pallas_call(
        _adaln_kernel,
        grid=(depth, n // TN_ADA),
        in_specs=[pl.BlockSpec((d, 1), lambda l, j: (0, 0)),
                  pl.BlockSpec((None, d, TN_ADA), lambda l, j: (l, 0, j)),
                  pl.BlockSpec((None, 1, TN_ADA), lambda l, j: (l, 0, j))],
        out_specs=pl.BlockSpec((None, 1, TN_ADA), lambda l, j: (l, 0, j)),
        out_shape=jax.ShapeDtypeStruct((depth, 1, n), F32),
        compiler_params=_cparams(("arbitrary", "arbitrary")),
        name="adaln",
    )(c_col, ada_w, ada_b.reshape(depth, 1, n))


def _rope_kernel(pos_ref, freq_ref, cos_ref, sin_ref):
    ang = pos_ref[...].astype(F32) * freq_ref[...]
    lane = lax.broadcasted_iota(jnp.int32, ang.shape, 1)
    s = jnp.sin(ang)
    cos_ref[...] = jnp.cos(ang)
    sin_ref[...] = jnp.where(lane < HEAD_DIM // 2, -s, s)


def _rope_tables(pos_col, freq_row):
    s = pos_col.shape[0]
    tm = 1024
    spec = pl.BlockSpec((tm, HEAD_DIM), lambda i: (i, 0))
    return pl.pallas_call(
        _rope_kernel,
        grid=(s // tm,),
        in_specs=[pl.BlockSpec((tm, 1), lambda i: (i, 0)), _vec_spec(HEAD_DIM)],
        out_specs=[spec, spec],
        out_shape=[jax.ShapeDtypeStruct((s, HEAD_DIM), F32)] * 2,
        compiler_params=_cparams(("arbitrary",)),
        name="rope_tables",
    )(pos_col, freq_row)


def _qkv_kernel(x_ref, sc_ref, sh_ref, w_ref, cos_ref, sin_ref, o_ref, u_scr, acc_scr, *, d, n_rope):
    j = pl.program_id(1)
    nh, tm, _ = acc_scr.shape

    @pl.when(j == 0)
    def _():
        u_scr[...] = (x_ref[...] * (1.0 + sc_ref[...]) + sh_ref[...]).astype(BF16)

    acc = jnp.dot(u_scr[...], w_ref[...], preferred_element_type=F32)

    @pl.when(j < n_rope)
    def _():
        cos = cos_ref[...]
        sin = sin_ref[...]
        for h in range(nh):
            a = acc[:, h * HEAD_DIM:(h + 1) * HEAD_DIM]
            acc_scr[h] = a * cos + pltpu.roll(a, HEAD_DIM // 2, 1) * sin

    @pl.when(j >= n_rope)
    def _():
        for h in range(nh):
            acc_scr[h] = acc[:, h * HEAD_DIM:(h + 1) * HEAD_DIM]

    for r in range(d):
        for h in range(nh):
            rows = acc_scr[h, pl.ds(r, tm // d, stride=d), :] if d > 1 else acc_scr[h]
            o_ref[r, :, h * HEAD_DIM:(h + 1) * HEAD_DIM] = rows.astype(BF16)


def _qkv_proj(x, sc, sh, w_bf, cos, sin, group):
    s = x.shape[0]
    d = DILATIONS[group]
    tm, tn = TM_PROJ, TN_PROJ
    ncol = 3 * HD
    col0 = group * (ncol // tn)
    return pl.pallas_call(
        functools.partial(_qkv_kernel, d=d, n_rope=2 * HD // tn),
        grid=(s // tm, ncol // tn),
        in_specs=[pl.BlockSpec((tm, D_MODEL), lambda i, j: (i, 0)),
                  pl.BlockSpec((1, D_MODEL), lambda i, j: (0, 0)),
                  pl.BlockSpec((1, D_MODEL), lambda i, j: (0, 0)),
                  pl.BlockSpec((D_MODEL, tn), lambda i, j: (0, col0 + j)),
                  pl.BlockSpec((tm, HEAD_DIM), lambda i, j: (i, 0)),
                  pl.BlockSpec((tm, HEAD_DIM), lambda i, j: (i, 0))],
        out_specs=pl.BlockSpec((d, tm // d, tn), lambda i, j: (0, i, j)),
        out_shape=jax.ShapeDtypeStruct((d, s // d, ncol), BF16),
        scratch_shapes=[pltpu.VMEM((tm, D_MODEL), BF16), pltpu.VMEM((tn // HEAD_DIM, tm, HEAD_DIM), F32)],
        compiler_params=_cparams(("arbitrary", "arbitrary")),
        name=f"qkv_proj_g{group}",
    )(x, sc, sh, w_bf, cos, sin)


def _attn_kernel(q_ref, kc_ref, vc_ref, kp_ref, vp_ref, o_ref, lse_ref, *, nblk):
    b = pl.program_id(0)
    not_first = (b % nblk) != 0
    row = lax.broadcasted_iota(jnp.int32, (Q_BLOCK, Q_BLOCK), 0)
    col = lax.broadcasted_iota(jnp.int32, (Q_BLOCK, Q_BLOCK), 1)
    mask_c = col <= row
    mask_p = jnp.logical_and(col >= row, not_first)
    scale = 1.0 / math.sqrt(HEAD_DIM)
    nt = (((1,), (1,)), ((), ()))
    for h in range(ATTN_HEADS):
        sl = slice(h * HEAD_DIM, (h + 1) * HEAD_DIM)
        q = q_ref[:, sl]
        s_c = lax.dot_general(q, kc_ref[:, sl], nt, preferred_element_type=F32) * scale
        s_p = lax.dot_general(q, kp_ref[:, sl], nt, preferred_element_type=F32) * scale
        s_c = jnp.where(mask_c, s_c, NEG_INF)
        s_p = jnp.where(mask_p, s_p, NEG_INF)
        m = jnp.max(jnp.maximum(s_c, s_p), axis=1, keepdims=True)
        p_c = jnp.exp(s_c - m)
        p_p = jnp.exp(s_p - m)
        den = jnp.sum(p_c + p_p, axis=1, keepdims=True)
        o = (jnp.dot(p_c.astype(BF16), vc_ref[:, sl], preferred_element_type=F32)
             + jnp.dot(p_p.astype(BF16), vp_ref[:, sl], preferred_element_type=F32))
        o_ref[:, sl] = (o / den).astype(BF16)
        lse_ref[:, h:h + 1] = m + jnp.log(den)


def _attention(qkv, group):
    d, l, ncol = qkv.shape
    s = d * l
    flat = qkv.reshape(s, ncol)
    nblk = l // Q_BLOCK
    cur = lambda c: pl.BlockSpec((Q_BLOCK, HD), lambda b: (b, c))
    prev = lambda c: pl.BlockSpec((Q_BLOCK, HD), lambda b: (jnp.maximum(b - 1, 0), c))
    return pl.pallas_call(
        functools.partial(_attn_kernel, nblk=nblk),
        grid=(s // Q_BLOCK,),
        in_specs=[cur(0), cur(1), cur(2), prev(1), prev(2)],
        out_specs=[pl.BlockSpec((Q_BLOCK, HD), lambda b: (b, 0)),
                   pl.BlockSpec((Q_BLOCK, ATTN_HEADS), lambda b: (b, 0))],
        out_shape=[jax.ShapeDtypeStruct((s, HD), BF16), jax.ShapeDtypeStruct((s, ATTN_HEADS), F32)],
        compiler_params=_cparams(("arbitrary",)),
        name=f"dil_attn_g{group}",
    )(flat, flat, flat, flat, flat)


def _attn_out_kernel(o0_ref, o1_ref, o2_ref, l0_ref, l1_ref, l2_ref, w_ref, x_ref, gate_ref, g_ref, b_ref,
                     out_ref, o_scr, a_scr):
    tm = x_ref.shape[0]
    for gi, o_ref in enumerate((o1_ref, o2_ref)):
        d = DILATIONS[gi + 1]
        for r in range(d):
            for h in range(ATTN_HEADS):
                o_scr[gi, h, pl.ds(r, tm // d, stride=d), :] = (
                    o_ref[r, :, h * HEAD_DIM:(h + 1) * HEAD_DIM].astype(F32))
    l0 = l0_ref[...]
    l1 = l1_ref[...]
    l2 = l2_ref[...]
    m = jnp.maximum(jnp.maximum(l0, l1), l2)
    e0 = jnp.exp(l0 - m)
    e1 = jnp.exp(l1 - m)
    e2 = jnp.exp(l2 - m)
    z = e0 + e1 + e2
    w0 = e0 / z
    w1 = e1 / z
    w2 = e2 / z
    for h in range(ATTN_HEADS):
        sl = slice(h * HEAD_DIM, (h + 1) * HEAD_DIM)
        a = (w0[:, h:h + 1] * o0_ref[:, sl].astype(F32)
             + w1[:, h:h + 1] * o_scr[0, h]
             + w2[:, h:h + 1] * o_scr[1, h])
        a_scr[:, sl] = a.astype(BF16)
    y = jnp.dot(a_scr[...], w_ref[...], preferred_element_type=F32)
    out_ref[...] = _deepnorm(x_ref[...], y, gate_ref[...], g_ref[...], b_ref[...])


def _attn_out(outs, lses, w_bf, x, gate, g, b):
    s = x.shape[0]
    tm = TM_OUT
    d1, d2 = DILATIONS[1], DILATIONS[2]
    o1 = outs[1].reshape(d1, s // d1, HD)
    o2 = outs[2].reshape(d2, s // d2, HD)
    l1 = lses[1].reshape(d1, s // d1, ATTN_HEADS).transpose(1, 0, 2).reshape(s, ATTN_HEADS)
    l2 = lses[2].reshape(d2, s // d2, ATTN_HEADS).transpose(1, 0, 2).reshape(s, ATTN_HEADS)
    perm = lambda d, w: pl.BlockSpec((d, tm // d, w), lambda i: (0, i, 0))
    return pl.pallas_call(
        _attn_out_kernel,
        grid=(s // tm,),
        in_specs=[_row_spec(tm, HD), perm(d1, HD), perm(d2, HD),
                  _row_spec(tm, ATTN_HEADS), _row_spec(tm, ATTN_HEADS), _row_spec(tm, ATTN_HEADS),
                  pl.BlockSpec((HD, D_MODEL), lambda i: (0, 0)),
                  _row_spec(tm, D_MODEL), _vec_spec(D_MODEL), _vec_spec(D_MODEL), _vec_spec(D_MODEL)],
        out_specs=_row_spec(tm, D_MODEL),
        out_shape=jax.ShapeDtypeStruct((s, D_MODEL), F32),
        scratch_shapes=[pltpu.VMEM((2, ATTN_HEADS, tm, HEAD_DIM), F32), pltpu.VMEM((tm, HD), BF16)],
        compiler_params=_cparams(("arbitrary",)),
        name="attn_out",
    )(outs[0], o1, o2, lses[0], l1, l2, w_bf, x, gate, g, b)


def _modulate_once(x_ref, sc_ref, sh_ref, u_scr):
    @pl.when(pl.program_id(1) == 0)
    def _():
        u_scr[...] = (x_ref[...] * (1.0 + sc_ref[...]) + sh_ref[...]).astype(BF16)


def _glu_in_kernel(x_ref, sc_ref, sh_ref, wa_ref, wg_ref, o_ref, u_scr):
    _modulate_once(x_ref, sc_ref, sh_ref, u_scr)
    u = u_scr[...]
    a = jnp.dot(u, wa_ref[...], preferred_element_type=F32)
    gt = jnp.dot(u, wg_ref[...], preferred_element_type=F32)
    o_ref[...] = a * jax.nn.sigmoid(gt)


def _sc_in_kernel(x_ref, sc_ref, sh_ref, wb_ref, wc_ref, wh_ref, b_ref, ch_ref, u_scr):
    _modulate_once(x_ref, sc_ref, sh_ref, u_scr)
    u = u_scr[...]
    b_ref[...] = jnp.dot(u, wb_ref[...], preferred_element_type=F32)
    c = jnp.dot(u, wc_ref[...], preferred_element_type=F32)
    h = jnp.dot(u, wh_ref[...], preferred_element_type=F32)
    ch_ref[...] = c * h


def _gated_in(kernel, n_parts, n_out, x, sc, sh, w_bf, name):
    s = x.shape[0]
    tm, tn = TM_PROJ, TN_GLU
    nj = D_MODEL // tn
    w_specs = [pl.BlockSpec((D_MODEL, tn), functools.partial(lambda i, j, p: (0, p * nj + j), p=p))
               for p in range(n_parts)]
    out_spec = pl.BlockSpec((tm, tn), lambda i, j: (i, j))
    out_shape = jax.ShapeDtypeStruct((s, D_MODEL), F32)
    return pl.pallas_call(
        kernel,
        grid=(s // tm, nj),
        in_specs=[pl.BlockSpec((tm, D_MODEL), lambda i, j: (i, 0)),
                  pl.BlockSpec((1, D_MODEL), lambda i, j: (0, 0)),
                  pl.BlockSpec((1, D_MODEL), lambda i, j: (0, 0))] + w_specs,
        out_specs=[out_spec] * n_out if n_out > 1 else out_spec,
        out_shape=[out_shape] * n_out if n_out > 1 else out_shape,
        scratch_shapes=[pltpu.VMEM((tm, D_MODEL), BF16)],
        compiler_params=_cparams(("arbitrary", "arbitrary")),
        name=name,
    )(x, sc, sh, *([w_bf] * n_parts))


def _fill_halo(hal_scr, prev_ref, cur_ref, halo):
    is_first = pl.program_id(0) == 0
    hal_scr[pl.ds(0, halo), :] = jnp.where(is_first, 0.0, prev_ref[...])
    hal_scr[pl.ds(halo, cur_ref.shape[0]), :] = cur_ref[...]


def _dwconv(hal_scr, w_ref, dst_scr, halo, ksize, tm):
    off = halo - (ksize - 1)

    for r0 in range(0, tm, CONV_ROWS):
        for c0 in range(0, D_MODEL, CONV_COLS):
            cs = slice(c0, c0 + CONV_COLS)
            acc = jnp.zeros((CONV_ROWS, CONV_COLS), F32)
            for k in range(ksize):
                acc = acc + w_ref[k:k + 1, cs] * hal_scr[r0 + off + k:r0 + off + k + CONV_ROWS, cs]
            dst_scr[r0:r0 + CONV_ROWS, cs] = acc


def _conv_out_kernel(cur_ref, prev_ref, wdw_ref, cg_ref, cb_ref, w_ref, x_ref, gate_ref, g_ref, b_ref,
                     out_ref, hal_scr, h_scr):
    tm = x_ref.shape[0]
    _fill_halo(hal_scr, prev_ref, cur_ref, HALO_CONV)
    _dwconv(hal_scr, wdw_ref, h_scr, HALO_CONV, CONV_KERNEL, tm)
    hn = _ln_rows(h_scr[...], cg_ref[...], cb_ref[...])
    a = (hn * jax.nn.sigmoid(hn)).astype(BF16)
    y = jnp.dot(a, w_ref[...], preferred_element_type=F32)
    out_ref[...] = _deepnorm(x_ref[...], y, gate_ref[...], g_ref[...], b_ref[...])


def _sc_out_kernel(cur_ref, prev_ref, bg_ref, wdw_ref, w_ref, x_ref, gate_ref, g_ref, b_ref,
                   out_ref, hal_scr, h_scr):
    tm = x_ref.shape[0]
    _fill_halo(hal_scr, prev_ref, cur_ref, HALO_SC)
    _dwconv(hal_scr, wdw_ref, h_scr, HALO_SC, SHORT_CONV_KERNEL, tm)
    a = (bg_ref[...] * h_scr[...]).astype(BF16)
    y = jnp.dot(a, w_ref[...], preferred_element_type=F32)
    out_ref[...] = _deepnorm(x_ref[...], y, gate_ref[...], g_ref[...], b_ref[...])


def _halo_spec(tm, halo):
    per = tm // halo
    return pl.BlockSpec((halo, D_MODEL), lambda i: (jnp.maximum(i * per - 1, 0), 0))


def _conv_out(glu, w_dw, cg, cb, w_bf, x, gate, g, b):
    s = x.shape[0]
    tm = TM_OUT
    full = lambda shape: pl.BlockSpec(shape, lambda i: (0, 0))
    return pl.pallas_call(
        _conv_out_kernel,
        grid=(s // tm,),
        in_specs=[_row_spec(tm, D_MODEL), _halo_spec(tm, HALO_CONV), full((CONV_KERNEL, D_MODEL)),
                  _vec_spec(D_MODEL), _vec_spec(D_MODEL), full((D_MODEL, D_MODEL)),
                  _row_spec(tm, D_MODEL), _vec_spec(D_MODEL), _vec_spec(D_MODEL), _vec_spec(D_MODEL)],
        out_specs=_row_spec(tm, D_MODEL),
        out_shape=jax.ShapeDtypeStruct((s, D_MODEL), F32),
        scratch_shapes=[pltpu.VMEM((tm + HALO_CONV, D_MODEL), F32), pltpu.VMEM((tm, D_MODEL), F32)],
        compiler_params=_cparams(("arbitrary",)),
        name="conv_out",
    )(glu, glu, w_dw, cg, cb, w_bf, x, gate, g, b)


def _sc_out(ch, bgate, w_dw, w_bf, x, gate, g, b):
    s = x.shape[0]
    tm = TM_OUT
    full = lambda shape: pl.BlockSpec(shape, lambda i: (0, 0))
    return pl.pallas_call(
        _sc_out_kernel,
        grid=(s // tm,),
        in_specs=[_row_spec(tm, D_MODEL), _halo_spec(tm, HALO_SC), _row_spec(tm, D_MODEL),
                  full((SHORT_CONV_KERNEL, D_MODEL)), full((D_MODEL, D_MODEL)),
                  _row_spec(tm, D_MODEL), _vec_spec(D_MODEL), _vec_spec(D_MODEL), _vec_spec(D_MODEL)],
        out_specs=_row_spec(tm, D_MODEL),
        out_shape=jax.ShapeDtypeStruct((s, D_MODEL), F32),
        scratch_shapes=[pltpu.VMEM((tm + HALO_SC, D_MODEL), F32), pltpu.VMEM((tm, D_MODEL), F32)],
        compiler_params=_cparams(("arbitrary",)),
        name="sc_out",
    )(ch, ch, bgate, w_dw, w_bf, x, gate, g, b)


def _argmax4_first(v):
    i01 = jnp.where(v[1] > v[0], 1, 0)
    m01 = jnp.maximum(v[0], v[1])
    i23 = jnp.where(v[3] > v[2], 3, 2)
    m23 = jnp.maximum(v[2], v[3])
    return jnp.where(m23 > m01, i23, i01), jnp.maximum(m01, m23)


def _select4(idx, v):
    return jnp.where(idx == 0, v[0], jnp.where(idx == 1, v[1], jnp.where(idx == 2, v[2], v[3])))


def _router_kernel(x_ref, sc_ref, sh_ref, wt_ref, rb_ref, e_ref, gt_ref, rk_ref, cnt_ref, carry_scr):
    i = pl.program_id(0)
    tm = x_ref.shape[0]

    @pl.when(i == 0)
    def _():
        carry_scr[...] = jnp.zeros_like(carry_scr)

    u = x_ref[...] * (1.0 + sc_ref[...]) + sh_ref[...]
    logits = lax.dot_general(wt_ref[...], u, (((1,), (1,)), ((), ())), precision=lax.Precision.HIGHEST,
                             preferred_element_type=F32) + rb_ref[...]
    ex = jnp.exp(logits - jnp.max(logits, axis=0, keepdims=True))
    probs = ex / jnp.sum(ex, axis=0, keepdims=True)

    top1_i, top1_v, top2_i, top2_v, score = [], [], [], [], []
    for grp in range(N_EXPERT_GROUPS):
        p = [probs[grp * EXPERTS_PER_GROUP + j:grp * EXPERTS_PER_GROUP + j + 1, :]
             for j in range(EXPERTS_PER_GROUP)]
        i1, v1 = _argmax4_first(p)
        rest = [jnp.where(i1 == j, -1.0, p[j]) for j in range(EXPERTS_PER_GROUP)]
        i2, v2 = _argmax4_first(rest)
        top1_i.append(i1)
        top1_v.append(v1)
        top2_i.append(i2)
        top2_v.append(v2)
        score.append(v1 + v2)
    gsel, _ = _argmax4_first(score)
    p1 = _select4(gsel, top1_v)
    p2 = _select4(gsel, top2_v)
    e1 = gsel * EXPERTS_PER_GROUP + _select4(gsel, top1_i)
    e2 = gsel * EXPERTS_PER_GROUP + _select4(gsel, top2_i)
    psum = p1 + p2
    e_ref[0:1, :] = e1
    e_ref[1:2, :] = e2
    gt_ref[0:1, :] = p1 / psum
    gt_ref[1:2, :] = p2 / psum

    eid = lax.broadcasted_iota(jnp.int32, (N_EXPERTS, tm), 0)
    earlier = (lax.broadcasted_iota(jnp.int32, (tm, tm), 0)
               < lax.broadcasted_iota(jnp.int32, (tm, tm), 1)).astype(BF16)
    base = carry_scr[:, 0:1]
    for slot, e_sel in enumerate((e1, e2)):
        onehot = (eid == e_sel).astype(F32)
        before = jnp.dot(onehot.astype(BF16), earlier, preferred_element_type=F32)
        rank = jnp.sum(onehot * (before + base), axis=0, keepdims=True)
        rk_ref[slot:slot + 1, :] = rank.astype(jnp.int32)
        base = base + jnp.sum(onehot, axis=1, keepdims=True)
    carry_scr[...] = jnp.broadcast_to(base, carry_scr.shape)
    cnt_ref[...] = carry_scr[...]


def _router(x, sc, sh, rw_t, rb_col):
    s = x.shape[0]
    tm = TM_ROUTE
    slot_spec = pl.BlockSpec((TOP_K, tm), lambda i: (0, i))
    full = lambda shape: pl.BlockSpec(shape, lambda i: (0, 0))
    return pl.pallas_call(
        _router_kernel,
        grid=(s // tm,),
        in_specs=[_row_spec(tm, D_MODEL), _vec_spec(D_MODEL), _vec_spec(D_MODEL),
                  full((N_EXPERTS, D_MODEL)), full((N_EXPERTS, 1))],
        out_specs=[slot_spec, slot_spec, slot_spec, full((N_EXPERTS, LANES_V7X))],
        out_shape=[jax.ShapeDtypeStruct((TOP_K, s), jnp.int32), jax.ShapeDtypeStruct((TOP_K, s), F32),
                   jax.ShapeDtypeStruct((TOP_K, s), jnp.int32),
                   jax.ShapeDtypeStruct((N_EXPERTS, LANES_V7X), F32)],
        scratch_shapes=[pltpu.VMEM((N_EXPERTS, LANES_V7X), F32)],
        compiler_params=_cparams(("arbitrary",)),
        name="router",
    )(x, sc, sh, rw_t, rb_col)


def _row_gather(src_hbm, idx_ref, base, n, dst, sem):
    for r in range(n):
        tok = idx_ref[base + r]
        pltpu.make_async_copy(src_hbm.at[pl.ds(tok, 1), :], dst.at[pl.ds(r, 1), :], sem).start()


def _row_gather_wait(src_hbm, n, dst, sem):
    pltpu.make_async_copy(src_hbm.at[pl.ds(0, n), :], dst, sem).wait()


def _ffn_kernel(rt_ref, be_ref, nbu_ref, x_hbm, sc_ref, sh_ref, wg_ref, wu_ref, wd_ref, o_ref, xbuf, sem):
    del be_ref
    i = pl.program_id(0)
    nbu = nbu_ref[0]
    tm = xbuf.shape[1]
    slot = i % 2

    @pl.when(i == 0)
    def _():
        _row_gather(x_hbm, rt_ref, 0, tm, xbuf.at[0], sem.at[0])

    @pl.when(i < nbu)
    def _():
        _row_gather_wait(x_hbm, tm, xbuf.at[slot], sem.at[slot])
        _row_gather(x_hbm, rt_ref, (i + 1) * tm, tm, xbuf.at[1 - slot], sem.at[1 - slot])
        u = (xbuf[slot] * (1.0 + sc_ref[...]) + sh_ref[...]).astype(BF16)
        gt = jnp.dot(u, wg_ref[...], preferred_element_type=F32)
        up = jnp.dot(u, wu_ref[...], preferred_element_type=F32)
        h = (gt * jax.nn.sigmoid(gt) * up).astype(BF16)
        o_ref[...] = jnp.dot(h, wd_ref[...], preferred_element_type=F32)

    @pl.when(i >= nbu)
    def _():
        @pl.when(i == nbu)
        def _():
            _row_gather_wait(x_hbm, tm, xbuf.at[slot], sem.at[slot])

        o_ref[...] = jnp.zeros_like(o_ref)


def _moe_ffn(x, sc, sh, wg_bf, wu_bf, wd_bf, layer, row_token, block_expert, nb_used):
    tm = TM_MOE
    p_rows = row_token.shape[0]
    nb = p_rows // tm

    def last_used(i, nbu):
        return jnp.minimum(i, nbu[0] - 1)

    w_in = pl.BlockSpec((None, None, D_MODEL, D_EXPERT),
                        lambda i, rt, be, nbu: (layer, be[last_used(i, nbu)], 0, 0))
    w_out = pl.BlockSpec((None, None, D_EXPERT, D_MODEL),
                         lambda i, rt, be, nbu: (layer, be[last_used(i, nbu)], 0, 0))
    vec = pl.BlockSpec((1, D_MODEL), lambda i, rt, be, nbu: (0, 0))
    return pl.pallas_call(
        _ffn_kernel,
        grid_spec=pltpu.PrefetchScalarGridSpec(
            num_scalar_prefetch=3,
            grid=(nb,),
            in_specs=[pl.BlockSpec(memory_space=pl.ANY), vec, vec, w_in, w_in, w_out],
            out_specs=pl.BlockSpec((tm, D_MODEL), lambda i, rt, be, nbu: (i, 0)),
            scratch_shapes=[pltpu.VMEM((2, tm, D_MODEL), F32), pltpu.SemaphoreType.DMA((2,))]),
        out_shape=jax.ShapeDtypeStruct((p_rows, D_MODEL), F32),
        compiler_params=_cparams(("arbitrary",)),
        name="moe_ffn",
    )(row_token, block_expert, nb_used, x, sc, sh, wg_bf, wu_bf, wd_bf)


def _combine_kernel(dest_ref, rows_hbm, gt_ref, x_ref, gate_ref, g_ref, b_ref, out_ref, ybuf, sem):
    i = pl.program_id(0)
    n = pl.num_programs(0)
    tm = x_ref.shape[0]
    s = tm * n
    slot = i % 2

    def start(tile, to):
        for k in range(TOP_K):
            _row_gather(rows_hbm, dest_ref, k * s + tile * tm, tm, ybuf.at[to, k], sem.at[to, k])

    def wait(at):
        for k in range(TOP_K):
            _row_gather_wait(rows_hbm, tm, ybuf.at[at, k], sem.at[at, k])

    @pl.when(i == 0)
    def _():
        start(0, 0)

    wait(slot)
    start(jnp.minimum(i + 1, n - 1), 1 - slot)
    gt = gt_ref[...]
    y = gt[:, 0:1] * ybuf[slot, 0] + gt[:, 1:2] * ybuf[slot, 1]
    out_ref[...] = _deepnorm(x_ref[...], y, gate_ref[...], g_ref[...], b_ref[...])

    @pl.when(i == n - 1)
    def _():
        wait(1 - slot)


def _moe_combine(dest_flat, out_rows, gates_t, x, gate, g, b):
    s = x.shape[0]
    tm = TM_COMB
    vec = pl.BlockSpec((1, D_MODEL), lambda i, dst: (0, 0))
    return pl.pallas_call(
        _combine_kernel,
        grid_spec=pltpu.PrefetchScalarGridSpec(
            num_scalar_prefetch=1,
            grid=(s // tm,),
            in_specs=[pl.BlockSpec(memory_space=pl.ANY),
                      pl.BlockSpec((tm, TOP_K), lambda i, dst: (i, 0)),
                      pl.BlockSpec((tm, D_MODEL), lambda i, dst: (i, 0)), vec, vec, vec],
            out_specs=pl.BlockSpec((tm, D_MODEL), lambda i, dst: (i, 0)),
            scratch_shapes=[pltpu.VMEM((2, TOP_K, tm, D_MODEL), F32), pltpu.SemaphoreType.DMA((2, TOP_K))]),
        out_shape=jax.ShapeDtypeStruct((s, D_MODEL), F32),
        compiler_params=_cparams(("arbitrary",)),
        name="moe_combine",
    )(dest_flat, out_rows, gates_t, x, gate, g, b)


def _moe_layer(x, sc, sh, gate, g, b, rw_t, rb_col, wg_bf, wu_bf, wd_bf, layer):
    s = x.shape[0]
    tm = TM_MOE
    expert, gates, rank, cnt = _router(x, sc, sh, rw_t, rb_col)
    counts = cnt[:, 0].astype(jnp.int32)
    padded = (counts + tm - 1) // tm * tm
    pad_end = jnp.cumsum(padded)
    pad_start = pad_end - padded
    eids = jnp.arange(N_EXPERTS, dtype=jnp.int32)
    start_of = jnp.sum(jnp.where(expert[..., None] == eids, pad_start, 0), axis=-1)
    dest = (start_of + rank).reshape(-1)
    p_rows = TOP_K * s + N_EXPERTS * tm
    token = jnp.tile(jnp.arange(s, dtype=jnp.int32), TOP_K)
    row_token = jnp.zeros((p_rows,), jnp.int32).at[dest].set(token)
    nb = p_rows // tm
    block_row0 = jnp.arange(nb, dtype=jnp.int32) * tm
    block_expert = jnp.minimum(jnp.sum(block_row0[:, None] >= pad_end[None, :], axis=-1),
                               N_EXPERTS - 1).astype(jnp.int32)
    nb_used = (pad_end[-1:] // tm).astype(jnp.int32)
    out_rows = _moe_ffn(x, sc, sh, wg_bf, wu_bf, wd_bf, layer, row_token, block_expert, nb_used)
    return _moe_combine(dest.astype(jnp.int32), out_rows, gates.T, x, gate, g, b)


def kernel(x, c, positions, ada_w, ada_b, ln_g, ln_b, attn_w_qkv, attn_w_o, conv_w_pw1, conv_w_dw, conv_ln_g,
           conv_ln_b, conv_w_pw2, sc_w_in, sc_w_conv, sc_w_out, router_w, router_b, moe_w_gate, moe_w_up,
           moe_w_down):
    batch, s, d = x.shape
    assert batch == 1 and d == D_MODEL and s % (DILATIONS[-1] * Q_BLOCK) == 0
    xs = x.reshape(s, d)

    mod = _adaln(c.reshape(d, 1), ada_w, ada_b)
    half = HEAD_DIM // 2
    inv_freq = ROPE_THETA ** (-jnp.arange(half, dtype=F32) / half)
    freq_row = jnp.concatenate([inv_freq, inv_freq]).reshape(1, HEAD_DIM)
    cos, sin = _rope_tables(positions.reshape(s, 1), freq_row)

    rw_t = router_w.T
    rb_col = router_b.reshape(N_EXPERTS, 1)
    wg_bf = moe_w_gate.astype(BF16)
    wu_bf = moe_w_up.astype(BF16)
    wd_bf = moe_w_down.astype(BF16)
    vec = lambda a: a.reshape(1, d)

    for i in range(DEPTH):
        sh1, sc1, g1, sh2, sc2, g2 = [mod[i, :, k * d:(k + 1) * d] for k in range(6)]
        m, j = i % N_MIXERS, i // N_MIXERS
        lg, lb = vec(ln_g[i, 0]), vec(ln_b[i, 0])
        if m == 0:
            w_qkv = attn_w_qkv[j].astype(BF16)
            outs, lses = [], []
            for grp in range(len(DILATIONS)):
                o_g, l_g = _attention(_qkv_proj(xs, sc1, sh1, w_qkv, cos, sin, grp), grp)
                outs.append(o_g)
                lses.append(l_g)
            xs = _attn_out(outs, lses, attn_w_o[j].astype(BF16), xs, g1, lg, lb)
        elif m == 1:
            glu = _gated_in(_glu_in_kernel, 2, 1, xs, sc1, sh1, conv_w_pw1[j].astype(BF16), "conv_in")
            xs = _conv_out(glu, conv_w_dw[j], vec(conv_ln_g[j]), vec(conv_ln_b[j]),
                           conv_w_pw2[j].astype(BF16), xs, g1, lg, lb)
        else:
            bgate, ch = _gated_in(_sc_in_kernel, 3, 2, xs, sc1, sh1, sc_w_in[j].astype(BF16), "sc_in")
            xs = _sc_out(ch, bgate, sc_w_conv[j], sc_w_out[j].astype(BF16), xs, g1, lg, lb)
        xs = _moe_layer(xs, sc2, sh2, g2, vec(ln_g[i, 1]), vec(ln_b[i, 1]), rw_t, rb_col,
                        wg_bf, wu_bf, wd_bf, i)
    return xs.reshape(batch, s, d)
```

```python
import functools
import math

import jax
import jax.numpy as jnp
from jax import lax
from jax.experimental import pallas as pl
from jax.experimental.pallas import tpu as pltpu

F32 = jnp.float32
BF16 = jnp.bfloat16

D_MODEL = 2048
DEPTH = 4
N_MIXERS = 3
ATTN_HEADS = 16
HEAD_DIM = 128
HD = ATTN_HEADS * HEAD_DIM
DILATIONS = (1, 4, 16)
Q_BLOCK = 128
ROPE_THETA = 10000.0
CONV_KERNEL = 31
SHORT_CONV_KERNEL = 3
N_EXPERTS = 16
N_EXPERT_GROUPS = 4
EXPERTS_PER_GROUP = 4
TOP_K = 2
D_EXPERT = 1408
ALPHA = (2 * DEPTH) ** 0.25
LN_EPS = 1e-5
NEG_INF = -1e30

LANES_V7X = 128
MXU_COLS_V7X = 256
VMEM_LIMIT_V7X = 56 * 1024 * 1024
TM_PROJ = 512
TN_PROJ = 1024
TN_GLU = 512
TM_OUT = 256
TM_ROUTE = 512
TM_MOE = 256
TM_COMB = 256
TN_ADA = 1024
HALO_CONV = 32
HALO_SC = 8
CONV_ROWS = 32
CONV_COLS = 512


def _cparams(sem):
    return pltpu.CompilerParams(dimension_semantics=sem, vmem_limit_bytes=VMEM_LIMIT_V7X)


def _ln_rows(z, g, b):
    mu = jnp.mean(z, axis=-1, keepdims=True)
    zc = z - mu
    var = jnp.mean(zc * zc, axis=-1, keepdims=True)
    return zc * lax.rsqrt(var + LN_EPS) * g + b


def _deepnorm(x, y, gate, g, b):
    return _ln_rows(ALPHA * x + (1.0 + gate) * y, g, b)


def _row_spec(tm, width):
    return pl.BlockSpec((tm, width), lambda i: (i, 0))


def _vec_spec(width):
    return pl.BlockSpec((1, width), lambda i: (0, 0))


def _adaln_kernel(c_ref, w_ref, b_ref, o_ref):
    c = c_ref[...]
    ca = c * jax.nn.sigmoid(c)
    o_ref[...] = jnp.sum(w_ref[...] * ca, axis=0, keepdims=True) + b_ref[...]


def _adaln(c_col, ada_w, ada_b):
    depth, d, n = ada_w.shape
    return pl.pallas_call(
        _adaln_kernel,
        grid=(depth, n // TN_ADA),
        in_specs=[pl.BlockSpec((d, 1), lambda l, j: (0, 0)),
                  pl.BlockSpec((None, d, TN_ADA), lambda l, j: (l, 0, j)),
                  pl.BlockSpec((None, 1, TN_ADA), lambda l, j: (l, 0, j))],
        out_specs=pl.BlockSpec((None, 1, TN_ADA), lambda l, j: (l, 0, j)),
        out_shape=jax.ShapeDtypeStruct((depth, 1, n), F32),
        compiler_params=_cparams(("arbitrary", "arbitrary")),
        name="adaln",
    )(c_col, ada_w, ada_b.reshape(depth, 1, n))


def _rope_kernel(pos_ref, freq_ref, cos_ref, sin_ref):
    ang = pos_ref[...].astype(F32) * freq_ref[...]
    lane = lax.broadcasted_iota(jnp.int32, ang.shape, 1)
    s = jnp.sin(ang)
    cos_ref[...] = jnp.cos(ang)
    sin_ref[...] = jnp.where(lane < HEAD_DIM // 2, -s, s)


def _rope_tables(pos_col, freq_row):
    s = pos_col.shape[0]
    tm = 1024
    spec = pl.BlockSpec((tm, HEAD_DIM), lambda i: (i, 0))
    return pl.pallas_call(
        _rope_kernel,
        grid=(s // tm,),
        in_specs=[pl.BlockSpec((tm, 1), lambda i: (i, 0)), _vec_spec(HEAD_DIM)],
        out_specs=[spec, spec],
        out_shape=[jax.ShapeDtypeStruct((s, HEAD_DIM), F32)] * 2,
        compiler_params=_cparams(("arbitrary",)),
        name="rope_tables",
    )(pos_col, freq_row)


def _qkv_kernel(x_ref, sc_ref, sh_ref, w_ref, cos_ref, sin_ref, o_ref, u_scr, acc_scr, *, d, n_rope):
    j = pl.program_id(1)
    nh, tm, _ = acc_scr.shape

    @pl.when(j == 0)
    def _():
        u_scr[...] = (x_ref[...] * (1.0 + sc_ref[...]) + sh_ref[...]).astype(BF16)

    is_rope = j < n_rope
    cos = cos_ref[...]
    sin = sin_ref[...]
    u = u_scr[...]
    for c in range(nh // 2):
        acc = jnp.dot(u, w_ref[:, c * MXU_COLS_V7X:(c + 1) * MXU_COLS_V7X], preferred_element_type=F32)
        for h in (2 * c, 2 * c + 1):
            a = acc[:, (h - 2 * c) * HEAD_DIM:(h - 2 * c + 1) * HEAD_DIM]
            a = jnp.where(is_rope, a * cos + pltpu.roll(a, HEAD_DIM // 2, 1) * sin, a)
            hs = slice(h * HEAD_DIM, (h + 1) * HEAD_DIM)
            if d == 1:
                o_ref[0, :, hs] = a.astype(BF16)
            else:
                acc_scr[h] = a
                for r in range(d):
                    o_ref[r, :, hs] = acc_scr[h, pl.ds(r, tm // d, stride=d), :].astype(BF16)


def _qkv_proj(x, sc, sh, w_bf, cos, sin, group):
    s = x.shape[0]
    d = DILATIONS[group]
    tm, tn = TM_PROJ, TN_PROJ
    ncol = 3 * HD
    col0 = group * (ncol // tn)
    return pl.pallas_call(
        functools.partial(_qkv_kernel, d=d, n_rope=2 * HD // tn),
        grid=(s // tm, ncol // tn),
        in_specs=[pl.BlockSpec((tm, D_MODEL), lambda i, j: (i, 0)),
                  pl.BlockSpec((1, D_MODEL), lambda i, j: (0, 0)),
                  pl.BlockSpec((1, D_MODEL), lambda i, j: (0, 0)),
                  pl.BlockSpec((D_MODEL, tn), lambda i, j: (0, col0 + j)),
                  pl.BlockSpec((tm, HEAD_DIM), lambda i, j: (i, 0)),
                  pl.BlockSpec((tm, HEAD_DIM), lambda i, j: (i, 0))],
        out_specs=pl.BlockSpec((d, tm // d, tn), lambda i, j: (0, i, j)),
        out_shape=jax.ShapeDtypeStruct((d, s // d, ncol), BF16),
        scratch_shapes=[pltpu.VMEM((tm, D_MODEL), BF16), pltpu.VMEM((tn // HEAD_DIM, tm, HEAD_DIM), F32)],
        compiler_params=_cparams(("arbitrary", "arbitrary")),
        name=f"qkv_proj_g{group}",
    )(x, sc, sh, w_bf, cos, sin)


def _attn_kernel(q_ref, kc_ref, vc_ref, kp_ref, vp_ref, o_ref, lse_ref, k_scr, v_scr, s_scr, p_scr, *, nblk):
    b = pl.program_id(0)
    not_first = (b % nblk) != 0
    k_scr[0:Q_BLOCK] = kp_ref[...]
    k_scr[Q_BLOCK:] = kc_ref[...]
    v_scr[0:Q_BLOCK] = vp_ref[...]
    v_scr[Q_BLOCK:] = vc_ref[...]
    nt = (((1,), (1,)), ((), ()))
    for h in range(ATTN_HEADS):
        sl = slice(h * HEAD_DIM, (h + 1) * HEAD_DIM)
        s_scr[h] = lax.dot_general(q_ref[:, sl], k_scr[:, sl], nt, preferred_element_type=F32)

    row = lax.broadcasted_iota(jnp.int32, (Q_BLOCK, 2 * Q_BLOCK), 0)
    col = lax.broadcasted_iota(jnp.int32, (Q_BLOCK, 2 * Q_BLOCK), 1)
    prev_ok = jnp.logical_and(jnp.logical_and(col < Q_BLOCK, col >= row), not_first)
    valid = jnp.logical_or(prev_ok, jnp.logical_and(col >= Q_BLOCK, col - Q_BLOCK <= row))
    scale = 1.0 / math.sqrt(HEAD_DIM)
    s = jnp.where(valid[None], s_scr[...], NEG_INF)
    m = jnp.max(s, axis=2, keepdims=True)
    p = jnp.exp2((s - m) * (scale * math.log2(math.e)))
    den = jnp.sum(p, axis=2, keepdims=True)
    p_scr[...] = p.astype(BF16)
    inv = 1.0 / den
    lse = m * scale + jnp.log(den)
    for h in range(ATTN_HEADS):
        sl = slice(h * HEAD_DIM, (h + 1) * HEAD_DIM)
        o = jnp.dot(p_scr[h], v_scr[:, sl], preferred_element_type=F32)
        o_ref[:, sl] = (o * inv[h]).astype(BF16)
        lse_ref[:, h:h + 1] = lse[h]


def _attention(qkv, group):
    d, l, ncol = qkv.shape
    s = d * l
    flat = qkv.reshape(s, ncol)
    nblk = l // Q_BLOCK
    cur = lambda c: pl.BlockSpec((Q_BLOCK, HD), lambda b: (b, c))
    prev = lambda c: pl.BlockSpec((Q_BLOCK, HD), lambda b: (jnp.maximum(b - 1, 0), c))
    return pl.pallas_call(
        functools.partial(_attn_kernel, nblk=nblk),
        grid=(s // Q_BLOCK,),
        in_specs=[cur(0), cur(1), cur(2), prev(1), prev(2)],
        out_specs=[pl.BlockSpec((Q_BLOCK, HD), lambda b: (b, 0)),
                   pl.BlockSpec((Q_BLOCK, ATTN_HEADS), lambda b: (b, 0))],
        out_shape=[jax.ShapeDtypeStruct((s, HD), BF16), jax.ShapeDtypeStruct((s, ATTN_HEADS), F32)],
        scratch_shapes=[pltpu.VMEM((2 * Q_BLOCK, HD), BF16), pltpu.VMEM((2 * Q_BLOCK, HD), BF16),
                        pltpu.VMEM((ATTN_HEADS, Q_BLOCK, 2 * Q_BLOCK), F32),
                        pltpu.VMEM((ATTN_HEADS, Q_BLOCK, 2 * Q_BLOCK), BF16)],
        compiler_params=_cparams(("arbitrary",)),
        name=f"dil_attn_g{group}",
    )(flat, flat, flat, flat, flat)


def _attn_out_kernel(o0_ref, o1_ref, o2_ref, l0_ref, l1_ref, l2_ref, w_ref, x_ref, gate_ref, g_ref, b_ref,
                     out_ref, o_scr, a_scr):
    tm = x_ref.shape[0]
    for gi, o_ref in enumerate((o1_ref, o2_ref)):
        d = DILATIONS[gi + 1]
        for r in range(d):
            for h in range(ATTN_HEADS):
                o_scr[gi, h, pl.ds(r, tm // d, stride=d), :] = (
                    o_ref[r, :, h * HEAD_DIM:(h + 1) * HEAD_DIM].astype(F32))
    l0 = l0_ref[...]
    l1 = l1_ref[...]
    l2 = l2_ref[...]
    m = jnp.maximum(jnp.maximum(l0, l1), l2)
    e0 = jnp.exp(l0 - m)
    e1 = jnp.exp(l1 - m)
    e2 = jnp.exp(l2 - m)
    z = e0 + e1 + e2
    w0 = e0 / z
    w1 = e1 / z
    w2 = e2 / z
    for h in range(ATTN_HEADS):
        sl = slice(h * HEAD_DIM, (h + 1) * HEAD_DIM)
        a = (w0[:, h:h + 1] * o0_ref[:, sl].astype(F32)
             + w1[:, h:h + 1] * o_scr[0, h]
             + w2[:, h:h + 1] * o_scr[1, h])
        a_scr[:, sl] = a.astype(BF16)
    y = jnp.dot(a_scr[...], w_ref[...], preferred_element_type=F32)
    out_ref[...] = _deepnorm(x_ref[...], y, gate_ref[...], g_ref[...], b_ref[...])


def _attn_out(outs, lses, w_bf, x, gate, g, b):
    s = x.shape[0]
    tm = TM_OUT
    d1, d2 = DILATIONS[1], DILATIONS[2]
    o1 = outs[1].reshape(d1, s // d1, HD)
    o2 = outs[2].reshape(d2, s // d2, HD)
    l1 = lses[1].reshape(d1, s // d1, ATTN_HEADS).transpose(1, 0, 2).reshape(s, ATTN_HEADS)
    l2 = lses[2].reshape(d2, s // d2, ATTN_HEADS).transpose(1, 0, 2).reshape(s, ATTN_HEADS)
    perm = lambda d, w: pl.BlockSpec((d, tm // d, w), lambda i: (0, i, 0))
    return pl.pallas_call(
        _attn_out_kernel,
        grid=(s // tm,),
        in_specs=[_row_spec(tm, HD), perm(d1, HD), perm(d2, HD),
                  _row_spec(tm, ATTN_HEADS), _row_spec(tm, ATTN_HEADS), _row_spec(tm, ATTN_HEADS),
                  pl.BlockSpec((HD, D_MODEL), lambda i: (0, 0)),
                  _row_spec(tm, D_MODEL), _vec_spec(D_MODEL), _vec_spec(D_MODEL), _vec_spec(D_MODEL)],
        out_specs=_row_spec(tm, D_MODEL),
        out_shape=jax.ShapeDtypeStruct((s, D_MODEL), F32),
        scratch_shapes=[pltpu.VMEM((2, ATTN_HEADS, tm, HEAD_DIM), F32), pltpu.VMEM((tm, HD), BF16)],
        compiler_params=_cparams(("arbitrary",)),
        name="attn_out",
    )(outs[0], o1, o2, lses[0], l1, l2, w_bf, x, gate, g, b)


def _modulate_once(x_ref, sc_ref, sh_ref, u_scr):
    @pl.when(pl.program_id(1) == 0)
    def _():
        u_scr[...] = (x_ref[...] * (1.0 + sc_ref[...]) + sh_ref[...]).astype(BF16)


def _glu_in_kernel(x_ref, sc_ref, sh_ref, wa_ref, wg_ref, o_ref, u_scr):
    _modulate_once(x_ref, sc_ref, sh_ref, u_scr)
    u = u_scr[...]
    a = jnp.dot(u, wa_ref[...], preferred_element_type=F32)
    gt = jnp.dot(u, wg_ref[...], preferred_element_type=F32)
    o_ref[...] = a * jax.nn.sigmoid(gt)


def _sc_in_kernel(x_ref, sc_ref, sh_ref, wb_ref, wc_ref, wh_ref, b_ref, ch_ref, u_scr):
    _modulate_once(x_ref, sc_ref, sh_ref, u_scr)
    u = u_scr[...]
    b_ref[...] = jnp.dot(u, wb_ref[...], preferred_element_type=F32)
    c = jnp.dot(u, wc_ref[...], preferred_element_type=F32)
    h = jnp.dot(u, wh_ref[...], preferred_element_type=F32)
    ch_ref[...] = c * h


def _gated_in(kernel, n_parts, n_out, x, sc, sh, w_bf, name):
    s = x.shape[0]
    tm, tn = TM_PROJ, TN_GLU
    nj = D_MODEL // tn
    w_specs = [pl.BlockSpec((D_MODEL, tn), functools.partial(lambda i, j, p: (0, p * nj + j), p=p))
               for p in range(n_parts)]
    out_spec = pl.BlockSpec((tm, tn), lambda i, j: (i, j))
    out_shape = jax.ShapeDtypeStruct((s, D_MODEL), F32)
    return pl.pallas_call(
        kernel,
        grid=(s // tm, nj),
        in_specs=[pl.BlockSpec((tm, D_MODEL), lambda i, j: (i, 0)),
                  pl.BlockSpec((1, D_MODEL), lambda i, j: (0, 0)),
                  pl.BlockSpec((1, D_MODEL), lambda i, j: (0, 0))] + w_specs,
        out_specs=[out_spec] * n_out if n_out > 1 else out_spec,
        out_shape=[out_shape] * n_out if n_out > 1 else out_shape,
        scratch_shapes=[pltpu.VMEM((tm, D_MODEL), BF16)],
        compiler_params=_cparams(("arbitrary", "arbitrary")),
        name=name,
    )(x, sc, sh, *([w_bf] * n_parts))


def _fill_halo(hal_scr, prev_ref, cur_ref, halo):
    is_first = pl.program_id(0) == 0
    hal_scr[pl.ds(0, halo), :] = jnp.where(is_first, 0.0, prev_ref[...])
    hal_scr[pl.ds(halo, cur_ref.shape[0]), :] = cur_ref[...]


def _dwconv(hal_scr, w_ref, dst_scr, halo, ksize, tm):
    off = halo - (ksize - 1)

    for r0 in range(0, tm, CONV_ROWS):
        for c0 in range(0, D_MODEL, CONV_COLS):
            cs = slice(c0, c0 + CONV_COLS)
            acc = jnp.zeros((CONV_ROWS, CONV_COLS), F32)
            for k in range(ksize):
                acc = acc + w_ref[k:k + 1, cs] * hal_scr[r0 + off + k:r0 + off + k + CONV_ROWS, cs]
            dst_scr[r0:r0 + CONV_ROWS, cs] = acc


def _conv_out_kernel(cur_ref, prev_ref, wdw_ref, cg_ref, cb_ref, w_ref, x_ref, gate_ref, g_ref, b_ref,
                     out_ref, hal_scr, h_scr):
    tm = x_ref.shape[0]
    _fill_halo(hal_scr, prev_ref, cur_ref, HALO_CONV)
    _dwconv(hal_scr, wdw_ref, h_scr, HALO_CONV, CONV_KERNEL, tm)
    hn = _ln_rows(h_scr[...], cg_ref[...], cb_ref[...])
    a = (hn * jax.nn.sigmoid(hn)).astype(BF16)
    y = jnp.dot(a, w_ref[...], preferred_element_type=F32)
    out_ref[...] = _deepnorm(x_ref[...], y, gate_ref[...], g_ref[...], b_ref[...])


def _sc_out_kernel(cur_ref, prev_ref, bg_ref, wdw_ref, w_ref, x_ref, gate_ref, g_ref, b_ref,
                   out_ref, hal_scr, h_scr):
    tm = x_ref.shape[0]
    _fill_halo(hal_scr, prev_ref, cur_ref, HALO_SC)
    _dwconv(hal_scr, wdw_ref, h_scr, HALO_SC, SHORT_CONV_KERNEL, tm)
    a = (bg_ref[...] * h_scr[...]).astype(BF16)
    y = jnp.dot(a, w_ref[...], preferred_element_type=F32)
    out_ref[...] = _deepnorm(x_ref[...], y, gate_ref[...], g_ref[...], b_ref[...])


def _halo_spec(tm, halo):
    per = tm // halo
    return pl.BlockSpec((halo, D_MODEL), lambda i: (jnp.maximum(i * per - 1, 0), 0))


def _conv_out(glu, w_dw, cg, cb, w_bf, x, gate, g, b):
    s = x.shape[0]
    tm = TM_OUT
    full = lambda shape: pl.BlockSpec(shape, lambda i: (0, 0))
    return pl.pallas_call(
        _conv_out_kernel,
        grid=(s // tm,),
        in_specs=[_row_spec(tm, D_MODEL), _halo_spec(tm, HALO_CONV), full((CONV_KERNEL, D_MODEL)),
                  _vec_spec(D_MODEL), _vec_spec(D_MODEL), full((D_MODEL, D_MODEL)),
                  _row_spec(tm, D_MODEL), _vec_spec(D_MODEL), _vec_spec(D_MODEL), _vec_spec(D_MODEL)],
        out_specs=_row_spec(tm, D_MODEL),
        out_shape=jax.ShapeDtypeStruct((s, D_MODEL), F32),
        scratch_shapes=[pltpu.VMEM((tm + HALO_CONV, D_MODEL), F32), pltpu.VMEM((tm, D_MODEL), F32)],
        compiler_params=_cparams(("arbitrary",)),
        name="conv_out",
    )(glu, glu, w_dw, cg, cb, w_bf, x, gate, g, b)


def _sc_out(ch, bgate, w_dw, w_bf, x, gate, g, b):
    s = x.shape[0]
    tm = TM_OUT
    full = lambda shape: pl.BlockSpec(shape, lambda i: (0, 0))
    return pl.pallas_call(
        _sc_out_kernel,
        grid=(s // tm,),
        in_specs=[_row_spec(tm, D_MODEL), _halo_spec(tm, HALO_SC), _row_spec(tm, D_MODEL),
                  full((SHORT_CONV_KERNEL, D_MODEL)), full((D_MODEL, D_MODEL)),
                  _row_spec(tm, D_MODEL), _vec_spec(D_MODEL), _vec_spec(D_MODEL), _vec_spec(D_MODEL)],
        out_specs=_row_spec(tm, D_MODEL),
        out_shape=jax.ShapeDtypeStruct((s, D_MODEL), F32),
        scratch_shapes=[pltpu.VMEM((tm + HALO_SC, D_MODEL), F32), pltpu.VMEM((tm, D_MODEL), F32)],
        compiler_params=_cparams(("arbitrary",)),
        name="sc_out",
    )(ch, ch, bgate, w_dw, w_bf, x, gate, g, b)


def _argmax4_first(v):
    i01 = jnp.where(v[1] > v[0], 1, 0)
    m01 = jnp.maximum(v[0], v[1])
    i23 = jnp.where(v[3] > v[2], 3, 2)
    m23 = jnp.maximum(v[2], v[3])
    return jnp.where(m23 > m01, i23, i01), jnp.maximum(m01, m23)


def _select4(idx, v):
    return jnp.where(idx == 0, v[0], jnp.where(idx == 1, v[1], jnp.where(idx == 2, v[2], v[3])))


def _router_kernel(x_ref, sc_ref, sh_ref, wt_ref, rb_ref, e_ref, gt_ref, rk_ref, cnt_ref, carry_scr):
    i = pl.program_id(0)
    tm = x_ref.shape[0]

    @pl.when(i == 0)
    def _():
        carry_scr[...] = jnp.zeros_like(carry_scr)

    u = x_ref[...] * (1.0 + sc_ref[...]) + sh_ref[...]
    logits = lax.dot_general(wt_ref[...], u, (((1,), (1,)), ((), ())), precision=lax.Precision.HIGHEST,
                             preferred_element_type=F32) + rb_ref[...]
    ex = jnp.exp(logits - jnp.max(logits, axis=0, keepdims=True))
    probs = ex / jnp.sum(ex, axis=0, keepdims=True)

    top1_i, top1_v, top2_i, top2_v, score = [], [], [], [], []
    for grp in range(N_EXPERT_GROUPS):
        p = [probs[grp * EXPERTS_PER_GROUP + j:grp * EXPERTS_PER_GROUP + j + 1, :]
             for j in range(EXPERTS_PER_GROUP)]
        i1, v1 = _argmax4_first(p)
        rest = [jnp.where(i1 == j, -1.0, p[j]) for j in range(EXPERTS_PER_GROUP)]
        i2, v2 = _argmax4_first(rest)
        top1_i.append(i1)
        top1_v.append(v1)
        top2_i.append(i2)
        top2_v.append(v2)
        score.append(v1 + v2)
    gsel, _ = _argmax4_first(score)
    p1 = _select4(gsel, top1_v)
    p2 = _select4(gsel, top2_v)
    e1 = gsel * EXPERTS_PER_GROUP + _select4(gsel, top1_i)
    e2 = gsel * EXPERTS_PER_GROUP + _select4(gsel, top2_i)
    psum = p1 + p2
    e_ref[0:1, :] = e1
    e_ref[1:2, :] = e2
    gt_ref[0:1, :] = p1 / psum
    gt_ref[1:2, :] = p2 / psum

    eid = lax.broadcasted_iota(jnp.int32, (N_EXPERTS, tm), 0)
    earlier = (lax.broadcasted_iota(jnp.int32, (tm, tm), 0)
               < lax.broadcasted_iota(jnp.int32, (tm, tm), 1)).astype(BF16)
    base = carry_scr[:, 0:1]
    for slot, e_sel in enumerate((e1, e2)):
        onehot = (eid == e_sel).astype(F32)
        before = jnp.dot(onehot.astype(BF16), earlier, preferred_element_type=F32)
        rank = jnp.sum(onehot * (before + base), axis=0, keepdims=True)
        rk_ref[slot:slot + 1, :] = rank.astype(jnp.int32)
        base = base + jnp.sum(onehot, axis=1, keepdims=True)
    carry_scr[...] = jnp.broadcast_to(base, carry_scr.shape)
    cnt_ref[...] = carry_scr[...]


def _router(x, sc, sh, rw_t, rb_col):
    s = x.shape[0]
    tm = TM_ROUTE
    slot_spec = pl.BlockSpec((TOP_K, tm), lambda i: (0, i))
    full = lambda shape: pl.BlockSpec(shape, lambda i: (0, 0))
    return pl.pallas_call(
        _router_kernel,
        grid=(s // tm,),
        in_specs=[_row_spec(tm, D_MODEL), _vec_spec(D_MODEL), _vec_spec(D_MODEL),
                  full((N_EXPERTS, D_MODEL)), full((N_EXPERTS, 1))],
        out_specs=[slot_spec, slot_spec, slot_spec, full((N_EXPERTS, LANES_V7X))],
        out_shape=[jax.ShapeDtypeStruct((TOP_K, s), jnp.int32), jax.ShapeDtypeStruct((TOP_K, s), F32),
                   jax.ShapeDtypeStruct((TOP_K, s), jnp.int32),
                   jax.ShapeDtypeStruct((N_EXPERTS, LANES_V7X), F32)],
        scratch_shapes=[pltpu.VMEM((N_EXPERTS, LANES_V7X), F32)],
        compiler_params=_cparams(("arbitrary",)),
        name="router",
    )(x, sc, sh, rw_t, rb_col)


def _row_gather(src_hbm, idx_ref, base, n, dst, sem):
    for r in range(n):
        tok = idx_ref[base + r]
        pltpu.make_async_copy(src_hbm.at[pl.ds(tok, 1), :], dst.at[pl.ds(r, 1), :], sem).start()


def _row_gather_wait(src_hbm, n, dst, sem):
    pltpu.make_async_copy(src_hbm.at[pl.ds(0, n), :], dst, sem).wait()


def _ffn_kernel(rt_ref, be_ref, nbu_ref, x_hbm, sc_ref, sh_ref, wg_ref, wu_ref, wd_ref, o_ref, xbuf, sem):
    del be_ref
    i = pl.program_id(0)
    nbu = nbu_ref[0]
    tm = xbuf.shape[1]
    slot = i % 2

    @pl.when(i == 0)
    def _():
        _row_gather(x_hbm, rt_ref, 0, tm, xbuf.at[0], sem.at[0])

    @pl.when(i < nbu)
    def _():
        _row_gather_wait(x_hbm, tm, xbuf.at[slot], sem.at[slot])
        _row_gather(x_hbm, rt_ref, (i + 1) * tm, tm, xbuf.at[1 - slot], sem.at[1 - slot])
        u = (xbuf[slot] * (1.0 + sc_ref[...]) + sh_ref[...]).astype(BF16)
        gt = jnp.dot(u, wg_ref[...], preferred_element_type=F32)
        up = jnp.dot(u, wu_ref[...], preferred_element_type=F32)
        h = (gt * jax.nn.sigmoid(gt) * up).astype(BF16)
        o_ref[...] = jnp.dot(h, wd_ref[...], preferred_element_type=F32)

    @pl.when(i >= nbu)
    def _():
        @pl.when(i == nbu)
        def _():
            _row_gather_wait(x_hbm, tm, xbuf.at[slot], sem.at[slot])

        o_ref[...] = jnp.zeros_like(o_ref)


def _moe_ffn(x, sc, sh, wg_bf, wu_bf, wd_bf, layer, row_token, block_expert, nb_used):
    tm = TM_MOE
    p_rows = row_token.shape[0]
    nb = p_rows // tm

    def last_used(i, nbu):
        return jnp.minimum(i, nbu[0] - 1)

    w_in = pl.BlockSpec((None, None, D_MODEL, D_EXPERT),
                        lambda i, rt, be, nbu: (layer, be[last_used(i, nbu)], 0, 0))
    w_out = pl.BlockSpec((None, None, D_EXPERT, D_MODEL),
                         lambda i, rt, be, nbu: (layer, be[last_used(i, nbu)], 0, 0))
    vec = pl.BlockSpec((1, D_MODEL), lambda i, rt, be, nbu: (0, 0))
    return pl.pallas_call(
        _ffn_kernel,
        grid_spec=pltpu.PrefetchScalarGridSpec(
            num_scalar_prefetch=3,
            grid=(nb,),
            in_specs=[pl.BlockSpec(memory_space=pl.ANY), vec, vec, w_in, w_in, w_out],
            out_specs=pl.BlockSpec((tm, D_MODEL), lambda i, rt, be, nbu: (i, 0)),
            scratch_shapes=[pltpu.VMEM((2, tm, D_MODEL), F32), pltpu.SemaphoreType.DMA((2,))]),
        out_shape=jax.ShapeDtypeStruct((p_rows, D_MODEL), F32),
        compiler_params=_cparams(("arbitrary",)),
        name="moe_ffn",
    )(row_token, block_expert, nb_used, x, sc, sh, wg_bf, wu_bf, wd_bf)


def _combine_kernel(dest_ref, rows_hbm, gt_ref, x_ref, gate_ref, g_ref, b_ref, out_ref, ybuf, sem):
    i = pl.program_id(0)
    n = pl.num_programs(0)
    tm = x_ref.shape[0]
    s = tm * n
    slot = i % 2

    def start(tile, to):
        for k in range(TOP_K):
            _row_gather(rows_hbm, dest_ref, k * s + tile * tm, tm, ybuf.at[to, k], sem.at[to, k])

    def wait(at):
        for k in range(TOP_K):
            _row_gather_wait(rows_hbm, tm, ybuf.at[at, k], sem.at[at, k])

    @pl.when(i == 0)
    def _():
        start(0, 0)

    wait(slot)
    start(jnp.minimum(i + 1, n - 1), 1 - slot)
    gt = gt_ref[...]
    y = gt[:, 0:1] * ybuf[slot, 0] + gt[:, 1:2] * ybuf[slot, 1]
    out_ref[...] = _deepnorm(x_ref[...], y, gate_ref[...], g_ref[...], b_ref[...])

    @pl.when(i == n - 1)
    def _():
        wait(1 - slot)


def _moe_combine(dest_flat, out_rows, gates_t, x, gate, g, b):
    s = x.shape[0]
    tm = TM_COMB
    vec = pl.BlockSpec((1, D_MODEL), lambda i, dst: (0, 0))
    return pl.pallas_call(
        _combine_kernel,
        grid_spec=pltpu.PrefetchScalarGridSpec(
            num_scalar_prefetch=1,
            grid=(s // tm,),
            in_specs=[pl.BlockSpec(memory_space=pl.ANY),
                      pl.BlockSpec((tm, TOP_K), lambda i, dst: (i, 0)),
                      pl.BlockSpec((tm, D_MODEL), lambda i, dst: (i, 0)), vec, vec, vec],
            out_specs=pl.BlockSpec((tm, D_MODEL), lambda i, dst: (i, 0)),
            scratch_shapes=[pltpu.VMEM((2, TOP_K, tm, D_MODEL), F32), pltpu.SemaphoreType.DMA((2, TOP_K))]),
        out_shape=jax.ShapeDtypeStruct((s, D_MODEL), F32),
        compiler_params=_cparams(("arbitrary",)),
        name="moe_combine",
    )(dest_flat, out_rows, gates_t, x, gate, g, b)


def _moe_layer(x, sc, sh, gate, g, b, rw_t, rb_col, wg_bf, wu_bf, wd_bf, layer):
    s = x.shape[0]
    tm = TM_MOE
    expert, gates, rank, cnt = _router(x, sc, sh, rw_t, rb_col)
    counts = cnt[:, 0].astype(jnp.int32)
    padded = (counts + tm - 1) // tm * tm
    pad_end = jnp.cumsum(padded)
    pad_start = pad_end - padded
    eids = jnp.arange(N_EXPERTS, dtype=jnp.int32)
    start_of = jnp.sum(jnp.where(expert[..., None] == eids, pad_start, 0), axis=-1)
    dest = (start_of + rank).reshape(-1)
    p_rows = TOP_K * s + N_EXPERTS * tm
    token = jnp.tile(jnp.arange(s, dtype=jnp.int32), TOP_K)
    row_token = jnp.zeros((p_rows,), jnp.int32).at[dest].set(token)
    nb = p_rows // tm
    block_row0 = jnp.arange(nb, dtype=jnp.int32) * tm
    block_expert = jnp.minimum(jnp.sum(block_row0[:, None] >= pad_end[None, :], axis=-1),
                               N_EXPERTS - 1).astype(jnp.int32)
    nb_used = (pad_end[-1:] // tm).astype(jnp.int32)
    out_rows = _moe_ffn(x, sc, sh, wg_bf, wu_bf, wd_bf, layer, row_token, block_expert, nb_used)
    return _moe_combine(dest.astype(jnp.int32), out_rows, gates.T, x, gate, g, b)


def kernel(x, c, positions, ada_w, ada_b, ln_g, ln_b, attn_w_qkv, attn_w_o, conv_w_pw1, conv_w_dw, conv_ln_g,
           conv_ln_b, conv_w_pw2, sc_w_in, sc_w_conv, sc_w_out, router_w, router_b, moe_w_gate, moe_w_up,
           moe_w_down):
    batch, s, d = x.shape
    assert batch == 1 and d == D_MODEL and s % (DILATIONS[-1] * Q_BLOCK) == 0
    xs = x.reshape(s, d)

    mod = _adaln(c.reshape(d, 1), ada_w, ada_b)
    half = HEAD_DIM // 2
    inv_freq = ROPE_THETA ** (-jnp.arange(half, dtype=F32) / half)
    freq_row = jnp.concatenate([inv_freq, inv_freq]).reshape(1, HEAD_DIM)
    cos, sin = _rope_tables(positions.reshape(s, 1), freq_row)

    rw_t = router_w.T
    rb_col = router_b.reshape(N_EXPERTS, 1)
    wg_bf = moe_w_gate.astype(BF16)
    wu_bf = moe_w_up.astype(BF16)
    wd_bf = moe_w_down.astype(BF16)
    vec = lambda a: a.reshape(1, d)

    for i in range(DEPTH):
        sh1, sc1, g1, sh2, sc2, g2 = [mod[i, :, k * d:(k + 1) * d] for k in range(6)]
        m, j = i % N_MIXERS, i // N_MIXERS
        lg, lb = vec(ln_g[i, 0]), vec(ln_b[i, 0])
        if m == 0:
            w_qkv = attn_w_qkv[j].astype(BF16)
            outs, lses = [], []
            for grp in range(len(DILATIONS)):
                o_g, l_g = _attention(_qkv_proj(xs, sc1, sh1, w_qkv, cos, sin, grp), grp)
                outs.append(o_g)
                lses.append(l_g)
            xs = _attn_out(outs, lses, attn_w_o[j].astype(BF16), xs, g1, lg, lb)
        elif m == 1:
            glu = _gated_in(_glu_in_kernel, 2, 1, xs, sc1, sh1, conv_w_pw1[j].astype(BF16), "conv_in")
            xs = _conv_out(glu, conv_w_dw[j], vec(conv_ln_g[j]), vec(conv_ln_b[j]),
                           conv_w_pw2[j].astype(BF16), xs, g1, lg, lb)
        else:
            bgate, ch = _gated_in(_sc_in_kernel, 3, 2, xs, sc1, sh1, sc_w_in[j].astype(BF16), "sc_in")
            xs = _sc_out(ch, bgate, sc_w_conv[j], sc_w_out[j].astype(BF16), xs, g1, lg, lb)
        xs = _moe_layer(xs, sc2, sh2, g2, vec(ln_g[i, 1]), vec(ln_b[i, 1]), rw_t, rb_col,
                        wg_bf, wu_bf, wd_bf, i)
    return xs.reshape(batch, s, d)
```

```python
import functools
import math

import jax
import jax.numpy as jnp
from jax import lax
from jax.experimental import pallas as pl
from jax.experimental.pallas import tpu as pltpu

F32 = jnp.float32
BF16 = jnp.bfloat16

D_MODEL = 2048
DEPTH = 4
N_MIXERS = 3
ATTN_HEADS = 16
HEAD_DIM = 128
HD = ATTN_HEADS * HEAD_DIM
DILATIONS = (1, 4, 16)
Q_BLOCK = 128
ROPE_THETA = 10000.0
CONV_KERNEL = 31
SHORT_CONV_KERNEL = 3
N_EXPERTS = 16
N_EXPERT_GROUPS = 4
EXPERTS_PER_GROUP = 4
TOP_K = 2
D_EXPERT = 1408
ALPHA = (2 * DEPTH) ** 0.25
LN_EPS = 1e-5
NEG_INF = -1e30

LANES_V7X = 128
SUBLANES_V7X = 8
MXU_COLS_V7X = 256
VMEM_LIMIT_V7X = 56 * 1024 * 1024
TM_PROJ = 512
TN_PROJ = 1024
TN_GLU = 512
TM_OUT = 256
TM_ROUTE = 512
TM_MOE = 256
TM_COMB = 256
TN_ADA = 1024
HALO_CONV = 32
HALO_SC = 8
CONV_ROWS = 64
CONV_COLS = 256


def _cparams(sem):
    return pltpu.CompilerParams(dimension_semantics=sem, vmem_limit_bytes=VMEM_LIMIT_V7X)


def _ln_rows(z, g, b):
    mu = jnp.mean(z, axis=-1, keepdims=True)
    zc = z - mu
    var = jnp.mean(zc * zc, axis=-1, keepdims=True)
    return zc * lax.rsqrt(var + LN_EPS) * g + b


def _deepnorm(x, y, gate, g, b):
    return _ln_rows(ALPHA * x + (1.0 + gate) * y, g, b)


def _row_spec(tm, width):
    return pl.BlockSpec((tm, width), lambda i: (i, 0))


def _vec_spec(width):
    return pl.BlockSpec((1, width), lambda i: (0, 0))


def _adaln_kernel(c_ref, w_ref, b_ref, o_ref):
    c = c_ref[...]
    ca = c * jax.nn.sigmoid(c)
    o_ref[...] = jnp.sum(w_ref[...] * ca, axis=0, keepdims=True) + b_ref[...]


def _adaln(c_col, ada_w, ada_b):
    depth, d, n = ada_w.shape
    return pl.pallas_call(
        _adaln_kernel,
        grid=(depth, n // TN_ADA),
        in_specs=[pl.BlockSpec((d, 1), lambda l, j: (0, 0)),
                  pl.BlockSpec((None, d, TN_ADA), lambda l, j: (l, 0, j)),
                  pl.BlockSpec((None, 1, TN_ADA), lambda l, j: (l, 0, j))],
        out_specs=pl.BlockSpec((None, 1, TN_ADA), lambda l, j: (l, 0, j)),
        out_shape=jax.ShapeDtypeStruct((depth, 1, n), F32),
        compiler_params=_cparams(("arbitrary", "arbitrary")),
        name="adaln",
    )(c_col, ada_w, ada_b.reshape(depth, 1, n))


def _rope_kernel(pos_ref, freq_ref, cos_ref, sin_ref):
    ang = pos_ref[...].astype(F32) * freq_ref[...]
    lane = lax.broadcasted_iota(jnp.int32, ang.shape, 1)
    s = jnp.sin(ang)
    cos_ref[...] = jnp.cos(ang)
    sin_ref[...] = jnp.where(lane < HEAD_DIM // 2, -s, s)


def _rope_tables(pos_col, freq_row):
    s = pos_col.shape[0]
    tm = 1024
    spec = pl.BlockSpec((tm, HEAD_DIM), lambda i: (i, 0))
    return pl.pallas_call(
        _rope_kernel,
        grid=(s // tm,),
        in_specs=[pl.BlockSpec((tm, 1), lambda i: (i, 0)), _vec_spec(HEAD_DIM)],
        out_specs=[spec, spec],
        out_shape=[jax.ShapeDtypeStruct((s, HEAD_DIM), F32)] * 2,
        compiler_params=_cparams(("arbitrary",)),
        name="rope_tables",
    )(pos_col, freq_row)


def _qkv_kernel(x_ref, sc_ref, sh_ref, w_ref, cos_ref, sin_ref, o_ref, u_scr, acc_scr, *, d, n_rope):
    j = pl.program_id(1)
    nh, tm, _ = acc_scr.shape

    @pl.when(j == 0)
    def _():
        u_scr[...] = (x_ref[...] * (1.0 + sc_ref[...]) + sh_ref[...]).astype(BF16)

    is_rope = j < n_rope
    cos = cos_ref[...]
    sin = sin_ref[...]
    u = u_scr[...]
    for c in range(nh // 2):
        acc = jnp.dot(u, w_ref[:, c * MXU_COLS_V7X:(c + 1) * MXU_COLS_V7X], preferred_element_type=F32)
        for h in (2 * c, 2 * c + 1):
            a = acc[:, (h - 2 * c) * HEAD_DIM:(h - 2 * c + 1) * HEAD_DIM]
            a = jnp.where(is_rope, a * cos + pltpu.roll(a, HEAD_DIM // 2, 1) * sin, a)
            hs = slice(h * HEAD_DIM, (h + 1) * HEAD_DIM)
            if d == 1:
                o_ref[0, :, hs] = a.astype(BF16)
            else:
                acc_scr[h] = a
                for r in range(d):
                    o_ref[r, :, hs] = acc_scr[h, pl.ds(r, tm // d, stride=d), :].astype(BF16)


def _qkv_proj(x, sc, sh, w_bf, cos, sin, group):
    s = x.shape[0]
    d = DILATIONS[group]
    tm, tn = TM_PROJ, TN_PROJ
    ncol = 3 * HD
    col0 = group * (ncol // tn)
    return pl.pallas_call(
        functools.partial(_qkv_kernel, d=d, n_rope=2 * HD // tn),
        grid=(s // tm, ncol // tn),
        in_specs=[pl.BlockSpec((tm, D_MODEL), lambda i, j: (i, 0)),
                  pl.BlockSpec((1, D_MODEL), lambda i, j: (0, 0)),
                  pl.BlockSpec((1, D_MODEL), lambda i, j: (0, 0)),
                  pl.BlockSpec((D_MODEL, tn), lambda i, j: (0, col0 + j)),
                  pl.BlockSpec((tm, HEAD_DIM), lambda i, j: (i, 0)),
                  pl.BlockSpec((tm, HEAD_DIM), lambda i, j: (i, 0))],
        out_specs=pl.BlockSpec((d, tm // d, tn), lambda i, j: (0, i, j)),
        out_shape=jax.ShapeDtypeStruct((d, s // d, ncol), BF16),
        scratch_shapes=[pltpu.VMEM((tm, D_MODEL), BF16), pltpu.VMEM((tn // HEAD_DIM, tm, HEAD_DIM), F32)],
        compiler_params=_cparams(("arbitrary", "arbitrary")),
        name=f"qkv_proj_g{group}",
    )(x, sc, sh, w_bf, cos, sin)


def _attn_kernel(q_ref, kc_ref, vc_ref, o_ref, lse_ref, k_scr, v_scr, s_scr, p_scr, *, nblk):
    b = pl.program_id(0)
    not_first = (b % nblk) != 0

    @pl.when(b == 0)
    def _():
        k_scr[0:Q_BLOCK] = jnp.zeros((Q_BLOCK, HD), BF16)
        v_scr[0:Q_BLOCK] = jnp.zeros((Q_BLOCK, HD), BF16)

    k_scr[Q_BLOCK:] = kc_ref[...]
    v_scr[Q_BLOCK:] = vc_ref[...]
    nt = (((1,), (1,)), ((), ()))
    for h in range(ATTN_HEADS):
        sl = slice(h * HEAD_DIM, (h + 1) * HEAD_DIM)
        s_scr[h] = lax.dot_general(q_ref[:, sl], k_scr[:, sl], nt, preferred_element_type=F32)

    row = lax.broadcasted_iota(jnp.int32, (Q_BLOCK, 2 * Q_BLOCK), 0)
    col = lax.broadcasted_iota(jnp.int32, (Q_BLOCK, 2 * Q_BLOCK), 1)
    prev_ok = jnp.logical_and(jnp.logical_and(col < Q_BLOCK, col >= row), not_first)
    valid = jnp.logical_or(prev_ok, jnp.logical_and(col >= Q_BLOCK, col - Q_BLOCK <= row))
    scale = 1.0 / math.sqrt(HEAD_DIM)
    s = jnp.where(valid[None], s_scr[...], NEG_INF)
    m = jnp.max(s, axis=2, keepdims=True)
    p = jnp.exp2((s - m) * (scale * math.log2(math.e)))
    den = jnp.sum(p, axis=2, keepdims=True)
    p_scr[...] = p.astype(BF16)
    inv = 1.0 / den
    lse = m * scale + jnp.log(den)
    for h in range(ATTN_HEADS):
        sl = slice(h * HEAD_DIM, (h + 1) * HEAD_DIM)
        o = jnp.dot(p_scr[h], v_scr[:, sl], preferred_element_type=F32)
        o_ref[:, sl] = (o * inv[h]).astype(BF16)
        lse_ref[:, h:h + 1] = lse[h]
    k_scr[0:Q_BLOCK] = kc_ref[...]
    v_scr[0:Q_BLOCK] = vc_ref[...]


def _attention(qkv, group):
    d, l, ncol = qkv.shape
    s = d * l
    flat = qkv.reshape(s, ncol)
    nblk = l // Q_BLOCK
    cur = lambda c: pl.BlockSpec((Q_BLOCK, HD), lambda b: (b, c))
    return pl.pallas_call(
        functools.partial(_attn_kernel, nblk=nblk),
        grid=(s // Q_BLOCK,),
        in_specs=[cur(0), cur(1), cur(2)],
        out_specs=[pl.BlockSpec((Q_BLOCK, HD), lambda b: (b, 0)),
                   pl.BlockSpec((Q_BLOCK, ATTN_HEADS), lambda b: (b, 0))],
        out_shape=[jax.ShapeDtypeStruct((s, HD), BF16), jax.ShapeDtypeStruct((s, ATTN_HEADS), F32)],
        scratch_shapes=[pltpu.VMEM((2 * Q_BLOCK, HD), BF16), pltpu.VMEM((2 * Q_BLOCK, HD), BF16),
                        pltpu.VMEM((ATTN_HEADS, Q_BLOCK, 2 * Q_BLOCK), F32),
                        pltpu.VMEM((ATTN_HEADS, Q_BLOCK, 2 * Q_BLOCK), BF16)],
        compiler_params=_cparams(("arbitrary",)),
        name=f"dil_attn_g{group}",
    )(flat, flat, flat)


def _attn_out_kernel(o0_ref, o1_ref, o2_ref, l0_ref, l1_ref, l2_ref, w_ref, x_ref, gate_ref, g_ref, b_ref,
                     out_ref, o_scr, a_scr):
    tm = x_ref.shape[0]
    for gi, o_ref in enumerate((o1_ref, o2_ref)):
        d = DILATIONS[gi + 1]
        for r in range(d):
            for h in range(ATTN_HEADS):
                o_scr[gi, h, pl.ds(r, tm // d, stride=d), :] = (
                    o_ref[r, :, h * HEAD_DIM:(h + 1) * HEAD_DIM].astype(F32))
    l0 = l0_ref[...]
    l1 = l1_ref[...]
    l2 = l2_ref[...]
    m = jnp.maximum(jnp.maximum(l0, l1), l2)
    e0 = jnp.exp(l0 - m)
    e1 = jnp.exp(l1 - m)
    e2 = jnp.exp(l2 - m)
    z = e0 + e1 + e2
    w0 = e0 / z
    w1 = e1 / z
    w2 = e2 / z
    for h in range(ATTN_HEADS):
        sl = slice(h * HEAD_DIM, (h + 1) * HEAD_DIM)
        a = (w0[:, h:h + 1] * o0_ref[:, sl].astype(F32)
             + w1[:, h:h + 1] * o_scr[0, h]
             + w2[:, h:h + 1] * o_scr[1, h])
        a_scr[:, sl] = a.astype(BF16)
    y = jnp.dot(a_scr[...], w_ref[...], preferred_element_type=F32)
    out_ref[...] = _deepnorm(x_ref[...], y, gate_ref[...], g_ref[...], b_ref[...])


def _attn_out(outs, lses, w_bf, x, gate, g, b):
    s = x.shape[0]
    tm = TM_OUT
    d1, d2 = DILATIONS[1], DILATIONS[2]
    o1 = outs[1].reshape(d1, s // d1, HD)
    o2 = outs[2].reshape(d2, s // d2, HD)
    l1 = lses[1].reshape(d1, s // d1, ATTN_HEADS).transpose(1, 0, 2).reshape(s, ATTN_HEADS)
    l2 = lses[2].reshape(d2, s // d2, ATTN_HEADS).transpose(1, 0, 2).reshape(s, ATTN_HEADS)
    perm = lambda d, w: pl.BlockSpec((d, tm // d, w), lambda i: (0, i, 0))
    return pl.pallas_call(
        _attn_out_kernel,
        grid=(s // tm,),
        in_specs=[_row_spec(tm, HD), perm(d1, HD), perm(d2, HD),
                  _row_spec(tm, ATTN_HEADS), _row_spec(tm, ATTN_HEADS), _row_spec(tm, ATTN_HEADS),
                  pl.BlockSpec((HD, D_MODEL), lambda i: (0, 0)),
                  _row_spec(tm, D_MODEL), _vec_spec(D_MODEL), _vec_spec(D_MODEL), _vec_spec(D_MODEL)],
        out_specs=_row_spec(tm, D_MODEL),
        out_shape=jax.ShapeDtypeStruct((s, D_MODEL), F32),
        scratch_shapes=[pltpu.VMEM((2, ATTN_HEADS, tm, HEAD_DIM), F32), pltpu.VMEM((tm, HD), BF16)],
        compiler_params=_cparams(("arbitrary",)),
        name="attn_out",
    )(outs[0], o1, o2, lses[0], l1, l2, w_bf, x, gate, g, b)


def _modulate_once(x_ref, sc_ref, sh_ref, u_scr):
    @pl.when(pl.program_id(1) == 0)
    def _():
        u_scr[...] = (x_ref[...] * (1.0 + sc_ref[...]) + sh_ref[...]).astype(BF16)


def _glu_in_kernel(x_ref, sc_ref, sh_ref, wa_ref, wg_ref, o_ref, u_scr):
    _modulate_once(x_ref, sc_ref, sh_ref, u_scr)
    u = u_scr[...]
    a = jnp.dot(u, wa_ref[...], preferred_element_type=F32)
    gt = jnp.dot(u, wg_ref[...], preferred_element_type=F32)
    o_ref[...] = a * jax.nn.sigmoid(gt)


def _sc_in_kernel(x_ref, sc_ref, sh_ref, wb_ref, wc_ref, wh_ref, b_ref, ch_ref, u_scr):
    _modulate_once(x_ref, sc_ref, sh_ref, u_scr)
    u = u_scr[...]
    b_ref[...] = jnp.dot(u, wb_ref[...], preferred_element_type=F32)
    c = jnp.dot(u, wc_ref[...], preferred_element_type=F32)
    h = jnp.dot(u, wh_ref[...], preferred_element_type=F32)
    ch_ref[...] = c * h


def _gated_in(kernel, n_parts, n_out, x, sc, sh, w_bf, name):
    s = x.shape[0]
    tm, tn = TM_PROJ, TN_GLU
    nj = D_MODEL // tn
    w_specs = [pl.BlockSpec((D_MODEL, tn), functools.partial(lambda i, j, p: (0, p * nj + j), p=p))
               for p in range(n_parts)]
    out_spec = pl.BlockSpec((tm, tn), lambda i, j: (i, j))
    out_shape = jax.ShapeDtypeStruct((s, D_MODEL), F32)
    return pl.pallas_call(
        kernel,
        grid=(s // tm, nj),
        in_specs=[pl.BlockSpec((tm, D_MODEL), lambda i, j: (i, 0)),
                  pl.BlockSpec((1, D_MODEL), lambda i, j: (0, 0)),
                  pl.BlockSpec((1, D_MODEL), lambda i, j: (0, 0))] + w_specs,
        out_specs=[out_spec] * n_out if n_out > 1 else out_spec,
        out_shape=[out_shape] * n_out if n_out > 1 else out_shape,
        scratch_shapes=[pltpu.VMEM((tm, D_MODEL), BF16)],
        compiler_params=_cparams(("arbitrary", "arbitrary")),
        name=name,
    )(x, sc, sh, *([w_bf] * n_parts))


def _fill_halo(hal_scr, prev_ref, cur_ref, halo):
    is_first = pl.program_id(0) == 0
    tm = cur_ref.shape[0]
    hal_scr[pl.ds(0, halo), :] = jnp.where(is_first, 0.0, prev_ref[...])
    hal_scr[pl.ds(halo, tm), :] = cur_ref[...]
    hal_scr[pl.ds(halo + tm, SUBLANES_V7X), :] = jnp.zeros((SUBLANES_V7X, D_MODEL), F32)


def _dwconv(hal_scr, w_ref, dst_scr, ph_scr, halo, ksize, tm):
    off = halo - (ksize - 1)
    win = CONV_ROWS + SUBLANES_V7X
    for r0 in range(0, tm, CONV_ROWS):
        for c0 in range(0, D_MODEL, CONV_COLS):
            cs = slice(c0, c0 + CONV_COLS)
            out = None
            for phase in range(SUBLANES_V7X):
                part = None
                for k in range(ksize):
                    if (off + k) % SUBLANES_V7X != phase:
                        continue
                    base = r0 + off + k - phase
                    term = w_ref[k:k + 1, cs] * hal_scr[base:base + win, cs]
                    part = term if part is None else part + term
                if part is None:
                    continue
                if phase == 0:
                    shifted = part[:CONV_ROWS]
                else:
                    ph_scr[phase] = part
                    shifted = ph_scr[phase, phase:phase + CONV_ROWS, :]
                out = shifted if out is None else out + shifted
            dst_scr[r0:r0 + CONV_ROWS, cs] = out


def _conv_out_kernel(cur_ref, prev_ref, wdw_ref, cg_ref, cb_ref, w_ref, x_ref, gate_ref, g_ref, b_ref,
                     out_ref, hal_scr, h_scr, ph_scr):
    tm = x_ref.shape[0]
    _fill_halo(hal_scr, prev_ref, cur_ref, HALO_CONV)
    _dwconv(hal_scr, wdw_ref, h_scr, ph_scr, HALO_CONV, CONV_KERNEL, tm)
    hn = _ln_rows(h_scr[...], cg_ref[...], cb_ref[...])
    a = (hn * jax.nn.sigmoid(hn)).astype(BF16)
    y = jnp.dot(a, w_ref[...], preferred_element_type=F32)
    out_ref[...] = _deepnorm(x_ref[...], y, gate_ref[...], g_ref[...], b_ref[...])


def _sc_out_kernel(cur_ref, prev_ref, bg_ref, wdw_ref, w_ref, x_ref, gate_ref, g_ref, b_ref,
                   out_ref, hal_scr, h_scr, ph_scr):
    tm = x_ref.shape[0]
    _fill_halo(hal_scr, prev_ref, cur_ref, HALO_SC)
    _dwconv(hal_scr, wdw_ref, h_scr, ph_scr, HALO_SC, SHORT_CONV_KERNEL, tm)
    a = (bg_ref[...] * h_scr[...]).astype(BF16)
    y = jnp.dot(a, w_ref[...], preferred_element_type=F32)
    out_ref[...] = _deepnorm(x_ref[...], y, gate_ref[...], g_ref[...], b_ref[...])


def _conv_phase_scratch():
    return pltpu.VMEM((SUBLANES_V7X, CONV_ROWS + SUBLANES_V7X, CONV_COLS), F32)


def _halo_spec(tm, halo):
    per = tm // halo
    return pl.BlockSpec((halo, D_MODEL), lambda i: (jnp.maximum(i * per - 1, 0), 0))


def _conv_out(glu, w_dw, cg, cb, w_bf, x, gate, g, b):
    s = x.shape[0]
    tm = TM_OUT
    full = lambda shape: pl.BlockSpec(shape, lambda i: (0, 0))
    return pl.pallas_call(
        _conv_out_kernel,
        grid=(s // tm,),
        in_specs=[_row_spec(tm, D_MODEL), _halo_spec(tm, HALO_CONV), full((CONV_KERNEL, D_MODEL)),
                  _vec_spec(D_MODEL), _vec_spec(D_MODEL), full((D_MODEL, D_MODEL)),
                  _row_spec(tm, D_MODEL), _vec_spec(D_MODEL), _vec_spec(D_MODEL), _vec_spec(D_MODEL)],
        out_specs=_row_spec(tm, D_MODEL),
        out_shape=jax.ShapeDtypeStruct((s, D_MODEL), F32),
        scratch_shapes=[pltpu.VMEM((tm + HALO_CONV + SUBLANES_V7X, D_MODEL), F32), pltpu.VMEM((tm, D_MODEL), F32),
                        _conv_phase_scratch()],
        compiler_params=_cparams(("arbitrary",)),
        name="conv_out",
    )(glu, glu, w_dw, cg, cb, w_bf, x, gate, g, b)


def _sc_out(ch, bgate, w_dw, w_bf, x, gate, g, b):
    s = x.shape[0]
    tm = TM_OUT
    full = lambda shape: pl.BlockSpec(shape, lambda i: (0, 0))
    return pl.pallas_call(
        _sc_out_kernel,
        grid=(s // tm,),
        in_specs=[_row_spec(tm, D_MODEL), _halo_spec(tm, HALO_SC), _row_spec(tm, D_MODEL),
                  full((SHORT_CONV_KERNEL, D_MODEL)), full((D_MODEL, D_MODEL)),
                  _row_spec(tm, D_MODEL), _vec_spec(D_MODEL), _vec_spec(D_MODEL), _vec_spec(D_MODEL)],
        out_specs=_row_spec(tm, D_MODEL),
        out_shape=jax.ShapeDtypeStruct((s, D_MODEL), F32),
        scratch_shapes=[pltpu.VMEM((tm + HALO_SC + SUBLANES_V7X, D_MODEL), F32), pltpu.VMEM((tm, D_MODEL), F32),
                        _conv_phase_scratch()],
        compiler_params=_cparams(("arbitrary",)),
        name="sc_out",
    )(ch, ch, bgate, w_dw, w_bf, x, gate, g, b)


def _argmax4_first(v):
    i01 = jnp.where(v[1] > v[0], 1, 0)
    m01 = jnp.maximum(v[0], v[1])
    i23 = jnp.where(v[3] > v[2], 3, 2)
    m23 = jnp.maximum(v[2], v[3])
    return jnp.where(m23 > m01, i23, i01), jnp.maximum(m01, m23)


def _select4(idx, v):
    return jnp.where(idx == 0, v[0], jnp.where(idx == 1, v[1], jnp.where(idx == 2, v[2], v[3])))


def _router_kernel(x_ref, sc_ref, sh_ref, wt_ref, rb_ref, e_ref, gt_ref, rk_ref, cnt_ref, carry_scr):
    i = pl.program_id(0)
    tm = x_ref.shape[0]

    @pl.when(i == 0)
    def _():
        carry_scr[...] = jnp.zeros_like(carry_scr)

    u = x_ref[...] * (1.0 + sc_ref[...]) + sh_ref[...]
    logits = lax.dot_general(wt_ref[...], u, (((1,), (1,)), ((), ())), precision=lax.Precision.HIGHEST,
                             preferred_element_type=F32) + rb_ref[...]
    ex = jnp.exp(logits - jnp.max(logits, axis=0, keepdims=True))
    probs = ex / jnp.sum(ex, axis=0, keepdims=True)

    top1_i, top1_v, top2_i, top2_v, score = [], [], [], [], []
    for grp in range(N_EXPERT_GROUPS):
        p = [probs[grp * EXPERTS_PER_GROUP + j:grp * EXPERTS_PER_GROUP + j + 1, :]
             for j in range(EXPERTS_PER_GROUP)]
        i1, v1 = _argmax4_first(p)
        rest = [jnp.where(i1 == j, -1.0, p[j]) for j in range(EXPERTS_PER_GROUP)]
        i2, v2 = _argmax4_first(rest)
        top1_i.append(i1)
        top1_v.append(v1)
        top2_i.append(i2)
        top2_v.append(v2)
        score.append(v1 + v2)
    gsel, _ = _argmax4_first(score)
    p1 = _select4(gsel, top1_v)
    p2 = _select4(gsel, top2_v)
    e1 = gsel * EXPERTS_PER_GROUP + _select4(gsel, top1_i)
    e2 = gsel * EXPERTS_PER_GROUP + _select4(gsel, top2_i)
    psum = p1 + p2
    e_ref[0:1, :] = e1
    e_ref[1:2, :] = e2
    gt_ref[0:1, :] = p1 / psum
    gt_ref[1:2, :] = p2 / psum

    eid = lax.broadcasted_iota(jnp.int32, (N_EXPERTS, tm), 0)
    earlier = (lax.broadcasted_iota(jnp.int32, (tm, tm), 0)
               < lax.broadcasted_iota(jnp.int32, (tm, tm), 1)).astype(BF16)
    base = carry_scr[:, 0:1]
    for slot, e_sel in enumerate((e1, e2)):
        onehot = (eid == e_sel).astype(F32)
        before = jnp.dot(onehot.astype(BF16), earlier, preferred_element_type=F32)
        rank = jnp.sum(onehot * (before + base), axis=0, keepdims=True)
        rk_ref[slot:slot + 1, :] = rank.astype(jnp.int32)
        base = base + jnp.sum(onehot, axis=1, keepdims=True)
    carry_scr[...] = jnp.broadcast_to(base, carry_scr.shape)
    cnt_ref[...] = carry_scr[...]


def _router(x, sc, sh, rw_t, rb_col):
    s = x.shape[0]
    tm = TM_ROUTE
    slot_spec = pl.BlockSpec((TOP_K, tm), lambda i: (0, i))
    full = lambda shape: pl.BlockSpec(shape, lambda i: (0, 0))
    return pl.pallas_call(
        _router_kernel,
        grid=(s // tm,),
        in_specs=[_row_spec(tm, D_MODEL), _vec_spec(D_MODEL), _vec_spec(D_MODEL),
                  full((N_EXPERTS, D_MODEL)), full((N_EXPERTS, 1))],
        out_specs=[slot_spec, slot_spec, slot_spec, full((N_EXPERTS, LANES_V7X))],
        out_shape=[jax.ShapeDtypeStruct((TOP_K, s), jnp.int32), jax.ShapeDtypeStruct((TOP_K, s), F32),
                   jax.ShapeDtypeStruct((TOP_K, s), jnp.int32),
                   jax.ShapeDtypeStruct((N_EXPERTS, LANES_V7X), F32)],
        scratch_shapes=[pltpu.VMEM((N_EXPERTS, LANES_V7X), F32)],
        compiler_params=_cparams(("arbitrary",)),
        name="router",
    )(x, sc, sh, rw_t, rb_col)


def _row_gather(src_hbm, idx_ref, base, n, dst, sem):
    for r in range(n):
        tok = idx_ref[base + r]
        pltpu.make_async_copy(src_hbm.at[pl.ds(tok, 1), :], dst.at[pl.ds(r, 1), :], sem).start()


def _row_gather_wait(src_hbm, n, dst, sem):
    pltpu.make_async_copy(src_hbm.at[pl.ds(0, n), :], dst, sem).wait()


def _ffn_kernel(rt_ref, be_ref, nbu_ref, x_hbm, sc_ref, sh_ref, wg_ref, wu_ref, wd_ref, o_ref, xbuf, sem):
    del be_ref
    i = pl.program_id(0)
    nbu = nbu_ref[0]
    tm = xbuf.shape[1]
    slot = i % 2

    @pl.when(i == 0)
    def _():
        _row_gather(x_hbm, rt_ref, 0, tm, xbuf.at[0], sem.at[0])

    @pl.when(i < nbu)
    def _():
        _row_gather_wait(x_hbm, tm, xbuf.at[slot], sem.at[slot])
        _row_gather(x_hbm, rt_ref, (i + 1) * tm, tm, xbuf.at[1 - slot], sem.at[1 - slot])
        u = (xbuf[slot] * (1.0 + sc_ref[...]) + sh_ref[...]).astype(BF16)
        gt = jnp.dot(u, wg_ref[...], preferred_element_type=F32)
        up = jnp.dot(u, wu_ref[...], preferred_element_type=F32)
        h = (gt * jax.nn.sigmoid(gt) * up).astype(BF16)
        o_ref[...] = jnp.dot(h, wd_ref[...], preferred_element_type=F32)

    @pl.when(i >= nbu)
    def _():
        @pl.when(i == nbu)
        def _():
            _row_gather_wait(x_hbm, tm, xbuf.at[slot], sem.at[slot])

        o_ref[...] = jnp.zeros_like(o_ref)


def _moe_ffn(x, sc, sh, wg_bf, wu_bf, wd_bf, layer, row_token, block_expert, nb_used):
    tm = TM_MOE
    p_rows = row_token.shape[0]
    nb = p_rows // tm

    def last_used(i, nbu):
        return jnp.maximum(jnp.minimum(i, nbu[0] - 1), 0)

    w_in = pl.BlockSpec((None, None, D_MODEL, D_EXPERT),
                        lambda i, rt, be, nbu: (layer, be[last_used(i, nbu)], 0, 0))
    w_out = pl.BlockSpec((None, None, D_EXPERT, D_MODEL),
                         lambda i, rt, be, nbu: (layer, be[last_used(i, nbu)], 0, 0))
    vec = pl.BlockSpec((1, D_MODEL), lambda i, rt, be, nbu: (0, 0))
    return pl.pallas_call(
        _ffn_kernel,
        grid_spec=pltpu.PrefetchScalarGridSpec(
            num_scalar_prefetch=3,
            grid=(nb,),
            in_specs=[pl.BlockSpec(memory_space=pl.ANY), vec, vec, w_in, w_in, w_out],
            out_specs=pl.BlockSpec((tm, D_MODEL), lambda i, rt, be, nbu: (i, 0)),
            scratch_shapes=[pltpu.VMEM((2, tm, D_MODEL), F32), pltpu.SemaphoreType.DMA((2,))]),
        out_shape=jax.ShapeDtypeStruct((p_rows, D_MODEL), F32),
        compiler_params=_cparams(("arbitrary",)),
        name="moe_ffn",
    )(row_token, block_expert, nb_used, x, sc, sh, wg_bf, wu_bf, wd_bf)


def _combine_kernel(dest_ref, rows_hbm, gt_ref, x_ref, gate_ref, g_ref, b_ref, out_ref, ybuf, sem):
    i = pl.program_id(0)
    n = pl.num_programs(0)
    tm = x_ref.shape[0]
    s = tm * n
    slot = i % 2

    def start(tile, to):
        for k in range(TOP_K):
            _row_gather(rows_hbm, dest_ref, k * s + tile * tm, tm, ybuf.at[to, k], sem.at[to, k])

    def wait(at):
        for k in range(TOP_K):
            _row_gather_wait(rows_hbm, tm, ybuf.at[at, k], sem.at[at, k])

    @pl.when(i == 0)
    def _():
        start(0, 0)

    wait(slot)
    start(jnp.minimum(i + 1, n - 1), 1 - slot)
    gt = gt_ref[...]
    y = gt[:, 0:1] * ybuf[slot, 0] + gt[:, 1:2] * ybuf[slot, 1]
    out_ref[...] = _deepnorm(x_ref[...], y, gate_ref[...], g_ref[...], b_ref[...])

    @pl.when(i == n - 1)
    def _():
        wait(1 - slot)


def _moe_combine(dest_flat, out_rows, gates_t, x, gate, g, b):
    s = x.shape[0]
    tm = TM_COMB
    vec = pl.BlockSpec((1, D_MODEL), lambda i, dst: (0, 0))
    return pl.pallas_call(
        _combine_kernel,
        grid_spec=pltpu.PrefetchScalarGridSpec(
            num_scalar_prefetch=1,
            grid=(s // tm,),
            in_specs=[pl.BlockSpec(memory_space=pl.ANY),
                      pl.BlockSpec((tm, TOP_K), lambda i, dst: (i, 0)),
                      pl.BlockSpec((tm, D_MODEL), lambda i, dst: (i, 0)), vec, vec, vec],
            out_specs=pl.BlockSpec((tm, D_MODEL), lambda i, dst: (i, 0)),
            scratch_shapes=[pltpu.VMEM((2, TOP_K, tm, D_MODEL), F32), pltpu.SemaphoreType.DMA((2, TOP_K))]),
        out_shape=jax.ShapeDtypeStruct((s, D_MODEL), F32),
        compiler_params=_cparams(("arbitrary",)),
        name="moe_combine",
    )(dest_flat, out_rows, gates_t, x, gate, g, b)


def _moe_layer(x, sc, sh, gate, g, b, rw_t, rb_col, wg_bf, wu_bf, wd_bf, layer):
    s = x.shape[0]
    tm = TM_MOE
    expert, gates, rank, cnt = _router(x, sc, sh, rw_t, rb_col)
    counts = cnt[:, 0].astype(jnp.int32)
    padded = (counts + tm - 1) // tm * tm
    pad_end = jnp.cumsum(padded)
    pad_start = pad_end - padded
    eids = jnp.arange(N_EXPERTS, dtype=jnp.int32)
    start_of = jnp.sum(jnp.where(expert[..., None] == eids, pad_start, 0), axis=-1)
    dest = (start_of + rank).reshape(-1)
    p_rows = TOP_K * s + N_EXPERTS * tm
    token = jnp.tile(jnp.arange(s, dtype=jnp.int32), TOP_K)
    row_token = jnp.zeros((p_rows,), jnp.int32).at[dest].set(token)
    nb = p_rows // tm
    block_row0 = jnp.arange(nb, dtype=jnp.int32) * tm
    block_expert = jnp.minimum(jnp.sum(block_row0[:, None] >= pad_end[None, :], axis=-1),
                               N_EXPERTS - 1).astype(jnp.int32)
    nb_used = (pad_end[-1:] // tm).astype(jnp.int32)
    out_rows = _moe_ffn(x, sc, sh, wg_bf, wu_bf, wd_bf, layer, row_token, block_expert, nb_used)
    return _moe_combine(dest.astype(jnp.int32), out_rows, gates.T, x, gate, g, b)


def kernel(x, c, positions, ada_w, ada_b, ln_g, ln_b, attn_w_qkv, attn_w_o, conv_w_pw1, conv_w_dw, conv_ln_g,
           conv_ln_b, conv_w_pw2, sc_w_in, sc_w_conv, sc_w_out, router_w, router_b, moe_w_gate, moe_w_up,
           moe_w_down):
    batch, s, d = x.shape
    assert batch == 1 and d == D_MODEL and s % (DILATIONS[-1] * Q_BLOCK) == 0
    xs = x.reshape(s, d)

    mod = _adaln(c.reshape(d, 1), ada_w, ada_b)
    half = HEAD_DIM // 2
    inv_freq = ROPE_THETA ** (-jnp.arange(half, dtype=F32) / half)
    freq_row = jnp.concatenate([inv_freq, inv_freq]).reshape(1, HEAD_DIM)
    cos, sin = _rope_tables(positions.reshape(s, 1), freq_row)

    rw_t = router_w.T
    rb_col = router_b.reshape(N_EXPERTS, 1)
    wg_bf = moe_w_gate.astype(BF16)
    wu_bf = moe_w_up.astype(BF16)
    wd_bf = moe_w_down.astype(BF16)
    vec = lambda a: a.reshape(1, d)

    for i in range(DEPTH):
        sh1, sc1, g1, sh2, sc2, g2 = [mod[i, :, k * d:(k + 1) * d] for k in range(6)]
        m, j = i % N_MIXERS, i // N_MIXERS
        lg, lb = vec(ln_g[i, 0]), vec(ln_b[i, 0])
        if m == 0:
            w_qkv = attn_w_qkv[j].astype(BF16)
            outs, lses = [], []
            for grp in range(len(DILATIONS)):
                o_g, l_g = _attention(_qkv_proj(xs, sc1, sh1, w_qkv, cos, sin, grp), grp)
                outs.append(o_g)
                lses.append(l_g)
            xs = _attn_out(outs, lses, attn_w_o[j].astype(BF16), xs, g1, lg, lb)
        elif m == 1:
            glu = _gated_in(_glu_in_kernel, 2, 1, xs, sc1, sh1, conv_w_pw1[j].astype(BF16), "conv_in")
            xs = _conv_out(glu, conv_w_dw[j], vec(conv_ln_g[j]), vec(conv_ln_b[j]),
                           conv_w_pw2[j].astype(BF16), xs, g1, lg, lb)
        else:
            bgate, ch = _gated_in(_sc_in_kernel, 3, 2, xs, sc1, sh1, sc_w_in[j].astype(BF16), "sc_in")
            xs = _sc_out(ch, bgate, sc_w_conv[j], sc_w_out[j].astype(BF16), xs, g1, lg, lb)
        xs = _moe_layer(xs, sc2, sh2, g2, vec(ln_g[i, 1]), vec(ln_b[i, 1]), rw_t, rb_col,
                        wg_bf, wu_bf, wd_bf, i)
    return xs.reshape(batch, s, d)
```

```python
import functools
import math

import jax
import jax.numpy as jnp
from jax import lax
from jax.experimental import pallas as pl
from jax.experimental.pallas import tpu as pltpu

F32 = jnp.float32
BF16 = jnp.bfloat16

D_MODEL = 2048
DEPTH = 4
N_MIXERS = 3
ATTN_HEADS = 16
HEAD_DIM = 128
HD = ATTN_HEADS * HEAD_DIM
DILATIONS = (1, 4, 16)
Q_BLOCK = 128
ROPE_THETA = 10000.0
CONV_KERNEL = 31
SHORT_CONV_KERNEL = 3
N_EXPERTS = 16
N_EXPERT_GROUPS = 4
EXPERTS_PER_GROUP = 4
TOP_K = 2
D_EXPERT = 1408
ALPHA = (2 * DEPTH) ** 0.25
LN_EPS = 1e-5
NEG_INF = -1e30

LANES_V7X = 128
SUBLANES_V7X = 8
MXU_COLS_V7X = 256
VMEM_LIMIT_V7X = 56 * 1024 * 1024
VMEM_LIMIT_FFN_V7X = 61 * 1024 * 1024
W_CHUNKS = 8
TM_PROJ = 512
TN_PROJ = 1024
TN_GLU = 512
TM_OUT = 256
TM_ROUTE = 512
TM_MOE = 256
TM_COMB = 256
TN_ADA = 1024
HALO_CONV = 32
HALO_SC = 8
CONV_ROWS = 64
CONV_COLS = 256


def _cparams(sem):
    return pltpu.CompilerParams(dimension_semantics=sem, vmem_limit_bytes=VMEM_LIMIT_V7X)


def _ln_rows(z, g, b):
    mu = jnp.mean(z, axis=-1, keepdims=True)
    zc = z - mu
    var = jnp.mean(zc * zc, axis=-1, keepdims=True)
    return zc * lax.rsqrt(var + LN_EPS) * g + b


def _deepnorm(x, y, gate, g, b):
    return _ln_rows(ALPHA * x + (1.0 + gate) * y, g, b)


def _row_spec(tm, width):
    return pl.BlockSpec((tm, width), lambda i: (i, 0))


def _vec_spec(width):
    return pl.BlockSpec((1, width), lambda i: (0, 0))


def _adaln_kernel(c_ref, w_ref, b_ref, o_ref):
    c = c_ref[...]
    ca = c * jax.nn.sigmoid(c)
    o_ref[...] = jnp.sum(w_ref[...] * ca, axis=0, keepdims=True) + b_ref[...]


def _adaln(c_col, ada_w, ada_b):
    depth, d, n = ada_w.shape
    return pl.pallas_call(
        _adaln_kernel,
        grid=(depth, n // TN_ADA),
        in_specs=[pl.BlockSpec((d, 1), lambda l, j: (0, 0)),
                  pl.BlockSpec((None, d, TN_ADA), lambda l, j: (l, 0, j)),
                  pl.BlockSpec((None, 1, TN_ADA), lambda l, j: (l, 0, j))],
        out_specs=pl.BlockSpec((None, 1, TN_ADA), lambda l, j: (l, 0, j)),
        out_shape=jax.ShapeDtypeStruct((depth, 1, n), F32),
        compiler_params=_cparams(("arbitrary", "arbitrary")),
        name="adaln",
    )(c_col, ada_w, ada_b.reshape(depth, 1, n))


def _rope_kernel(pos_ref, freq_ref, cos_ref, sin_ref):
    ang = pos_ref[...].astype(F32) * freq_ref[...]
    lane = lax.broadcasted_iota(jnp.int32, ang.shape, 1)
    s = jnp.sin(ang)
    cos_ref[...] = jnp.cos(ang)
    sin_ref[...] = jnp.where(lane < HEAD_DIM // 2, -s, s)


def _rope_tables(pos_col, freq_row):
    s = pos_col.shape[0]
    tm = 1024
    spec = pl.BlockSpec((tm, HEAD_DIM), lambda i: (i, 0))
    return pl.pallas_call(
        _rope_kernel,
        grid=(s // tm,),
        in_specs=[pl.BlockSpec((tm, 1), lambda i: (i, 0)), _vec_spec(HEAD_DIM)],
        out_specs=[spec, spec],
        out_shape=[jax.ShapeDtypeStruct((s, HEAD_DIM), F32)] * 2,
        compiler_params=_cparams(("arbitrary",)),
        name="rope_tables",
    )(pos_col, freq_row)


def _qkv_kernel(x_ref, sc_ref, sh_ref, w_ref, cos_ref, sin_ref, o_ref, u_scr, acc_scr, *, d, n_rope):
    j = pl.program_id(1)
    nh, tm, _ = acc_scr.shape

    @pl.when(j == 0)
    def _():
        u_scr[...] = (x_ref[...] * (1.0 + sc_ref[...]) + sh_ref[...]).astype(BF16)

    is_rope = j < n_rope
    cos = cos_ref[...]
    sin = sin_ref[...]
    u = u_scr[...]
    for c in range(nh // 2):
        acc = jnp.dot(u, w_ref[:, c * MXU_COLS_V7X:(c + 1) * MXU_COLS_V7X], preferred_element_type=F32)
        for h in (2 * c, 2 * c + 1):
            a = acc[:, (h - 2 * c) * HEAD_DIM:(h - 2 * c + 1) * HEAD_DIM]
            a = jnp.where(is_rope, a * cos + pltpu.roll(a, HEAD_DIM // 2, 1) * sin, a)
            hs = slice(h * HEAD_DIM, (h + 1) * HEAD_DIM)
            if d == 1:
                o_ref[0, :, hs] = a.astype(BF16)
            else:
                acc_scr[h] = a
                for r in range(d):
                    o_ref[r, :, hs] = acc_scr[h, pl.ds(r, tm // d, stride=d), :].astype(BF16)


def _qkv_proj(x, sc, sh, w_bf, cos, sin, group):
    s = x.shape[0]
    d = DILATIONS[group]
    tm, tn = TM_PROJ, TN_PROJ
    ncol = 3 * HD
    col0 = group * (ncol // tn)
    return pl.pallas_call(
        functools.partial(_qkv_kernel, d=d, n_rope=2 * HD // tn),
        grid=(s // tm, ncol // tn),
        in_specs=[pl.BlockSpec((tm, D_MODEL), lambda i, j: (i, 0)),
                  pl.BlockSpec((1, D_MODEL), lambda i, j: (0, 0)),
                  pl.BlockSpec((1, D_MODEL), lambda i, j: (0, 0)),
                  pl.BlockSpec((D_MODEL, tn), lambda i, j: (0, col0 + j)),
                  pl.BlockSpec((tm, HEAD_DIM), lambda i, j: (i, 0)),
                  pl.BlockSpec((tm, HEAD_DIM), lambda i, j: (i, 0))],
        out_specs=pl.BlockSpec((d, tm // d, tn), lambda i, j: (0, i, j)),
        out_shape=jax.ShapeDtypeStruct((d, s // d, ncol), BF16),
        scratch_shapes=[pltpu.VMEM((tm, D_MODEL), BF16), pltpu.VMEM((tn // HEAD_DIM, tm, HEAD_DIM), F32)],
        compiler_params=_cparams(("arbitrary", "arbitrary")),
        name=f"qkv_proj_g{group}",
    )(x, sc, sh, w_bf, cos, sin)


def _attn_kernel(q_ref, kc_ref, vc_ref, o_ref, lse_ref, k_scr, v_scr, s_scr, p_scr, *, nblk):
    b = pl.program_id(0)
    not_first = (b % nblk) != 0

    @pl.when(b == 0)
    def _():
        k_scr[0:Q_BLOCK] = jnp.zeros((Q_BLOCK, HD), BF16)
        v_scr[0:Q_BLOCK] = jnp.zeros((Q_BLOCK, HD), BF16)

    k_scr[Q_BLOCK:] = kc_ref[...]
    v_scr[Q_BLOCK:] = vc_ref[...]
    nt = (((1,), (1,)), ((), ()))
    for h in range(ATTN_HEADS):
        sl = slice(h * HEAD_DIM, (h + 1) * HEAD_DIM)
        s_scr[h] = lax.dot_general(q_ref[:, sl], k_scr[:, sl], nt, preferred_element_type=F32)

    row = lax.broadcasted_iota(jnp.int32, (Q_BLOCK, 2 * Q_BLOCK), 0)
    col = lax.broadcasted_iota(jnp.int32, (Q_BLOCK, 2 * Q_BLOCK), 1)
    prev_ok = jnp.logical_and(jnp.logical_and(col < Q_BLOCK, col >= row), not_first)
    valid = jnp.logical_or(prev_ok, jnp.logical_and(col >= Q_BLOCK, col - Q_BLOCK <= row))
    scale = 1.0 / math.sqrt(HEAD_DIM)
    s = jnp.where(valid[None], s_scr[...], NEG_INF)
    m = jnp.max(s, axis=2, keepdims=True)
    p = jnp.exp2((s - m) * (scale * math.log2(math.e)))
    den = jnp.sum(p, axis=2, keepdims=True)
    p_scr[...] = p.astype(BF16)
    inv = 1.0 / den
    lse = m * scale + jnp.log(den)
    for h in range(ATTN_HEADS):
        sl = slice(h * HEAD_DIM, (h + 1) * HEAD_DIM)
        o = jnp.dot(p_scr[h], v_scr[:, sl], preferred_element_type=F32)
        o_ref[:, sl] = (o * inv[h]).astype(BF16)
        lse_ref[:, h:h + 1] = lse[h]
    k_scr[0:Q_BLOCK] = kc_ref[...]
    v_scr[0:Q_BLOCK] = vc_ref[...]


def _attention(qkv, group):
    d, l, ncol = qkv.shape
    s = d * l
    flat = qkv.reshape(s, ncol)
    nblk = l // Q_BLOCK
    cur = lambda c: pl.BlockSpec((Q_BLOCK, HD), lambda b: (b, c))
    return pl.pallas_call(
        functools.partial(_attn_kernel, nblk=nblk),
        grid=(s // Q_BLOCK,),
        in_specs=[cur(0), cur(1), cur(2)],
        out_specs=[pl.BlockSpec((Q_BLOCK, HD), lambda b: (b, 0)),
                   pl.BlockSpec((Q_BLOCK, ATTN_HEADS), lambda b: (b, 0))],
        out_shape=[jax.ShapeDtypeStruct((s, HD), BF16), jax.ShapeDtypeStruct((s, ATTN_HEADS), F32)],
        scratch_shapes=[pltpu.VMEM((2 * Q_BLOCK, HD), BF16), pltpu.VMEM((2 * Q_BLOCK, HD), BF16),
                        pltpu.VMEM((ATTN_HEADS, Q_BLOCK, 2 * Q_BLOCK), F32),
                        pltpu.VMEM((ATTN_HEADS, Q_BLOCK, 2 * Q_BLOCK), BF16)],
        compiler_params=_cparams(("arbitrary",)),
        name=f"dil_attn_g{group}",
    )(flat, flat, flat)


def _attn_out_kernel(o0_ref, o1_ref, o2_ref, l0_ref, l1_ref, l2_ref, w_ref, x_ref, gate_ref, g_ref, b_ref,
                     out_ref, o_scr, a_scr):
    tm = x_ref.shape[0]
    for gi, o_ref in enumerate((o1_ref, o2_ref)):
        d = DILATIONS[gi + 1]
        for r in range(d):
            for h in range(ATTN_HEADS):
                o_scr[gi, h, pl.ds(r, tm // d, stride=d), :] = (
                    o_ref[r, :, h * HEAD_DIM:(h + 1) * HEAD_DIM].astype(F32))
    l0 = l0_ref[...]
    l1 = l1_ref[...]
    l2 = l2_ref[...]
    m = jnp.maximum(jnp.maximum(l0, l1), l2)
    e0 = jnp.exp(l0 - m)
    e1 = jnp.exp(l1 - m)
    e2 = jnp.exp(l2 - m)
    z = e0 + e1 + e2
    w0 = e0 / z
    w1 = e1 / z
    w2 = e2 / z
    for h in range(ATTN_HEADS):
        sl = slice(h * HEAD_DIM, (h + 1) * HEAD_DIM)
        a = (w0[:, h:h + 1] * o0_ref[:, sl].astype(F32)
             + w1[:, h:h + 1] * o_scr[0, h]
             + w2[:, h:h + 1] * o_scr[1, h])
        a_scr[:, sl] = a.astype(BF16)
    y = jnp.dot(a_scr[...], w_ref[...], preferred_element_type=F32)
    out_ref[...] = _deepnorm(x_ref[...], y, gate_ref[...], g_ref[...], b_ref[...])


def _attn_out(outs, lses, w_bf, x, gate, g, b):
    s = x.shape[0]
    tm = TM_OUT
    d1, d2 = DILATIONS[1], DILATIONS[2]
    o1 = outs[1].reshape(d1, s // d1, HD)
    o2 = outs[2].reshape(d2, s // d2, HD)
    l1 = lses[1].reshape(d1, s // d1, ATTN_HEADS).transpose(1, 0, 2).reshape(s, ATTN_HEADS)
    l2 = lses[2].reshape(d2, s // d2, ATTN_HEADS).transpose(1, 0, 2).reshape(s, ATTN_HEADS)
    perm = lambda d, w: pl.BlockSpec((d, tm // d, w), lambda i: (0, i, 0))
    return pl.pallas_call(
        _attn_out_kernel,
        grid=(s // tm,),
        in_specs=[_row_spec(tm, HD), perm(d1, HD), perm(d2, HD),
                  _row_spec(tm, ATTN_HEADS), _row_spec(tm, ATTN_HEADS), _row_spec(tm, ATTN_HEADS),
                  pl.BlockSpec((HD, D_MODEL), lambda i: (0, 0)),
                  _row_spec(tm, D_MODEL), _vec_spec(D_MODEL), _vec_spec(D_MODEL), _vec_spec(D_MODEL)],
        out_specs=_row_spec(tm, D_MODEL),
        out_shape=jax.ShapeDtypeStruct((s, D_MODEL), F32),
        scratch_shapes=[pltpu.VMEM((2, ATTN_HEADS, tm, HEAD_DIM), F32), pltpu.VMEM((tm, HD), BF16)],
        compiler_params=_cparams(("arbitrary",)),
        name="attn_out",
    )(outs[0], o1, o2, lses[0], l1, l2, w_bf, x, gate, g, b)


def _modulate_once(x_ref, sc_ref, sh_ref, u_scr):
    @pl.when(pl.program_id(1) == 0)
    def _():
        u_scr[...] = (x_ref[...] * (1.0 + sc_ref[...]) + sh_ref[...]).astype(BF16)


def _glu_in_kernel(x_ref, sc_ref, sh_ref, wa_ref, wg_ref, o_ref, u_scr):
    _modulate_once(x_ref, sc_ref, sh_ref, u_scr)
    u = u_scr[...]
    a = jnp.dot(u, wa_ref[...], preferred_element_type=F32)
    gt = jnp.dot(u, wg_ref[...], preferred_element_type=F32)
    o_ref[...] = a * jax.nn.sigmoid(gt)


def _sc_in_kernel(x_ref, sc_ref, sh_ref, wb_ref, wc_ref, wh_ref, b_ref, ch_ref, u_scr):
    _modulate_once(x_ref, sc_ref, sh_ref, u_scr)
    u = u_scr[...]
    b_ref[...] = jnp.dot(u, wb_ref[...], preferred_element_type=F32)
    c = jnp.dot(u, wc_ref[...], preferred_element_type=F32)
    h = jnp.dot(u, wh_ref[...], preferred_element_type=F32)
    ch_ref[...] = c * h


def _gated_in(kernel, n_parts, n_out, x, sc, sh, w_bf, name):
    s = x.shape[0]
    tm, tn = TM_PROJ, TN_GLU
    nj = D_MODEL // tn
    w_specs = [pl.BlockSpec((D_MODEL, tn), functools.partial(lambda i, j, p: (0, p * nj + j), p=p))
               for p in range(n_parts)]
    out_spec = pl.BlockSpec((tm, tn), lambda i, j: (i, j))
    out_shape = jax.ShapeDtypeStruct((s, D_MODEL), F32)
    return pl.pallas_call(
        kernel,
        grid=(s // tm, nj),
        in_specs=[pl.BlockSpec((tm, D_MODEL), lambda i, j: (i, 0)),
                  pl.BlockSpec((1, D_MODEL), lambda i, j: (0, 0)),
                  pl.BlockSpec((1, D_MODEL), lambda i, j: (0, 0))] + w_specs,
        out_specs=[out_spec] * n_out if n_out > 1 else out_spec,
        out_shape=[out_shape] * n_out if n_out > 1 else out_shape,
        scratch_shapes=[pltpu.VMEM((tm, D_MODEL), BF16)],
        compiler_params=_cparams(("arbitrary", "arbitrary")),
        name=name,
    )(x, sc, sh, *([w_bf] * n_parts))


def _fill_halo(hal_scr, prev_ref, cur_ref, halo):
    is_first = pl.program_id(0) == 0
    tm = cur_ref.shape[0]
    hal_scr[pl.ds(0, halo), :] = jnp.where(is_first, 0.0, prev_ref[...])
    hal_scr[pl.ds(halo, tm), :] = cur_ref[...]
    hal_scr[pl.ds(halo + tm, SUBLANES_V7X), :] = jnp.zeros((SUBLANES_V7X, D_MODEL), F32)


def _dwconv(hal_scr, w_ref, dst_scr, ph_scr, halo, ksize, tm):
    off = halo - (ksize - 1)
    win = CONV_ROWS + SUBLANES_V7X
    for r0 in range(0, tm, CONV_ROWS):
        for c0 in range(0, D_MODEL, CONV_COLS):
            cs = slice(c0, c0 + CONV_COLS)
            out = None
            for phase in range(SUBLANES_V7X):
                part = None
                for k in range(ksize):
                    if (off + k) % SUBLANES_V7X != phase:
                        continue
                    base = r0 + off + k - phase
                    term = w_ref[k:k + 1, cs] * hal_scr[base:base + win, cs]
                    part = term if part is None else part + term
                if part is None:
                    continue
                if phase == 0:
                    shifted = part[:CONV_ROWS]
                else:
                    ph_scr[phase] = part
                    shifted = ph_scr[phase, phase:phase + CONV_ROWS, :]
                out = shifted if out is None else out + shifted
            dst_scr[r0:r0 + CONV_ROWS, cs] = out


def _conv_out_kernel(cur_ref, prev_ref, wdw_ref, cg_ref, cb_ref, w_ref, x_ref, gate_ref, g_ref, b_ref,
                     out_ref, hal_scr, h_scr, ph_scr):
    tm = x_ref.shape[0]
    _fill_halo(hal_scr, prev_ref, cur_ref, HALO_CONV)
    _dwconv(hal_scr, wdw_ref, h_scr, ph_scr, HALO_CONV, CONV_KERNEL, tm)
    hn = _ln_rows(h_scr[...], cg_ref[...], cb_ref[...])
    a = (hn * jax.nn.sigmoid(hn)).astype(BF16)
    y = jnp.dot(a, w_ref[...], preferred_element_type=F32)
    out_ref[...] = _deepnorm(x_ref[...], y, gate_ref[...], g_ref[...], b_ref[...])


def _sc_out_kernel(cur_ref, prev_ref, bg_ref, wdw_ref, w_ref, x_ref, gate_ref, g_ref, b_ref,
                   out_ref, hal_scr, h_scr, ph_scr):
    tm = x_ref.shape[0]
    _fill_halo(hal_scr, prev_ref, cur_ref, HALO_SC)
    _dwconv(hal_scr, wdw_ref, h_scr, ph_scr, HALO_SC, SHORT_CONV_KERNEL, tm)
    a = (bg_ref[...] * h_scr[...]).astype(BF16)
    y = jnp.dot(a, w_ref[...], preferred_element_type=F32)
    out_ref[...] = _deepnorm(x_ref[...], y, gate_ref[...], g_ref[...], b_ref[...])


def _conv_phase_scratch():
    return pltpu.VMEM((SUBLANES_V7X, CONV_ROWS + SUBLANES_V7X, CONV_COLS), F32)


def _halo_spec(tm, halo):
    per = tm // halo
    return pl.BlockSpec((halo, D_MODEL), lambda i: (jnp.maximum(i * per - 1, 0), 0))


def _conv_out(glu, w_dw, cg, cb, w_bf, x, gate, g, b):
    s = x.shape[0]
    tm = TM_OUT
    full = lambda shape: pl.BlockSpec(shape, lambda i: (0, 0))
    return pl.pallas_call(
        _conv_out_kernel,
        grid=(s // tm,),
        in_specs=[_row_spec(tm, D_MODEL), _halo_spec(tm, HALO_CONV), full((CONV_KERNEL, D_MODEL)),
                  _vec_spec(D_MODEL), _vec_spec(D_MODEL), full((D_MODEL, D_MODEL)),
                  _row_spec(tm, D_MODEL), _vec_spec(D_MODEL), _vec_spec(D_MODEL), _vec_spec(D_MODEL)],
        out_specs=_row_spec(tm, D_MODEL),
        out_shape=jax.ShapeDtypeStruct((s, D_MODEL), F32),
        scratch_shapes=[pltpu.VMEM((tm + HALO_CONV + SUBLANES_V7X, D_MODEL), F32), pltpu.VMEM((tm, D_MODEL), F32),
                        _conv_phase_scratch()],
        compiler_params=_cparams(("arbitrary",)),
        name="conv_out",
    )(glu, glu, w_dw, cg, cb, w_bf, x, gate, g, b)


def _sc_out(ch, bgate, w_dw, w_bf, x, gate, g, b):
    s = x.shape[0]
    tm = TM_OUT
    full = lambda shape: pl.BlockSpec(shape, lambda i: (0, 0))
    return pl.pallas_call(
        _sc_out_kernel,
        grid=(s // tm,),
        in_specs=[_row_spec(tm, D_MODEL), _halo_spec(tm, HALO_SC), _row_spec(tm, D_MODEL),
                  full((SHORT_CONV_KERNEL, D_MODEL)), full((D_MODEL, D_MODEL)),
                  _row_spec(tm, D_MODEL), _vec_spec(D_MODEL), _vec_spec(D_MODEL), _vec_spec(D_MODEL)],
        out_specs=_row_spec(tm, D_MODEL),
        out_shape=jax.ShapeDtypeStruct((s, D_MODEL), F32),
        scratch_shapes=[pltpu.VMEM((tm + HALO_SC + SUBLANES_V7X, D_MODEL), F32), pltpu.VMEM((tm, D_MODEL), F32),
                        _conv_phase_scratch()],
        compiler_params=_cparams(("arbitrary",)),
        name="sc_out",
    )(ch, ch, bgate, w_dw, w_bf, x, gate, g, b)


def _argmax4_first(v):
    i01 = jnp.where(v[1] > v[0], 1, 0)
    m01 = jnp.maximum(v[0], v[1])
    i23 = jnp.where(v[3] > v[2], 3, 2)
    m23 = jnp.maximum(v[2], v[3])
    return jnp.where(m23 > m01, i23, i01), jnp.maximum(m01, m23)


def _select4(idx, v):
    return jnp.where(idx == 0, v[0], jnp.where(idx == 1, v[1], jnp.where(idx == 2, v[2], v[3])))


def _router_kernel(x_ref, sc_ref, sh_ref, wt_ref, rb_ref, e_ref, gt_ref, rk_ref, cnt_ref, carry_scr):
    i = pl.program_id(0)
    tm = x_ref.shape[0]

    @pl.when(i == 0)
    def _():
        carry_scr[...] = jnp.zeros_like(carry_scr)

    u = x_ref[...] * (1.0 + sc_ref[...]) + sh_ref[...]
    logits = lax.dot_general(wt_ref[...], u, (((1,), (1,)), ((), ())), precision=lax.Precision.HIGHEST,
                             preferred_element_type=F32) + rb_ref[...]
    ex = jnp.exp(logits - jnp.max(logits, axis=0, keepdims=True))
    probs = ex / jnp.sum(ex, axis=0, keepdims=True)

    top1_i, top1_v, top2_i, top2_v, score = [], [], [], [], []
    for grp in range(N_EXPERT_GROUPS):
        p = [probs[grp * EXPERTS_PER_GROUP + j:grp * EXPERTS_PER_GROUP + j + 1, :]
             for j in range(EXPERTS_PER_GROUP)]
        i1, v1 = _argmax4_first(p)
        rest = [jnp.where(i1 == j, -1.0, p[j]) for j in range(EXPERTS_PER_GROUP)]
        i2, v2 = _argmax4_first(rest)
        top1_i.append(i1)
        top1_v.append(v1)
        top2_i.append(i2)
        top2_v.append(v2)
        score.append(v1 + v2)
    gsel, _ = _argmax4_first(score)
    p1 = _select4(gsel, top1_v)
    p2 = _select4(gsel, top2_v)
    e1 = gsel * EXPERTS_PER_GROUP + _select4(gsel, top1_i)
    e2 = gsel * EXPERTS_PER_GROUP + _select4(gsel, top2_i)
    psum = p1 + p2
    e_ref[0:1, :] = e1
    e_ref[1:2, :] = e2
    gt_ref[0:1, :] = p1 / psum
    gt_ref[1:2, :] = p2 / psum

    eid = lax.broadcasted_iota(jnp.int32, (N_EXPERTS, tm), 0)
    earlier = (lax.broadcasted_iota(jnp.int32, (tm, tm), 0)
               < lax.broadcasted_iota(jnp.int32, (tm, tm), 1)).astype(BF16)
    base = carry_scr[:, 0:1]
    for slot, e_sel in enumerate((e1, e2)):
        onehot = (eid == e_sel).astype(F32)
        before = jnp.dot(onehot.astype(BF16), earlier, preferred_element_type=F32)
        rank = jnp.sum(onehot * (before + base), axis=0, keepdims=True)
        rk_ref[slot:slot + 1, :] = rank.astype(jnp.int32)
        base = base + jnp.sum(onehot, axis=1, keepdims=True)
    carry_scr[...] = jnp.broadcast_to(base, carry_scr.shape)
    cnt_ref[...] = carry_scr[...]


def _router(x, sc, sh, rw_t, rb_col):
    s = x.shape[0]
    tm = TM_ROUTE
    slot_spec = pl.BlockSpec((TOP_K, tm), lambda i: (0, i))
    full = lambda shape: pl.BlockSpec(shape, lambda i: (0, 0))
    return pl.pallas_call(
        _router_kernel,
        grid=(s // tm,),
        in_specs=[_row_spec(tm, D_MODEL), _vec_spec(D_MODEL), _vec_spec(D_MODEL),
                  full((N_EXPERTS, D_MODEL)), full((N_EXPERTS, 1))],
        out_specs=[slot_spec, slot_spec, slot_spec, full((N_EXPERTS, LANES_V7X))],
        out_shape=[jax.ShapeDtypeStruct((TOP_K, s), jnp.int32), jax.ShapeDtypeStruct((TOP_K, s), F32),
                   jax.ShapeDtypeStruct((TOP_K, s), jnp.int32),
                   jax.ShapeDtypeStruct((N_EXPERTS, LANES_V7X), F32)],
        scratch_shapes=[pltpu.VMEM((N_EXPERTS, LANES_V7X), F32)],
        compiler_params=_cparams(("arbitrary",)),
        name="router",
    )(x, sc, sh, rw_t, rb_col)


def _row_gather(src_hbm, idx_ref, base, n, dst, sem):
    for r in range(n):
        tok = idx_ref[base + r]
        pltpu.make_async_copy(src_hbm.at[pl.ds(tok, 1), :], dst.at[pl.ds(r, 1), :], sem).start()


def _row_gather_wait(src_hbm, n, dst, sem):
    pltpu.make_async_copy(src_hbm.at[pl.ds(0, n), :], dst, sem).wait()


class _WeightStream:
    def __init__(self, layer, w_hbm, stage, wbuf, sem):
        self.layer, self.w_hbm, self.stage, self.wbuf, self.sem = layer, w_hbm, stage, wbuf, sem

    def _copy(self, mat, expert, q, slot):
        kind = 0 if mat < 2 else 1
        rows = self.stage[kind].shape[1]
        src = self.w_hbm[mat].at[self.layer, expert, pl.ds(pl.multiple_of(q * rows, rows), rows), :]
        return pltpu.make_async_copy(src, self.stage[kind].at[slot], self.sem.at[kind, slot])

    def _per_matrix(self, c, fn):
        mat, q, slot = c // W_CHUNKS, c % W_CHUNKS, c % 2
        for m in range(3):
            @pl.when(mat == m)
            def _(m=m):
                fn(m, q, slot)

    def start(self, c, expert):
        self._per_matrix(c, lambda m, q, slot: self._copy(m, expert, q, slot).start())

    def finish(self, c, dst):
        def fn(m, q, slot):
            self._copy(m, 0, q, slot).wait()
            kind = 0 if m < 2 else 1
            rows = self.stage[kind].shape[1]
            self.wbuf[m][dst, pl.ds(pl.multiple_of(q * rows, rows), rows), :] = self.stage[kind][slot].astype(BF16)

        self._per_matrix(c, fn)

    def prefetch(self, expert, lo, hi):
        for t in range(2):
            @pl.when(lo + t < hi)
            def _(t=t):
                self.start(lo + t, expert)

    def drain(self, expert, dst, lo, hi):
        def body(c, carry):
            self.finish(c, dst)

            @pl.when(c + 2 < hi)
            def _():
                self.start(c + 2, expert)

            return carry

        lax.fori_loop(lo, hi, body, 0)


def _ffn_kernel(rt_ref, be_ref, nbu_ref, ws_ref, nx_ref, lo_ref, hi_ref,
                x_hbm, sc_ref, sh_ref, wg_hbm, wu_hbm, wd_hbm, o_ref,
                xbuf, sem, wg_buf, wu_buf, wd_buf, stage_in, stage_dn, wsem, *, layer):
    i = pl.program_id(0)
    nbu = nbu_ref[0]
    tm = xbuf.shape[1]
    slot = i % 2
    stream = _WeightStream(layer, (wg_hbm, wu_hbm, wd_hbm), (stage_in, stage_dn), (wg_buf, wu_buf, wd_buf), wsem)
    n_chunks = 3 * W_CHUNKS

    @pl.when(i == 0)
    def _():
        _row_gather(x_hbm, rt_ref, 0, tm, xbuf.at[0], sem.at[0])
        stream.prefetch(be_ref[0], 0, n_chunks)
        stream.drain(be_ref[0], ws_ref[0], 0, n_chunks)

    @pl.when(i < nbu)
    def _():
        ws, nxt, lo, hi = ws_ref[i], nx_ref[i], lo_ref[i], hi_ref[i]
        stream.prefetch(nxt, lo, hi)
        _row_gather_wait(x_hbm, tm, xbuf.at[slot], sem.at[slot])
        _row_gather(x_hbm, rt_ref, (i + 1) * tm, tm, xbuf.at[1 - slot], sem.at[1 - slot])
        u = (xbuf[slot] * (1.0 + sc_ref[...]) + sh_ref[...]).astype(BF16)
        gt = jnp.dot(u, wg_buf[ws], preferred_element_type=F32)
        up = jnp.dot(u, wu_buf[ws], preferred_element_type=F32)
        h = (gt * jax.nn.sigmoid(gt) * up).astype(BF16)
        o_ref[...] = jnp.dot(h, wd_buf[ws], preferred_element_type=F32)
        stream.drain(nxt, 1 - ws, lo, hi)

    @pl.when(i >= nbu)
    def _():
        @pl.when(i == nbu)
        def _():
            _row_gather_wait(x_hbm, tm, xbuf.at[slot], sem.at[slot])

        o_ref[...] = jnp.zeros_like(o_ref)


def _moe_ffn(x, sc, sh, w_gate, w_up, w_down, layer, row_token, block_expert, nb_used, w_slot, next_expert,
             chunk_lo, chunk_hi):
    tm = TM_MOE
    p_rows = row_token.shape[0]
    nb = p_rows // tm
    vec = pl.BlockSpec((1, D_MODEL), lambda i, *_: (0, 0))
    hbm = pl.BlockSpec(memory_space=pl.ANY)
    return pl.pallas_call(
        functools.partial(_ffn_kernel, layer=layer),
        grid_spec=pltpu.PrefetchScalarGridSpec(
            num_scalar_prefetch=7,
            grid=(nb,),
            in_specs=[hbm, vec, vec, hbm, hbm, hbm],
            out_specs=pl.BlockSpec((tm, D_MODEL), lambda i, *_: (i, 0)),
            scratch_shapes=[pltpu.VMEM((2, tm, D_MODEL), F32), pltpu.SemaphoreType.DMA((2,)),
                            pltpu.VMEM((2, D_MODEL, D_EXPERT), BF16), pltpu.VMEM((2, D_MODEL, D_EXPERT), BF16),
                            pltpu.VMEM((2, D_EXPERT, D_MODEL), BF16),
                            pltpu.VMEM((2, D_MODEL // W_CHUNKS, D_EXPERT), F32),
                            pltpu.VMEM((2, D_EXPERT // W_CHUNKS, D_MODEL), F32),
                            pltpu.SemaphoreType.DMA((2, 2))]),
        out_shape=jax.ShapeDtypeStruct((p_rows, D_MODEL), F32),
        compiler_params=pltpu.CompilerParams(dimension_semantics=("arbitrary",),
                                             vmem_limit_bytes=VMEM_LIMIT_FFN_V7X),
        name="moe_ffn",
    )(row_token, block_expert, nb_used, w_slot, next_expert, chunk_lo, chunk_hi,
      x, sc, sh, w_gate, w_up, w_down)


def _combine_kernel(dest_ref, rows_hbm, gt_ref, x_ref, gate_ref, g_ref, b_ref, out_ref, ybuf, sem):
    i = pl.program_id(0)
    n = pl.num_programs(0)
    tm = x_ref.shape[0]
    s = tm * n
    slot = i % 2

    def start(tile, to):
        for k in range(TOP_K):
            _row_gather(rows_hbm, dest_ref, k * s + tile * tm, tm, ybuf.at[to, k], sem.at[to, k])

    def wait(at):
        for k in range(TOP_K):
            _row_gather_wait(rows_hbm, tm, ybuf.at[at, k], sem.at[at, k])

    @pl.when(i == 0)
    def _():
        start(0, 0)

    wait(slot)
    start(jnp.minimum(i + 1, n - 1), 1 - slot)
    gt = gt_ref[...]
    y = gt[:, 0:1] * ybuf[slot, 0] + gt[:, 1:2] * ybuf[slot, 1]
    out_ref[...] = _deepnorm(x_ref[...], y, gate_ref[...], g_ref[...], b_ref[...])

    @pl.when(i == n - 1)
    def _():
        wait(1 - slot)


def _moe_combine(dest_flat, out_rows, gates_t, x, gate, g, b):
    s = x.shape[0]
    tm = TM_COMB
    vec = pl.BlockSpec((1, D_MODEL), lambda i, dst: (0, 0))
    return pl.pallas_call(
        _combine_kernel,
        grid_spec=pltpu.PrefetchScalarGridSpec(
            num_scalar_prefetch=1,
            grid=(s // tm,),
            in_specs=[pl.BlockSpec(memory_space=pl.ANY),
                      pl.BlockSpec((tm, TOP_K), lambda i, dst: (i, 0)),
                      pl.BlockSpec((tm, D_MODEL), lambda i, dst: (i, 0)), vec, vec, vec],
            out_specs=pl.BlockSpec((tm, D_MODEL), lambda i, dst: (i, 0)),
            scratch_shapes=[pltpu.VMEM((2, TOP_K, tm, D_MODEL), F32), pltpu.SemaphoreType.DMA((2, TOP_K))]),
        out_shape=jax.ShapeDtypeStruct((s, D_MODEL), F32),
        compiler_params=_cparams(("arbitrary",)),
        name="moe_combine",
    )(dest_flat, out_rows, gates_t, x, gate, g, b)


def _moe_layer(x, sc, sh, gate, g, b, rw_t, rb_col, w_gate, w_up, w_down, layer):
    s = x.shape[0]
    tm = TM_MOE
    expert, gates, rank, cnt = _router(x, sc, sh, rw_t, rb_col)
    counts = cnt[:, 0].astype(jnp.int32)
    padded = (counts + tm - 1) // tm * tm
    pad_end = jnp.cumsum(padded)
    pad_start = pad_end - padded
    eids = jnp.arange(N_EXPERTS, dtype=jnp.int32)
    start_of = jnp.sum(jnp.where(expert[..., None] == eids, pad_start, 0), axis=-1)
    dest = (start_of + rank).reshape(-1)
    p_rows = TOP_K * s + N_EXPERTS * tm
    token = jnp.tile(jnp.arange(s, dtype=jnp.int32), TOP_K)
    row_token = jnp.zeros((p_rows,), jnp.int32).at[dest].set(token)
    nb = p_rows // tm
    block_row0 = jnp.arange(nb, dtype=jnp.int32) * tm
    block_expert = jnp.minimum(jnp.sum(block_row0[:, None] >= pad_end[None, :], axis=-1),
                               N_EXPERTS - 1).astype(jnp.int32)
    nb_used = (pad_end[-1:] // tm).astype(jnp.int32)
    has = padded > 0
    order = jnp.cumsum(has.astype(jnp.int32)) - 1
    later = jnp.logical_and(eids[None, :] > eids[:, None], has[None, :])
    next_of = jnp.min(jnp.where(later, eids[None, :], N_EXPERTS), axis=-1)
    of_block = lambda v: jnp.sum(jnp.where(block_expert[:, None] == eids[None, :], v[None, :], 0), axis=-1)
    run_len = jnp.maximum(of_block(padded) // tm, 1)
    run_pos = jnp.arange(nb, dtype=jnp.int32) - of_block(pad_start) // tm
    nxt = of_block(next_of)
    streams = jnp.logical_and(nxt < N_EXPERTS, jnp.arange(nb) < nb_used[0])
    n_chunks = 3 * W_CHUNKS
    chunk_lo = jnp.where(streams, n_chunks * run_pos // run_len, 0).astype(jnp.int32)
    chunk_hi = jnp.where(streams, n_chunks * (run_pos + 1) // run_len, 0).astype(jnp.int32)
    next_expert = jnp.where(streams, nxt, block_expert).astype(jnp.int32)
    w_slot = (of_block(order) % 2).astype(jnp.int32)
    out_rows = _moe_ffn(x, sc, sh, w_gate, w_up, w_down, layer, row_token, block_expert, nb_used,
                        w_slot, next_expert, chunk_lo, chunk_hi)
    return _moe_combine(dest.astype(jnp.int32), out_rows, gates.T, x, gate, g, b)


def kernel(x, c, positions, ada_w, ada_b, ln_g, ln_b, attn_w_qkv, attn_w_o, conv_w_pw1, conv_w_dw, conv_ln_g,
           conv_ln_b, conv_w_pw2, sc_w_in, sc_w_conv, sc_w_out, router_w, router_b, moe_w_gate, moe_w_up,
           moe_w_down):
    batch, s, d = x.shape
    assert batch == 1 and d == D_MODEL and s % (DILATIONS[-1] * Q_BLOCK) == 0
    xs = x.reshape(s, d)

    mod = _adaln(c.reshape(d, 1), ada_w, ada_b)
    half = HEAD_DIM // 2
    inv_freq = ROPE_THETA ** (-jnp.arange(half, dtype=F32) / half)
    freq_row = jnp.concatenate([inv_freq, inv_freq]).reshape(1, HEAD_DIM)
    cos, sin = _rope_tables(positions.reshape(s, 1), freq_row)

    rw_t = router_w.T
    rb_col = router_b.reshape(N_EXPERTS, 1)
    vec = lambda a: a.reshape(1, d)

    for i in range(DEPTH):
        sh1, sc1, g1, sh2, sc2, g2 = [mod[i, :, k * d:(k + 1) * d] for k in range(6)]
        m, j = i % N_MIXERS, i // N_MIXERS
        lg, lb = vec(ln_g[i, 0]), vec(ln_b[i, 0])
        if m == 0:
            w_qkv = attn_w_qkv[j].astype(BF16)
            outs, lses = [], []
            for grp in range(len(DILATIONS)):
                o_g, l_g = _attention(_qkv_proj(xs, sc1, sh1, w_qkv, cos, sin, grp), grp)
                outs.append(o_g)
                lses.append(l_g)
            xs = _attn_out(outs, lses, attn_w_o[j].astype(BF16), xs, g1, lg, lb)
        elif m == 1:
            glu = _gated_in(_glu_in_kernel, 2, 1, xs, sc1, sh1, conv_w_pw1[j].astype(BF16), "conv_in")
            xs = _conv_out(glu, conv_w_dw[j], vec(conv_ln_g[j]), vec(conv_ln_b[j]),
                           conv_w_pw2[j].astype(BF16), xs, g1, lg, lb)
        else:
            bgate, ch = _gated_in(_sc_in_kernel, 3, 2, xs, sc1, sh1, sc_w_in[j].astype(BF16), "sc_in")
            xs = _sc_out(ch, bgate, sc_w_conv[j], sc_w_out[j].astype(BF16), xs, g1, lg, lb)
        xs = _moe_layer(xs, sc2, sh2, g2, vec(ln_g[i, 1]), vec(ln_b[i, 1]), rw_t, rb_col,
                        moe_w_gate, moe_w_up, moe_w_down, i)
    return xs.reshape(batch, s, d)
```

```python
import functools
import math

import jax
import jax.numpy as jnp
from jax import lax
from jax.experimental import pallas as pl
from jax.experimental.pallas import tpu as pltpu

F32 = jnp.float32
BF16 = jnp.bfloat16

D_MODEL = 2048
DEPTH = 4
N_MIXERS = 3
ATTN_HEADS = 16
HEAD_DIM = 128
HD = ATTN_HEADS * HEAD_DIM
DILATIONS = (1, 4, 16)
Q_BLOCK = 128
ROPE_THETA = 10000.0
CONV_KERNEL = 31
SHORT_CONV_KERNEL = 3
N_EXPERTS = 16
N_EXPERT_GROUPS = 4
EXPERTS_PER_GROUP = 4
TOP_K = 2
D_EXPERT = 1408
ALPHA = (2 * DEPTH) ** 0.25
LN_EPS = 1e-5
NEG_INF = -1e30

LANES_V7X = 128
SUBLANES_V7X = 8
MXU_COLS_V7X = 256
VMEM_LIMIT_V7X = 56 * 1024 * 1024
VMEM_LIMIT_FFN_V7X = 61 * 1024 * 1024
W_CHUNKS = 8
TM_PROJ = 512
TM_QKV = 1024
TN_PROJ = 1024
TN_GLU = 512
TM_OUT = 256
TM_ROUTE = 512
TM_MOE = 256
TM_COMB = 256
TN_ADA = 1024
HALO_CONV = 32
HALO_SC = 8
CONV_ROWS = 64
CONV_COLS = 256


def _cparams(sem):
    return pltpu.CompilerParams(dimension_semantics=sem, vmem_limit_bytes=VMEM_LIMIT_V7X)


def _ln_rows(z, g, b):
    mu = jnp.mean(z, axis=-1, keepdims=True)
    zc = z - mu
    var = jnp.mean(zc * zc, axis=-1, keepdims=True)
    return zc * lax.rsqrt(var + LN_EPS) * g + b


def _deepnorm(x, y, gate, g, b):
    return _ln_rows(ALPHA * x + (1.0 + gate) * y, g, b)


def _row_spec(tm, width):
    return pl.BlockSpec((tm, width), lambda i: (i, 0))


def _vec_spec(width):
    return pl.BlockSpec((1, width), lambda i: (0, 0))


def _adaln_kernel(c_ref, w_ref, b_ref, o_ref):
    c = c_ref[...]
    ca = c * jax.nn.sigmoid(c)
    o_ref[...] = jnp.sum(w_ref[...] * ca, axis=0, keepdims=True) + b_ref[...]


def _adaln(c_col, ada_w, ada_b):
    depth, d, n = ada_w.shape
    return pl.pallas_call(
        _adaln_kernel,
        grid=(depth, n // TN_ADA),
        in_specs=[pl.BlockSpec((d, 1), lambda l, j: (0, 0)),
                  pl.BlockSpec((None, d, TN_ADA), lambda l, j: (l, 0, j)),
                  pl.BlockSpec((None, 1, TN_ADA), lambda l, j: (l, 0, j))],
        out_specs=pl.BlockSpec((None, 1, TN_ADA), lambda l, j: (l, 0, j)),
        out_shape=jax.ShapeDtypeStruct((depth, 1, n), F32),
        compiler_params=_cparams(("arbitrary", "arbitrary")),
        name="adaln",
    )(c_col, ada_w, ada_b.reshape(depth, 1, n))


def _rope_kernel(pos_ref, freq_ref, cos_ref, sin_ref):
    ang = pos_ref[...].astype(F32) * freq_ref[...]
    lane = lax.broadcasted_iota(jnp.int32, ang.shape, 1)
    s = jnp.sin(ang)
    cos_ref[...] = jnp.cos(ang)
    sin_ref[...] = jnp.where(lane < HEAD_DIM // 2, -s, s)


def _rope_tables(pos_col, freq_row):
    s = pos_col.shape[0]
    tm = 1024
    spec = pl.BlockSpec((tm, HEAD_DIM), lambda i: (i, 0))
    return pl.pallas_call(
        _rope_kernel,
        grid=(s // tm,),
        in_specs=[pl.BlockSpec((tm, 1), lambda i: (i, 0)), _vec_spec(HEAD_DIM)],
        out_specs=[spec, spec],
        out_shape=[jax.ShapeDtypeStruct((s, HEAD_DIM), F32)] * 2,
        compiler_params=_cparams(("arbitrary",)),
        name="rope_tables",
    )(pos_col, freq_row)


def _qkv_kernel(x_ref, sc_ref, sh_ref, w_ref, cos_ref, sin_ref, o_ref, u_scr, acc_scr, *, d, n_rope):
    j = pl.program_id(1)
    nh, tm, _ = acc_scr.shape

    @pl.when(j == 0)
    def _():
        u_scr[...] = (x_ref[...] * (1.0 + sc_ref[...]) + sh_ref[...]).astype(BF16)

    is_rope = j < n_rope
    cos = cos_ref[...]
    sin = sin_ref[...]
    u = u_scr[...]
    for c in range(nh // 2):
        w = w_ref[:, c * MXU_COLS_V7X:(c + 1) * MXU_COLS_V7X].astype(BF16)
        acc = jnp.dot(u, w, preferred_element_type=F32)
        for h in (2 * c, 2 * c + 1):
            a = acc[:, (h - 2 * c) * HEAD_DIM:(h - 2 * c + 1) * HEAD_DIM]
            a = jnp.where(is_rope, a * cos + pltpu.roll(a, HEAD_DIM // 2, 1) * sin, a)
            hs = slice(h * HEAD_DIM, (h + 1) * HEAD_DIM)
            if d == 1:
                o_ref[0, :, hs] = a.astype(BF16)
            else:
                acc_scr[h] = a
                for r in range(d):
                    o_ref[r, :, hs] = acc_scr[h, pl.ds(r, tm // d, stride=d), :].astype(BF16)


def _qkv_proj(x, sc, sh, w_all, layer, cos, sin, group):
    s = x.shape[0]
    d = DILATIONS[group]
    tm, tn = TM_QKV, TN_PROJ
    ncol = 3 * HD
    col0 = group * (ncol // tn)
    return pl.pallas_call(
        functools.partial(_qkv_kernel, d=d, n_rope=2 * HD // tn),
        grid=(s // tm, ncol // tn),
        in_specs=[pl.BlockSpec((tm, D_MODEL), lambda i, j: (i, 0)),
                  pl.BlockSpec((1, D_MODEL), lambda i, j: (0, 0)),
                  pl.BlockSpec((1, D_MODEL), lambda i, j: (0, 0)),
                  pl.BlockSpec((None, D_MODEL, tn), lambda i, j: (layer, 0, col0 + j)),
                  pl.BlockSpec((tm, HEAD_DIM), lambda i, j: (i, 0)),
                  pl.BlockSpec((tm, HEAD_DIM), lambda i, j: (i, 0))],
        out_specs=pl.BlockSpec((d, tm // d, tn), lambda i, j: (0, i, j)),
        out_shape=jax.ShapeDtypeStruct((d, s // d, ncol), BF16),
        scratch_shapes=[pltpu.VMEM((tm, D_MODEL), BF16), pltpu.VMEM((tn // HEAD_DIM, tm, HEAD_DIM), F32)],
        compiler_params=_cparams(("arbitrary", "arbitrary")),
        name=f"qkv_proj_g{group}",
    )(x, sc, sh, w_all, cos, sin)


def _attn_kernel(q_ref, kc_ref, vc_ref, o_ref, lse_ref, k_scr, v_scr, s_scr, p_scr, *, nblk):
    b = pl.program_id(0)
    not_first = (b % nblk) != 0

    @pl.when(b == 0)
    def _():
        k_scr[0:Q_BLOCK] = jnp.zeros((Q_BLOCK, HD), BF16)
        v_scr[0:Q_BLOCK] = jnp.zeros((Q_BLOCK, HD), BF16)

    k_scr[Q_BLOCK:] = kc_ref[...]
    v_scr[Q_BLOCK:] = vc_ref[...]
    nt = (((1,), (1,)), ((), ()))
    for h in range(ATTN_HEADS):
        sl = slice(h * HEAD_DIM, (h + 1) * HEAD_DIM)
        s_scr[h] = lax.dot_general(q_ref[:, sl], k_scr[:, sl], nt, preferred_element_type=F32)

    row = lax.broadcasted_iota(jnp.int32, (Q_BLOCK, 2 * Q_BLOCK), 0)
    col = lax.broadcasted_iota(jnp.int32, (Q_BLOCK, 2 * Q_BLOCK), 1)
    prev_ok = jnp.logical_and(jnp.logical_and(col < Q_BLOCK, col >= row), not_first)
    valid = jnp.logical_or(prev_ok, jnp.logical_and(col >= Q_BLOCK, col - Q_BLOCK <= row))
    scale = 1.0 / math.sqrt(HEAD_DIM)
    s = jnp.where(valid[None], s_scr[...], NEG_INF)
    m = jnp.max(s, axis=2, keepdims=True)
    p = jnp.exp2((s - m) * (scale * math.log2(math.e)))
    den = jnp.sum(p, axis=2, keepdims=True)
    p_scr[...] = p.astype(BF16)
    inv = 1.0 / den
    lse = m * scale + jnp.log(den)
    for h in range(ATTN_HEADS):
        sl = slice(h * HEAD_DIM, (h + 1) * HEAD_DIM)
        o = jnp.dot(p_scr[h], v_scr[:, sl], preferred_element_type=F32)
        o_ref[:, sl] = (o * inv[h]).astype(BF16)
        lse_ref[:, h:h + 1] = lse[h]
    k_scr[0:Q_BLOCK] = kc_ref[...]
    v_scr[0:Q_BLOCK] = vc_ref[...]


def _attention(qkv, group):
    d, l, ncol = qkv.shape
    s = d * l
    flat = qkv.reshape(s, ncol)
    nblk = l // Q_BLOCK
    cur = lambda c: pl.BlockSpec((Q_BLOCK, HD), lambda b: (b, c))
    return pl.pallas_call(
        functools.partial(_attn_kernel, nblk=nblk),
        grid=(s // Q_BLOCK,),
        in_specs=[cur(0), cur(1), cur(2)],
        out_specs=[pl.BlockSpec((Q_BLOCK, HD), lambda b: (b, 0)),
                   pl.BlockSpec((Q_BLOCK, ATTN_HEADS), lambda b: (b, 0))],
        out_shape=[jax.ShapeDtypeStruct((s, HD), BF16), jax.ShapeDtypeStruct((s, ATTN_HEADS), F32)],
        scratch_shapes=[pltpu.VMEM((2 * Q_BLOCK, HD), BF16), pltpu.VMEM((2 * Q_BLOCK, HD), BF16),
                        pltpu.VMEM((ATTN_HEADS, Q_BLOCK, 2 * Q_BLOCK), F32),
                        pltpu.VMEM((ATTN_HEADS, Q_BLOCK, 2 * Q_BLOCK), BF16)],
        compiler_params=_cparams(("arbitrary",)),
        name=f"dil_attn_g{group}",
    )(flat, flat, flat)


def _attn_out_kernel(o0_ref, o1_ref, o2_ref, l0_ref, l1_ref, l2_ref, w_ref, x_ref, gate_ref, g_ref, b_ref,
                     out_ref, o_scr, a_scr):
    tm = x_ref.shape[0]
    for gi, o_ref in enumerate((o1_ref, o2_ref)):
        d = DILATIONS[gi + 1]
        for r in range(d):
            for h in range(ATTN_HEADS):
                o_scr[gi, h, pl.ds(r, tm // d, stride=d), :] = (
                    o_ref[r, :, h * HEAD_DIM:(h + 1) * HEAD_DIM].astype(F32))
    l0 = l0_ref[...]
    l1 = l1_ref[...]
    l2 = l2_ref[...]
    m = jnp.maximum(jnp.maximum(l0, l1), l2)
    e0 = jnp.exp(l0 - m)
    e1 = jnp.exp(l1 - m)
    e2 = jnp.exp(l2 - m)
    z = e0 + e1 + e2
    w0 = e0 / z
    w1 = e1 / z
    w2 = e2 / z
    for h in range(ATTN_HEADS):
        sl = slice(h * HEAD_DIM, (h + 1) * HEAD_DIM)
        a = (w0[:, h:h + 1] * o0_ref[:, sl].astype(F32)
             + w1[:, h:h + 1] * o_scr[0, h]
             + w2[:, h:h + 1] * o_scr[1, h])
        a_scr[:, sl] = a.astype(BF16)
    y = jnp.dot(a_scr[...], w_ref[...], preferred_element_type=F32)
    out_ref[...] = _deepnorm(x_ref[...], y, gate_ref[...], g_ref[...], b_ref[...])


def _attn_out(outs, lses, w_bf, x, gate, g, b):
    s = x.shape[0]
    tm = TM_OUT
    d1, d2 = DILATIONS[1], DILATIONS[2]
    o1 = outs[1].reshape(d1, s // d1, HD)
    o2 = outs[2].reshape(d2, s // d2, HD)
    l1 = lses[1].reshape(d1, s // d1, ATTN_HEADS).transpose(1, 0, 2).reshape(s, ATTN_HEADS)
    l2 = lses[2].reshape(d2, s // d2, ATTN_HEADS).transpose(1, 0, 2).reshape(s, ATTN_HEADS)
    perm = lambda d, w: pl.BlockSpec((d, tm // d, w), lambda i: (0, i, 0))
    return pl.pallas_call(
        _attn_out_kernel,
        grid=(s // tm,),
        in_specs=[_row_spec(tm, HD), perm(d1, HD), perm(d2, HD),
                  _row_spec(tm, ATTN_HEADS), _row_spec(tm, ATTN_HEADS), _row_spec(tm, ATTN_HEADS),
                  pl.BlockSpec((HD, D_MODEL), lambda i: (0, 0)),
                  _row_spec(tm, D_MODEL), _vec_spec(D_MODEL), _vec_spec(D_MODEL), _vec_spec(D_MODEL)],
        out_specs=_row_spec(tm, D_MODEL),
        out_shape=jax.ShapeDtypeStruct((s, D_MODEL), F32),
        scratch_shapes=[pltpu.VMEM((2, ATTN_HEADS, tm, HEAD_DIM), F32), pltpu.VMEM((tm, HD), BF16)],
        compiler_params=_cparams(("arbitrary",)),
        name="attn_out",
    )(outs[0], o1, o2, lses[0], l1, l2, w_bf, x, gate, g, b)


def _modulate_once(x_ref, sc_ref, sh_ref, u_scr):
    @pl.when(pl.program_id(1) == 0)
    def _():
        u_scr[...] = (x_ref[...] * (1.0 + sc_ref[...]) + sh_ref[...]).astype(BF16)


def _glu_in_kernel(x_ref, sc_ref, sh_ref, wa_ref, wg_ref, o_ref, u_scr):
    _modulate_once(x_ref, sc_ref, sh_ref, u_scr)
    u = u_scr[...]
    a = jnp.dot(u, wa_ref[...], preferred_element_type=F32)
    gt = jnp.dot(u, wg_ref[...], preferred_element_type=F32)
    o_ref[...] = a * jax.nn.sigmoid(gt)


def _sc_in_kernel(x_ref, sc_ref, sh_ref, wb_ref, wc_ref, wh_ref, b_ref, ch_ref, u_scr):
    _modulate_once(x_ref, sc_ref, sh_ref, u_scr)
    u = u_scr[...]
    b_ref[...] = jnp.dot(u, wb_ref[...], preferred_element_type=F32)
    c = jnp.dot(u, wc_ref[...], preferred_element_type=F32)
    h = jnp.dot(u, wh_ref[...], preferred_element_type=F32)
    ch_ref[...] = c * h


def _gated_in(kernel, n_parts, n_out, x, sc, sh, w_bf, name):
    s = x.shape[0]
    tm, tn = TM_PROJ, TN_GLU
    nj = D_MODEL // tn
    w_specs = [pl.BlockSpec((D_MODEL, tn), functools.partial(lambda i, j, p: (0, p * nj + j), p=p))
               for p in range(n_parts)]
    out_spec = pl.BlockSpec((tm, tn), lambda i, j: (i, j))
    out_shape = jax.ShapeDtypeStruct((s, D_MODEL), F32)
    return pl.pallas_call(
        kernel,
        grid=(s // tm, nj),
        in_specs=[pl.BlockSpec((tm, D_MODEL), lambda i, j: (i, 0)),
                  pl.BlockSpec((1, D_MODEL), lambda i, j: (0, 0)),
                  pl.BlockSpec((1, D_MODEL), lambda i, j: (0, 0))] + w_specs,
        out_specs=[out_spec] * n_out if n_out > 1 else out_spec,
        out_shape=[out_shape] * n_out if n_out > 1 else out_shape,
        scratch_shapes=[pltpu.VMEM((tm, D_MODEL), BF16)],
        compiler_params=_cparams(("arbitrary", "arbitrary")),
        name=name,
    )(x, sc, sh, *([w_bf] * n_parts))


def _fill_halo(hal_scr, prev_ref, cur_ref, halo):
    is_first = pl.program_id(0) == 0
    tm = cur_ref.shape[0]
    hal_scr[pl.ds(0, halo), :] = jnp.where(is_first, 0.0, prev_ref[...])
    hal_scr[pl.ds(halo, tm), :] = cur_ref[...]
    hal_scr[pl.ds(halo + tm, SUBLANES_V7X), :] = jnp.zeros((SUBLANES_V7X, D_MODEL), F32)


def _dwconv(hal_scr, w_ref, dst_scr, ph_scr, halo, ksize, tm):
    off = halo - (ksize - 1)
    win = CONV_ROWS + SUBLANES_V7X
    for r0 in range(0, tm, CONV_ROWS):
        for c0 in range(0, D_MODEL, CONV_COLS):
            cs = slice(c0, c0 + CONV_COLS)
            out = None
            for phase in range(SUBLANES_V7X):
                part = None
                for k in range(ksize):
                    if (off + k) % SUBLANES_V7X != phase:
                        continue
                    base = r0 + off + k - phase
                    term = w_ref[k:k + 1, cs] * hal_scr[base:base + win, cs]
                    part = term if part is None else part + term
                if part is None:
                    continue
                if phase == 0:
                    shifted = part[:CONV_ROWS]
                else:
                    ph_scr[phase] = part
                    shifted = ph_scr[phase, phase:phase + CONV_ROWS, :]
                out = shifted if out is None else out + shifted
            dst_scr[r0:r0 + CONV_ROWS, cs] = out


def _conv_out_kernel(cur_ref, prev_ref, wdw_ref, cg_ref, cb_ref, w_ref, x_ref, gate_ref, g_ref, b_ref,
                     out_ref, hal_scr, h_scr, ph_scr):
    tm = x_ref.shape[0]
    _fill_halo(hal_scr, prev_ref, cur_ref, HALO_CONV)
    _dwconv(hal_scr, wdw_ref, h_scr, ph_scr, HALO_CONV, CONV_KERNEL, tm)
    hn = _ln_rows(h_scr[...], cg_ref[...], cb_ref[...])
    a = (hn * jax.nn.sigmoid(hn)).astype(BF16)
    y = jnp.dot(a, w_ref[...], preferred_element_type=F32)
    out_ref[...] = _deepnorm(x_ref[...], y, gate_ref[...], g_ref[...], b_ref[...])


def _sc_out_kernel(cur_ref, prev_ref, bg_ref, wdw_ref, w_ref, x_ref, gate_ref, g_ref, b_ref,
                   out_ref, hal_scr, h_scr, ph_scr):
    tm = x_ref.shape[0]
    _fill_halo(hal_scr, prev_ref, cur_ref, HALO_SC)
    _dwconv(hal_scr, wdw_ref, h_scr, ph_scr, HALO_SC, SHORT_CONV_KERNEL, tm)
    a = (bg_ref[...] * h_scr[...]).astype(BF16)
    y = jnp.dot(a, w_ref[...], preferred_element_type=F32)
    out_ref[...] = _deepnorm(x_ref[...], y, gate_ref[...], g_ref[...], b_ref[...])


def _conv_phase_scratch():
    return pltpu.VMEM((SUBLANES_V7X, CONV_ROWS + SUBLANES_V7X, CONV_COLS), F32)


def _halo_spec(tm, halo):
    per = tm // halo
    return pl.BlockSpec((halo, D_MODEL), lambda i: (jnp.maximum(i * per - 1, 0), 0))


def _conv_out(glu, w_dw, cg, cb, w_bf, x, gate, g, b):
    s = x.shape[0]
    tm = TM_OUT
    full = lambda shape: pl.BlockSpec(shape, lambda i: (0, 0))
    return pl.pallas_call(
        _conv_out_kernel,
        grid=(s // tm,),
        in_specs=[_row_spec(tm, D_MODEL), _halo_spec(tm, HALO_CONV), full((CONV_KERNEL, D_MODEL)),
                  _vec_spec(D_MODEL), _vec_spec(D_MODEL), full((D_MODEL, D_MODEL)),
                  _row_spec(tm, D_MODEL), _vec_spec(D_MODEL), _vec_spec(D_MODEL), _vec_spec(D_MODEL)],
        out_specs=_row_spec(tm, D_MODEL),
        out_shape=jax.ShapeDtypeStruct((s, D_MODEL), F32),
        scratch_shapes=[pltpu.VMEM((tm + HALO_CONV + SUBLANES_V7X, D_MODEL), F32), pltpu.VMEM((tm, D_MODEL), F32),
                        _conv_phase_scratch()],
        compiler_params=_cparams(("arbitrary",)),
        name="conv_out",
    )(glu, glu, w_dw, cg, cb, w_bf, x, gate, g, b)


def _sc_out(ch, bgate, w_dw, w_bf, x, gate, g, b):
    s = x.shape[0]
    tm = TM_OUT
    full = lambda shape: pl.BlockSpec(shape, lambda i: (0, 0))
    return pl.pallas_call(
        _sc_out_kernel,
        grid=(s // tm,),
        in_specs=[_row_spec(tm, D_MODEL), _halo_spec(tm, HALO_SC), _row_spec(tm, D_MODEL),
                  full((SHORT_CONV_KERNEL, D_MODEL)), full((D_MODEL, D_MODEL)),
                  _row_spec(tm, D_MODEL), _vec_spec(D_MODEL), _vec_spec(D_MODEL), _vec_spec(D_MODEL)],
        out_specs=_row_spec(tm, D_MODEL),
        out_shape=jax.ShapeDtypeStruct((s, D_MODEL), F32),
        scratch_shapes=[pltpu.VMEM((tm + HALO_SC + SUBLANES_V7X, D_MODEL), F32), pltpu.VMEM((tm, D_MODEL), F32),
                        _conv_phase_scratch()],
        compiler_params=_cparams(("arbitrary",)),
        name="sc_out",
    )(ch, ch, bgate, w_dw, w_bf, x, gate, g, b)


def _argmax4_first(v):
    i01 = jnp.where(v[1] > v[0], 1, 0)
    m01 = jnp.maximum(v[0], v[1])
    i23 = jnp.where(v[3] > v[2], 3, 2)
    m23 = jnp.maximum(v[2], v[3])
    return jnp.where(m23 > m01, i23, i01), jnp.maximum(m01, m23)


def _select4(idx, v):
    return jnp.where(idx == 0, v[0], jnp.where(idx == 1, v[1], jnp.where(idx == 2, v[2], v[3])))


def _router_kernel(x_ref, sc_ref, sh_ref, wt_ref, rb_ref, e_ref, gt_ref, rk_ref, cnt_ref, carry_scr):
    i = pl.program_id(0)
    tm = x_ref.shape[0]

    @pl.when(i == 0)
    def _():
        carry_scr[...] = jnp.zeros_like(carry_scr)

    u = x_ref[...] * (1.0 + sc_ref[...]) + sh_ref[...]
    logits = lax.dot_general(wt_ref[...], u, (((1,), (1,)), ((), ())), precision=lax.Precision.HIGHEST,
                             preferred_element_type=F32) + rb_ref[...]
    ex = jnp.exp(logits - jnp.max(logits, axis=0, keepdims=True))
    probs = ex / jnp.sum(ex, axis=0, keepdims=True)

    top1_i, top1_v, top2_i, top2_v, score = [], [], [], [], []
    for grp in range(N_EXPERT_GROUPS):
        p = [probs[grp * EXPERTS_PER_GROUP + j:grp * EXPERTS_PER_GROUP + j + 1, :]
             for j in range(EXPERTS_PER_GROUP)]
        i1, v1 = _argmax4_first(p)
        rest = [jnp.where(i1 == j, -1.0, p[j]) for j in range(EXPERTS_PER_GROUP)]
        i2, v2 = _argmax4_first(rest)
        top1_i.append(i1)
        top1_v.append(v1)
        top2_i.append(i2)
        top2_v.append(v2)
        score.append(v1 + v2)
    gsel, _ = _argmax4_first(score)
    p1 = _select4(gsel, top1_v)
    p2 = _select4(gsel, top2_v)
    e1 = gsel * EXPERTS_PER_GROUP + _select4(gsel, top1_i)
    e2 = gsel * EXPERTS_PER_GROUP + _select4(gsel, top2_i)
    psum = p1 + p2
    e_ref[0:1, :] = e1
    e_ref[1:2, :] = e2
    gt_ref[0:1, :] = p1 / psum
    gt_ref[1:2, :] = p2 / psum

    eid = lax.broadcasted_iota(jnp.int32, (N_EXPERTS, tm), 0)
    earlier = (lax.broadcasted_iota(jnp.int32, (tm, tm), 0)
               < lax.broadcasted_iota(jnp.int32, (tm, tm), 1)).astype(BF16)
    base = carry_scr[:, 0:1]
    for slot, e_sel in enumerate((e1, e2)):
        onehot = (eid == e_sel).astype(F32)
        before = jnp.dot(onehot.astype(BF16), earlier, preferred_element_type=F32)
        rank = jnp.sum(onehot * (before + base), axis=0, keepdims=True)
        rk_ref[slot:slot + 1, :] = rank.astype(jnp.int32)
        base = base + jnp.sum(onehot, axis=1, keepdims=True)
    carry_scr[...] = jnp.broadcast_to(base, carry_scr.shape)
    cnt_ref[...] = carry_scr[...]


def _router(x, sc, sh, rw_t, rb_col):
    s = x.shape[0]
    tm = TM_ROUTE
    slot_spec = pl.BlockSpec((TOP_K, tm), lambda i: (0, i))
    full = lambda shape: pl.BlockSpec(shape, lambda i: (0, 0))
    return pl.pallas_call(
        _router_kernel,
        grid=(s // tm,),
        in_specs=[_row_spec(tm, D_MODEL), _vec_spec(D_MODEL), _vec_spec(D_MODEL),
                  full((N_EXPERTS, D_MODEL)), full((N_EXPERTS, 1))],
        out_specs=[slot_spec, slot_spec, slot_spec, full((N_EXPERTS, LANES_V7X))],
        out_shape=[jax.ShapeDtypeStruct((TOP_K, s), jnp.int32), jax.ShapeDtypeStruct((TOP_K, s), F32),
                   jax.ShapeDtypeStruct((TOP_K, s), jnp.int32),
                   jax.ShapeDtypeStruct((N_EXPERTS, LANES_V7X), F32)],
        scratch_shapes=[pltpu.VMEM((N_EXPERTS, LANES_V7X), F32)],
        compiler_params=_cparams(("arbitrary",)),
        name="router",
    )(x, sc, sh, rw_t, rb_col)


def _row_gather(src_hbm, idx_ref, base, n, dst, sem):
    for r in range(n):
        tok = idx_ref[base + r]
        pltpu.make_async_copy(src_hbm.at[pl.ds(tok, 1), :], dst.at[pl.ds(r, 1), :], sem).start()


def _row_gather_wait(src_hbm, n, dst, sem):
    pltpu.make_async_copy(src_hbm.at[pl.ds(0, n), :], dst, sem).wait()


class _WeightStream:
    def __init__(self, layer, w_hbm, stage, wbuf, sem):
        self.layer, self.w_hbm, self.stage, self.wbuf, self.sem = layer, w_hbm, stage, wbuf, sem

    def _copy(self, mat, expert, q, slot):
        kind = 0 if mat < 2 else 1
        rows = self.stage[kind].shape[1]
        src = self.w_hbm[mat].at[self.layer, expert, pl.ds(pl.multiple_of(q * rows, rows), rows), :]
        return pltpu.make_async_copy(src, self.stage[kind].at[slot], self.sem.at[kind, slot])

    def _per_matrix(self, c, fn):
        mat, q, slot = c // W_CHUNKS, c % W_CHUNKS, c % 2
        for m in range(3):
            @pl.when(mat == m)
            def _(m=m):
                fn(m, q, slot)

    def start(self, c, expert):
        self._per_matrix(c, lambda m, q, slot: self._copy(m, expert, q, slot).start())

    def finish(self, c, dst):
        def fn(m, q, slot):
            self._copy(m, 0, q, slot).wait()
            kind = 0 if m < 2 else 1
            rows = self.stage[kind].shape[1]
            self.wbuf[m][dst, pl.ds(pl.multiple_of(q * rows, rows), rows), :] = self.stage[kind][slot].astype(BF16)

        self._per_matrix(c, fn)

    def prefetch(self, expert, lo, hi):
        for t in range(2):
            @pl.when(lo + t < hi)
            def _(t=t):
                self.start(lo + t, expert)

    def drain(self, expert, dst, lo, upto, hi):
        def body(c, carry):
            self.finish(c, dst)

            @pl.when(c + 2 < hi)
            def _():
                self.start(c + 2, expert)

            return carry

        lax.fori_loop(lo, upto, body, 0)


def _ffn_kernel(rt_ref, be_ref, nbu_ref, ws_ref, nx_ref, lo_ref, hi_ref,
                x_hbm, sc_ref, sh_ref, wg_hbm, wu_hbm, wd_hbm, o_ref,
                xbuf, sem, wg_buf, wu_buf, wd_buf, stage_in, stage_dn, wsem, *, layer):
    i = pl.program_id(0)
    nbu = nbu_ref[0]
    tm = xbuf.shape[1]
    slot = i % 2
    stream = _WeightStream(layer, (wg_hbm, wu_hbm, wd_hbm), (stage_in, stage_dn), (wg_buf, wu_buf, wd_buf), wsem)
    n_chunks = 3 * W_CHUNKS

    @pl.when(i == 0)
    def _():
        _row_gather(x_hbm, rt_ref, 0, tm, xbuf.at[0], sem.at[0])
        stream.prefetch(be_ref[0], 0, n_chunks)
        stream.drain(be_ref[0], ws_ref[0], 0, n_chunks, n_chunks)

    @pl.when(i < nbu)
    def _():
        ws, nxt, lo, hi = ws_ref[i], nx_ref[i], lo_ref[i], hi_ref[i]
        stream.prefetch(nxt, lo, hi)
        _row_gather_wait(x_hbm, tm, xbuf.at[slot], sem.at[slot])
        _row_gather(x_hbm, rt_ref, (i + 1) * tm, tm, xbuf.at[1 - slot], sem.at[1 - slot])
        u = (xbuf[slot] * (1.0 + sc_ref[...]) + sh_ref[...]).astype(BF16)
        mid1, mid2 = jnp.minimum(lo + 2, hi), jnp.minimum(lo + 4, hi)
        gt = jnp.dot(u, wg_buf[ws], preferred_element_type=F32)
        stream.drain(nxt, 1 - ws, lo, mid1, hi)
        up = jnp.dot(u, wu_buf[ws], preferred_element_type=F32)
        stream.drain(nxt, 1 - ws, mid1, mid2, hi)
        h = (gt * jax.nn.sigmoid(gt) * up).astype(BF16)
        o_ref[...] = jnp.dot(h, wd_buf[ws], preferred_element_type=F32)
        stream.drain(nxt, 1 - ws, mid2, hi, hi)

    @pl.when(i >= nbu)
    def _():
        @pl.when(i == nbu)
        def _():
            _row_gather_wait(x_hbm, tm, xbuf.at[slot], sem.at[slot])

        o_ref[...] = jnp.zeros_like(o_ref)


def _moe_ffn(x, sc, sh, w_gate, w_up, w_down, layer, row_token, block_expert, nb_used, w_slot, next_expert,
             chunk_lo, chunk_hi):
    tm = TM_MOE
    p_rows = row_token.shape[0]
    nb = p_rows // tm
    vec = pl.BlockSpec((1, D_MODEL), lambda i, *_: (0, 0))
    hbm = pl.BlockSpec(memory_space=pl.ANY)
    return pl.pallas_call(
        functools.partial(_ffn_kernel, layer=layer),
        grid_spec=pltpu.PrefetchScalarGridSpec(
            num_scalar_prefetch=7,
            grid=(nb,),
            in_specs=[hbm, vec, vec, hbm, hbm, hbm],
            out_specs=pl.BlockSpec((tm, D_MODEL), lambda i, *_: (i, 0)),
            scratch_shapes=[pltpu.VMEM((2, tm, D_MODEL), F32), pltpu.SemaphoreType.DMA((2,)),
                            pltpu.VMEM((2, D_MODEL, D_EXPERT), BF16), pltpu.VMEM((2, D_MODEL, D_EXPERT), BF16),
                            pltpu.VMEM((2, D_EXPERT, D_MODEL), BF16),
                            pltpu.VMEM((2, D_MODEL // W_CHUNKS, D_EXPERT), F32),
                            pltpu.VMEM((2, D_EXPERT // W_CHUNKS, D_MODEL), F32),
                            pltpu.SemaphoreType.DMA((2, 2))]),
        out_shape=jax.ShapeDtypeStruct((p_rows, D_MODEL), F32),
        compiler_params=pltpu.CompilerParams(dimension_semantics=("arbitrary",),
                                             vmem_limit_bytes=VMEM_LIMIT_FFN_V7X),
        name="moe_ffn",
    )(row_token, block_expert, nb_used, w_slot, next_expert, chunk_lo, chunk_hi,
      x, sc, sh, w_gate, w_up, w_down)


def _combine_kernel(dest_ref, rows_hbm, gt_ref, x_ref, gate_ref, g_ref, b_ref, out_ref, ybuf, sem):
    i = pl.program_id(0)
    n = pl.num_programs(0)
    tm = x_ref.shape[0]
    s = tm * n
    slot = i % 2

    def start(tile, to):
        for k in range(TOP_K):
            _row_gather(rows_hbm, dest_ref, k * s + tile * tm, tm, ybuf.at[to, k], sem.at[to, k])

    def wait(at):
        for k in range(TOP_K):
            _row_gather_wait(rows_hbm, tm, ybuf.at[at, k], sem.at[at, k])

    @pl.when(i == 0)
    def _():
        start(0, 0)

    wait(slot)
    start(jnp.minimum(i + 1, n - 1), 1 - slot)
    gt = gt_ref[...]
    y = gt[:, 0:1] * ybuf[slot, 0] + gt[:, 1:2] * ybuf[slot, 1]
    out_ref[...] = _deepnorm(x_ref[...], y, gate_ref[...], g_ref[...], b_ref[...])

    @pl.when(i == n - 1)
    def _():
        wait(1 - slot)


def _moe_combine(dest_flat, out_rows, gates_t, x, gate, g, b):
    s = x.shape[0]
    tm = TM_COMB
    vec = pl.BlockSpec((1, D_MODEL), lambda i, dst: (0, 0))
    return pl.pallas_call(
        _combine_kernel,
        grid_spec=pltpu.PrefetchScalarGridSpec(
            num_scalar_prefetch=1,
            grid=(s // tm,),
            in_specs=[pl.BlockSpec(memory_space=pl.ANY),
                      pl.BlockSpec((tm, TOP_K), lambda i, dst: (i, 0)),
                      pl.BlockSpec((tm, D_MODEL), lambda i, dst: (i, 0)), vec, vec, vec],
            out_specs=pl.BlockSpec((tm, D_MODEL), lambda i, dst: (i, 0)),
            scratch_shapes=[pltpu.VMEM((2, TOP_K, tm, D_MODEL), F32), pltpu.SemaphoreType.DMA((2, TOP_K))]),
        out_shape=jax.ShapeDtypeStruct((s, D_MODEL), F32),
        compiler_params=_cparams(("arbitrary",)),
        name="moe_combine",
    )(dest_flat, out_rows, gates_t, x, gate, g, b)


def _moe_layer(x, sc, sh, gate, g, b, rw_t, rb_col, w_gate, w_up, w_down, layer):
    s = x.shape[0]
    tm = TM_MOE
    expert, gates, rank, cnt = _router(x, sc, sh, rw_t, rb_col)
    counts = cnt[:, 0].astype(jnp.int32)
    padded = (counts + tm - 1) // tm * tm
    pad_end = jnp.cumsum(padded)
    pad_start = pad_end - padded
    eids = jnp.arange(N_EXPERTS, dtype=jnp.int32)
    start_of = jnp.sum(jnp.where(expert[..., None] == eids, pad_start, 0), axis=-1)
    dest = (start_of + rank).reshape(-1)
    p_rows = TOP_K * s + N_EXPERTS * tm
    token = jnp.tile(jnp.arange(s, dtype=jnp.int32), TOP_K)
    row_token = jnp.zeros((p_rows,), jnp.int32).at[dest].set(token)
    nb = p_rows // tm
    block_row0 = jnp.arange(nb, dtype=jnp.int32) * tm
    block_expert = jnp.minimum(jnp.sum(block_row0[:, None] >= pad_end[None, :], axis=-1),
                               N_EXPERTS - 1).astype(jnp.int32)
    nb_used = (pad_end[-1:] // tm).astype(jnp.int32)
    has = padded > 0
    order = jnp.cumsum(has.astype(jnp.int32)) - 1
    later = jnp.logical_and(eids[None, :] > eids[:, None], has[None, :])
    next_of = jnp.min(jnp.where(later, eids[None, :], N_EXPERTS), axis=-1)
    of_block = lambda v: jnp.sum(jnp.where(block_expert[:, None] == eids[None, :], v[None, :], 0), axis=-1)
    run_len = jnp.maximum(of_block(padded) // tm, 1)
    run_pos = jnp.arange(nb, dtype=jnp.int32) - of_block(pad_start) // tm
    nxt = of_block(next_of)
    streams = jnp.logical_and(nxt < N_EXPERTS, jnp.arange(nb) < nb_used[0])
    n_chunks = 3 * W_CHUNKS
    chunk_lo = jnp.where(streams, n_chunks * run_pos // run_len, 0).astype(jnp.int32)
    chunk_hi = jnp.where(streams, n_chunks * (run_pos + 1) // run_len, 0).astype(jnp.int32)
    next_expert = jnp.where(streams, nxt, block_expert).astype(jnp.int32)
    w_slot = (of_block(order) % 2).astype(jnp.int32)
    out_rows = _moe_ffn(x, sc, sh, w_gate, w_up, w_down, layer, row_token, block_expert, nb_used,
                        w_slot, next_expert, chunk_lo, chunk_hi)
    return _moe_combine(dest.astype(jnp.int32), out_rows, gates.T, x, gate, g, b)


def kernel(x, c, positions, ada_w, ada_b, ln_g, ln_b, attn_w_qkv, attn_w_o, conv_w_pw1, conv_w_dw, conv_ln_g,
           conv_ln_b, conv_w_pw2, sc_w_in, sc_w_conv, sc_w_out, router_w, router_b, moe_w_gate, moe_w_up,
           moe_w_down):
    batch, s, d = x.shape
    assert batch == 1 and d == D_MODEL and s % (DILATIONS[-1] * Q_BLOCK) == 0
    xs = x.reshape(s, d)

    mod = _adaln(c.reshape(d, 1), ada_w, ada_b)
    half = HEAD_DIM // 2
    inv_freq = ROPE_THETA ** (-jnp.arange(half, dtype=F32) / half)
    freq_row = jnp.concatenate([inv_freq, inv_freq]).reshape(1, HEAD_DIM)
    cos, sin = _rope_tables(positions.reshape(s, 1), freq_row)

    rw_t = router_w.T
    rb_col = router_b.reshape(N_EXPERTS, 1)
    vec = lambda a: a.reshape(1, d)

    for i in range(DEPTH):
        sh1, sc1, g1, sh2, sc2, g2 = [mod[i, :, k * d:(k + 1) * d] for k in range(6)]
        m, j = i % N_MIXERS, i // N_MIXERS
        lg, lb = vec(ln_g[i, 0]), vec(ln_b[i, 0])
        if m == 0:
            outs, lses = [], []
            for grp in range(len(DILATIONS)):
                o_g, l_g = _attention(_qkv_proj(xs, sc1, sh1, attn_w_qkv, j, cos, sin, grp), grp)
                outs.append(o_g)
                lses.append(l_g)
            xs = _attn_out(outs, lses, attn_w_o[j].astype(BF16), xs, g1, lg, lb)
        elif m == 1:
            glu = _gated_in(_glu_in_kernel, 2, 1, xs, sc1, sh1, conv_w_pw1[j].astype(BF16), "conv_in")
            xs = _conv_out(glu, conv_w_dw[j], vec(conv_ln_g[j]), vec(conv_ln_b[j]),
                           conv_w_pw2[j].astype(BF16), xs, g1, lg, lb)
        else:
            bgate, ch = _gated_in(_sc_in_kernel, 3, 2, xs, sc1, sh1, sc_w_in[j].astype(BF16), "sc_in")
            xs = _sc_out(ch, bgate, sc_w_conv[j], sc_w_out[j].astype(BF16), xs, g1, lg, lb)
        xs = _moe_layer(xs, sc2, sh2, g2, vec(ln_g[i, 1]), vec(ln_b[i, 1]), rw_t, rb_col,
                        moe_w_gate, moe_w_up, moe_w_down, i)
    return xs.reshape(batch, s, d)
```

```python
import functools
import math

import jax
import jax.numpy as jnp
from jax import lax
from jax.experimental import pallas as pl
from jax.experimental.pallas import tpu as pltpu

F32 = jnp.float32
BF16 = jnp.bfloat16

D_MODEL = 2048
DEPTH = 4
N_MIXERS = 3
ATTN_HEADS = 16
HEAD_DIM = 128
HD = ATTN_HEADS * HEAD_DIM
DILATIONS = (1, 4, 16)
Q_BLOCK = 128
ROPE_THETA = 10000.0
CONV_KERNEL = 31
SHORT_CONV_KERNEL = 3
N_EXPERTS = 16
N_EXPERT_GROUPS = 4
EXPERTS_PER_GROUP = 4
TOP_K = 2
D_EXPERT = 1408
ALPHA = (2 * DEPTH) ** 0.25
LN_EPS = 1e-5
NEG_INF = -1e30

LANES_V7X = 128
SUBLANES_V7X = 8
MXU_COLS_V7X = 256
VMEM_LIMIT_V7X = 56 * 1024 * 1024
VMEM_LIMIT_FFN_V7X = 61 * 1024 * 1024
W_CHUNKS = 8
TM_PROJ = 512
TM_QKV = 1024
TN_PROJ = 1024
TN_GLU = 1024
TM_OUT = 256
TM_ROUTE = 512
TM_MOE = 256
TM_COMB = 256
TN_ADA = 1024
HALO_CONV = 32
HALO_SC = 8
CONV_ROWS = 64
CONV_COLS = 256


def _cparams(sem):
    return pltpu.CompilerParams(dimension_semantics=sem, vmem_limit_bytes=VMEM_LIMIT_V7X)


def _ln_rows(z, g, b):
    mu = jnp.mean(z, axis=-1, keepdims=True)
    zc = z - mu
    var = jnp.mean(zc * zc, axis=-1, keepdims=True)
    return zc * lax.rsqrt(var + LN_EPS) * g + b


def _deepnorm(x, y, gate, g, b):
    return _ln_rows(ALPHA * x + (1.0 + gate) * y, g, b)


def _row_spec(tm, width):
    return pl.BlockSpec((tm, width), lambda i: (i, 0))


def _vec_spec(width):
    return pl.BlockSpec((1, width), lambda i: (0, 0))


def _adaln_kernel(c_ref, w_ref, b_ref, o_ref):
    c = c_ref[...]
    ca = c * jax.nn.sigmoid(c)
    o_ref[...] = jnp.sum(w_ref[...] * ca, axis=0, keepdims=True) + b_ref[...]


def _adaln(c_col, ada_w, ada_b):
    depth, d, n = ada_w.shape
    return pl.pallas_call(
        _adaln_kernel,
        grid=(depth, n // TN_ADA),
        in_specs=[pl.BlockSpec((d, 1), lambda l, j: (0, 0)),
                  pl.BlockSpec((None, d, TN_ADA), lambda l, j: (l, 0, j)),
                  pl.BlockSpec((None, 1, TN_ADA), lambda l, j: (l, 0, j))],
        out_specs=pl.BlockSpec((None, 1, TN_ADA), lambda l, j: (l, 0, j)),
        out_shape=jax.ShapeDtypeStruct((depth, 1, n), F32),
        compiler_params=_cparams(("arbitrary", "arbitrary")),
        name="adaln",
    )(c_col, ada_w, ada_b.reshape(depth, 1, n))


def _rope_kernel(pos_ref, freq_ref, cos_ref, sin_ref):
    ang = pos_ref[...].astype(F32) * freq_ref[...]
    lane = lax.broadcasted_iota(jnp.int32, ang.shape, 1)
    s = jnp.sin(ang)
    cos_ref[...] = jnp.cos(ang)
    sin_ref[...] = jnp.where(lane < HEAD_DIM // 2, -s, s)


def _rope_tables(pos_col, freq_row):
    s = pos_col.shape[0]
    tm = 1024
    spec = pl.BlockSpec((tm, HEAD_DIM), lambda i: (i, 0))
    return pl.pallas_call(
        _rope_kernel,
        grid=(s // tm,),
        in_specs=[pl.BlockSpec((tm, 1), lambda i: (i, 0)), _vec_spec(HEAD_DIM)],
        out_specs=[spec, spec],
        out_shape=[jax.ShapeDtypeStruct((s, HEAD_DIM), F32)] * 2,
        compiler_params=_cparams(("arbitrary",)),
        name="rope_tables",
    )(pos_col, freq_row)


def _qkv_kernel(x_ref, sc_ref, sh_ref, w_ref, cos_ref, sin_ref, o_ref, u_scr, acc_scr, *, d, n_rope):
    j = pl.program_id(1)
    nh, tm, _ = acc_scr.shape

    @pl.when(j == 0)
    def _():
        u_scr[...] = (x_ref[...] * (1.0 + sc_ref[...]) + sh_ref[...]).astype(BF16)

    is_rope = j < n_rope
    cos = cos_ref[...]
    sin = sin_ref[...]
    u = u_scr[...]
    for c in range(nh // 2):
        w = w_ref[:, c * MXU_COLS_V7X:(c + 1) * MXU_COLS_V7X].astype(BF16)
        acc = jnp.dot(u, w, preferred_element_type=F32)
        for h in (2 * c, 2 * c + 1):
            a = acc[:, (h - 2 * c) * HEAD_DIM:(h - 2 * c + 1) * HEAD_DIM]
            a = jnp.where(is_rope, a * cos + pltpu.roll(a, HEAD_DIM // 2, 1) * sin, a)
            hs = slice(h * HEAD_DIM, (h + 1) * HEAD_DIM)
            if d == 1:
                o_ref[0, :, hs] = a.astype(BF16)
            else:
                acc_scr[h] = a
                for r in range(d):
                    o_ref[r, :, hs] = acc_scr[h, pl.ds(r, tm // d, stride=d), :].astype(BF16)


def _qkv_proj(x, sc, sh, w_all, layer, cos, sin, group):
    s = x.shape[0]
    d = DILATIONS[group]
    tm, tn = TM_QKV, TN_PROJ
    ncol = 3 * HD
    col0 = group * (ncol // tn)
    return pl.pallas_call(
        functools.partial(_qkv_kernel, d=d, n_rope=2 * HD // tn),
        grid=(s // tm, ncol // tn),
        in_specs=[pl.BlockSpec((tm, D_MODEL), lambda i, j: (i, 0)),
                  pl.BlockSpec((1, D_MODEL), lambda i, j: (0, 0)),
                  pl.BlockSpec((1, D_MODEL), lambda i, j: (0, 0)),
                  pl.BlockSpec((None, D_MODEL, tn), lambda i, j: (layer, 0, col0 + j)),
                  pl.BlockSpec((tm, HEAD_DIM), lambda i, j: (i, 0)),
                  pl.BlockSpec((tm, HEAD_DIM), lambda i, j: (i, 0))],
        out_specs=pl.BlockSpec((d, tm // d, tn), lambda i, j: (0, i, j)),
        out_shape=jax.ShapeDtypeStruct((d, s // d, ncol), BF16),
        scratch_shapes=[pltpu.VMEM((tm, D_MODEL), BF16), pltpu.VMEM((tn // HEAD_DIM, tm, HEAD_DIM), F32)],
        compiler_params=_cparams(("arbitrary", "arbitrary")),
        name=f"qkv_proj_g{group}",
    )(x, sc, sh, w_all, cos, sin)


def _attn_kernel(q_ref, kc_ref, vc_ref, o_ref, lse_ref, k_scr, v_scr, s_scr, p_scr, *, nblk):
    b = pl.program_id(0)
    not_first = (b % nblk) != 0

    @pl.when(b == 0)
    def _():
        k_scr[0:Q_BLOCK] = jnp.zeros((Q_BLOCK, HD), BF16)
        v_scr[0:Q_BLOCK] = jnp.zeros((Q_BLOCK, HD), BF16)

    k_scr[Q_BLOCK:] = kc_ref[...]
    v_scr[Q_BLOCK:] = vc_ref[...]
    nt = (((1,), (1,)), ((), ()))
    for h in range(ATTN_HEADS):
        sl = slice(h * HEAD_DIM, (h + 1) * HEAD_DIM)
        s_scr[h] = lax.dot_general(q_ref[:, sl], k_scr[:, sl], nt, preferred_element_type=F32)

    row = lax.broadcasted_iota(jnp.int32, (Q_BLOCK, 2 * Q_BLOCK), 0)
    col = lax.broadcasted_iota(jnp.int32, (Q_BLOCK, 2 * Q_BLOCK), 1)
    prev_ok = jnp.logical_and(jnp.logical_and(col < Q_BLOCK, col >= row), not_first)
    valid = jnp.logical_or(prev_ok, jnp.logical_and(col >= Q_BLOCK, col - Q_BLOCK <= row))
    scale = 1.0 / math.sqrt(HEAD_DIM)
    s = jnp.where(valid[None], s_scr[...], NEG_INF)
    m = jnp.max(s, axis=2, keepdims=True)
    p = jnp.exp2((s - m) * (scale * math.log2(math.e)))
    den = jnp.sum(p, axis=2, keepdims=True)
    p_scr[...] = p.astype(BF16)
    inv = 1.0 / den
    lse = m * scale + jnp.log(den)
    for h in range(ATTN_HEADS):
        sl = slice(h * HEAD_DIM, (h + 1) * HEAD_DIM)
        o = jnp.dot(p_scr[h], v_scr[:, sl], preferred_element_type=F32)
        o_ref[:, sl] = (o * inv[h]).astype(BF16)
        lse_ref[:, h:h + 1] = lse[h]
    k_scr[0:Q_BLOCK] = kc_ref[...]
    v_scr[0:Q_BLOCK] = vc_ref[...]


def _attention(qkv, group):
    d, l, ncol = qkv.shape
    s = d * l
    flat = qkv.reshape(s, ncol)
    nblk = l // Q_BLOCK
    cur = lambda c: pl.BlockSpec((Q_BLOCK, HD), lambda b: (b, c))
    return pl.pallas_call(
        functools.partial(_attn_kernel, nblk=nblk),
        grid=(s // Q_BLOCK,),
        in_specs=[cur(0), cur(1), cur(2)],
        out_specs=[pl.BlockSpec((Q_BLOCK, HD), lambda b: (b, 0)),
                   pl.BlockSpec((Q_BLOCK, ATTN_HEADS), lambda b: (b, 0))],
        out_shape=[jax.ShapeDtypeStruct((s, HD), BF16), jax.ShapeDtypeStruct((s, ATTN_HEADS), F32)],
        scratch_shapes=[pltpu.VMEM((2 * Q_BLOCK, HD), BF16), pltpu.VMEM((2 * Q_BLOCK, HD), BF16),
                        pltpu.VMEM((ATTN_HEADS, Q_BLOCK, 2 * Q_BLOCK), F32),
                        pltpu.VMEM((ATTN_HEADS, Q_BLOCK, 2 * Q_BLOCK), BF16)],
        compiler_params=_cparams(("arbitrary",)),
        name=f"dil_attn_g{group}",
    )(flat, flat, flat)


def _attn_out_kernel(o0_ref, o1_ref, o2_ref, l0_ref, l1_ref, l2_ref, w_ref, x_ref, gate_ref, g_ref, b_ref,
                     out_ref, o_scr, a_scr):
    tm = x_ref.shape[0]
    for gi, o_ref in enumerate((o1_ref, o2_ref)):
        d = DILATIONS[gi + 1]
        for r in range(d):
            for h in range(ATTN_HEADS):
                o_scr[gi, h, pl.ds(r, tm // d, stride=d), :] = (
                    o_ref[r, :, h * HEAD_DIM:(h + 1) * HEAD_DIM].astype(F32))
    l0 = l0_ref[...]
    l1 = l1_ref[...]
    l2 = l2_ref[...]
    m = jnp.maximum(jnp.maximum(l0, l1), l2)
    e0 = jnp.exp(l0 - m)
    e1 = jnp.exp(l1 - m)
    e2 = jnp.exp(l2 - m)
    z = e0 + e1 + e2
    w0 = e0 / z
    w1 = e1 / z
    w2 = e2 / z
    for h in range(ATTN_HEADS):
        sl = slice(h * HEAD_DIM, (h + 1) * HEAD_DIM)
        a = (w0[:, h:h + 1] * o0_ref[:, sl].astype(F32)
             + w1[:, h:h + 1] * o_scr[0, h]
             + w2[:, h:h + 1] * o_scr[1, h])
        a_scr[:, sl] = a.astype(BF16)
    y = jnp.dot(a_scr[...], w_ref[...], preferred_element_type=F32)
    out_ref[...] = _deepnorm(x_ref[...], y, gate_ref[...], g_ref[...], b_ref[...])


def _attn_out(outs, lses, w_bf, x, gate, g, b):
    s = x.shape[0]
    tm = TM_OUT
    d1, d2 = DILATIONS[1], DILATIONS[2]
    o1 = outs[1].reshape(d1, s // d1, HD)
    o2 = outs[2].reshape(d2, s // d2, HD)
    l1 = lses[1].reshape(d1, s // d1, ATTN_HEADS).transpose(1, 0, 2).reshape(s, ATTN_HEADS)
    l2 = lses[2].reshape(d2, s // d2, ATTN_HEADS).transpose(1, 0, 2).reshape(s, ATTN_HEADS)
    perm = lambda d, w: pl.BlockSpec((d, tm // d, w), lambda i: (0, i, 0))
    return pl.pallas_call(
        _attn_out_kernel,
        grid=(s // tm,),
        in_specs=[_row_spec(tm, HD), perm(d1, HD), perm(d2, HD),
                  _row_spec(tm, ATTN_HEADS), _row_spec(tm, ATTN_HEADS), _row_spec(tm, ATTN_HEADS),
                  pl.BlockSpec((HD, D_MODEL), lambda i: (0, 0)),
                  _row_spec(tm, D_MODEL), _vec_spec(D_MODEL), _vec_spec(D_MODEL), _vec_spec(D_MODEL)],
        out_specs=_row_spec(tm, D_MODEL),
        out_shape=jax.ShapeDtypeStruct((s, D_MODEL), F32),
        scratch_shapes=[pltpu.VMEM((2, ATTN_HEADS, tm, HEAD_DIM), F32), pltpu.VMEM((tm, HD), BF16)],
        compiler_params=_cparams(("arbitrary",)),
        name="attn_out",
    )(outs[0], o1, o2, lses[0], l1, l2, w_bf, x, gate, g, b)


def _modulate_once(x_ref, sc_ref, sh_ref, u_scr):
    @pl.when(pl.program_id(1) == 0)
    def _():
        u_scr[...] = (x_ref[...] * (1.0 + sc_ref[...]) + sh_ref[...]).astype(BF16)


def _glu_in_kernel(x_ref, sc_ref, sh_ref, wa_ref, wg_ref, o_ref, u_scr):
    _modulate_once(x_ref, sc_ref, sh_ref, u_scr)
    u = u_scr[...]
    a = jnp.dot(u, wa_ref[...], preferred_element_type=F32)
    gt = jnp.dot(u, wg_ref[...], preferred_element_type=F32)
    o_ref[...] = a * jax.nn.sigmoid(gt)


def _sc_in_kernel(x_ref, sc_ref, sh_ref, wb_ref, wc_ref, wh_ref, b_ref, ch_ref, u_scr):
    _modulate_once(x_ref, sc_ref, sh_ref, u_scr)
    u = u_scr[...]
    b_ref[...] = jnp.dot(u, wb_ref[...], preferred_element_type=F32)
    c = jnp.dot(u, wc_ref[...], preferred_element_type=F32)
    h = jnp.dot(u, wh_ref[...], preferred_element_type=F32)
    ch_ref[...] = c * h


def _gated_in(kernel, n_parts, n_out, x, sc, sh, w_bf, name):
    s = x.shape[0]
    tm, tn = TM_PROJ, TN_GLU
    nj = D_MODEL // tn
    w_specs = [pl.BlockSpec((D_MODEL, tn), functools.partial(lambda i, j, p: (0, p * nj + j), p=p))
               for p in range(n_parts)]
    out_spec = pl.BlockSpec((tm, tn), lambda i, j: (i, j))
    out_shape = jax.ShapeDtypeStruct((s, D_MODEL), F32)
    return pl.pallas_call(
        kernel,
        grid=(s // tm, nj),
        in_specs=[pl.BlockSpec((tm, D_MODEL), lambda i, j: (i, 0)),
                  pl.BlockSpec((1, D_MODEL), lambda i, j: (0, 0)),
                  pl.BlockSpec((1, D_MODEL), lambda i, j: (0, 0))] + w_specs,
        out_specs=[out_spec] * n_out if n_out > 1 else out_spec,
        out_shape=[out_shape] * n_out if n_out > 1 else out_shape,
        scratch_shapes=[pltpu.VMEM((tm, D_MODEL), BF16)],
        compiler_params=_cparams(("arbitrary", "arbitrary")),
        name=name,
    )(x, sc, sh, *([w_bf] * n_parts))


def _fill_halo(hal_scr, prev_ref, cur_ref, halo):
    is_first = pl.program_id(0) == 0
    tm = cur_ref.shape[0]
    hal_scr[pl.ds(0, halo), :] = jnp.where(is_first, 0.0, prev_ref[...])
    hal_scr[pl.ds(halo, tm), :] = cur_ref[...]
    hal_scr[pl.ds(halo + tm, SUBLANES_V7X), :] = jnp.zeros((SUBLANES_V7X, D_MODEL), F32)


def _dwconv(hal_scr, w_ref, dst_scr, ph_scr, halo, ksize, tm):
    off = halo - (ksize - 1)
    win = CONV_ROWS + SUBLANES_V7X
    for r0 in range(0, tm, CONV_ROWS):
        for c0 in range(0, D_MODEL, CONV_COLS):
            cs = slice(c0, c0 + CONV_COLS)
            out = None
            for phase in range(SUBLANES_V7X):
                part = None
                for k in range(ksize):
                    if (off + k) % SUBLANES_V7X != phase:
                        continue
                    base = r0 + off + k - phase
                    term = w_ref[k:k + 1, cs] * hal_scr[base:base + win, cs]
                    part = term if part is None else part + term
                if part is None:
                    continue
                if phase == 0:
                    shifted = part[:CONV_ROWS]
                else:
                    ph_scr[phase] = part
                    shifted = ph_scr[phase, phase:phase + CONV_ROWS, :]
                out = shifted if out is None else out + shifted
            dst_scr[r0:r0 + CONV_ROWS, cs] = out


def _conv_out_kernel(cur_ref, prev_ref, wdw_ref, cg_ref, cb_ref, w_ref, x_ref, gate_ref, g_ref, b_ref,
                     out_ref, hal_scr, h_scr, ph_scr):
    tm = x_ref.shape[0]
    _fill_halo(hal_scr, prev_ref, cur_ref, HALO_CONV)
    _dwconv(hal_scr, wdw_ref, h_scr, ph_scr, HALO_CONV, CONV_KERNEL, tm)
    hn = _ln_rows(h_scr[...], cg_ref[...], cb_ref[...])
    a = (hn * jax.nn.sigmoid(hn)).astype(BF16)
    y = jnp.dot(a, w_ref[...], preferred_element_type=F32)
    out_ref[...] = _deepnorm(x_ref[...], y, gate_ref[...], g_ref[...], b_ref[...])


def _sc_out_kernel(cur_ref, prev_ref, bg_ref, wdw_ref, w_ref, x_ref, gate_ref, g_ref, b_ref,
                   out_ref, hal_scr, h_scr, ph_scr):
    tm = x_ref.shape[0]
    _fill_halo(hal_scr, prev_ref, cur_ref, HALO_SC)
    _dwconv(hal_scr, wdw_ref, h_scr, ph_scr, HALO_SC, SHORT_CONV_KERNEL, tm)
    a = (bg_ref[...] * h_scr[...]).astype(BF16)
    y = jnp.dot(a, w_ref[...], preferred_element_type=F32)
    out_ref[...] = _deepnorm(x_ref[...], y, gate_ref[...], g_ref[...], b_ref[...])


def _conv_phase_scratch():
    return pltpu.VMEM((SUBLANES_V7X, CONV_ROWS + SUBLANES_V7X, CONV_COLS), F32)


def _halo_spec(tm, halo):
    per = tm // halo
    return pl.BlockSpec((halo, D_MODEL), lambda i: (jnp.maximum(i * per - 1, 0), 0))


def _conv_out(glu, w_dw, cg, cb, w_bf, x, gate, g, b):
    s = x.shape[0]
    tm = TM_OUT
    full = lambda shape: pl.BlockSpec(shape, lambda i: (0, 0))
    return pl.pallas_call(
        _conv_out_kernel,
        grid=(s // tm,),
        in_specs=[_row_spec(tm, D_MODEL), _halo_spec(tm, HALO_CONV), full((CONV_KERNEL, D_MODEL)),
                  _vec_spec(D_MODEL), _vec_spec(D_MODEL), full((D_MODEL, D_MODEL)),
                  _row_spec(tm, D_MODEL), _vec_spec(D_MODEL), _vec_spec(D_MODEL), _vec_spec(D_MODEL)],
        out_specs=_row_spec(tm, D_MODEL),
        out_shape=jax.ShapeDtypeStruct((s, D_MODEL), F32),
        scratch_shapes=[pltpu.VMEM((tm + HALO_CONV + SUBLANES_V7X, D_MODEL), F32), pltpu.VMEM((tm, D_MODEL), F32),
                        _conv_phase_scratch()],
        compiler_params=_cparams(("arbitrary",)),
        name="conv_out",
    )(glu, glu, w_dw, cg, cb, w_bf, x, gate, g, b)


def _sc_out(ch, bgate, w_dw, w_bf, x, gate, g, b):
    s = x.shape[0]
    tm = TM_OUT
    full = lambda shape: pl.BlockSpec(shape, lambda i: (0, 0))
    return pl.pallas_call(
        _sc_out_kernel,
        grid=(s // tm,),
        in_specs=[_row_spec(tm, D_MODEL), _halo_spec(tm, HALO_SC), _row_spec(tm, D_MODEL),
                  full((SHORT_CONV_KERNEL, D_MODEL)), full((D_MODEL, D_MODEL)),
                  _row_spec(tm, D_MODEL), _vec_spec(D_MODEL), _vec_spec(D_MODEL), _vec_spec(D_MODEL)],
        out_specs=_row_spec(tm, D_MODEL),
        out_shape=jax.ShapeDtypeStruct((s, D_MODEL), F32),
        scratch_shapes=[pltpu.VMEM((tm + HALO_SC + SUBLANES_V7X, D_MODEL), F32), pltpu.VMEM((tm, D_MODEL), F32),
                        _conv_phase_scratch()],
        compiler_params=_cparams(("arbitrary",)),
        name="sc_out",
    )(ch, ch, bgate, w_dw, w_bf, x, gate, g, b)


def _argmax4_first(v):
    i01 = jnp.where(v[1] > v[0], 1, 0)
    m01 = jnp.maximum(v[0], v[1])
    i23 = jnp.where(v[3] > v[2], 3, 2)
    m23 = jnp.maximum(v[2], v[3])
    return jnp.where(m23 > m01, i23, i01), jnp.maximum(m01, m23)


def _select4(idx, v):
    return jnp.where(idx == 0, v[0], jnp.where(idx == 1, v[1], jnp.where(idx == 2, v[2], v[3])))


def _router_kernel(x_ref, sc_ref, sh_ref, wt_ref, rb_ref, e_ref, gt_ref, rk_ref, cnt_ref, carry_scr):
    i = pl.program_id(0)
    tm = x_ref.shape[0]

    @pl.when(i == 0)
    def _():
        carry_scr[...] = jnp.zeros_like(carry_scr)

    u = x_ref[...] * (1.0 + sc_ref[...]) + sh_ref[...]
    logits = lax.dot_general(wt_ref[...], u, (((1,), (1,)), ((), ())), precision=lax.Precision.HIGHEST,
                             preferred_element_type=F32) + rb_ref[...]
    ex = jnp.exp(logits - jnp.max(logits, axis=0, keepdims=True))
    probs = ex / jnp.sum(ex, axis=0, keepdims=True)

    top1_i, top1_v, top2_i, top2_v, score = [], [], [], [], []
    for grp in range(N_EXPERT_GROUPS):
        p = [probs[grp * EXPERTS_PER_GROUP + j:grp * EXPERTS_PER_GROUP + j + 1, :]
             for j in range(EXPERTS_PER_GROUP)]
        i1, v1 = _argmax4_first(p)
        rest = [jnp.where(i1 == j, -1.0, p[j]) for j in range(EXPERTS_PER_GROUP)]
        i2, v2 = _argmax4_first(rest)
        top1_i.append(i1)
        top1_v.append(v1)
        top2_i.append(i2)
        top2_v.append(v2)
        score.append(v1 + v2)
    gsel, _ = _argmax4_first(score)
    p1 = _select4(gsel, top1_v)
    p2 = _select4(gsel, top2_v)
    e1 = gsel * EXPERTS_PER_GROUP + _select4(gsel, top1_i)
    e2 = gsel * EXPERTS_PER_GROUP + _select4(gsel, top2_i)
    psum = p1 + p2
    e_ref[0:1, :] = e1
    e_ref[1:2, :] = e2
    gt_ref[0:1, :] = p1 / psum
    gt_ref[1:2, :] = p2 / psum

    eid = lax.broadcasted_iota(jnp.int32, (N_EXPERTS, tm), 0)
    earlier = (lax.broadcasted_iota(jnp.int32, (tm, tm), 0)
               < lax.broadcasted_iota(jnp.int32, (tm, tm), 1)).astype(BF16)
    base = carry_scr[:, 0:1]
    for slot, e_sel in enumerate((e1, e2)):
        onehot = (eid == e_sel).astype(F32)
        before = jnp.dot(onehot.astype(BF16), earlier, preferred_element_type=F32)
        rank = jnp.sum(onehot * (before + base), axis=0, keepdims=True)
        rk_ref[slot:slot + 1, :] = rank.astype(jnp.int32)
        base = base + jnp.sum(onehot, axis=1, keepdims=True)
    carry_scr[...] = jnp.broadcast_to(base, carry_scr.shape)
    cnt_ref[...] = carry_scr[...]


def _router(x, sc, sh, rw_t, rb_col):
    s = x.shape[0]
    tm = TM_ROUTE
    slot_spec = pl.BlockSpec((TOP_K, tm), lambda i: (0, i))
    full = lambda shape: pl.BlockSpec(shape, lambda i: (0, 0))
    return pl.pallas_call(
        _router_kernel,
        grid=(s // tm,),
        in_specs=[_row_spec(tm, D_MODEL), _vec_spec(D_MODEL), _vec_spec(D_MODEL),
                  full((N_EXPERTS, D_MODEL)), full((N_EXPERTS, 1))],
        out_specs=[slot_spec, slot_spec, slot_spec, full((N_EXPERTS, LANES_V7X))],
        out_shape=[jax.ShapeDtypeStruct((TOP_K, s), jnp.int32), jax.ShapeDtypeStruct((TOP_K, s), F32),
                   jax.ShapeDtypeStruct((TOP_K, s), jnp.int32),
                   jax.ShapeDtypeStruct((N_EXPERTS, LANES_V7X), F32)],
        scratch_shapes=[pltpu.VMEM((N_EXPERTS, LANES_V7X), F32)],
        compiler_params=_cparams(("arbitrary",)),
        name="router",
    )(x, sc, sh, rw_t, rb_col)


def _row_gather(src_hbm, idx_ref, base, n, dst, sem):
    for r in range(n):
        tok = idx_ref[base + r]
        pltpu.make_async_copy(src_hbm.at[pl.ds(tok, 1), :], dst.at[pl.ds(r, 1), :], sem).start(priority=r % 2)


def _row_gather_wait(src_hbm, n, dst, sem):
    pltpu.make_async_copy(src_hbm.at[pl.ds(0, n), :], dst, sem).wait()


class _WeightStream:
    def __init__(self, layer, w_hbm, stage, wbuf, sem):
        self.layer, self.w_hbm, self.stage, self.wbuf, self.sem = layer, w_hbm, stage, wbuf, sem

    def _copy(self, mat, expert, q, slot):
        kind = 0 if mat < 2 else 1
        rows = self.stage[kind].shape[1]
        src = self.w_hbm[mat].at[self.layer, expert, pl.ds(pl.multiple_of(q * rows, rows), rows), :]
        return pltpu.make_async_copy(src, self.stage[kind].at[slot], self.sem.at[kind, slot])

    def _per_matrix(self, c, fn):
        mat, q, slot = c // W_CHUNKS, c % W_CHUNKS, c % 2
        for m in range(3):
            @pl.when(mat == m)
            def _(m=m):
                fn(m, q, slot)

    def start(self, c, expert):
        self._per_matrix(c, lambda m, q, slot: self._copy(m, expert, q, slot).start())

    def finish(self, c, dst):
        def fn(m, q, slot):
            self._copy(m, 0, q, slot).wait()
            kind = 0 if m < 2 else 1
            rows = self.stage[kind].shape[1]
            self.wbuf[m][dst, pl.ds(pl.multiple_of(q * rows, rows), rows), :] = self.stage[kind][slot].astype(BF16)

        self._per_matrix(c, fn)

    def prefetch(self, expert, lo, hi):
        for t in range(2):
            @pl.when(lo + t < hi)
            def _(t=t):
                self.start(lo + t, expert)

    def drain(self, expert, dst, lo, upto, hi):
        def body(c, carry):
            self.finish(c, dst)

            @pl.when(c + 2 < hi)
            def _():
                self.start(c + 2, expert)

            return carry

        lax.fori_loop(lo, upto, body, 0)


def _ffn_kernel(rt_ref, be_ref, nbu_ref, ws_ref, nx_ref, lo_ref, hi_ref,
                x_hbm, sc_ref, sh_ref, wg_hbm, wu_hbm, wd_hbm, o_ref,
                xbuf, sem, wg_buf, wu_buf, wd_buf, stage_in, stage_dn, wsem, *, layer):
    i = pl.program_id(0)
    nbu = nbu_ref[0]
    tm = xbuf.shape[1]
    slot = i % 2
    stream = _WeightStream(layer, (wg_hbm, wu_hbm, wd_hbm), (stage_in, stage_dn), (wg_buf, wu_buf, wd_buf), wsem)
    n_chunks = 3 * W_CHUNKS

    @pl.when(i == 0)
    def _():
        _row_gather(x_hbm, rt_ref, 0, tm, xbuf.at[0], sem.at[0])
        stream.prefetch(be_ref[0], 0, n_chunks)
        stream.drain(be_ref[0], ws_ref[0], 0, n_chunks, n_chunks)

    @pl.when(i < nbu)
    def _():
        ws, nxt, lo, hi = ws_ref[i], nx_ref[i], lo_ref[i], hi_ref[i]
        stream.prefetch(nxt, lo, hi)
        _row_gather_wait(x_hbm, tm, xbuf.at[slot], sem.at[slot])
        _row_gather(x_hbm, rt_ref, (i + 1) * tm, tm, xbuf.at[1 - slot], sem.at[1 - slot])
        u = (xbuf[slot] * (1.0 + sc_ref[...]) + sh_ref[...]).astype(BF16)
        mid1, mid2 = jnp.minimum(lo + 2, hi), jnp.minimum(lo + 4, hi)
        gt = jnp.dot(u, wg_buf[ws], preferred_element_type=F32)
        stream.drain(nxt, 1 - ws, lo, mid1, hi)
        up = jnp.dot(u, wu_buf[ws], preferred_element_type=F32)
        stream.drain(nxt, 1 - ws, mid1, mid2, hi)
        h = (gt * jax.nn.sigmoid(gt) * up).astype(BF16)
        o_ref[...] = jnp.dot(h, wd_buf[ws], preferred_element_type=F32)
        stream.drain(nxt, 1 - ws, mid2, hi, hi)

    @pl.when(i >= nbu)
    def _():
        @pl.when(i == nbu)
        def _():
            _row_gather_wait(x_hbm, tm, xbuf.at[slot], sem.at[slot])

        o_ref[...] = jnp.zeros_like(o_ref)


def _moe_ffn(x, sc, sh, w_gate, w_up, w_down, layer, row_token, block_expert, nb_used, w_slot, next_expert,
             chunk_lo, chunk_hi):
    tm = TM_MOE
    p_rows = row_token.shape[0]
    nb = p_rows // tm
    vec = pl.BlockSpec((1, D_MODEL), lambda i, *_: (0, 0))
    hbm = pl.BlockSpec(memory_space=pl.ANY)
    return pl.pallas_call(
        functools.partial(_ffn_kernel, layer=layer),
        grid_spec=pltpu.PrefetchScalarGridSpec(
            num_scalar_prefetch=7,
            grid=(nb,),
            in_specs=[hbm, vec, vec, hbm, hbm, hbm],
            out_specs=pl.BlockSpec((tm, D_MODEL), lambda i, *_: (i, 0)),
            scratch_shapes=[pltpu.VMEM((2, tm, D_MODEL), F32), pltpu.SemaphoreType.DMA((2,)),
                            pltpu.VMEM((2, D_MODEL, D_EXPERT), BF16), pltpu.VMEM((2, D_MODEL, D_EXPERT), BF16),
                            pltpu.VMEM((2, D_EXPERT, D_MODEL), BF16),
                            pltpu.VMEM((2, D_MODEL // W_CHUNKS, D_EXPERT), F32),
                            pltpu.VMEM((2, D_EXPERT // W_CHUNKS, D_MODEL), F32),
                            pltpu.SemaphoreType.DMA((2, 2))]),
        out_shape=jax.ShapeDtypeStruct((p_rows, D_MODEL), F32),
        compiler_params=pltpu.CompilerParams(dimension_semantics=("arbitrary",),
                                             vmem_limit_bytes=VMEM_LIMIT_FFN_V7X),
        name="moe_ffn",
    )(row_token, block_expert, nb_used, w_slot, next_expert, chunk_lo, chunk_hi,
      x, sc, sh, w_gate, w_up, w_down)


def _combine_kernel(dest_ref, rows_hbm, gt_ref, x_ref, gate_ref, g_ref, b_ref, out_ref, ybuf, sem):
    i = pl.program_id(0)
    n = pl.num_programs(0)
    tm = x_ref.shape[0]
    s = tm * n
    slot = i % 2

    def start(tile, to):
        for k in range(TOP_K):
            _row_gather(rows_hbm, dest_ref, k * s + tile * tm, tm, ybuf.at[to, k], sem.at[to, k])

    def wait(at):
        for k in range(TOP_K):
            _row_gather_wait(rows_hbm, tm, ybuf.at[at, k], sem.at[at, k])

    @pl.when(i == 0)
    def _():
        start(0, 0)

    wait(slot)
    start(jnp.minimum(i + 1, n - 1), 1 - slot)
    gt = gt_ref[...]
    y = gt[:, 0:1] * ybuf[slot, 0] + gt[:, 1:2] * ybuf[slot, 1]
    out_ref[...] = _deepnorm(x_ref[...], y, gate_ref[...], g_ref[...], b_ref[...])

    @pl.when(i == n - 1)
    def _():
        wait(1 - slot)


def _moe_combine(dest_flat, out_rows, gates_t, x, gate, g, b):
    s = x.shape[0]
    tm = TM_COMB
    vec = pl.BlockSpec((1, D_MODEL), lambda i, dst: (0, 0))
    return pl.pallas_call(
        _combine_kernel,
        grid_spec=pltpu.PrefetchScalarGridSpec(
            num_scalar_prefetch=1,
            grid=(s // tm,),
            in_specs=[pl.BlockSpec(memory_space=pl.ANY),
                      pl.BlockSpec((tm, TOP_K), lambda i, dst: (i, 0)),
                      pl.BlockSpec((tm, D_MODEL), lambda i, dst: (i, 0)), vec, vec, vec],
            out_specs=pl.BlockSpec((tm, D_MODEL), lambda i, dst: (i, 0)),
            scratch_shapes=[pltpu.VMEM((2, TOP_K, tm, D_MODEL), F32), pltpu.SemaphoreType.DMA((2, TOP_K))]),
        out_shape=jax.ShapeDtypeStruct((s, D_MODEL), F32),
        compiler_params=_cparams(("arbitrary",)),
        name="moe_combine",
    )(dest_flat, out_rows, gates_t, x, gate, g, b)


def _moe_layer(x, sc, sh, gate, g, b, rw_t, rb_col, w_gate, w_up, w_down, layer):
    s = x.shape[0]
    tm = TM_MOE
    expert, gates, rank, cnt = _router(x, sc, sh, rw_t, rb_col)
    counts = cnt[:, 0].astype(jnp.int32)
    padded = (counts + tm - 1) // tm * tm
    pad_end = jnp.cumsum(padded)
    pad_start = pad_end - padded
    eids = jnp.arange(N_EXPERTS, dtype=jnp.int32)
    start_of = jnp.sum(jnp.where(expert[..., None] == eids, pad_start, 0), axis=-1)
    dest = (start_of + rank).reshape(-1)
    p_rows = TOP_K * s + N_EXPERTS * tm
    token = jnp.tile(jnp.arange(s, dtype=jnp.int32), TOP_K)
    row_token = jnp.zeros((p_rows,), jnp.int32).at[dest].set(token)
    nb = p_rows // tm
    block_row0 = jnp.arange(nb, dtype=jnp.int32) * tm
    block_expert = jnp.minimum(jnp.sum(block_row0[:, None] >= pad_end[None, :], axis=-1),
                               N_EXPERTS - 1).astype(jnp.int32)
    nb_used = (pad_end[-1:] // tm).astype(jnp.int32)
    has = padded > 0
    order = jnp.cumsum(has.astype(jnp.int32)) - 1
    later = jnp.logical_and(eids[None, :] > eids[:, None], has[None, :])
    next_of = jnp.min(jnp.where(later, eids[None, :], N_EXPERTS), axis=-1)
    of_block = lambda v: jnp.sum(jnp.where(block_expert[:, None] == eids[None, :], v[None, :], 0), axis=-1)
    run_len = jnp.maximum(of_block(padded) // tm, 1)
    run_pos = jnp.arange(nb, dtype=jnp.int32) - of_block(pad_start) // tm
    nxt = of_block(next_of)
    streams = jnp.logical_and(nxt < N_EXPERTS, jnp.arange(nb) < nb_used[0])
    n_chunks = 3 * W_CHUNKS
    chunk_lo = jnp.where(streams, n_chunks * run_pos // run_len, 0).astype(jnp.int32)
    chunk_hi = jnp.where(streams, n_chunks * (run_pos + 1) // run_len, 0).astype(jnp.int32)
    next_expert = jnp.where(streams, nxt, block_expert).astype(jnp.int32)
    w_slot = (of_block(order) % 2).astype(jnp.int32)
    out_rows = _moe_ffn(x, sc, sh, w_gate, w_up, w_down, layer, row_token, block_expert, nb_used,
                        w_slot, next_expert, chunk_lo, chunk_hi)
    return _moe_combine(dest.astype(jnp.int32), out_rows, gates.T, x, gate, g, b)


def kernel(x, c, positions, ada_w, ada_b, ln_g, ln_b, attn_w_qkv, attn_w_o, conv_w_pw1, conv_w_dw, conv_ln_g,
           conv_ln_b, conv_w_pw2, sc_w_in, sc_w_conv, sc_w_out, router_w, router_b, moe_w_gate, moe_w_up,
           moe_w_down):
    batch, s, d = x.shape
    assert batch == 1 and d == D_MODEL and s % (DILATIONS[-1] * Q_BLOCK) == 0
    xs = x.reshape(s, d)

    mod = _adaln(c.reshape(d, 1), ada_w, ada_b)
    half = HEAD_DIM // 2
    inv_freq = ROPE_THETA ** (-jnp.arange(half, dtype=F32) / half)
    freq_row = jnp.concatenate([inv_freq, inv_freq]).reshape(1, HEAD_DIM)
    cos, sin = _rope_tables(positions.reshape(s, 1), freq_row)

    rw_t = router_w.T
    rb_col = router_b.reshape(N_EXPERTS, 1)
    vec = lambda a: a.reshape(1, d)

    for i in range(DEPTH):
        sh1, sc1, g1, sh2, sc2, g2 = [mod[i, :, k * d:(k + 1) * d] for k in range(6)]
        m, j = i % N_MIXERS, i // N_MIXERS
        lg, lb = vec(ln_g[i, 0]), vec(ln_b[i, 0])
        if m == 0:
            outs, lses = [], []
            for grp in range(len(DILATIONS)):
                o_g, l_g = _attention(_qkv_proj(xs, sc1, sh1, attn_w_qkv, j, cos, sin, grp), grp)
                outs.append(o_g)
                lses.append(l_g)
            xs = _attn_out(outs, lses, attn_w_o[j].astype(BF16), xs, g1, lg, lb)
        elif m == 1:
            glu = _gated_in(_glu_in_kernel, 2, 1, xs, sc1, sh1, conv_w_pw1[j].astype(BF16), "conv_in")
            xs = _conv_out(glu, conv_w_dw[j], vec(conv_ln_g[j]), vec(conv_ln_b[j]),
                           conv_w_pw2[j].astype(BF16), xs, g1, lg, lb)
        else:
            bgate, ch = _gated_in(_sc_in_kernel, 3, 2, xs, sc1, sh1, sc_w_in[j].astype(BF16), "sc_in")
            xs = _sc_out(ch, bgate, sc_w_conv[j], sc_w_out[j].astype(BF16), xs, g1, lg, lb)
        xs = _moe_layer(xs, sc2, sh2, g2, vec(ln_g[i, 1]), vec(ln_b[i, 1]), rw_t, rb_col,
                        moe_w_gate, moe_w_up, moe_w_down, i)
    return xs.reshape(batch, s, d)
```

```python
import functools
import math

import jax
import jax.numpy as jnp
from jax import lax
from jax.experimental import pallas as pl
from jax.experimental.pallas import tpu as pltpu

F32 = jnp.float32
BF16 = jnp.bfloat16

D_MODEL = 2048
DEPTH = 4
N_MIXERS = 3
ATTN_HEADS = 16
HEAD_DIM = 128
HD = ATTN_HEADS * HEAD_DIM
DILATIONS = (1, 4, 16)
Q_BLOCK = 128
ROPE_THETA = 10000.0
CONV_KERNEL = 31
SHORT_CONV_KERNEL = 3
N_EXPERTS = 16
N_EXPERT_GROUPS = 4
EXPERTS_PER_GROUP = 4
TOP_K = 2
D_EXPERT = 1408
ALPHA = (2 * DEPTH) ** 0.25
LN_EPS = 1e-5
NEG_INF = -1e30

LANES_V7X = 128
SUBLANES_V7X = 8
MXU_COLS_V7X = 256
VMEM_LIMIT_V7X = 56 * 1024 * 1024
VMEM_LIMIT_FFN_V7X = 61 * 1024 * 1024
W_CHUNKS = 8
ROW_GATHER_PRIORITY = (0,)
W_STREAM_PRIORITY = 1
TM_PROJ = 512
TM_QKV = 1024
TN_PROJ = 1024
TN_GLU = 1024
TM_OUT = 256
TM_ROUTE = 512
TM_MOE = 256
TM_COMB = 256
TN_ADA = 1024
HALO_CONV = 32
HALO_SC = 8
CONV_ROWS = 64
CONV_COLS = 256


def _cparams(sem):
    return pltpu.CompilerParams(dimension_semantics=sem, vmem_limit_bytes=VMEM_LIMIT_V7X)


def _ln_rows(z, g, b):
    mu = jnp.mean(z, axis=-1, keepdims=True)
    zc = z - mu
    var = jnp.mean(zc * zc, axis=-1, keepdims=True)
    return zc * lax.rsqrt(var + LN_EPS) * g + b


def _deepnorm(x, y, gate, g, b):
    return _ln_rows(ALPHA * x + (1.0 + gate) * y, g, b)


def _row_spec(tm, width):
    return pl.BlockSpec((tm, width), lambda i: (i, 0))


def _vec_spec(width):
    return pl.BlockSpec((1, width), lambda i: (0, 0))


def _adaln_kernel(c_ref, w_ref, b_ref, o_ref):
    c = c_ref[...]
    ca = c * jax.nn.sigmoid(c)
    o_ref[...] = jnp.sum(w_ref[...] * ca, axis=0, keepdims=True) + b_ref[...]


def _adaln(c_col, ada_w, ada_b):
    depth, d, n = ada_w.shape
    return pl.pallas_call(
        _adaln_kernel,
        grid=(depth, n // TN_ADA),
        in_specs=[pl.BlockSpec((d, 1), lambda l, j: (0, 0)),
                  pl.BlockSpec((None, d, TN_ADA), lambda l, j: (l, 0, j)),
                  pl.BlockSpec((None, 1, TN_ADA), lambda l, j: (l, 0, j))],
        out_specs=pl.BlockSpec((None, 1, TN_ADA), lambda l, j: (l, 0, j)),
        out_shape=jax.ShapeDtypeStruct((depth, 1, n), F32),
        compiler_params=_cparams(("arbitrary", "arbitrary")),
        name="adaln",
    )(c_col, ada_w, ada_b.reshape(depth, 1, n))


def _rope_kernel(pos_ref, freq_ref, cos_ref, sin_ref):
    ang = pos_ref[...].astype(F32) * freq_ref[...]
    lane = lax.broadcasted_iota(jnp.int32, ang.shape, 1)
    s = jnp.sin(ang)
    cos_ref[...] = jnp.cos(ang)
    sin_ref[...] = jnp.where(lane < HEAD_DIM // 2, -s, s)


def _rope_tables(pos_col, freq_row):
    s = pos_col.shape[0]
    tm = 1024
    spec = pl.BlockSpec((tm, HEAD_DIM), lambda i: (i, 0))
    return pl.pallas_call(
        _rope_kernel,
        grid=(s // tm,),
        in_specs=[pl.BlockSpec((tm, 1), lambda i: (i, 0)), _vec_spec(HEAD_DIM)],
        out_specs=[spec, spec],
        out_shape=[jax.ShapeDtypeStruct((s, HEAD_DIM), F32)] * 2,
        compiler_params=_cparams(("arbitrary",)),
        name="rope_tables",
    )(pos_col, freq_row)


def _qkv_kernel(x_ref, sc_ref, sh_ref, w_ref, cos_ref, sin_ref, o_ref, u_scr, acc_scr, *, d, n_rope):
    j = pl.program_id(1)
    nh, tm, _ = acc_scr.shape

    @pl.when(j == 0)
    def _():
        u_scr[...] = (x_ref[...] * (1.0 + sc_ref[...]) + sh_ref[...]).astype(BF16)

    is_rope = j < n_rope
    cos = cos_ref[...]
    sin = sin_ref[...]
    u = u_scr[...]
    for c in range(nh // 2):
        w = w_ref[:, c * MXU_COLS_V7X:(c + 1) * MXU_COLS_V7X].astype(BF16)
        acc = jnp.dot(u, w, preferred_element_type=F32)
        for h in (2 * c, 2 * c + 1):
            a = acc[:, (h - 2 * c) * HEAD_DIM:(h - 2 * c + 1) * HEAD_DIM]
            a = jnp.where(is_rope, a * cos + pltpu.roll(a, HEAD_DIM // 2, 1) * sin, a)
            hs = slice(h * HEAD_DIM, (h + 1) * HEAD_DIM)
            if d == 1:
                o_ref[0, :, hs] = a.astype(BF16)
            else:
                acc_scr[h] = a
                for r in range(d):
                    o_ref[r, :, hs] = acc_scr[h, pl.ds(r, tm // d, stride=d), :].astype(BF16)


def _qkv_proj(x, sc, sh, w_all, layer, cos, sin, group):
    s = x.shape[0]
    d = DILATIONS[group]
    tm, tn = TM_QKV, TN_PROJ
    ncol = 3 * HD
    col0 = group * (ncol // tn)
    return pl.pallas_call(
        functools.partial(_qkv_kernel, d=d, n_rope=2 * HD // tn),
        grid=(s // tm, ncol // tn),
        in_specs=[pl.BlockSpec((tm, D_MODEL), lambda i, j: (i, 0)),
                  pl.BlockSpec((1, D_MODEL), lambda i, j: (0, 0)),
                  pl.BlockSpec((1, D_MODEL), lambda i, j: (0, 0)),
                  pl.BlockSpec((None, D_MODEL, tn), lambda i, j: (layer, 0, col0 + j)),
                  pl.BlockSpec((tm, HEAD_DIM), lambda i, j: (i, 0)),
                  pl.BlockSpec((tm, HEAD_DIM), lambda i, j: (i, 0))],
        out_specs=pl.BlockSpec((d, tm // d, tn), lambda i, j: (0, i, j)),
        out_shape=jax.ShapeDtypeStruct((d, s // d, ncol), BF16),
        scratch_shapes=[pltpu.VMEM((tm, D_MODEL), BF16), pltpu.VMEM((tn // HEAD_DIM, tm, HEAD_DIM), F32)],
        compiler_params=_cparams(("arbitrary", "arbitrary")),
        name=f"qkv_proj_g{group}",
    )(x, sc, sh, w_all, cos, sin)


def _attn_kernel(q_ref, kc_ref, vc_ref, o_ref, lse_ref, k_scr, v_scr, s_scr, p_scr, *, nblk):
    b = pl.program_id(0)
    not_first = (b % nblk) != 0

    @pl.when(b == 0)
    def _():
        k_scr[0:Q_BLOCK] = jnp.zeros((Q_BLOCK, HD), BF16)
        v_scr[0:Q_BLOCK] = jnp.zeros((Q_BLOCK, HD), BF16)

    k_scr[Q_BLOCK:] = kc_ref[...]
    v_scr[Q_BLOCK:] = vc_ref[...]
    nt = (((1,), (1,)), ((), ()))
    for h in range(ATTN_HEADS):
        sl = slice(h * HEAD_DIM, (h + 1) * HEAD_DIM)
        s_scr[h] = lax.dot_general(q_ref[:, sl], k_scr[:, sl], nt, preferred_element_type=F32)

    row = lax.broadcasted_iota(jnp.int32, (Q_BLOCK, 2 * Q_BLOCK), 0)
    col = lax.broadcasted_iota(jnp.int32, (Q_BLOCK, 2 * Q_BLOCK), 1)
    prev_ok = jnp.logical_and(jnp.logical_and(col < Q_BLOCK, col >= row), not_first)
    valid = jnp.logical_or(prev_ok, jnp.logical_and(col >= Q_BLOCK, col - Q_BLOCK <= row))
    scale = 1.0 / math.sqrt(HEAD_DIM)
    s = jnp.where(valid[None], s_scr[...], NEG_INF)
    m = jnp.max(s, axis=2, keepdims=True)
    p = jnp.exp2((s - m) * (scale * math.log2(math.e)))
    den = jnp.sum(p, axis=2, keepdims=True)
    p_scr[...] = p.astype(BF16)
    inv = 1.0 / den
    lse = m * scale + jnp.log(den)
    for h in range(ATTN_HEADS):
        sl = slice(h * HEAD_DIM, (h + 1) * HEAD_DIM)
        o = jnp.dot(p_scr[h], v_scr[:, sl], preferred_element_type=F32)
        o_ref[:, sl] = (o * inv[h]).astype(BF16)
        lse_ref[:, h:h + 1] = lse[h]
    k_scr[0:Q_BLOCK] = kc_ref[...]
    v_scr[0:Q_BLOCK] = vc_ref[...]


def _attention(qkv, group):
    d, l, ncol = qkv.shape
    s = d * l
    flat = qkv.reshape(s, ncol)
    nblk = l // Q_BLOCK
    cur = lambda c: pl.BlockSpec((Q_BLOCK, HD), lambda b: (b, c))
    return pl.pallas_call(
        functools.partial(_attn_kernel, nblk=nblk),
        grid=(s // Q_BLOCK,),
        in_specs=[cur(0), cur(1), cur(2)],
        out_specs=[pl.BlockSpec((Q_BLOCK, HD), lambda b: (b, 0)),
                   pl.BlockSpec((Q_BLOCK, ATTN_HEADS), lambda b: (b, 0))],
        out_shape=[jax.ShapeDtypeStruct((s, HD), BF16), jax.ShapeDtypeStruct((s, ATTN_HEADS), F32)],
        scratch_shapes=[pltpu.VMEM((2 * Q_BLOCK, HD), BF16), pltpu.VMEM((2 * Q_BLOCK, HD), BF16),
                        pltpu.VMEM((ATTN_HEADS, Q_BLOCK, 2 * Q_BLOCK), F32),
                        pltpu.VMEM((ATTN_HEADS, Q_BLOCK, 2 * Q_BLOCK), BF16)],
        compiler_params=_cparams(("arbitrary",)),
        name=f"dil_attn_g{group}",
    )(flat, flat, flat)


def _attn_out_kernel(o0_ref, o1_ref, o2_ref, l0_ref, l1_ref, l2_ref, w_ref, x_ref, gate_ref, g_ref, b_ref,
                     out_ref, o_scr, a_scr):
    tm = x_ref.shape[0]
    for gi, o_ref in enumerate((o1_ref, o2_ref)):
        d = DILATIONS[gi + 1]
        for r in range(d):
            for h in range(ATTN_HEADS):
                o_scr[gi, h, pl.ds(r, tm // d, stride=d), :] = (
                    o_ref[r, :, h * HEAD_DIM:(h + 1) * HEAD_DIM].astype(F32))
    l0 = l0_ref[...]
    l1 = l1_ref[...]
    l2 = l2_ref[...]
    m = jnp.maximum(jnp.maximum(l0, l1), l2)
    e0 = jnp.exp(l0 - m)
    e1 = jnp.exp(l1 - m)
    e2 = jnp.exp(l2 - m)
    z = e0 + e1 + e2
    w0 = e0 / z
    w1 = e1 / z
    w2 = e2 / z
    for h in range(ATTN_HEADS):
        sl = slice(h * HEAD_DIM, (h + 1) * HEAD_DIM)
        a = (w0[:, h:h + 1] * o0_ref[:, sl].astype(F32)
             + w1[:, h:h + 1] * o_scr[0, h]
             + w2[:, h:h + 1] * o_scr[1, h])
        a_scr[:, sl] = a.astype(BF16)
    y = jnp.dot(a_scr[...], w_ref[...], preferred_element_type=F32)
    out_ref[...] = _deepnorm(x_ref[...], y, gate_ref[...], g_ref[...], b_ref[...])


def _attn_out(outs, lses, w_bf, x, gate, g, b):
    s = x.shape[0]
    tm = TM_OUT
    d1, d2 = DILATIONS[1], DILATIONS[2]
    o1 = outs[1].reshape(d1, s // d1, HD)
    o2 = outs[2].reshape(d2, s // d2, HD)
    l1 = lses[1].reshape(d1, s // d1, ATTN_HEADS).transpose(1, 0, 2).reshape(s, ATTN_HEADS)
    l2 = lses[2].reshape(d2, s // d2, ATTN_HEADS).transpose(1, 0, 2).reshape(s, ATTN_HEADS)
    perm = lambda d, w: pl.BlockSpec((d, tm // d, w), lambda i: (0, i, 0))
    return pl.pallas_call(
        _attn_out_kernel,
        grid=(s // tm,),
        in_specs=[_row_spec(tm, HD), perm(d1, HD), perm(d2, HD),
                  _row_spec(tm, ATTN_HEADS), _row_spec(tm, ATTN_HEADS), _row_spec(tm, ATTN_HEADS),
                  pl.BlockSpec((HD, D_MODEL), lambda i: (0, 0)),
                  _row_spec(tm, D_MODEL), _vec_spec(D_MODEL), _vec_spec(D_MODEL), _vec_spec(D_MODEL)],
        out_specs=_row_spec(tm, D_MODEL),
        out_shape=jax.ShapeDtypeStruct((s, D_MODEL), F32),
        scratch_shapes=[pltpu.VMEM((2, ATTN_HEADS, tm, HEAD_DIM), F32), pltpu.VMEM((tm, HD), BF16)],
        compiler_params=_cparams(("arbitrary",)),
        name="attn_out",
    )(outs[0], o1, o2, lses[0], l1, l2, w_bf, x, gate, g, b)


def _modulate_once(x_ref, sc_ref, sh_ref, u_scr):
    @pl.when(pl.program_id(1) == 0)
    def _():
        u_scr[...] = (x_ref[...] * (1.0 + sc_ref[...]) + sh_ref[...]).astype(BF16)


def _glu_in_kernel(x_ref, sc_ref, sh_ref, wa_ref, wg_ref, o_ref, u_scr):
    _modulate_once(x_ref, sc_ref, sh_ref, u_scr)
    u = u_scr[...]
    a = jnp.dot(u, wa_ref[...], preferred_element_type=F32)
    gt = jnp.dot(u, wg_ref[...], preferred_element_type=F32)
    o_ref[...] = a * jax.nn.sigmoid(gt)


def _sc_in_kernel(x_ref, sc_ref, sh_ref, wb_ref, wc_ref, wh_ref, b_ref, ch_ref, u_scr):
    _modulate_once(x_ref, sc_ref, sh_ref, u_scr)
    u = u_scr[...]
    b_ref[...] = jnp.dot(u, wb_ref[...], preferred_element_type=F32)
    c = jnp.dot(u, wc_ref[...], preferred_element_type=F32)
    h = jnp.dot(u, wh_ref[...], preferred_element_type=F32)
    ch_ref[...] = c * h


def _gated_in(kernel, n_parts, n_out, x, sc, sh, w_bf, name):
    s = x.shape[0]
    tm, tn = TM_PROJ, TN_GLU
    nj = D_MODEL // tn
    w_specs = [pl.BlockSpec((D_MODEL, tn), functools.partial(lambda i, j, p: (0, p * nj + j), p=p))
               for p in range(n_parts)]
    out_spec = pl.BlockSpec((tm, tn), lambda i, j: (i, j))
    out_shape = jax.ShapeDtypeStruct((s, D_MODEL), F32)
    return pl.pallas_call(
        kernel,
        grid=(s // tm, nj),
        in_specs=[pl.BlockSpec((tm, D_MODEL), lambda i, j: (i, 0)),
                  pl.BlockSpec((1, D_MODEL), lambda i, j: (0, 0)),
                  pl.BlockSpec((1, D_MODEL), lambda i, j: (0, 0))] + w_specs,
        out_specs=[out_spec] * n_out if n_out > 1 else out_spec,
        out_shape=[out_shape] * n_out if n_out > 1 else out_shape,
        scratch_shapes=[pltpu.VMEM((tm, D_MODEL), BF16)],
        compiler_params=_cparams(("arbitrary", "arbitrary")),
        name=name,
    )(x, sc, sh, *([w_bf] * n_parts))


def _fill_halo(hal_scr, prev_ref, cur_ref, halo):
    is_first = pl.program_id(0) == 0
    tm = cur_ref.shape[0]
    hal_scr[pl.ds(0, halo), :] = jnp.where(is_first, 0.0, prev_ref[...])
    hal_scr[pl.ds(halo, tm), :] = cur_ref[...]
    hal_scr[pl.ds(halo + tm, SUBLANES_V7X), :] = jnp.zeros((SUBLANES_V7X, D_MODEL), F32)


def _dwconv(hal_scr, w_ref, dst_scr, ph_scr, halo, ksize, tm):
    off = halo - (ksize - 1)
    win = CONV_ROWS + SUBLANES_V7X
    for r0 in range(0, tm, CONV_ROWS):
        for c0 in range(0, D_MODEL, CONV_COLS):
            cs = slice(c0, c0 + CONV_COLS)
            out = None
            for phase in range(SUBLANES_V7X):
                part = None
                for k in range(ksize):
                    if (off + k) % SUBLANES_V7X != phase:
                        continue
                    base = r0 + off + k - phase
                    term = w_ref[k:k + 1, cs] * hal_scr[base:base + win, cs]
                    part = term if part is None else part + term
                if part is None:
                    continue
                if phase == 0:
                    shifted = part[:CONV_ROWS]
                else:
                    ph_scr[phase] = part
                    shifted = ph_scr[phase, phase:phase + CONV_ROWS, :]
                out = shifted if out is None else out + shifted
            dst_scr[r0:r0 + CONV_ROWS, cs] = out


def _conv_out_kernel(cur_ref, prev_ref, wdw_ref, cg_ref, cb_ref, w_ref, x_ref, gate_ref, g_ref, b_ref,
                     out_ref, hal_scr, h_scr, ph_scr):
    tm = x_ref.shape[0]
    _fill_halo(hal_scr, prev_ref, cur_ref, HALO_CONV)
    _dwconv(hal_scr, wdw_ref, h_scr, ph_scr, HALO_CONV, CONV_KERNEL, tm)
    hn = _ln_rows(h_scr[...], cg_ref[...], cb_ref[...])
    a = (hn * jax.nn.sigmoid(hn)).astype(BF16)
    y = jnp.dot(a, w_ref[...], preferred_element_type=F32)
    out_ref[...] = _deepnorm(x_ref[...], y, gate_ref[...], g_ref[...], b_ref[...])


def _sc_out_kernel(cur_ref, prev_ref, bg_ref, wdw_ref, w_ref, x_ref, gate_ref, g_ref, b_ref,
                   out_ref, hal_scr, h_scr, ph_scr):
    tm = x_ref.shape[0]
    _fill_halo(hal_scr, prev_ref, cur_ref, HALO_SC)
    _dwconv(hal_scr, wdw_ref, h_scr, ph_scr, HALO_SC, SHORT_CONV_KERNEL, tm)
    a = (bg_ref[...] * h_scr[...]).astype(BF16)
    y = jnp.dot(a, w_ref[...], preferred_element_type=F32)
    out_ref[...] = _deepnorm(x_ref[...], y, gate_ref[...], g_ref[...], b_ref[...])


def _conv_phase_scratch():
    return pltpu.VMEM((SUBLANES_V7X, CONV_ROWS + SUBLANES_V7X, CONV_COLS), F32)


def _halo_spec(tm, halo):
    per = tm // halo
    return pl.BlockSpec((halo, D_MODEL), lambda i: (jnp.maximum(i * per - 1, 0), 0))


def _conv_out(glu, w_dw, cg, cb, w_bf, x, gate, g, b):
    s = x.shape[0]
    tm = TM_OUT
    full = lambda shape: pl.BlockSpec(shape, lambda i: (0, 0))
    return pl.pallas_call(
        _conv_out_kernel,
        grid=(s // tm,),
        in_specs=[_row_spec(tm, D_MODEL), _halo_spec(tm, HALO_CONV), full((CONV_KERNEL, D_MODEL)),
                  _vec_spec(D_MODEL), _vec_spec(D_MODEL), full((D_MODEL, D_MODEL)),
                  _row_spec(tm, D_MODEL), _vec_spec(D_MODEL), _vec_spec(D_MODEL), _vec_spec(D_MODEL)],
        out_specs=_row_spec(tm, D_MODEL),
        out_shape=jax.ShapeDtypeStruct((s, D_MODEL), F32),
        scratch_shapes=[pltpu.VMEM((tm + HALO_CONV + SUBLANES_V7X, D_MODEL), F32), pltpu.VMEM((tm, D_MODEL), F32),
                        _conv_phase_scratch()],
        compiler_params=_cparams(("arbitrary",)),
        name="conv_out",
    )(glu, glu, w_dw, cg, cb, w_bf, x, gate, g, b)


def _sc_out(ch, bgate, w_dw, w_bf, x, gate, g, b):
    s = x.shape[0]
    tm = TM_OUT
    full = lambda shape: pl.BlockSpec(shape, lambda i: (0, 0))
    return pl.pallas_call(
        _sc_out_kernel,
        grid=(s // tm,),
        in_specs=[_row_spec(tm, D_MODEL), _halo_spec(tm, HALO_SC), _row_spec(tm, D_MODEL),
                  full((SHORT_CONV_KERNEL, D_MODEL)), full((D_MODEL, D_MODEL)),
                  _row_spec(tm, D_MODEL), _vec_spec(D_MODEL), _vec_spec(D_MODEL), _vec_spec(D_MODEL)],
        out_specs=_row_spec(tm, D_MODEL),
        out_shape=jax.ShapeDtypeStruct((s, D_MODEL), F32),
        scratch_shapes=[pltpu.VMEM((tm + HALO_SC + SUBLANES_V7X, D_MODEL), F32), pltpu.VMEM((tm, D_MODEL), F32),
                        _conv_phase_scratch()],
        compiler_params=_cparams(("arbitrary",)),
        name="sc_out",
    )(ch, ch, bgate, w_dw, w_bf, x, gate, g, b)


def _argmax4_first(v):
    i01 = jnp.where(v[1] > v[0], 1, 0)
    m01 = jnp.maximum(v[0], v[1])
    i23 = jnp.where(v[3] > v[2], 3, 2)
    m23 = jnp.maximum(v[2], v[3])
    return jnp.where(m23 > m01, i23, i01), jnp.maximum(m01, m23)


def _select4(idx, v):
    return jnp.where(idx == 0, v[0], jnp.where(idx == 1, v[1], jnp.where(idx == 2, v[2], v[3])))


def _router_kernel(x_ref, sc_ref, sh_ref, wt_ref, rb_ref, e_ref, gt_ref, rk_ref, cnt_ref, carry_scr):
    i = pl.program_id(0)
    tm = x_ref.shape[0]

    @pl.when(i == 0)
    def _():
        carry_scr[...] = jnp.zeros_like(carry_scr)

    u = x_ref[...] * (1.0 + sc_ref[...]) + sh_ref[...]
    logits = lax.dot_general(wt_ref[...], u, (((1,), (1,)), ((), ())), precision=lax.Precision.HIGHEST,
                             preferred_element_type=F32) + rb_ref[...]
    ex = jnp.exp(logits - jnp.max(logits, axis=0, keepdims=True))
    probs = ex / jnp.sum(ex, axis=0, keepdims=True)

    top1_i, top1_v, top2_i, top2_v, score = [], [], [], [], []
    for grp in range(N_EXPERT_GROUPS):
        p = [probs[grp * EXPERTS_PER_GROUP + j:grp * EXPERTS_PER_GROUP + j + 1, :]
             for j in range(EXPERTS_PER_GROUP)]
        i1, v1 = _argmax4_first(p)
        rest = [jnp.where(i1 == j, -1.0, p[j]) for j in range(EXPERTS_PER_GROUP)]
        i2, v2 = _argmax4_first(rest)
        top1_i.append(i1)
        top1_v.append(v1)
        top2_i.append(i2)
        top2_v.append(v2)
        score.append(v1 + v2)
    gsel, _ = _argmax4_first(score)
    p1 = _select4(gsel, top1_v)
    p2 = _select4(gsel, top2_v)
    e1 = gsel * EXPERTS_PER_GROUP + _select4(gsel, top1_i)
    e2 = gsel * EXPERTS_PER_GROUP + _select4(gsel, top2_i)
    psum = p1 + p2
    e_ref[0:1, :] = e1
    e_ref[1:2, :] = e2
    gt_ref[0:1, :] = p1 / psum
    gt_ref[1:2, :] = p2 / psum

    eid = lax.broadcasted_iota(jnp.int32, (N_EXPERTS, tm), 0)
    earlier = (lax.broadcasted_iota(jnp.int32, (tm, tm), 0)
               < lax.broadcasted_iota(jnp.int32, (tm, tm), 1)).astype(BF16)
    base = carry_scr[:, 0:1]
    for slot, e_sel in enumerate((e1, e2)):
        onehot = (eid == e_sel).astype(F32)
        before = jnp.dot(onehot.astype(BF16), earlier, preferred_element_type=F32)
        rank = jnp.sum(onehot * (before + base), axis=0, keepdims=True)
        rk_ref[slot:slot + 1, :] = rank.astype(jnp.int32)
        base = base + jnp.sum(onehot, axis=1, keepdims=True)
    carry_scr[...] = jnp.broadcast_to(base, carry_scr.shape)
    cnt_ref[...] = carry_scr[...]


def _router(x, sc, sh, rw_t, rb_col):
    s = x.shape[0]
    tm = TM_ROUTE
    slot_spec = pl.BlockSpec((TOP_K, tm), lambda i: (0, i))
    full = lambda shape: pl.BlockSpec(shape, lambda i: (0, 0))
    return pl.pallas_call(
        _router_kernel,
        grid=(s // tm,),
        in_specs=[_row_spec(tm, D_MODEL), _vec_spec(D_MODEL), _vec_spec(D_MODEL),
                  full((N_EXPERTS, D_MODEL)), full((N_EXPERTS, 1))],
        out_specs=[slot_spec, slot_spec, slot_spec, full((N_EXPERTS, LANES_V7X))],
        out_shape=[jax.ShapeDtypeStruct((TOP_K, s), jnp.int32), jax.ShapeDtypeStruct((TOP_K, s), F32),
                   jax.ShapeDtypeStruct((TOP_K, s), jnp.int32),
                   jax.ShapeDtypeStruct((N_EXPERTS, LANES_V7X), F32)],
        scratch_shapes=[pltpu.VMEM((N_EXPERTS, LANES_V7X), F32)],
        compiler_params=_cparams(("arbitrary",)),
        name="router",
    )(x, sc, sh, rw_t, rb_col)


def _row_gather(src_hbm, idx_ref, base, n, dst, sem, priorities=(0, 1)):
    for r in range(n):
        tok = idx_ref[base + r]
        copy = pltpu.make_async_copy(src_hbm.at[pl.ds(tok, 1), :], dst.at[pl.ds(r, 1), :], sem)
        copy.start(priority=priorities[r % len(priorities)])


def _row_gather_wait(src_hbm, n, dst, sem):
    pltpu.make_async_copy(src_hbm.at[pl.ds(0, n), :], dst, sem).wait()


class _WeightStream:
    def __init__(self, layer, w_hbm, stage, wbuf, sem):
        self.layer, self.w_hbm, self.stage, self.wbuf, self.sem = layer, w_hbm, stage, wbuf, sem

    def _copy(self, mat, expert, q, slot):
        kind = 0 if mat < 2 else 1
        rows = self.stage[kind].shape[1]
        src = self.w_hbm[mat].at[self.layer, expert, pl.ds(pl.multiple_of(q * rows, rows), rows), :]
        return pltpu.make_async_copy(src, self.stage[kind].at[slot], self.sem.at[kind, slot])

    def _per_matrix(self, c, fn):
        mat, q, slot = c // W_CHUNKS, c % W_CHUNKS, c % 2
        for m in range(3):
            @pl.when(mat == m)
            def _(m=m):
                fn(m, q, slot)

    def start(self, c, expert):
        self._per_matrix(c, lambda m, q, slot: self._copy(m, expert, q, slot).start(priority=W_STREAM_PRIORITY))

    def finish(self, c, dst):
        def fn(m, q, slot):
            self._copy(m, 0, q, slot).wait()
            kind = 0 if m < 2 else 1
            rows = self.stage[kind].shape[1]
            self.wbuf[m][dst, pl.ds(pl.multiple_of(q * rows, rows), rows), :] = self.stage[kind][slot].astype(BF16)

        self._per_matrix(c, fn)

    def prefetch(self, expert, lo, hi):
        for t in range(2):
            @pl.when(lo + t < hi)
            def _(t=t):
                self.start(lo + t, expert)

    def drain(self, expert, dst, lo, upto, hi):
        def body(c, carry):
            self.finish(c, dst)

            @pl.when(c + 2 < hi)
            def _():
                self.start(c + 2, expert)

            return carry

        lax.fori_loop(lo, upto, body, 0)


def _ffn_kernel(rt_ref, be_ref, nbu_ref, ws_ref, nx_ref, lo_ref, hi_ref,
                x_hbm, sc_ref, sh_ref, wg_hbm, wu_hbm, wd_hbm, o_ref,
                xbuf, sem, wg_buf, wu_buf, wd_buf, stage_in, stage_dn, wsem, *, layer):
    i = pl.program_id(0)
    nbu = nbu_ref[0]
    tm = xbuf.shape[1]
    slot = i % 2
    stream = _WeightStream(layer, (wg_hbm, wu_hbm, wd_hbm), (stage_in, stage_dn), (wg_buf, wu_buf, wd_buf), wsem)
    n_chunks = 3 * W_CHUNKS

    @pl.when(i == 0)
    def _():
        _row_gather(x_hbm, rt_ref, 0, tm, xbuf.at[0], sem.at[0], ROW_GATHER_PRIORITY)
        stream.prefetch(be_ref[0], 0, n_chunks)
        stream.drain(be_ref[0], ws_ref[0], 0, n_chunks, n_chunks)

    @pl.when(i < nbu)
    def _():
        ws, nxt, lo, hi = ws_ref[i], nx_ref[i], lo_ref[i], hi_ref[i]
        stream.prefetch(nxt, lo, hi)
        _row_gather_wait(x_hbm, tm, xbuf.at[slot], sem.at[slot])
        _row_gather(x_hbm, rt_ref, (i + 1) * tm, tm, xbuf.at[1 - slot], sem.at[1 - slot], ROW_GATHER_PRIORITY)
        u = (xbuf[slot] * (1.0 + sc_ref[...]) + sh_ref[...]).astype(BF16)
        mid1, mid2 = jnp.minimum(lo + 2, hi), jnp.minimum(lo + 4, hi)
        gt = jnp.dot(u, wg_buf[ws], preferred_element_type=F32)
        stream.drain(nxt, 1 - ws, lo, mid1, hi)
        up = jnp.dot(u, wu_buf[ws], preferred_element_type=F32)
        stream.drain(nxt, 1 - ws, mid1, mid2, hi)
        h = (gt * jax.nn.sigmoid(gt) * up).astype(BF16)
        o_ref[...] = jnp.dot(h, wd_buf[ws], preferred_element_type=F32)
        stream.drain(nxt, 1 - ws, mid2, hi, hi)

    @pl.when(i >= nbu)
    def _():
        @pl.when(i == nbu)
        def _():
            _row_gather_wait(x_hbm, tm, xbuf.at[slot], sem.at[slot])

        o_ref[...] = jnp.zeros_like(o_ref)


def _moe_ffn(x, sc, sh, w_gate, w_up, w_down, layer, row_token, block_expert, nb_used, w_slot, next_expert,
             chunk_lo, chunk_hi):
    tm = TM_MOE
    p_rows = row_token.shape[0]
    nb = p_rows // tm
    vec = pl.BlockSpec((1, D_MODEL), lambda i, *_: (0, 0))
    hbm = pl.BlockSpec(memory_space=pl.ANY)
    return pl.pallas_call(
        functools.partial(_ffn_kernel, layer=layer),
        grid_spec=pltpu.PrefetchScalarGridSpec(
            num_scalar_prefetch=7,
            grid=(nb,),
            in_specs=[hbm, vec, vec, hbm, hbm, hbm],
            out_specs=pl.BlockSpec((tm, D_MODEL), lambda i, *_: (i, 0)),
            scratch_shapes=[pltpu.VMEM((2, tm, D_MODEL), F32), pltpu.SemaphoreType.DMA((2,)),
                            pltpu.VMEM((2, D_MODEL, D_EXPERT), BF16), pltpu.VMEM((2, D_MODEL, D_EXPERT), BF16),
                            pltpu.VMEM((2, D_EXPERT, D_MODEL), BF16),
                            pltpu.VMEM((2, D_MODEL // W_CHUNKS, D_EXPERT), F32),
                            pltpu.VMEM((2, D_EXPERT // W_CHUNKS, D_MODEL), F32),
                            pltpu.SemaphoreType.DMA((2, 2))]),
        out_shape=jax.ShapeDtypeStruct((p_rows, D_MODEL), F32),
        compiler_params=pltpu.CompilerParams(dimension_semantics=("arbitrary",),
                                             vmem_limit_bytes=VMEM_LIMIT_FFN_V7X),
        name="moe_ffn",
    )(row_token, block_expert, nb_used, w_slot, next_expert, chunk_lo, chunk_hi,
      x, sc, sh, w_gate, w_up, w_down)


def _combine_kernel(dest_ref, rows_hbm, gt_ref, x_ref, gate_ref, g_ref, b_ref, out_ref, ybuf, sem):
    i = pl.program_id(0)
    n = pl.num_programs(0)
    tm = x_ref.shape[0]
    s = tm * n
    slot = i % 2

    def start(tile, to):
        for k in range(TOP_K):
            _row_gather(rows_hbm, dest_ref, k * s + tile * tm, tm, ybuf.at[to, k], sem.at[to, k])

    def wait(at):
        for k in range(TOP_K):
            _row_gather_wait(rows_hbm, tm, ybuf.at[at, k], sem.at[at, k])

    @pl.when(i == 0)
    def _():
        start(0, 0)

    wait(slot)
    start(jnp.minimum(i + 1, n - 1), 1 - slot)
    gt = gt_ref[...]
    y = gt[:, 0:1] * ybuf[slot, 0] + gt[:, 1:2] * ybuf[slot, 1]
    out_ref[...] = _deepnorm(x_ref[...], y, gate_ref[...], g_ref[...], b_ref[...])

    @pl.when(i == n - 1)
    def _():
        wait(1 - slot)


def _moe_combine(dest_flat, out_rows, gates_t, x, gate, g, b):
    s = x.shape[0]
    tm = TM_COMB
    vec = pl.BlockSpec((1, D_MODEL), lambda i, dst: (0, 0))
    return pl.pallas_call(
        _combine_kernel,
        grid_spec=pltpu.PrefetchScalarGridSpec(
            num_scalar_prefetch=1,
            grid=(s // tm,),
            in_specs=[pl.BlockSpec(memory_space=pl.ANY),
                      pl.BlockSpec((tm, TOP_K), lambda i, dst: (i, 0)),
                      pl.BlockSpec((tm, D_MODEL), lambda i, dst: (i, 0)), vec, vec, vec],
            out_specs=pl.BlockSpec((tm, D_MODEL), lambda i, dst: (i, 0)),
            scratch_shapes=[pltpu.VMEM((2, TOP_K, tm, D_MODEL), F32), pltpu.SemaphoreType.DMA((2, TOP_K))]),
        out_shape=jax.ShapeDtypeStruct((s, D_MODEL), F32),
        compiler_params=_cparams(("arbitrary",)),
        name="moe_combine",
    )(dest_flat, out_rows, gates_t, x, gate, g, b)


def _moe_layer(x, sc, sh, gate, g, b, rw_t, rb_col, w_gate, w_up, w_down, layer):
    s = x.shape[0]
    tm = TM_MOE
    expert, gates, rank, cnt = _router(x, sc, sh, rw_t, rb_col)
    counts = cnt[:, 0].astype(jnp.int32)
    padded = (counts + tm - 1) // tm * tm
    pad_end = jnp.cumsum(padded)
    pad_start = pad_end - padded
    eids = jnp.arange(N_EXPERTS, dtype=jnp.int32)
    start_of = jnp.sum(jnp.where(expert[..., None] == eids, pad_start, 0), axis=-1)
    dest = (start_of + rank).reshape(-1)
    p_rows = TOP_K * s + N_EXPERTS * tm
    token = jnp.tile(jnp.arange(s, dtype=jnp.int32), TOP_K)
    row_token = jnp.zeros((p_rows,), jnp.int32).at[dest].set(token)
    nb = p_rows // tm
    block_row0 = jnp.arange(nb, dtype=jnp.int32) * tm
    block_expert = jnp.minimum(jnp.sum(block_row0[:, None] >= pad_end[None, :], axis=-1),
                               N_EXPERTS - 1).astype(jnp.int32)
    nb_used = (pad_end[-1:] // tm).astype(jnp.int32)
    has = padded > 0
    order = jnp.cumsum(has.astype(jnp.int32)) - 1
    later = jnp.logical_and(eids[None, :] > eids[:, None], has[None, :])
    next_of = jnp.min(jnp.where(later, eids[None, :], N_EXPERTS), axis=-1)
    of_block = lambda v: jnp.sum(jnp.where(block_expert[:, None] == eids[None, :], v[None, :], 0), axis=-1)
    run_len = jnp.maximum(of_block(padded) // tm, 1)
    run_pos = jnp.arange(nb, dtype=jnp.int32) - of_block(pad_start) // tm
    nxt = of_block(next_of)
    streams = jnp.logical_and(nxt < N_EXPERTS, jnp.arange(nb) < nb_used[0])
    n_chunks = 3 * W_CHUNKS
    chunk_lo = jnp.where(streams, n_chunks * run_pos // run_len, 0).astype(jnp.int32)
    chunk_hi = jnp.where(streams, n_chunks * (run_pos + 1) // run_len, 0).astype(jnp.int32)
    next_expert = jnp.where(streams, nxt, block_expert).astype(jnp.int32)
    w_slot = (of_block(order) % 2).astype(jnp.int32)
    out_rows = _moe_ffn(x, sc, sh, w_gate, w_up, w_down, layer, row_token, block_expert, nb_used,
                        w_slot, next_expert, chunk_lo, chunk_hi)
    return _moe_combine(dest.astype(jnp.int32), out_rows, gates.T, x, gate, g, b)


def kernel(x, c, positions, ada_w, ada_b, ln_g, ln_b, attn_w_qkv, attn_w_o, conv_w_pw1, conv_w_dw, conv_ln_g,
           conv_ln_b, conv_w_pw2, sc_w_in, sc_w_conv, sc_w_out, router_w, router_b, moe_w_gate, moe_w_up,
           moe_w_down):
    batch, s, d = x.shape
    assert batch == 1 and d == D_MODEL and s % (DILATIONS[-1] * Q_BLOCK) == 0
    xs = x.reshape(s, d)

    mod = _adaln(c.reshape(d, 1), ada_w, ada_b)
    half = HEAD_DIM // 2
    inv_freq = ROPE_THETA ** (-jnp.arange(half, dtype=F32) / half)
    freq_row = jnp.concatenate([inv_freq, inv_freq]).reshape(1, HEAD_DIM)
    cos, sin = _rope_tables(positions.reshape(s, 1), freq_row)

    rw_t = router_w.T
    rb_col = router_b.reshape(N_EXPERTS, 1)
    vec = lambda a: a.reshape(1, d)

    for i in range(DEPTH):
        sh1, sc1, g1, sh2, sc2, g2 = [mod[i, :, k * d:(k + 1) * d] for k in range(6)]
        m, j = i % N_MIXERS, i // N_MIXERS
        lg, lb = vec(ln_g[i, 0]), vec(ln_b[i, 0])
        if m == 0:
            outs, lses = [], []
            for grp in range(len(DILATIONS)):
                o_g, l_g = _attention(_qkv_proj(xs, sc1, sh1, attn_w_qkv, j, cos, sin, grp), grp)
                outs.append(o_g)
                lses.append(l_g)
            xs = _attn_out(outs, lses, attn_w_o[j].astype(BF16), xs, g1, lg, lb)
        elif m == 1:
            glu = _gated_in(_glu_in_kernel, 2, 1, xs, sc1, sh1, conv_w_pw1[j].astype(BF16), "conv_in")
            xs = _conv_out(glu, conv_w_dw[j], vec(conv_ln_g[j]), vec(conv_ln_b[j]),
                           conv_w_pw2[j].astype(BF16), xs, g1, lg, lb)
        else:
            bgate, ch = _gated_in(_sc_in_kernel, 3, 2, xs, sc1, sh1, sc_w_in[j].astype(BF16), "sc_in")
            xs = _sc_out(ch, bgate, sc_w_conv[j], sc_w_out[j].astype(BF16), xs, g1, lg, lb)
        xs = _moe_layer(xs, sc2, sh2, g2, vec(ln_g[i, 1]), vec(ln_b[i, 1]), rw_t, rb_col,
                        moe_w_gate, moe_w_up, moe_w_down, i)
    return xs.reshape(batch, s, d)
```

```python
import functools
import math

import jax
import jax.numpy as jnp
from jax import lax
from jax.experimental import pallas as pl
from jax.experimental.pallas import tpu as pltpu

F32 = jnp.float32
BF16 = jnp.bfloat16

D_MODEL = 2048
DEPTH = 4
N_MIXERS = 3
ATTN_HEADS = 16
HEAD_DIM = 128
HD = ATTN_HEADS * HEAD_DIM
DILATIONS = (1, 4, 16)
Q_BLOCK = 128
ROPE_THETA = 10000.0
CONV_KERNEL = 31
SHORT_CONV_KERNEL = 3
N_EXPERTS = 16
N_EXPERT_GROUPS = 4
EXPERTS_PER_GROUP = 4
TOP_K = 2
D_EXPERT = 1408
ALPHA = (2 * DEPTH) ** 0.25
LN_EPS = 1e-5
NEG_INF = -1e30

LANES_V7X = 128
SUBLANES_V7X = 8
MXU_COLS_V7X = 256
VMEM_LIMIT_V7X = 56 * 1024 * 1024
VMEM_LIMIT_FFN_V7X = 61 * 1024 * 1024
W_CHUNKS = 8
ROW_GATHER_PRIORITY = (0, 1)
W_STREAM_PRIORITY = 0
TM_PROJ = 512
TM_QKV = 1024
TN_PROJ = 1024
TN_GLU = 1024
TM_OUT = 256
TM_ROUTE = 512
TM_MOE = 256
TM_COMB = 256
TN_ADA = 1024
HALO_CONV = 32
HALO_SC = 8
CONV_ROWS = 64
CONV_COLS = 256


def _cparams(sem):
    return pltpu.CompilerParams(dimension_semantics=sem, vmem_limit_bytes=VMEM_LIMIT_V7X)


def _ln_rows(z, g, b):
    mu = jnp.mean(z, axis=-1, keepdims=True)
    zc = z - mu
    var = jnp.mean(zc * zc, axis=-1, keepdims=True)
    return zc * lax.rsqrt(var + LN_EPS) * g + b


def _deepnorm(x, y, gate, g, b):
    return _ln_rows(ALPHA * x + (1.0 + gate) * y, g, b)


def _row_spec(tm, width):
    return pl.BlockSpec((tm, width), lambda i: (i, 0))


def _vec_spec(width):
    return pl.BlockSpec((1, width), lambda i: (0, 0))


def _adaln_kernel(c_ref, w_ref, b_ref, o_ref):
    c = c_ref[...]
    ca = c * jax.nn.sigmoid(c)
    o_ref[...] = jnp.sum(w_ref[...] * ca, axis=0, keepdims=True) + b_ref[...]


def _adaln(c_col, ada_w, ada_b):
    depth, d, n = ada_w.shape
    return pl.pallas_call(
        _adaln_kernel,
        grid=(depth, n // TN_ADA),
        in_specs=[pl.BlockSpec((d, 1), lambda l, j: (0, 0)),
                  pl.BlockSpec((None, d, TN_ADA), lambda l, j: (l, 0, j)),
                  pl.BlockSpec((None, 1, TN_ADA), lambda l, j: (l, 0, j))],
        out_specs=pl.BlockSpec((None, 1, TN_ADA), lambda l, j: (l, 0, j)),
        out_shape=jax.ShapeDtypeStruct((depth, 1, n), F32),
        compiler_params=_cparams(("arbitrary", "arbitrary")),
        name="adaln",
    )(c_col, ada_w, ada_b.reshape(depth, 1, n))


def _rope_kernel(pos_ref, freq_ref, cos_ref, sin_ref):
    ang = pos_ref[...].astype(F32) * freq_ref[...]
    lane = lax.broadcasted_iota(jnp.int32, ang.shape, 1)
    s = jnp.sin(ang)
    cos_ref[...] = jnp.cos(ang)
    sin_ref[...] = jnp.where(lane < HEAD_DIM // 2, -s, s)


def _rope_tables(pos_col, freq_row):
    s = pos_col.shape[0]
    tm = 1024
    spec = pl.BlockSpec((tm, HEAD_DIM), lambda i: (i, 0))
    return pl.pallas_call(
        _rope_kernel,
        grid=(s // tm,),
        in_specs=[pl.BlockSpec((tm, 1), lambda i: (i, 0)), _vec_spec(HEAD_DIM)],
        out_specs=[spec, spec],
        out_shape=[jax.ShapeDtypeStruct((s, HEAD_DIM), F32)] * 2,
        compiler_params=_cparams(("arbitrary",)),
        name="rope_tables",
    )(pos_col, freq_row)


def _qkv_kernel(x_ref, sc_ref, sh_ref, w_ref, cos_ref, sin_ref, o_ref, u_scr, acc_scr, *, d, n_rope):
    j = pl.program_id(1)
    nh, tm, _ = acc_scr.shape

    @pl.when(j == 0)
    def _():
        u_scr[...] = (x_ref[...] * (1.0 + sc_ref[...]) + sh_ref[...]).astype(BF16)

    is_rope = j < n_rope
    cos = cos_ref[...]
    sin = sin_ref[...]
    u = u_scr[...]
    for c in range(nh // 2):
        w = w_ref[:, c * MXU_COLS_V7X:(c + 1) * MXU_COLS_V7X].astype(BF16)
        acc = jnp.dot(u, w, preferred_element_type=F32)
        for h in (2 * c, 2 * c + 1):
            a = acc[:, (h - 2 * c) * HEAD_DIM:(h - 2 * c + 1) * HEAD_DIM]
            a = jnp.where(is_rope, a * cos + pltpu.roll(a, HEAD_DIM // 2, 1) * sin, a)
            hs = slice(h * HEAD_DIM, (h + 1) * HEAD_DIM)
            if d == 1:
                o_ref[0, :, hs] = a.astype(BF16)
            else:
                acc_scr[h] = a
                for r in range(d):
                    o_ref[r, :, hs] = acc_scr[h, pl.ds(r, tm // d, stride=d), :].astype(BF16)


def _qkv_proj(x, sc, sh, w_all, layer, cos, sin, group):
    s = x.shape[0]
    d = DILATIONS[group]
    tm, tn = TM_QKV, TN_PROJ
    ncol = 3 * HD
    col0 = group * (ncol // tn)
    return pl.pallas_call(
        functools.partial(_qkv_kernel, d=d, n_rope=2 * HD // tn),
        grid=(s // tm, ncol // tn),
        in_specs=[pl.BlockSpec((tm, D_MODEL), lambda i, j: (i, 0)),
                  pl.BlockSpec((1, D_MODEL), lambda i, j: (0, 0)),
                  pl.BlockSpec((1, D_MODEL), lambda i, j: (0, 0)),
                  pl.BlockSpec((None, D_MODEL, tn), lambda i, j: (layer, 0, col0 + j)),
                  pl.BlockSpec((tm, HEAD_DIM), lambda i, j: (i, 0)),
                  pl.BlockSpec((tm, HEAD_DIM), lambda i, j: (i, 0))],
        out_specs=pl.BlockSpec((d, tm // d, tn), lambda i, j: (0, i, j)),
        out_shape=jax.ShapeDtypeStruct((d, s // d, ncol), BF16),
        scratch_shapes=[pltpu.VMEM((tm, D_MODEL), BF16), pltpu.VMEM((tn // HEAD_DIM, tm, HEAD_DIM), F32)],
        compiler_params=_cparams(("arbitrary", "arbitrary")),
        name=f"qkv_proj_g{group}",
    )(x, sc, sh, w_all, cos, sin)


def _attn_kernel(q_ref, kc_ref, vc_ref, o_ref, lse_ref, k_scr, v_scr, s_scr, p_scr, *, nblk):
    b = pl.program_id(0)
    not_first = (b % nblk) != 0

    @pl.when(b == 0)
    def _():
        k_scr[0:Q_BLOCK] = jnp.zeros((Q_BLOCK, HD), BF16)
        v_scr[0:Q_BLOCK] = jnp.zeros((Q_BLOCK, HD), BF16)

    k_scr[Q_BLOCK:] = kc_ref[...]
    v_scr[Q_BLOCK:] = vc_ref[...]
    nt = (((1,), (1,)), ((), ()))
    for h in range(ATTN_HEADS):
        sl = slice(h * HEAD_DIM, (h + 1) * HEAD_DIM)
        s_scr[h] = lax.dot_general(q_ref[:, sl], k_scr[:, sl], nt, preferred_element_type=F32)

    row = lax.broadcasted_iota(jnp.int32, (Q_BLOCK, 2 * Q_BLOCK), 0)
    col = lax.broadcasted_iota(jnp.int32, (Q_BLOCK, 2 * Q_BLOCK), 1)
    prev_ok = jnp.logical_and(jnp.logical_and(col < Q_BLOCK, col >= row), not_first)
    valid = jnp.logical_or(prev_ok, jnp.logical_and(col >= Q_BLOCK, col - Q_BLOCK <= row))
    scale = 1.0 / math.sqrt(HEAD_DIM)
    s = jnp.where(valid[None], s_scr[...], NEG_INF)
    m = jnp.max(s, axis=2, keepdims=True)
    p = jnp.exp2((s - m) * (scale * math.log2(math.e)))
    den = jnp.sum(p, axis=2, keepdims=True)
    p_scr[...] = p.astype(BF16)
    inv = 1.0 / den
    lse = m * scale + jnp.log(den)
    for h in range(ATTN_HEADS):
        sl = slice(h * HEAD_DIM, (h + 1) * HEAD_DIM)
        o = jnp.dot(p_scr[h], v_scr[:, sl], preferred_element_type=F32)
        o_ref[:, sl] = (o * inv[h]).astype(BF16)
        lse_ref[:, h:h + 1] = lse[h]
    k_scr[0:Q_BLOCK] = kc_ref[...]
    v_scr[0:Q_BLOCK] = vc_ref[...]


def _attention(qkv, group):
    d, l, ncol = qkv.shape
    s = d * l
    flat = qkv.reshape(s, ncol)
    nblk = l // Q_BLOCK
    cur = lambda c: pl.BlockSpec((Q_BLOCK, HD), lambda b: (b, c))
    return pl.pallas_call(
        functools.partial(_attn_kernel, nblk=nblk),
        grid=(s // Q_BLOCK,),
        in_specs=[cur(0), cur(1), cur(2)],
        out_specs=[pl.BlockSpec((Q_BLOCK, HD), lambda b: (b, 0)),
                   pl.BlockSpec((Q_BLOCK, ATTN_HEADS), lambda b: (b, 0))],
        out_shape=[jax.ShapeDtypeStruct((s, HD), BF16), jax.ShapeDtypeStruct((s, ATTN_HEADS), F32)],
        scratch_shapes=[pltpu.VMEM((2 * Q_BLOCK, HD), BF16), pltpu.VMEM((2 * Q_BLOCK, HD), BF16),
                        pltpu.VMEM((ATTN_HEADS, Q_BLOCK, 2 * Q_BLOCK), F32),
                        pltpu.VMEM((ATTN_HEADS, Q_BLOCK, 2 * Q_BLOCK), BF16)],
        compiler_params=_cparams(("arbitrary",)),
        name=f"dil_attn_g{group}",
    )(flat, flat, flat)


def _attn_out_kernel(o0_ref, o1_ref, o2_ref, l0_ref, l1_ref, l2_ref, w_ref, x_ref, gate_ref, g_ref, b_ref,
                     out_ref, o_scr, a_scr):
    tm = x_ref.shape[0]
    for gi, o_ref in enumerate((o1_ref, o2_ref)):
        d = DILATIONS[gi + 1]
        for r in range(d):
            for h in range(ATTN_HEADS):
                o_scr[gi, h, pl.ds(r, tm // d, stride=d), :] = (
                    o_ref[r, :, h * HEAD_DIM:(h + 1) * HEAD_DIM].astype(F32))
    l0 = l0_ref[...]
    l1 = l1_ref[...]
    l2 = l2_ref[...]
    m = jnp.maximum(jnp.maximum(l0, l1), l2)
    e0 = jnp.exp(l0 - m)
    e1 = jnp.exp(l1 - m)
    e2 = jnp.exp(l2 - m)
    z = e0 + e1 + e2
    w0 = e0 / z
    w1 = e1 / z
    w2 = e2 / z
    for h in range(ATTN_HEADS):
        sl = slice(h * HEAD_DIM, (h + 1) * HEAD_DIM)
        a = (w0[:, h:h + 1] * o0_ref[:, sl].astype(F32)
             + w1[:, h:h + 1] * o_scr[0, h]
             + w2[:, h:h + 1] * o_scr[1, h])
        a_scr[:, sl] = a.astype(BF16)
    y = jnp.dot(a_scr[...], w_ref[...], preferred_element_type=F32)
    out_ref[...] = _deepnorm(x_ref[...], y, gate_ref[...], g_ref[...], b_ref[...])


def _attn_out(outs, lses, w_bf, x, gate, g, b):
    s = x.shape[0]
    tm = TM_OUT
    d1, d2 = DILATIONS[1], DILATIONS[2]
    o1 = outs[1].reshape(d1, s // d1, HD)
    o2 = outs[2].reshape(d2, s // d2, HD)
    l1 = lses[1].reshape(d1, s // d1, ATTN_HEADS).transpose(1, 0, 2).reshape(s, ATTN_HEADS)
    l2 = lses[2].reshape(d2, s // d2, ATTN_HEADS).transpose(1, 0, 2).reshape(s, ATTN_HEADS)
    perm = lambda d, w: pl.BlockSpec((d, tm // d, w), lambda i: (0, i, 0))
    return pl.pallas_call(
        _attn_out_kernel,
        grid=(s // tm,),
        in_specs=[_row_spec(tm, HD), perm(d1, HD), perm(d2, HD),
                  _row_spec(tm, ATTN_HEADS), _row_spec(tm, ATTN_HEADS), _row_spec(tm, ATTN_HEADS),
                  pl.BlockSpec((HD, D_MODEL), lambda i: (0, 0)),
                  _row_spec(tm, D_MODEL), _vec_spec(D_MODEL), _vec_spec(D_MODEL), _vec_spec(D_MODEL)],
        out_specs=_row_spec(tm, D_MODEL),
        out_shape=jax.ShapeDtypeStruct((s, D_MODEL), F32),
        scratch_shapes=[pltpu.VMEM((2, ATTN_HEADS, tm, HEAD_DIM), F32), pltpu.VMEM((tm, HD), BF16)],
        compiler_params=_cparams(("arbitrary",)),
        name="attn_out",
    )(outs[0], o1, o2, lses[0], l1, l2, w_bf, x, gate, g, b)


def _modulate_once(x_ref, sc_ref, sh_ref, u_scr):
    @pl.when(pl.program_id(1) == 0)
    def _():
        u_scr[...] = (x_ref[...] * (1.0 + sc_ref[...]) + sh_ref[...]).astype(BF16)


def _glu_in_kernel(x_ref, sc_ref, sh_ref, wa_ref, wg_ref, o_ref, u_scr):
    _modulate_once(x_ref, sc_ref, sh_ref, u_scr)
    u = u_scr[...]
    a = jnp.dot(u, wa_ref[...], preferred_element_type=F32)
    gt = jnp.dot(u, wg_ref[...], preferred_element_type=F32)
    o_ref[...] = a * jax.nn.sigmoid(gt)


def _sc_in_kernel(x_ref, sc_ref, sh_ref, wb_ref, wc_ref, wh_ref, b_ref, ch_ref, u_scr):
    _modulate_once(x_ref, sc_ref, sh_ref, u_scr)
    u = u_scr[...]
    b_ref[...] = jnp.dot(u, wb_ref[...], preferred_element_type=F32)
    c = jnp.dot(u, wc_ref[...], preferred_element_type=F32)
    h = jnp.dot(u, wh_ref[...], preferred_element_type=F32)
    ch_ref[...] = c * h


def _gated_in(kernel, n_parts, n_out, x, sc, sh, w_bf, name):
    s = x.shape[0]
    tm, tn = TM_PROJ, TN_GLU
    nj = D_MODEL // tn
    w_specs = [pl.BlockSpec((D_MODEL, tn), functools.partial(lambda i, j, p: (0, p * nj + j), p=p))
               for p in range(n_parts)]
    out_spec = pl.BlockSpec((tm, tn), lambda i, j: (i, j))
    out_shape = jax.ShapeDtypeStruct((s, D_MODEL), F32)
    return pl.pallas_call(
        kernel,
        grid=(s // tm, nj),
        in_specs=[pl.BlockSpec((tm, D_MODEL), lambda i, j: (i, 0)),
                  pl.BlockSpec((1, D_MODEL), lambda i, j: (0, 0)),
                  pl.BlockSpec((1, D_MODEL), lambda i, j: (0, 0))] + w_specs,
        out_specs=[out_spec] * n_out if n_out > 1 else out_spec,
        out_shape=[out_shape] * n_out if n_out > 1 else out_shape,
        scratch_shapes=[pltpu.VMEM((tm, D_MODEL), BF16)],
        compiler_params=_cparams(("arbitrary", "arbitrary")),
        name=name,
    )(x, sc, sh, *([w_bf] * n_parts))


def _fill_halo(hal_scr, prev_ref, cur_ref, halo):
    is_first = pl.program_id(0) == 0
    tm = cur_ref.shape[0]
    hal_scr[pl.ds(0, halo), :] = jnp.where(is_first, 0.0, prev_ref[...])
    hal_scr[pl.ds(halo, tm), :] = cur_ref[...]
    hal_scr[pl.ds(halo + tm, SUBLANES_V7X), :] = jnp.zeros((SUBLANES_V7X, D_MODEL), F32)


def _dwconv(hal_scr, w_ref, dst_scr, ph_scr, halo, ksize, tm):
    off = halo - (ksize - 1)
    win = CONV_ROWS + SUBLANES_V7X
    for r0 in range(0, tm, CONV_ROWS):
        for c0 in range(0, D_MODEL, CONV_COLS):
            cs = slice(c0, c0 + CONV_COLS)
            out = None
            for phase in range(SUBLANES_V7X):
                part = None
                for k in range(ksize):
                    if (off + k) % SUBLANES_V7X != phase:
                        continue
                    base = r0 + off + k - phase
                    term = w_ref[k:k + 1, cs] * hal_scr[base:base + win, cs]
                    part = term if part is None else part + term
                if part is None:
                    continue
                if phase == 0:
                    shifted = part[:CONV_ROWS]
                else:
                    ph_scr[phase] = part
                    shifted = ph_scr[phase, phase:phase + CONV_ROWS, :]
                out = shifted if out is None else out + shifted
            dst_scr[r0:r0 + CONV_ROWS, cs] = out


def _conv_out_kernel(cur_ref, prev_ref, wdw_ref, cg_ref, cb_ref, w_ref, x_ref, gate_ref, g_ref, b_ref,
                     out_ref, hal_scr, h_scr, ph_scr):
    tm = x_ref.shape[0]
    _fill_halo(hal_scr, prev_ref, cur_ref, HALO_CONV)
    _dwconv(hal_scr, wdw_ref, h_scr, ph_scr, HALO_CONV, CONV_KERNEL, tm)
    hn = _ln_rows(h_scr[...], cg_ref[...], cb_ref[...])
    a = (hn * jax.nn.sigmoid(hn)).astype(BF16)
    y = jnp.dot(a, w_ref[...], preferred_element_type=F32)
    out_ref[...] = _deepnorm(x_ref[...], y, gate_ref[...], g_ref[...], b_ref[...])


def _sc_out_kernel(cur_ref, prev_ref, bg_ref, wdw_ref, w_ref, x_ref, gate_ref, g_ref, b_ref,
                   out_ref, hal_scr, h_scr, ph_scr):
    tm = x_ref.shape[0]
    _fill_halo(hal_scr, prev_ref, cur_ref, HALO_SC)
    _dwconv(hal_scr, wdw_ref, h_scr, ph_scr, HALO_SC, SHORT_CONV_KERNEL, tm)
    a = (bg_ref[...] * h_scr[...]).astype(BF16)
    y = jnp.dot(a, w_ref[...], preferred_element_type=F32)
    out_ref[...] = _deepnorm(x_ref[...], y, gate_ref[...], g_ref[...], b_ref[...])


def _conv_phase_scratch():
    return pltpu.VMEM((SUBLANES_V7X, CONV_ROWS + SUBLANES_V7X, CONV_COLS), F32)


def _halo_spec(tm, halo):
    per = tm // halo
    return pl.BlockSpec((halo, D_MODEL), lambda i: (jnp.maximum(i * per - 1, 0), 0))


def _conv_out(glu, w_dw, cg, cb, w_bf, x, gate, g, b):
    s = x.shape[0]
    tm = TM_OUT
    full = lambda shape: pl.BlockSpec(shape, lambda i: (0, 0))
    return pl.pallas_call(
        _conv_out_kernel,
        grid=(s // tm,),
        in_specs=[_row_spec(tm, D_MODEL), _halo_spec(tm, HALO_CONV), full((CONV_KERNEL, D_MODEL)),
                  _vec_spec(D_MODEL), _vec_spec(D_MODEL), full((D_MODEL, D_MODEL)),
                  _row_spec(tm, D_MODEL), _vec_spec(D_MODEL), _vec_spec(D_MODEL), _vec_spec(D_MODEL)],
        out_specs=_row_spec(tm, D_MODEL),
        out_shape=jax.ShapeDtypeStruct((s, D_MODEL), F32),
        scratch_shapes=[pltpu.VMEM((tm + HALO_CONV + SUBLANES_V7X, D_MODEL), F32), pltpu.VMEM((tm, D_MODEL), F32),
                        _conv_phase_scratch()],
        compiler_params=_cparams(("arbitrary",)),
        name="conv_out",
    )(glu, glu, w_dw, cg, cb, w_bf, x, gate, g, b)


def _sc_out(ch, bgate, w_dw, w_bf, x, gate, g, b):
    s = x.shape[0]
    tm = TM_OUT
    full = lambda shape: pl.BlockSpec(shape, lambda i: (0, 0))
    return pl.pallas_call(
        _sc_out_kernel,
        grid=(s // tm,),
        in_specs=[_row_spec(tm, D_MODEL), _halo_spec(tm, HALO_SC), _row_spec(tm, D_MODEL),
                  full((SHORT_CONV_KERNEL, D_MODEL)), full((D_MODEL, D_MODEL)),
                  _row_spec(tm, D_MODEL), _vec_spec(D_MODEL), _vec_spec(D_MODEL), _vec_spec(D_MODEL)],
        out_specs=_row_spec(tm, D_MODEL),
        out_shape=jax.ShapeDtypeStruct((s, D_MODEL), F32),
        scratch_shapes=[pltpu.VMEM((tm + HALO_SC + SUBLANES_V7X, D_MODEL), F32), pltpu.VMEM((tm, D_MODEL), F32),
                        _conv_phase_scratch()],
        compiler_params=_cparams(("arbitrary",)),
        name="sc_out",
    )(ch, ch, bgate, w_dw, w_bf, x, gate, g, b)


def _argmax4_first(v):
    i01 = jnp.where(v[1] > v[0], 1, 0)
    m01 = jnp.maximum(v[0], v[1])
    i23 = jnp.where(v[3] > v[2], 3, 2)
    m23 = jnp.maximum(v[2], v[3])
    return jnp.where(m23 > m01, i23, i01), jnp.maximum(m01, m23)


def _select4(idx, v):
    return jnp.where(idx == 0, v[0], jnp.where(idx == 1, v[1], jnp.where(idx == 2, v[2], v[3])))


def _router_kernel(x_ref, sc_ref, sh_ref, wt_ref, rb_ref, e_ref, gt_ref, rk_ref, cnt_ref, carry_scr):
    i = pl.program_id(0)
    tm = x_ref.shape[0]

    @pl.when(i == 0)
    def _():
        carry_scr[...] = jnp.zeros_like(carry_scr)

    u = x_ref[...] * (1.0 + sc_ref[...]) + sh_ref[...]
    logits = lax.dot_general(wt_ref[...], u, (((1,), (1,)), ((), ())), precision=lax.Precision.HIGHEST,
                             preferred_element_type=F32) + rb_ref[...]
    ex = jnp.exp(logits - jnp.max(logits, axis=0, keepdims=True))
    probs = ex / jnp.sum(ex, axis=0, keepdims=True)

    top1_i, top1_v, top2_i, top2_v, score = [], [], [], [], []
    for grp in range(N_EXPERT_GROUPS):
        p = [probs[grp * EXPERTS_PER_GROUP + j:grp * EXPERTS_PER_GROUP + j + 1, :]
             for j in range(EXPERTS_PER_GROUP)]
        i1, v1 = _argmax4_first(p)
        rest = [jnp.where(i1 == j, -1.0, p[j]) for j in range(EXPERTS_PER_GROUP)]
        i2, v2 = _argmax4_first(rest)
        top1_i.append(i1)
        top1_v.append(v1)
        top2_i.append(i2)
        top2_v.append(v2)
        score.append(v1 + v2)
    gsel, _ = _argmax4_first(score)
    p1 = _select4(gsel, top1_v)
    p2 = _select4(gsel, top2_v)
    e1 = gsel * EXPERTS_PER_GROUP + _select4(gsel, top1_i)
    e2 = gsel * EXPERTS_PER_GROUP + _select4(gsel, top2_i)
    psum = p1 + p2
    e_ref[0:1, :] = e1
    e_ref[1:2, :] = e2
    gt_ref[0:1, :] = p1 / psum
    gt_ref[1:2, :] = p2 / psum

    eid = lax.broadcasted_iota(jnp.int32, (N_EXPERTS, tm), 0)
    earlier = (lax.broadcasted_iota(jnp.int32, (tm, tm), 0)
               < lax.broadcasted_iota(jnp.int32, (tm, tm), 1)).astype(BF16)
    base = carry_scr[:, 0:1]
    for slot, e_sel in enumerate((e1, e2)):
        onehot = (eid == e_sel).astype(F32)
        before = jnp.dot(onehot.astype(BF16), earlier, preferred_element_type=F32)
        rank = jnp.sum(onehot * (before + base), axis=0, keepdims=True)
        rk_ref[slot:slot + 1, :] = rank.astype(jnp.int32)
        base = base + jnp.sum(onehot, axis=1, keepdims=True)
    carry_scr[...] = jnp.broadcast_to(base, carry_scr.shape)
    cnt_ref[...] = carry_scr[...]


def _router(x, sc, sh, rw_t, rb_col):
    s = x.shape[0]
    tm = TM_ROUTE
    slot_spec = pl.BlockSpec((TOP_K, tm), lambda i: (0, i))
    full = lambda shape: pl.BlockSpec(shape, lambda i: (0, 0))
    return pl.pallas_call(
        _router_kernel,
        grid=(s // tm,),
        in_specs=[_row_spec(tm, D_MODEL), _vec_spec(D_MODEL), _vec_spec(D_MODEL),
                  full((N_EXPERTS, D_MODEL)), full((N_EXPERTS, 1))],
        out_specs=[slot_spec, slot_spec, slot_spec, full((N_EXPERTS, LANES_V7X))],
        out_shape=[jax.ShapeDtypeStruct((TOP_K, s), jnp.int32), jax.ShapeDtypeStruct((TOP_K, s), F32),
                   jax.ShapeDtypeStruct((TOP_K, s), jnp.int32),
                   jax.ShapeDtypeStruct((N_EXPERTS, LANES_V7X), F32)],
        scratch_shapes=[pltpu.VMEM((N_EXPERTS, LANES_V7X), F32)],
        compiler_params=_cparams(("arbitrary",)),
        name="router",
    )(x, sc, sh, rw_t, rb_col)


def _row_gather(src_hbm, idx_ref, base, n, dst, sem, priorities=(0, 1), part=(0, 1)):
    k, m = part
    for r in range(k * n // m, (k + 1) * n // m):
        tok = idx_ref[base + r]
        copy = pltpu.make_async_copy(src_hbm.at[pl.ds(tok, 1), :], dst.at[pl.ds(r, 1), :], sem)
        copy.start(priority=priorities[r % len(priorities)])


def _row_gather_wait(src_hbm, n, dst, sem):
    pltpu.make_async_copy(src_hbm.at[pl.ds(0, n), :], dst, sem).wait()


class _WeightStream:
    def __init__(self, layer, w_hbm, stage, wbuf, sem):
        self.layer, self.w_hbm, self.stage, self.wbuf, self.sem = layer, w_hbm, stage, wbuf, sem

    def _copy(self, mat, expert, q, slot):
        kind = 0 if mat < 2 else 1
        rows = self.stage[kind].shape[1]
        src = self.w_hbm[mat].at[self.layer, expert, pl.ds(pl.multiple_of(q * rows, rows), rows), :]
        return pltpu.make_async_copy(src, self.stage[kind].at[slot], self.sem.at[kind, slot])

    def _per_matrix(self, c, fn):
        mat, q, slot = c // W_CHUNKS, c % W_CHUNKS, c % 2
        for m in range(3):
            @pl.when(mat == m)
            def _(m=m):
                fn(m, q, slot)

    def start(self, c, expert):
        self._per_matrix(c, lambda m, q, slot: self._copy(m, expert, q, slot).start(priority=W_STREAM_PRIORITY))

    def finish(self, c, dst):
        def fn(m, q, slot):
            self._copy(m, 0, q, slot).wait()
            kind = 0 if m < 2 else 1
            rows = self.stage[kind].shape[1]
            self.wbuf[m][dst, pl.ds(pl.multiple_of(q * rows, rows), rows), :] = self.stage[kind][slot].astype(BF16)

        self._per_matrix(c, fn)

    def prefetch(self, expert, lo, hi):
        for t in range(2):
            @pl.when(lo + t < hi)
            def _(t=t):
                self.start(lo + t, expert)

    def drain(self, expert, dst, lo, upto, hi):
        def body(c, carry):
            self.finish(c, dst)

            @pl.when(c + 2 < hi)
            def _():
                self.start(c + 2, expert)

            return carry

        lax.fori_loop(lo, upto, body, 0)


def _ffn_kernel(rt_ref, be_ref, nbu_ref, ws_ref, nx_ref, lo_ref, hi_ref,
                x_hbm, sc_ref, sh_ref, wg_hbm, wu_hbm, wd_hbm, o_ref,
                xbuf, sem, wg_buf, wu_buf, wd_buf, stage_in, stage_dn, wsem, *, layer):
    i = pl.program_id(0)
    nbu = nbu_ref[0]
    tm = xbuf.shape[1]
    slot = i % 2
    stream = _WeightStream(layer, (wg_hbm, wu_hbm, wd_hbm), (stage_in, stage_dn), (wg_buf, wu_buf, wd_buf), wsem)
    n_chunks = 3 * W_CHUNKS

    @pl.when(i == 0)
    def _():
        _row_gather(x_hbm, rt_ref, 0, tm, xbuf.at[0], sem.at[0], ROW_GATHER_PRIORITY)
        stream.prefetch(be_ref[0], 0, n_chunks)
        stream.drain(be_ref[0], ws_ref[0], 0, n_chunks, n_chunks)

    @pl.when(i < nbu)
    def _():
        ws, nxt, lo, hi = ws_ref[i], nx_ref[i], lo_ref[i], hi_ref[i]
        stream.prefetch(nxt, lo, hi)
        _row_gather_wait(x_hbm, tm, xbuf.at[slot], sem.at[slot])

        def gather_next(k):
            _row_gather(x_hbm, rt_ref, (i + 1) * tm, tm, xbuf.at[1 - slot], sem.at[1 - slot],
                        ROW_GATHER_PRIORITY, part=(k, 3))

        u = (xbuf[slot] * (1.0 + sc_ref[...]) + sh_ref[...]).astype(BF16)
        mid1, mid2 = jnp.minimum(lo + 2, hi), jnp.minimum(lo + 4, hi)
        gather_next(0)
        gt = jnp.dot(u, wg_buf[ws], preferred_element_type=F32)
        stream.drain(nxt, 1 - ws, lo, mid1, hi)
        gather_next(1)
        up = jnp.dot(u, wu_buf[ws], preferred_element_type=F32)
        stream.drain(nxt, 1 - ws, mid1, mid2, hi)
        gather_next(2)
        h = (gt * jax.nn.sigmoid(gt) * up).astype(BF16)
        o_ref[...] = jnp.dot(h, wd_buf[ws], preferred_element_type=F32)
        stream.drain(nxt, 1 - ws, mid2, hi, hi)

    @pl.when(i >= nbu)
    def _():
        @pl.when(i == nbu)
        def _():
            _row_gather_wait(x_hbm, tm, xbuf.at[slot], sem.at[slot])

        o_ref[...] = jnp.zeros_like(o_ref)


def _moe_ffn(x, sc, sh, w_gate, w_up, w_down, layer, row_token, block_expert, nb_used, w_slot, next_expert,
             chunk_lo, chunk_hi):
    tm = TM_MOE
    p_rows = row_token.shape[0]
    nb = p_rows // tm
    vec = pl.BlockSpec((1, D_MODEL), lambda i, *_: (0, 0))
    hbm = pl.BlockSpec(memory_space=pl.ANY)
    return pl.pallas_call(
        functools.partial(_ffn_kernel, layer=layer),
        grid_spec=pltpu.PrefetchScalarGridSpec(
            num_scalar_prefetch=7,
            grid=(nb,),
            in_specs=[hbm, vec, vec, hbm, hbm, hbm],
            out_specs=pl.BlockSpec((tm, D_MODEL), lambda i, *_: (i, 0)),
            scratch_shapes=[pltpu.VMEM((2, tm, D_MODEL), F32), pltpu.SemaphoreType.DMA((2,)),
                            pltpu.VMEM((2, D_MODEL, D_EXPERT), BF16), pltpu.VMEM((2, D_MODEL, D_EXPERT), BF16),
                            pltpu.VMEM((2, D_EXPERT, D_MODEL), BF16),
                            pltpu.VMEM((2, D_MODEL // W_CHUNKS, D_EXPERT), F32),
                            pltpu.VMEM((2, D_EXPERT // W_CHUNKS, D_MODEL), F32),
                            pltpu.SemaphoreType.DMA((2, 2))]),
        out_shape=jax.ShapeDtypeStruct((p_rows, D_MODEL), F32),
        compiler_params=pltpu.CompilerParams(dimension_semantics=("arbitrary",),
                                             vmem_limit_bytes=VMEM_LIMIT_FFN_V7X),
        name="moe_ffn",
    )(row_token, block_expert, nb_used, w_slot, next_expert, chunk_lo, chunk_hi,
      x, sc, sh, w_gate, w_up, w_down)


def _combine_kernel(dest_ref, rows_hbm, gt_ref, x_ref, gate_ref, g_ref, b_ref, out_ref, ybuf, sem):
    i = pl.program_id(0)
    n = pl.num_programs(0)
    tm = x_ref.shape[0]
    s = tm * n
    slot = i % 2

    def start(tile, to):
        for k in range(TOP_K):
            _row_gather(rows_hbm, dest_ref, k * s + tile * tm, tm, ybuf.at[to, k], sem.at[to, k])

    def wait(at):
        for k in range(TOP_K):
            _row_gather_wait(rows_hbm, tm, ybuf.at[at, k], sem.at[at, k])

    @pl.when(i == 0)
    def _():
        start(0, 0)

    wait(slot)
    start(jnp.minimum(i + 1, n - 1), 1 - slot)
    gt = gt_ref[...]
    y = gt[:, 0:1] * ybuf[slot, 0] + gt[:, 1:2] * ybuf[slot, 1]
    out_ref[...] = _deepnorm(x_ref[...], y, gate_ref[...], g_ref[...], b_ref[...])

    @pl.when(i == n - 1)
    def _():
        wait(1 - slot)


def _moe_combine(dest_flat, out_rows, gates_t, x, gate, g, b):
    s = x.shape[0]
    tm = TM_COMB
    vec = pl.BlockSpec((1, D_MODEL), lambda i, dst: (0, 0))
    return pl.pallas_call(
        _combine_kernel,
        grid_spec=pltpu.PrefetchScalarGridSpec(
            num_scalar_prefetch=1,
            grid=(s // tm,),
            in_specs=[pl.BlockSpec(memory_space=pl.ANY),
                      pl.BlockSpec((tm, TOP_K), lambda i, dst: (i, 0)),
                      pl.BlockSpec((tm, D_MODEL), lambda i, dst: (i, 0)), vec, vec, vec],
            out_specs=pl.BlockSpec((tm, D_MODEL), lambda i, dst: (i, 0)),
            scratch_shapes=[pltpu.VMEM((2, TOP_K, tm, D_MODEL), F32), pltpu.SemaphoreType.DMA((2, TOP_K))]),
        out_shape=jax.ShapeDtypeStruct((s, D_MODEL), F32),
        compiler_params=_cparams(("arbitrary",)),
        name="moe_combine",
    )(dest_flat, out_rows, gates_t, x, gate, g, b)


def _moe_layer(x, sc, sh, gate, g, b, rw_t, rb_col, w_gate, w_up, w_down, layer):
    s = x.shape[0]
    tm = TM_MOE
    expert, gates, rank, cnt = _router(x, sc, sh, rw_t, rb_col)
    counts = cnt[:, 0].astype(jnp.int32)
    padded = (counts + tm - 1) // tm * tm
    pad_end = jnp.cumsum(padded)
    pad_start = pad_end - padded
    eids = jnp.arange(N_EXPERTS, dtype=jnp.int32)
    start_of = jnp.sum(jnp.where(expert[..., None] == eids, pad_start, 0), axis=-1)
    dest = (start_of + rank).reshape(-1)
    p_rows = TOP_K * s + N_EXPERTS * tm
    token = jnp.tile(jnp.arange(s, dtype=jnp.int32), TOP_K)
    row_token = jnp.zeros((p_rows,), jnp.int32).at[dest].set(token)
    nb = p_rows // tm
    block_row0 = jnp.arange(nb, dtype=jnp.int32) * tm
    block_expert = jnp.minimum(jnp.sum(block_row0[:, None] >= pad_end[None, :], axis=-1),
                               N_EXPERTS - 1).astype(jnp.int32)
    nb_used = (pad_end[-1:] // tm).astype(jnp.int32)
    has = padded > 0
    order = jnp.cumsum(has.astype(jnp.int32)) - 1
    later = jnp.logical_and(eids[None, :] > eids[:, None], has[None, :])
    next_of = jnp.min(jnp.where(later, eids[None, :], N_EXPERTS), axis=-1)
    of_block = lambda v: jnp.sum(jnp.where(block_expert[:, None] == eids[None, :], v[None, :], 0), axis=-1)
    run_len = jnp.maximum(of_block(padded) // tm, 1)
    run_pos = jnp.arange(nb, dtype=jnp.int32) - of_block(pad_start) // tm
    nxt = of_block(next_of)
    streams = jnp.logical_and(nxt < N_EXPERTS, jnp.arange(nb) < nb_used[0])
    n_chunks = 3 * W_CHUNKS
    chunk_lo = jnp.where(streams, n_chunks * run_pos // run_len, 0).astype(jnp.int32)
    chunk_hi = jnp.where(streams, n_chunks * (run_pos + 1) // run_len, 0).astype(jnp.int32)
    next_expert = jnp.where(streams, nxt, block_expert).astype(jnp.int32)
    w_slot = (of_block(order) % 2).astype(jnp.int32)
    out_rows = _moe_ffn(x, sc, sh, w_gate, w_up, w_down, layer, row_token, block_expert, nb_used,
                        w_slot, next_expert, chunk_lo, chunk_hi)
    return _moe_combine(dest.astype(jnp.int32), out_rows, gates.T, x, gate, g, b)


def kernel(x, c, positions, ada_w, ada_b, ln_g, ln_b, attn_w_qkv, attn_w_o, conv_w_pw1, conv_w_dw, conv_ln_g,
           conv_ln_b, conv_w_pw2, sc_w_in, sc_w_conv, sc_w_out, router_w, router_b, moe_w_gate, moe_w_up,
           moe_w_down):
    batch, s, d = x.shape
    assert batch == 1 and d == D_MODEL and s % (DILATIONS[-1] * Q_BLOCK) == 0
    xs = x.reshape(s, d)

    mod = _adaln(c.reshape(d, 1), ada_w, ada_b)
    half = HEAD_DIM // 2
    inv_freq = ROPE_THETA ** (-jnp.arange(half, dtype=F32) / half)
    freq_row = jnp.concatenate([inv_freq, inv_freq]).reshape(1, HEAD_DIM)
    cos, sin = _rope_tables(positions.reshape(s, 1), freq_row)

    rw_t = router_w.T
    rb_col = router_b.reshape(N_EXPERTS, 1)
    vec = lambda a: a.reshape(1, d)

    for i in range(DEPTH):
        sh1, sc1, g1, sh2, sc2, g2 = [mod[i, :, k * d:(k + 1) * d] for k in range(6)]
        m, j = i % N_MIXERS, i // N_MIXERS
        lg, lb = vec(ln_g[i, 0]), vec(ln_b[i, 0])
        if m == 0:
            outs, lses = [], []
            for grp in range(len(DILATIONS)):
                o_g, l_g = _attention(_qkv_proj(xs, sc1, sh1, attn_w_qkv, j, cos, sin, grp), grp)
                outs.append(o_g)
                lses.append(l_g)
            xs = _attn_out(outs, lses, attn_w_o[j].astype(BF16), xs, g1, lg, lb)
        elif m == 1:
            glu = _gated_in(_glu_in_kernel, 2, 1, xs, sc1, sh1, conv_w_pw1[j].astype(BF16), "conv_in")
            xs = _conv_out(glu, conv_w_dw[j], vec(conv_ln_g[j]), vec(conv_ln_b[j]),
                           conv_w_pw2[j].astype(BF16), xs, g1, lg, lb)
        else:
            bgate, ch = _gated_in(_sc_in_kernel, 3, 2, xs, sc1, sh1, sc_w_in[j].astype(BF16), "sc_in")
            xs = _sc_out(ch, bgate, sc_w_conv[j], sc_w_out[j].astype(BF16), xs, g1, lg, lb)
        xs = _moe_layer(xs, sc2, sh2, g2, vec(ln_g[i, 1]), vec(ln_b[i, 1]), rw_t, rb_col,
                        moe_w_gate, moe_w_up, moe_w_down, i)
    return xs.reshape(batch, s, d)
```

```python
import functools
import math

import jax
import jax.numpy as jnp
from jax import lax
from jax.experimental import pallas as pl
from jax.experimental.pallas import tpu as pltpu

F32 = jnp.float32
BF16 = jnp.bfloat16

D_MODEL = 2048
DEPTH = 4
N_MIXERS = 3
ATTN_HEADS = 16
HEAD_DIM = 128
HD = ATTN_HEADS * HEAD_DIM
DILATIONS = (1, 4, 16)
Q_BLOCK = 128
ROPE_THETA = 10000.0
CONV_KERNEL = 31
SHORT_CONV_KERNEL = 3
N_EXPERTS = 16
N_EXPERT_GROUPS = 4
EXPERTS_PER_GROUP = 4
TOP_K = 2
D_EXPERT = 1408
ALPHA = (2 * DEPTH) ** 0.25
LN_EPS = 1e-5
NEG_INF = -1e30

LANES_V7X = 128
SUBLANES_V7X = 8
MXU_COLS_V7X = 256
VMEM_LIMIT_V7X = 56 * 1024 * 1024
VMEM_LIMIT_FFN_V7X = 61 * 1024 * 1024
W_CHUNKS = 8
ROW_GATHER_PRIORITY = (0, 1)
W_STREAM_PRIORITY = 0
TM_PROJ = 512
TM_QKV = 1024
TN_PROJ = 1024
TN_GLU = 1024
TM_OUT = 256
TM_ROUTE = 512
TM_MOE = 256
TM_COMB = 256
TN_ADA = 1024
HALO_CONV = 32
HALO_SC = 8
CONV_ROWS = 64
CONV_COLS = 256


def _cparams(sem):
    return pltpu.CompilerParams(dimension_semantics=sem, vmem_limit_bytes=VMEM_LIMIT_V7X)


def _ln_rows(z, g, b):
    mu = jnp.mean(z, axis=-1, keepdims=True)
    zc = z - mu
    var = jnp.mean(zc * zc, axis=-1, keepdims=True)
    return zc * lax.rsqrt(var + LN_EPS) * g + b


def _deepnorm(x, y, gate, g, b):
    return _ln_rows(ALPHA * x + (1.0 + gate) * y, g, b)


def _row_spec(tm, width):
    return pl.BlockSpec((tm, width), lambda i: (i, 0))


def _vec_spec(width):
    return pl.BlockSpec((1, width), lambda i: (0, 0))


def _adaln_kernel(c_ref, w_ref, b_ref, o_ref):
    c = c_ref[...]
    ca = c * jax.nn.sigmoid(c)
    o_ref[...] = jnp.sum(w_ref[...] * ca, axis=0, keepdims=True) + b_ref[...]


def _adaln(c_col, ada_w, ada_b):
    depth, d, n = ada_w.shape
    return pl.pallas_call(
        _adaln_kernel,
        grid=(depth, n // TN_ADA),
        in_specs=[pl.BlockSpec((d, 1), lambda l, j: (0, 0)),
                  pl.BlockSpec((None, d, TN_ADA), lambda l, j: (l, 0, j)),
                  pl.BlockSpec((None, 1, TN_ADA), lambda l, j: (l, 0, j))],
        out_specs=pl.BlockSpec((None, 1, TN_ADA), lambda l, j: (l, 0, j)),
        out_shape=jax.ShapeDtypeStruct((depth, 1, n), F32),
        compiler_params=_cparams(("arbitrary", "arbitrary")),
        name="adaln",
    )(c_col, ada_w, ada_b.reshape(depth, 1, n))


def _rope_kernel(pos_ref, freq_ref, cos_ref, sin_ref):
    ang = pos_ref[...].astype(F32) * freq_ref[...]
    lane = lax.broadcasted_iota(jnp.int32, ang.shape, 1)
    s = jnp.sin(ang)
    cos_ref[...] = jnp.cos(ang)
    sin_ref[...] = jnp.where(lane < HEAD_DIM // 2, -s, s)


def _rope_tables(pos_col, freq_row):
    s = pos_col.shape[0]
    tm = 1024
    spec = pl.BlockSpec((tm, HEAD_DIM), lambda i: (i, 0))
    return pl.pallas_call(
        _rope_kernel,
        grid=(s // tm,),
        in_specs=[pl.BlockSpec((tm, 1), lambda i: (i, 0)), _vec_spec(HEAD_DIM)],
        out_specs=[spec, spec],
        out_shape=[jax.ShapeDtypeStruct((s, HEAD_DIM), F32)] * 2,
        compiler_params=_cparams(("arbitrary",)),
        name="rope_tables",
    )(pos_col, freq_row)


def _qkv_kernel(x_ref, sc_ref, sh_ref, w_ref, cos_ref, sin_ref, o_ref, u_scr, acc_scr, *, d, n_rope):
    j = pl.program_id(1)
    nh = acc_scr.shape[0]
    tm = x_ref.shape[0]
    skew = _residue_stride(d)

    @pl.when(j == 0)
    def _():
        u_scr[...] = (x_ref[...] * (1.0 + sc_ref[...]) + sh_ref[...]).astype(BF16)

    is_rope = j < n_rope
    cos = cos_ref[...]
    sin = sin_ref[...]
    u = u_scr[...]
    for c in range(nh // 2):
        w = w_ref[:, c * MXU_COLS_V7X:(c + 1) * MXU_COLS_V7X].astype(BF16)
        acc = jnp.dot(u, w, preferred_element_type=F32)
        for h in (2 * c, 2 * c + 1):
            a = acc[:, (h - 2 * c) * HEAD_DIM:(h - 2 * c + 1) * HEAD_DIM]
            a = jnp.where(is_rope, a * cos + pltpu.roll(a, HEAD_DIM // 2, 1) * sin, a)
            hs = slice(h * HEAD_DIM, (h + 1) * HEAD_DIM)
            if d == 1:
                o_ref[0, :, hs] = a.astype(BF16)
            else:
                if skew == d:
                    acc_scr[h] = a
                else:
                    for k in range(tm // d):
                        acc_scr[h, k * skew:k * skew + d] = a[k * d:(k + 1) * d]
                for r in range(d):
                    o_ref[r, :, hs] = acc_scr[h, pl.ds(r, tm // d, stride=skew), :].astype(BF16)


def _residue_stride(d):
    return d + 1 if d % SUBLANES_V7X == 0 else d


def _qkv_proj(x, sc, sh, w_all, layer, cos, sin, group):
    s = x.shape[0]
    d = DILATIONS[group]
    tm, tn = TM_QKV, TN_PROJ
    ncol = 3 * HD
    col0 = group * (ncol // tn)
    return pl.pallas_call(
        functools.partial(_qkv_kernel, d=d, n_rope=2 * HD // tn),
        grid=(s // tm, ncol // tn),
        in_specs=[pl.BlockSpec((tm, D_MODEL), lambda i, j: (i, 0)),
                  pl.BlockSpec((1, D_MODEL), lambda i, j: (0, 0)),
                  pl.BlockSpec((1, D_MODEL), lambda i, j: (0, 0)),
                  pl.BlockSpec((None, D_MODEL, tn), lambda i, j: (layer, 0, col0 + j)),
                  pl.BlockSpec((tm, HEAD_DIM), lambda i, j: (i, 0)),
                  pl.BlockSpec((tm, HEAD_DIM), lambda i, j: (i, 0))],
        out_specs=pl.BlockSpec((d, tm // d, tn), lambda i, j: (0, i, j)),
        out_shape=jax.ShapeDtypeStruct((d, s // d, ncol), BF16),
        scratch_shapes=[pltpu.VMEM((tm, D_MODEL), BF16),
                        pltpu.VMEM((tn // HEAD_DIM, tm // d * _residue_stride(d), HEAD_DIM), F32)],
        compiler_params=_cparams(("arbitrary", "arbitrary")),
        name=f"qkv_proj_g{group}",
    )(x, sc, sh, w_all, cos, sin)


def _attn_kernel(q_ref, kc_ref, vc_ref, o_ref, lse_ref, k_scr, v_scr, s_scr, p_scr, *, nblk):
    b = pl.program_id(0)
    not_first = (b % nblk) != 0

    @pl.when(b == 0)
    def _():
        k_scr[0:Q_BLOCK] = jnp.zeros((Q_BLOCK, HD), BF16)
        v_scr[0:Q_BLOCK] = jnp.zeros((Q_BLOCK, HD), BF16)

    k_scr[Q_BLOCK:] = kc_ref[...]
    v_scr[Q_BLOCK:] = vc_ref[...]
    nt = (((1,), (1,)), ((), ()))
    for h in range(ATTN_HEADS):
        sl = slice(h * HEAD_DIM, (h + 1) * HEAD_DIM)
        s_scr[h] = lax.dot_general(q_ref[:, sl], k_scr[:, sl], nt, preferred_element_type=F32)

    row = lax.broadcasted_iota(jnp.int32, (Q_BLOCK, 2 * Q_BLOCK), 0)
    col = lax.broadcasted_iota(jnp.int32, (Q_BLOCK, 2 * Q_BLOCK), 1)
    prev_ok = jnp.logical_and(jnp.logical_and(col < Q_BLOCK, col >= row), not_first)
    valid = jnp.logical_or(prev_ok, jnp.logical_and(col >= Q_BLOCK, col - Q_BLOCK <= row))
    scale = 1.0 / math.sqrt(HEAD_DIM)
    s = jnp.where(valid[None], s_scr[...], NEG_INF)
    m = jnp.max(s, axis=2, keepdims=True)
    p = jnp.exp2((s - m) * (scale * math.log2(math.e)))
    den = jnp.sum(p, axis=2, keepdims=True)
    p_scr[...] = p.astype(BF16)
    inv = 1.0 / den
    lse = m * scale + jnp.log(den)
    for h in range(ATTN_HEADS):
        sl = slice(h * HEAD_DIM, (h + 1) * HEAD_DIM)
        o = jnp.dot(p_scr[h], v_scr[:, sl], preferred_element_type=F32)
        o_ref[:, sl] = (o * inv[h]).astype(BF16)
        lse_ref[:, h:h + 1] = lse[h]
    k_scr[0:Q_BLOCK] = kc_ref[...]
    v_scr[0:Q_BLOCK] = vc_ref[...]


def _attention(qkv, group):
    d, l, ncol = qkv.shape
    s = d * l
    flat = qkv.reshape(s, ncol)
    nblk = l // Q_BLOCK
    cur = lambda c: pl.BlockSpec((Q_BLOCK, HD), lambda b: (b, c))
    return pl.pallas_call(
        functools.partial(_attn_kernel, nblk=nblk),
        grid=(s // Q_BLOCK,),
        in_specs=[cur(0), cur(1), cur(2)],
        out_specs=[pl.BlockSpec((Q_BLOCK, HD), lambda b: (b, 0)),
                   pl.BlockSpec((Q_BLOCK, ATTN_HEADS), lambda b: (b, 0))],
        out_shape=[jax.ShapeDtypeStruct((s, HD), BF16), jax.ShapeDtypeStruct((s, ATTN_HEADS), F32)],
        scratch_shapes=[pltpu.VMEM((2 * Q_BLOCK, HD), BF16), pltpu.VMEM((2 * Q_BLOCK, HD), BF16),
                        pltpu.VMEM((ATTN_HEADS, Q_BLOCK, 2 * Q_BLOCK), F32),
                        pltpu.VMEM((ATTN_HEADS, Q_BLOCK, 2 * Q_BLOCK), BF16)],
        compiler_params=_cparams(("arbitrary",)),
        name=f"dil_attn_g{group}",
    )(flat, flat, flat)


def _attn_out_kernel(o0_ref, o1_ref, o2_ref, l0_ref, l1_ref, l2_ref, w_ref, x_ref, gate_ref, g_ref, b_ref,
                     out_ref, o_scr, a_scr):
    tm = x_ref.shape[0]
    for gi, o_ref in enumerate((o1_ref, o2_ref)):
        d = DILATIONS[gi + 1]
        for r in range(d):
            for h in range(ATTN_HEADS):
                o_scr[gi, h, pl.ds(r, tm // d, stride=d), :] = (
                    o_ref[r, :, h * HEAD_DIM:(h + 1) * HEAD_DIM].astype(F32))
    l0 = l0_ref[...]
    l1 = l1_ref[...]
    l2 = l2_ref[...]
    m = jnp.maximum(jnp.maximum(l0, l1), l2)
    e0 = jnp.exp(l0 - m)
    e1 = jnp.exp(l1 - m)
    e2 = jnp.exp(l2 - m)
    z = e0 + e1 + e2
    w0 = e0 / z
    w1 = e1 / z
    w2 = e2 / z
    for h in range(ATTN_HEADS):
        sl = slice(h * HEAD_DIM, (h + 1) * HEAD_DIM)
        a = (w0[:, h:h + 1] * o0_ref[:, sl].astype(F32)
             + w1[:, h:h + 1] * o_scr[0, h]
             + w2[:, h:h + 1] * o_scr[1, h])
        a_scr[:, sl] = a.astype(BF16)
    y = jnp.dot(a_scr[...], w_ref[...], preferred_element_type=F32)
    out_ref[...] = _deepnorm(x_ref[...], y, gate_ref[...], g_ref[...], b_ref[...])


def _attn_out(outs, lses, w_bf, x, gate, g, b):
    s = x.shape[0]
    tm = TM_OUT
    d1, d2 = DILATIONS[1], DILATIONS[2]
    o1 = outs[1].reshape(d1, s // d1, HD)
    o2 = outs[2].reshape(d2, s // d2, HD)
    l1 = lses[1].reshape(d1, s // d1, ATTN_HEADS).transpose(1, 0, 2).reshape(s, ATTN_HEADS)
    l2 = lses[2].reshape(d2, s // d2, ATTN_HEADS).transpose(1, 0, 2).reshape(s, ATTN_HEADS)
    perm = lambda d, w: pl.BlockSpec((d, tm // d, w), lambda i: (0, i, 0))
    return pl.pallas_call(
        _attn_out_kernel,
        grid=(s // tm,),
        in_specs=[_row_spec(tm, HD), perm(d1, HD), perm(d2, HD),
                  _row_spec(tm, ATTN_HEADS), _row_spec(tm, ATTN_HEADS), _row_spec(tm, ATTN_HEADS),
                  pl.BlockSpec((HD, D_MODEL), lambda i: (0, 0)),
                  _row_spec(tm, D_MODEL), _vec_spec(D_MODEL), _vec_spec(D_MODEL), _vec_spec(D_MODEL)],
        out_specs=_row_spec(tm, D_MODEL),
        out_shape=jax.ShapeDtypeStruct((s, D_MODEL), F32),
        scratch_shapes=[pltpu.VMEM((2, ATTN_HEADS, tm, HEAD_DIM), F32), pltpu.VMEM((tm, HD), BF16)],
        compiler_params=_cparams(("arbitrary",)),
        name="attn_out",
    )(outs[0], o1, o2, lses[0], l1, l2, w_bf, x, gate, g, b)


def _modulate_once(x_ref, sc_ref, sh_ref, u_scr):
    @pl.when(pl.program_id(1) == 0)
    def _():
        u_scr[...] = (x_ref[...] * (1.0 + sc_ref[...]) + sh_ref[...]).astype(BF16)


def _glu_in_kernel(x_ref, sc_ref, sh_ref, wa_ref, wg_ref, o_ref, u_scr):
    _modulate_once(x_ref, sc_ref, sh_ref, u_scr)
    u = u_scr[...]
    a = jnp.dot(u, wa_ref[...], preferred_element_type=F32)
    gt = jnp.dot(u, wg_ref[...], preferred_element_type=F32)
    o_ref[...] = a * jax.nn.sigmoid(gt)


def _sc_in_kernel(x_ref, sc_ref, sh_ref, wb_ref, wc_ref, wh_ref, b_ref, ch_ref, u_scr):
    _modulate_once(x_ref, sc_ref, sh_ref, u_scr)
    u = u_scr[...]
    b_ref[...] = jnp.dot(u, wb_ref[...], preferred_element_type=F32)
    c = jnp.dot(u, wc_ref[...], preferred_element_type=F32)
    h = jnp.dot(u, wh_ref[...], preferred_element_type=F32)
    ch_ref[...] = c * h


def _gated_in(kernel, n_parts, n_out, x, sc, sh, w_bf, name):
    s = x.shape[0]
    tm, tn = TM_PROJ, TN_GLU
    nj = D_MODEL // tn
    w_specs = [pl.BlockSpec((D_MODEL, tn), functools.partial(lambda i, j, p: (0, p * nj + j), p=p))
               for p in range(n_parts)]
    out_spec = pl.BlockSpec((tm, tn), lambda i, j: (i, j))
    out_shape = jax.ShapeDtypeStruct((s, D_MODEL), F32)
    return pl.pallas_call(
        kernel,
        grid=(s // tm, nj),
        in_specs=[pl.BlockSpec((tm, D_MODEL), lambda i, j: (i, 0)),
                  pl.BlockSpec((1, D_MODEL), lambda i, j: (0, 0)),
                  pl.BlockSpec((1, D_MODEL), lambda i, j: (0, 0))] + w_specs,
        out_specs=[out_spec] * n_out if n_out > 1 else out_spec,
        out_shape=[out_shape] * n_out if n_out > 1 else out_shape,
        scratch_shapes=[pltpu.VMEM((tm, D_MODEL), BF16)],
        compiler_params=_cparams(("arbitrary", "arbitrary")),
        name=name,
    )(x, sc, sh, *([w_bf] * n_parts))


def _fill_halo(hal_scr, prev_ref, cur_ref, halo):
    is_first = pl.program_id(0) == 0
    tm = cur_ref.shape[0]
    hal_scr[pl.ds(0, halo), :] = jnp.where(is_first, 0.0, prev_ref[...])
    hal_scr[pl.ds(halo, tm), :] = cur_ref[...]
    hal_scr[pl.ds(halo + tm, SUBLANES_V7X), :] = jnp.zeros((SUBLANES_V7X, D_MODEL), F32)


def _dwconv(hal_scr, w_ref, dst_scr, ph_scr, halo, ksize, tm):
    off = halo - (ksize - 1)
    win = CONV_ROWS + SUBLANES_V7X
    for r0 in range(0, tm, CONV_ROWS):
        for c0 in range(0, D_MODEL, CONV_COLS):
            cs = slice(c0, c0 + CONV_COLS)
            out = None
            for phase in range(SUBLANES_V7X):
                part = None
                for k in range(ksize):
                    if (off + k) % SUBLANES_V7X != phase:
                        continue
                    base = r0 + off + k - phase
                    term = w_ref[k:k + 1, cs] * hal_scr[base:base + win, cs]
                    part = term if part is None else part + term
                if part is None:
                    continue
                if phase == 0:
                    shifted = part[:CONV_ROWS]
                else:
                    ph_scr[phase] = part
                    shifted = ph_scr[phase, phase:phase + CONV_ROWS, :]
                out = shifted if out is None else out + shifted
            dst_scr[r0:r0 + CONV_ROWS, cs] = out


def _conv_out_kernel(cur_ref, prev_ref, wdw_ref, cg_ref, cb_ref, w_ref, x_ref, gate_ref, g_ref, b_ref,
                     out_ref, hal_scr, h_scr, ph_scr):
    tm = x_ref.shape[0]
    _fill_halo(hal_scr, prev_ref, cur_ref, HALO_CONV)
    _dwconv(hal_scr, wdw_ref, h_scr, ph_scr, HALO_CONV, CONV_KERNEL, tm)
    hn = _ln_rows(h_scr[...], cg_ref[...], cb_ref[...])
    a = (hn * jax.nn.sigmoid(hn)).astype(BF16)
    y = jnp.dot(a, w_ref[...], preferred_element_type=F32)
    out_ref[...] = _deepnorm(x_ref[...], y, gate_ref[...], g_ref[...], b_ref[...])


def _sc_out_kernel(cur_ref, prev_ref, bg_ref, wdw_ref, w_ref, x_ref, gate_ref, g_ref, b_ref,
                   out_ref, hal_scr, h_scr, ph_scr):
    tm = x_ref.shape[0]
    _fill_halo(hal_scr, prev_ref, cur_ref, HALO_SC)
    _dwconv(hal_scr, wdw_ref, h_scr, ph_scr, HALO_SC, SHORT_CONV_KERNEL, tm)
    a = (bg_ref[...] * h_scr[...]).astype(BF16)
    y = jnp.dot(a, w_ref[...], preferred_element_type=F32)
    out_ref[...] = _deepnorm(x_ref[...], y, gate_ref[...], g_ref[...], b_ref[...])


def _conv_phase_scratch():
    return pltpu.VMEM((SUBLANES_V7X, CONV_ROWS + SUBLANES_V7X, CONV_COLS), F32)


def _halo_spec(tm, halo):
    per = tm // halo
    return pl.BlockSpec((halo, D_MODEL), lambda i: (jnp.maximum(i * per - 1, 0), 0))


def _conv_out(glu, w_dw, cg, cb, w_bf, x, gate, g, b):
    s = x.shape[0]
    tm = TM_OUT
    full = lambda shape: pl.BlockSpec(shape, lambda i: (0, 0))
    return pl.pallas_call(
        _conv_out_kernel,
        grid=(s // tm,),
        in_specs=[_row_spec(tm, D_MODEL), _halo_spec(tm, HALO_CONV), full((CONV_KERNEL, D_MODEL)),
                  _vec_spec(D_MODEL), _vec_spec(D_MODEL), full((D_MODEL, D_MODEL)),
                  _row_spec(tm, D_MODEL), _vec_spec(D_MODEL), _vec_spec(D_MODEL), _vec_spec(D_MODEL)],
        out_specs=_row_spec(tm, D_MODEL),
        out_shape=jax.ShapeDtypeStruct((s, D_MODEL), F32),
        scratch_shapes=[pltpu.VMEM((tm + HALO_CONV + SUBLANES_V7X, D_MODEL), F32), pltpu.VMEM((tm, D_MODEL), F32),
                        _conv_phase_scratch()],
        compiler_params=_cparams(("arbitrary",)),
        name="conv_out",
    )(glu, glu, w_dw, cg, cb, w_bf, x, gate, g, b)


def _sc_out(ch, bgate, w_dw, w_bf, x, gate, g, b):
    s = x.shape[0]
    tm = TM_OUT
    full = lambda shape: pl.BlockSpec(shape, lambda i: (0, 0))
    return pl.pallas_call(
        _sc_out_kernel,
        grid=(s // tm,),
        in_specs=[_row_spec(tm, D_MODEL), _halo_spec(tm, HALO_SC), _row_spec(tm, D_MODEL),
                  full((SHORT_CONV_KERNEL, D_MODEL)), full((D_MODEL, D_MODEL)),
                  _row_spec(tm, D_MODEL), _vec_spec(D_MODEL), _vec_spec(D_MODEL), _vec_spec(D_MODEL)],
        out_specs=_row_spec(tm, D_MODEL),
        out_shape=jax.ShapeDtypeStruct((s, D_MODEL), F32),
        scratch_shapes=[pltpu.VMEM((tm + HALO_SC + SUBLANES_V7X, D_MODEL), F32), pltpu.VMEM((tm, D_MODEL), F32),
                        _conv_phase_scratch()],
        compiler_params=_cparams(("arbitrary",)),
        name="sc_out",
    )(ch, ch, bgate, w_dw, w_bf, x, gate, g, b)


def _argmax4_first(v):
    i01 = jnp.where(v[1] > v[0], 1, 0)
    m01 = jnp.maximum(v[0], v[1])
    i23 = jnp.where(v[3] > v[2], 3, 2)
    m23 = jnp.maximum(v[2], v[3])
    return jnp.where(m23 > m01, i23, i01), jnp.maximum(m01, m23)


def _select4(idx, v):
    return jnp.where(idx == 0, v[0], jnp.where(idx == 1, v[1], jnp.where(idx == 2, v[2], v[3])))


def _router_kernel(x_ref, sc_ref, sh_ref, wt_ref, rb_ref, e_ref, gt_ref, rk_ref, cnt_ref, carry_scr):
    i = pl.program_id(0)
    tm = x_ref.shape[0]

    @pl.when(i == 0)
    def _():
        carry_scr[...] = jnp.zeros_like(carry_scr)

    u = x_ref[...] * (1.0 + sc_ref[...]) + sh_ref[...]
    logits = lax.dot_general(wt_ref[...], u, (((1,), (1,)), ((), ())), precision=lax.Precision.HIGHEST,
                             preferred_element_type=F32) + rb_ref[...]
    ex = jnp.exp(logits - jnp.max(logits, axis=0, keepdims=True))
    probs = ex / jnp.sum(ex, axis=0, keepdims=True)

    top1_i, top1_v, top2_i, top2_v, score = [], [], [], [], []
    for grp in range(N_EXPERT_GROUPS):
        p = [probs[grp * EXPERTS_PER_GROUP + j:grp * EXPERTS_PER_GROUP + j + 1, :]
             for j in range(EXPERTS_PER_GROUP)]
        i1, v1 = _argmax4_first(p)
        rest = [jnp.where(i1 == j, -1.0, p[j]) for j in range(EXPERTS_PER_GROUP)]
        i2, v2 = _argmax4_first(rest)
        top1_i.append(i1)
        top1_v.append(v1)
        top2_i.append(i2)
        top2_v.append(v2)
        score.append(v1 + v2)
    gsel, _ = _argmax4_first(score)
    p1 = _select4(gsel, top1_v)
    p2 = _select4(gsel, top2_v)
    e1 = gsel * EXPERTS_PER_GROUP + _select4(gsel, top1_i)
    e2 = gsel * EXPERTS_PER_GROUP + _select4(gsel, top2_i)
    psum = p1 + p2
    e_ref[0:1, :] = e1
    e_ref[1:2, :] = e2
    gt_ref[0:1, :] = p1 / psum
    gt_ref[1:2, :] = p2 / psum

    eid = lax.broadcasted_iota(jnp.int32, (N_EXPERTS, tm), 0)
    earlier = (lax.broadcasted_iota(jnp.int32, (tm, tm), 0)
               < lax.broadcasted_iota(jnp.int32, (tm, tm), 1)).astype(BF16)
    base = carry_scr[:, 0:1]
    for slot, e_sel in enumerate((e1, e2)):
        onehot = (eid == e_sel).astype(F32)
        before = jnp.dot(onehot.astype(BF16), earlier, preferred_element_type=F32)
        rank = jnp.sum(onehot * (before + base), axis=0, keepdims=True)
        rk_ref[slot:slot + 1, :] = rank.astype(jnp.int32)
        base = base + jnp.sum(onehot, axis=1, keepdims=True)
    carry_scr[...] = jnp.broadcast_to(base, carry_scr.shape)
    cnt_ref[...] = carry_scr[...]


def _router(x, sc, sh, rw_t, rb_col):
    s = x.shape[0]
    tm = TM_ROUTE
    slot_spec = pl.BlockSpec((TOP_K, tm), lambda i: (0, i))
    full = lambda shape: pl.BlockSpec(shape, lambda i: (0, 0))
    return pl.pallas_call(
        _router_kernel,
        grid=(s // tm,),
        in_specs=[_row_spec(tm, D_MODEL), _vec_spec(D_MODEL), _vec_spec(D_MODEL),
                  full((N_EXPERTS, D_MODEL)), full((N_EXPERTS, 1))],
        out_specs=[slot_spec, slot_spec, slot_spec, full((N_EXPERTS, LANES_V7X))],
        out_shape=[jax.ShapeDtypeStruct((TOP_K, s), jnp.int32), jax.ShapeDtypeStruct((TOP_K, s), F32),
                   jax.ShapeDtypeStruct((TOP_K, s), jnp.int32),
                   jax.ShapeDtypeStruct((N_EXPERTS, LANES_V7X), F32)],
        scratch_shapes=[pltpu.VMEM((N_EXPERTS, LANES_V7X), F32)],
        compiler_params=_cparams(("arbitrary",)),
        name="router",
    )(x, sc, sh, rw_t, rb_col)


def _row_gather(src_hbm, idx_ref, base, n, dst, sem, priorities=(0, 1)):
    for r in range(n):
        tok = idx_ref[base + r]
        copy = pltpu.make_async_copy(src_hbm.at[pl.ds(tok, 1), :], dst.at[pl.ds(r, 1), :], sem)
        copy.start(priority=priorities[r % len(priorities)])


def _row_gather_wait(src_hbm, n, dst, sem):
    pltpu.make_async_copy(src_hbm.at[pl.ds(0, n), :], dst, sem).wait()


class _WeightStream:
    def __init__(self, layer, w_hbm, stage, wbuf, sem):
        self.layer, self.w_hbm, self.stage, self.wbuf, self.sem = layer, w_hbm, stage, wbuf, sem

    def _copy(self, mat, expert, q, slot):
        kind = 0 if mat < 2 else 1
        rows = self.stage[kind].shape[1]
        src = self.w_hbm[mat].at[self.layer, expert, pl.ds(pl.multiple_of(q * rows, rows), rows), :]
        return pltpu.make_async_copy(src, self.stage[kind].at[slot], self.sem.at[kind, slot])

    def _per_matrix(self, c, fn):
        mat, q, slot = c // W_CHUNKS, c % W_CHUNKS, c % 2
        for m in range(3):
            @pl.when(mat == m)
            def _(m=m):
                fn(m, q, slot)

    def start(self, c, expert):
        self._per_matrix(c, lambda m, q, slot: self._copy(m, expert, q, slot).start(priority=W_STREAM_PRIORITY))

    def finish(self, c, dst):
        def fn(m, q, slot):
            self._copy(m, 0, q, slot).wait()
            kind = 0 if m < 2 else 1
            rows = self.stage[kind].shape[1]
            self.wbuf[m][dst, pl.ds(pl.multiple_of(q * rows, rows), rows), :] = self.stage[kind][slot].astype(BF16)

        self._per_matrix(c, fn)

    def prefetch(self, expert, lo, hi):
        for t in range(2):
            @pl.when(lo + t < hi)
            def _(t=t):
                self.start(lo + t, expert)

    def drain(self, expert, dst, lo, upto, hi):
        def body(c, carry):
            self.finish(c, dst)

            @pl.when(c + 2 < hi)
            def _():
                self.start(c + 2, expert)

            return carry

        lax.fori_loop(lo, upto, body, 0)


def _ffn_kernel(rt_ref, be_ref, nbu_ref, ws_ref, nx_ref, lo_ref, hi_ref,
                x_hbm, sc_ref, sh_ref, wg_hbm, wu_hbm, wd_hbm, o_ref,
                xbuf, sem, wg_buf, wu_buf, wd_buf, stage_in, stage_dn, wsem, *, layer):
    i = pl.program_id(0)
    nbu = nbu_ref[0]
    tm = xbuf.shape[1]
    slot = i % 2
    stream = _WeightStream(layer, (wg_hbm, wu_hbm, wd_hbm), (stage_in, stage_dn), (wg_buf, wu_buf, wd_buf), wsem)
    n_chunks = 3 * W_CHUNKS

    @pl.when(i == 0)
    def _():
        _row_gather(x_hbm, rt_ref, 0, tm, xbuf.at[0], sem.at[0], ROW_GATHER_PRIORITY)
        stream.prefetch(be_ref[0], 0, n_chunks)
        stream.drain(be_ref[0], ws_ref[0], 0, n_chunks, n_chunks)

    @pl.when(i < nbu)
    def _():
        ws, nxt, lo, hi = ws_ref[i], nx_ref[i], lo_ref[i], hi_ref[i]
        stream.prefetch(nxt, lo, hi)
        _row_gather_wait(x_hbm, tm, xbuf.at[slot], sem.at[slot])
        _row_gather(x_hbm, rt_ref, (i + 1) * tm, tm, xbuf.at[1 - slot], sem.at[1 - slot], ROW_GATHER_PRIORITY)
        u = (xbuf[slot] * (1.0 + sc_ref[...]) + sh_ref[...]).astype(BF16)
        mid1, mid2 = jnp.minimum(lo + 2, hi), jnp.minimum(lo + 4, hi)
        gt = jnp.dot(u, wg_buf[ws], preferred_element_type=F32)
        stream.drain(nxt, 1 - ws, lo, mid1, hi)
        up = jnp.dot(u, wu_buf[ws], preferred_element_type=F32)
        stream.drain(nxt, 1 - ws, mid1, mid2, hi)
        h = (gt * jax.nn.sigmoid(gt) * up).astype(BF16)
        o_ref[...] = jnp.dot(h, wd_buf[ws], preferred_element_type=F32)
        stream.drain(nxt, 1 - ws, mid2, hi, hi)

    @pl.when(i >= nbu)
    def _():
        @pl.when(i == nbu)
        def _():
            _row_gather_wait(x_hbm, tm, xbuf.at[slot], sem.at[slot])

        o_ref[...] = jnp.zeros_like(o_ref)


def _moe_ffn(x, sc, sh, w_gate, w_up, w_down, layer, row_token, block_expert, nb_used, w_slot, next_expert,
             chunk_lo, chunk_hi):
    tm = TM_MOE
    p_rows = row_token.shape[0]
    nb = p_rows // tm
    vec = pl.BlockSpec((1, D_MODEL), lambda i, *_: (0, 0))
    hbm = pl.BlockSpec(memory_space=pl.ANY)
    return pl.pallas_call(
        functools.partial(_ffn_kernel, layer=layer),
        grid_spec=pltpu.PrefetchScalarGridSpec(
            num_scalar_prefetch=7,
            grid=(nb,),
            in_specs=[hbm, vec, vec, hbm, hbm, hbm],
            out_specs=pl.BlockSpec((tm, D_MODEL), lambda i, *_: (i, 0)),
            scratch_shapes=[pltpu.VMEM((2, tm, D_MODEL), F32), pltpu.SemaphoreType.DMA((2,)),
                            pltpu.VMEM((2, D_MODEL, D_EXPERT), BF16), pltpu.VMEM((2, D_MODEL, D_EXPERT), BF16),
                            pltpu.VMEM((2, D_EXPERT, D_MODEL), BF16),
                            pltpu.VMEM((2, D_MODEL // W_CHUNKS, D_EXPERT), F32),
                            pltpu.VMEM((2, D_EXPERT // W_CHUNKS, D_MODEL), F32),
                            pltpu.SemaphoreType.DMA((2, 2))]),
        out_shape=jax.ShapeDtypeStruct((p_rows, D_MODEL), F32),
        compiler_params=pltpu.CompilerParams(dimension_semantics=("arbitrary",),
                                             vmem_limit_bytes=VMEM_LIMIT_FFN_V7X),
        name="moe_ffn",
    )(row_token, block_expert, nb_used, w_slot, next_expert, chunk_lo, chunk_hi,
      x, sc, sh, w_gate, w_up, w_down)


def _combine_kernel(dest_ref, rows_hbm, gt_ref, x_ref, gate_ref, g_ref, b_ref, out_ref, ybuf, sem):
    i = pl.program_id(0)
    n = pl.num_programs(0)
    tm = x_ref.shape[0]
    s = tm * n
    slot = i % 2

    def start(tile, to):
        for k in range(TOP_K):
            _row_gather(rows_hbm, dest_ref, k * s + tile * tm, tm, ybuf.at[to, k], sem.at[to, k])

    def wait(at):
        for k in range(TOP_K):
            _row_gather_wait(rows_hbm, tm, ybuf.at[at, k], sem.at[at, k])

    @pl.when(i == 0)
    def _():
        start(0, 0)

    wait(slot)
    start(jnp.minimum(i + 1, n - 1), 1 - slot)
    gt = gt_ref[...]
    y = gt[:, 0:1] * ybuf[slot, 0] + gt[:, 1:2] * ybuf[slot, 1]
    out_ref[...] = _deepnorm(x_ref[...], y, gate_ref[...], g_ref[...], b_ref[...])

    @pl.when(i == n - 1)
    def _():
        wait(1 - slot)


def _moe_combine(dest_flat, out_rows, gates_t, x, gate, g, b):
    s = x.shape[0]
    tm = TM_COMB
    vec = pl.BlockSpec((1, D_MODEL), lambda i, dst: (0, 0))
    return pl.pallas_call(
        _combine_kernel,
        grid_spec=pltpu.PrefetchScalarGridSpec(
            num_scalar_prefetch=1,
            grid=(s // tm,),
            in_specs=[pl.BlockSpec(memory_space=pl.ANY),
                      pl.BlockSpec((tm, TOP_K), lambda i, dst: (i, 0)),
                      pl.BlockSpec((tm, D_MODEL), lambda i, dst: (i, 0)), vec, vec, vec],
            out_specs=pl.BlockSpec((tm, D_MODEL), lambda i, dst: (i, 0)),
            scratch_shapes=[pltpu.VMEM((2, TOP_K, tm, D_MODEL), F32), pltpu.SemaphoreType.DMA((2, TOP_K))]),
        out_shape=jax.ShapeDtypeStruct((s, D_MODEL), F32),
        compiler_params=_cparams(("arbitrary",)),
        name="moe_combine",
    )(dest_flat, out_rows, gates_t, x, gate, g, b)


def _moe_layer(x, sc, sh, gate, g, b, rw_t, rb_col, w_gate, w_up, w_down, layer):
    s = x.shape[0]
    tm = TM_MOE
    expert, gates, rank, cnt = _router(x, sc, sh, rw_t, rb_col)
    counts = cnt[:, 0].astype(jnp.int32)
    padded = (counts + tm - 1) // tm * tm
    pad_end = jnp.cumsum(padded)
    pad_start = pad_end - padded
    eids = jnp.arange(N_EXPERTS, dtype=jnp.int32)
    start_of = jnp.sum(jnp.where(expert[..., None] == eids, pad_start, 0), axis=-1)
    dest = (start_of + rank).reshape(-1)
    p_rows = TOP_K * s + N_EXPERTS * tm
    token = jnp.tile(jnp.arange(s, dtype=jnp.int32), TOP_K)
    row_token = jnp.zeros((p_rows,), jnp.int32).at[dest].set(token)
    nb = p_rows // tm
    block_row0 = jnp.arange(nb, dtype=jnp.int32) * tm
    block_expert = jnp.minimum(jnp.sum(block_row0[:, None] >= pad_end[None, :], axis=-1),
                               N_EXPERTS - 1).astype(jnp.int32)
    nb_used = (pad_end[-1:] // tm).astype(jnp.int32)
    has = padded > 0
    order = jnp.cumsum(has.astype(jnp.int32)) - 1
    later = jnp.logical_and(eids[None, :] > eids[:, None], has[None, :])
    next_of = jnp.min(jnp.where(later, eids[None, :], N_EXPERTS), axis=-1)
    of_block = lambda v: jnp.sum(jnp.where(block_expert[:, None] == eids[None, :], v[None, :], 0), axis=-1)
    run_len = jnp.maximum(of_block(padded) // tm, 1)
    run_pos = jnp.arange(nb, dtype=jnp.int32) - of_block(pad_start) // tm
    nxt = of_block(next_of)
    streams = jnp.logical_and(nxt < N_EXPERTS, jnp.arange(nb) < nb_used[0])
    n_chunks = 3 * W_CHUNKS
    chunk_lo = jnp.where(streams, n_chunks * run_pos // run_len, 0).astype(jnp.int32)
    chunk_hi = jnp.where(streams, n_chunks * (run_pos + 1) // run_len, 0).astype(jnp.int32)
    next_expert = jnp.where(streams, nxt, block_expert).astype(jnp.int32)
    w_slot = (of_block(order) % 2).astype(jnp.int32)
    out_rows = _moe_ffn(x, sc, sh, w_gate, w_up, w_down, layer, row_token, block_expert, nb_used,
                        w_slot, next_expert, chunk_lo, chunk_hi)
    return _moe_combine(dest.astype(jnp.int32), out_rows, gates.T, x, gate, g, b)


def kernel(x, c, positions, ada_w, ada_b, ln_g, ln_b, attn_w_qkv, attn_w_o, conv_w_pw1, conv_w_dw, conv_ln_g,
           conv_ln_b, conv_w_pw2, sc_w_in, sc_w_conv, sc_w_out, router_w, router_b, moe_w_gate, moe_w_up,
           moe_w_down):
    batch, s, d = x.shape
    assert batch == 1 and d == D_MODEL and s % (DILATIONS[-1] * Q_BLOCK) == 0
    xs = x.reshape(s, d)

    mod = _adaln(c.reshape(d, 1), ada_w, ada_b)
    half = HEAD_DIM // 2
    inv_freq = ROPE_THETA ** (-jnp.arange(half, dtype=F32) / half)
    freq_row = jnp.concatenate([inv_freq, inv_freq]).reshape(1, HEAD_DIM)
    cos, sin = _rope_tables(positions.reshape(s, 1), freq_row)

    rw_t = router_w.T
    rb_col = router_b.reshape(N_EXPERTS, 1)
    vec = lambda a: a.reshape(1, d)

    for i in range(DEPTH):
        sh1, sc1, g1, sh2, sc2, g2 = [mod[i, :, k * d:(k + 1) * d] for k in range(6)]
        m, j = i % N_MIXERS, i // N_MIXERS
        lg, lb = vec(ln_g[i, 0]), vec(ln_b[i, 0])
        if m == 0:
            outs, lses = [], []
            for grp in range(len(DILATIONS)):
                o_g, l_g = _attention(_qkv_proj(xs, sc1, sh1, attn_w_qkv, j, cos, sin, grp), grp)
                outs.append(o_g)
                lses.append(l_g)
            xs = _attn_out(outs, lses, attn_w_o[j].astype(BF16), xs, g1, lg, lb)
        elif m == 1:
            glu = _gated_in(_glu_in_kernel, 2, 1, xs, sc1, sh1, conv_w_pw1[j].astype(BF16), "conv_in")
            xs = _conv_out(glu, conv_w_dw[j], vec(conv_ln_g[j]), vec(conv_ln_b[j]),
                           conv_w_pw2[j].astype(BF16), xs, g1, lg, lb)
        else:
            bgate, ch = _gated_in(_sc_in_kernel, 3, 2, xs, sc1, sh1, sc_w_in[j].astype(BF16), "sc_in")
            xs = _sc_out(ch, bgate, sc_w_conv[j], sc_w_out[j].astype(BF16), xs, g1, lg, lb)
        xs = _moe_layer(xs, sc2, sh2, g2, vec(ln_g[i, 1]), vec(ln_b[i, 1]), rw_t, rb_col,
                        moe_w_gate, moe_w_up, moe_w_down, i)
    return xs.reshape(batch, s, d)
```

```python
import functools
import math

import jax
import jax.numpy as jnp
from jax import lax
from jax.experimental import pallas as pl
from jax.experimental.pallas import tpu as pltpu

F32 = jnp.float32
BF16 = jnp.bfloat16

D_MODEL = 2048
DEPTH = 4
N_MIXERS = 3
ATTN_HEADS = 16
HEAD_DIM = 128
HD = ATTN_HEADS * HEAD_DIM
DILATIONS = (1, 4, 16)
Q_BLOCK = 128
ROPE_THETA = 10000.0
CONV_KERNEL = 31
SHORT_CONV_KERNEL = 3
N_EXPERTS = 16
N_EXPERT_GROUPS = 4
EXPERTS_PER_GROUP = 4
TOP_K = 2
D_EXPERT = 1408
ALPHA = (2 * DEPTH) ** 0.25
LN_EPS = 1e-5
NEG_INF = -1e30

LANES_V7X = 128
SUBLANES_V7X = 8
MXU_COLS_V7X = 256
VMEM_LIMIT_V7X = 56 * 1024 * 1024
VMEM_LIMIT_FFN_V7X = 61 * 1024 * 1024
W_CHUNKS = 8
ROW_GATHER_PRIORITY = (0, 1)
W_STREAM_PRIORITY = 0
TM_PROJ = 1024
TM_QKV = 1024
TN_PROJ = 1024
TN_GLU = 512
TM_OUT = 256
TM_ROUTE = 512
TM_MOE = 256
TM_COMB = 256
TN_ADA = 1024
HALO_CONV = 32
HALO_SC = 8
CONV_ROWS = 64
CONV_COLS = 256


def _cparams(sem):
    return pltpu.CompilerParams(dimension_semantics=sem, vmem_limit_bytes=VMEM_LIMIT_V7X)


def _ln_rows(z, g, b):
    mu = jnp.mean(z, axis=-1, keepdims=True)
    zc = z - mu
    var = jnp.mean(zc * zc, axis=-1, keepdims=True)
    return zc * lax.rsqrt(var + LN_EPS) * g + b


def _deepnorm(x, y, gate, g, b):
    return _ln_rows(ALPHA * x + (1.0 + gate) * y, g, b)


def _row_spec(tm, width):
    return pl.BlockSpec((tm, width), lambda i: (i, 0))


def _vec_spec(width):
    return pl.BlockSpec((1, width), lambda i: (0, 0))


def _adaln_kernel(c_ref, w_ref, b_ref, o_ref):
    c = c_ref[...]
    ca = c * jax.nn.sigmoid(c)
    o_ref[...] = jnp.sum(w_ref[...] * ca, axis=0, keepdims=True) + b_ref[...]


def _adaln(c_col, ada_w, ada_b):
    depth, d, n = ada_w.shape
    return pl.pallas_call(
        _adaln_kernel,
        grid=(depth, n // TN_ADA),
        in_specs=[pl.BlockSpec((d, 1), lambda l, j: (0, 0)),
                  pl.BlockSpec((None, d, TN_ADA), lambda l, j: (l, 0, j)),
                  pl.BlockSpec((None, 1, TN_ADA), lambda l, j: (l, 0, j))],
        out_specs=pl.BlockSpec((None, 1, TN_ADA), lambda l, j: (l, 0, j)),
        out_shape=jax.ShapeDtypeStruct((depth, 1, n), F32),
        compiler_params=_cparams(("arbitrary", "arbitrary")),
        name="adaln",
    )(c_col, ada_w, ada_b.reshape(depth, 1, n))


def _rope_kernel(pos_ref, freq_ref, cos_ref, sin_ref):
    ang = pos_ref[...].astype(F32) * freq_ref[...]
    lane = lax.broadcasted_iota(jnp.int32, ang.shape, 1)
    s = jnp.sin(ang)
    cos_ref[...] = jnp.cos(ang)
    sin_ref[...] = jnp.where(lane < HEAD_DIM // 2, -s, s)


def _rope_tables(pos_col, freq_row):
    s = pos_col.shape[0]
    tm = 1024
    spec = pl.BlockSpec((tm, HEAD_DIM), lambda i: (i, 0))
    return pl.pallas_call(
        _rope_kernel,
        grid=(s // tm,),
        in_specs=[pl.BlockSpec((tm, 1), lambda i: (i, 0)), _vec_spec(HEAD_DIM)],
        out_specs=[spec, spec],
        out_shape=[jax.ShapeDtypeStruct((s, HEAD_DIM), F32)] * 2,
        compiler_params=_cparams(("arbitrary",)),
        name="rope_tables",
    )(pos_col, freq_row)


def _qkv_kernel(x_ref, sc_ref, sh_ref, w_ref, cos_ref, sin_ref, o_ref, u_scr, acc_scr, *, d, n_rope):
    j = pl.program_id(1)
    nh = acc_scr.shape[0]
    tm = x_ref.shape[0]
    skew = _residue_stride(d)

    @pl.when(j == 0)
    def _():
        u_scr[...] = (x_ref[...] * (1.0 + sc_ref[...]) + sh_ref[...]).astype(BF16)

    is_rope = j < n_rope
    cos = cos_ref[...]
    sin = sin_ref[...]
    u = u_scr[...]
    for c in range(nh // 2):
        w = w_ref[:, c * MXU_COLS_V7X:(c + 1) * MXU_COLS_V7X].astype(BF16)
        acc = jnp.dot(u, w, preferred_element_type=F32)
        for h in (2 * c, 2 * c + 1):
            a = acc[:, (h - 2 * c) * HEAD_DIM:(h - 2 * c + 1) * HEAD_DIM]
            a = jnp.where(is_rope, a * cos + pltpu.roll(a, HEAD_DIM // 2, 1) * sin, a)
            hs = slice(h * HEAD_DIM, (h + 1) * HEAD_DIM)
            if d == 1:
                o_ref[0, :, hs] = a.astype(BF16)
            else:
                if skew == d:
                    acc_scr[h] = a
                else:
                    for k in range(tm // d):
                        acc_scr[h, k * skew:k * skew + d] = a[k * d:(k + 1) * d]
                for r in range(d):
                    o_ref[r, :, hs] = acc_scr[h, pl.ds(r, tm // d, stride=skew), :].astype(BF16)


def _residue_stride(d):
    return d + 1 if d % SUBLANES_V7X == 0 else d


def _qkv_proj(x, sc, sh, w_all, layer, cos, sin, group):
    s = x.shape[0]
    d = DILATIONS[group]
    tm, tn = TM_QKV, TN_PROJ
    ncol = 3 * HD
    col0 = group * (ncol // tn)
    return pl.pallas_call(
        functools.partial(_qkv_kernel, d=d, n_rope=2 * HD // tn),
        grid=(s // tm, ncol // tn),
        in_specs=[pl.BlockSpec((tm, D_MODEL), lambda i, j: (i, 0)),
                  pl.BlockSpec((1, D_MODEL), lambda i, j: (0, 0)),
                  pl.BlockSpec((1, D_MODEL), lambda i, j: (0, 0)),
                  pl.BlockSpec((None, D_MODEL, tn), lambda i, j: (layer, 0, col0 + j)),
                  pl.BlockSpec((tm, HEAD_DIM), lambda i, j: (i, 0)),
                  pl.BlockSpec((tm, HEAD_DIM), lambda i, j: (i, 0))],
        out_specs=pl.BlockSpec((d, tm // d, tn), lambda i, j: (0, i, j)),
        out_shape=jax.ShapeDtypeStruct((d, s // d, ncol), BF16),
        scratch_shapes=[pltpu.VMEM((tm, D_MODEL), BF16),
                        pltpu.VMEM((tn // HEAD_DIM, tm // d * _residue_stride(d), HEAD_DIM), F32)],
        compiler_params=_cparams(("arbitrary", "arbitrary")),
        name=f"qkv_proj_g{group}",
    )(x, sc, sh, w_all, cos, sin)


def _attn_kernel(q_ref, kc_ref, vc_ref, o_ref, lse_ref, k_scr, v_scr, s_scr, p_scr, *, nblk):
    b = pl.program_id(0)
    not_first = (b % nblk) != 0

    @pl.when(b == 0)
    def _():
        k_scr[0:Q_BLOCK] = jnp.zeros((Q_BLOCK, HD), BF16)
        v_scr[0:Q_BLOCK] = jnp.zeros((Q_BLOCK, HD), BF16)

    k_scr[Q_BLOCK:] = kc_ref[...]
    v_scr[Q_BLOCK:] = vc_ref[...]
    nt = (((1,), (1,)), ((), ()))
    for h in range(ATTN_HEADS):
        sl = slice(h * HEAD_DIM, (h + 1) * HEAD_DIM)
        s_scr[h] = lax.dot_general(q_ref[:, sl], k_scr[:, sl], nt, preferred_element_type=F32)

    row = lax.broadcasted_iota(jnp.int32, (Q_BLOCK, 2 * Q_BLOCK), 0)
    col = lax.broadcasted_iota(jnp.int32, (Q_BLOCK, 2 * Q_BLOCK), 1)
    prev_ok = jnp.logical_and(jnp.logical_and(col < Q_BLOCK, col >= row), not_first)
    valid = jnp.logical_or(prev_ok, jnp.logical_and(col >= Q_BLOCK, col - Q_BLOCK <= row))
    scale = 1.0 / math.sqrt(HEAD_DIM)
    inv = []
    for h in range(ATTN_HEADS):
        s = jnp.where(valid, s_scr[h], NEG_INF)
        m = jnp.max(s, axis=1, keepdims=True)
        p = jnp.exp2((s - m) * (scale * math.log2(math.e)))
        den = jnp.sum(p, axis=1, keepdims=True)
        p_scr[h] = p.astype(BF16)
        inv.append(1.0 / den)
        lse_ref[:, h:h + 1] = m * scale + jnp.log(den)
    for h in range(ATTN_HEADS):
        sl = slice(h * HEAD_DIM, (h + 1) * HEAD_DIM)
        o = jnp.dot(p_scr[h], v_scr[:, sl], preferred_element_type=F32)
        o_ref[:, sl] = (o * inv[h]).astype(BF16)
    k_scr[0:Q_BLOCK] = kc_ref[...]
    v_scr[0:Q_BLOCK] = vc_ref[...]


def _attention(qkv, group):
    d, l, ncol = qkv.shape
    s = d * l
    flat = qkv.reshape(s, ncol)
    nblk = l // Q_BLOCK
    cur = lambda c: pl.BlockSpec((Q_BLOCK, HD), lambda b: (b, c))
    return pl.pallas_call(
        functools.partial(_attn_kernel, nblk=nblk),
        grid=(s // Q_BLOCK,),
        in_specs=[cur(0), cur(1), cur(2)],
        out_specs=[pl.BlockSpec((Q_BLOCK, HD), lambda b: (b, 0)),
                   pl.BlockSpec((Q_BLOCK, ATTN_HEADS), lambda b: (b, 0))],
        out_shape=[jax.ShapeDtypeStruct((s, HD), BF16), jax.ShapeDtypeStruct((s, ATTN_HEADS), F32)],
        scratch_shapes=[pltpu.VMEM((2 * Q_BLOCK, HD), BF16), pltpu.VMEM((2 * Q_BLOCK, HD), BF16),
                        pltpu.VMEM((ATTN_HEADS, Q_BLOCK, 2 * Q_BLOCK), F32),
                        pltpu.VMEM((ATTN_HEADS, Q_BLOCK, 2 * Q_BLOCK), BF16)],
        compiler_params=_cparams(("arbitrary",)),
        name=f"dil_attn_g{group}",
    )(flat, flat, flat)


def _attn_out_kernel(o0_ref, o1_ref, o2_ref, l0_ref, l1_ref, l2_ref, w_ref, x_ref, gate_ref, g_ref, b_ref,
                     out_ref, o_scr, a_scr):
    tm = x_ref.shape[0]
    for gi, o_ref in enumerate((o1_ref, o2_ref)):
        d = DILATIONS[gi + 1]
        for r in range(d):
            for h in range(ATTN_HEADS):
                o_scr[gi, h, pl.ds(r, tm // d, stride=d), :] = (
                    o_ref[r, :, h * HEAD_DIM:(h + 1) * HEAD_DIM].astype(F32))
    l0 = l0_ref[...]
    l1 = l1_ref[...]
    l2 = l2_ref[...]
    m = jnp.maximum(jnp.maximum(l0, l1), l2)
    e0 = jnp.exp(l0 - m)
    e1 = jnp.exp(l1 - m)
    e2 = jnp.exp(l2 - m)
    z = e0 + e1 + e2
    w0 = e0 / z
    w1 = e1 / z
    w2 = e2 / z
    for h in range(ATTN_HEADS):
        sl = slice(h * HEAD_DIM, (h + 1) * HEAD_DIM)
        a = (w0[:, h:h + 1] * o0_ref[:, sl].astype(F32)
             + w1[:, h:h + 1] * o_scr[0, h]
             + w2[:, h:h + 1] * o_scr[1, h])
        a_scr[:, sl] = a.astype(BF16)
    y = jnp.dot(a_scr[...], w_ref[...], preferred_element_type=F32)
    out_ref[...] = _deepnorm(x_ref[...], y, gate_ref[...], g_ref[...], b_ref[...])


def _attn_out(outs, lses, w_bf, x, gate, g, b):
    s = x.shape[0]
    tm = TM_OUT
    d1, d2 = DILATIONS[1], DILATIONS[2]
    o1 = outs[1].reshape(d1, s // d1, HD)
    o2 = outs[2].reshape(d2, s // d2, HD)
    l1 = lses[1].reshape(d1, s // d1, ATTN_HEADS).transpose(1, 0, 2).reshape(s, ATTN_HEADS)
    l2 = lses[2].reshape(d2, s // d2, ATTN_HEADS).transpose(1, 0, 2).reshape(s, ATTN_HEADS)
    perm = lambda d, w: pl.BlockSpec((d, tm // d, w), lambda i: (0, i, 0))
    return pl.pallas_call(
        _attn_out_kernel,
        grid=(s // tm,),
        in_specs=[_row_spec(tm, HD), perm(d1, HD), perm(d2, HD),
                  _row_spec(tm, ATTN_HEADS), _row_spec(tm, ATTN_HEADS), _row_spec(tm, ATTN_HEADS),
                  pl.BlockSpec((HD, D_MODEL), lambda i: (0, 0)),
                  _row_spec(tm, D_MODEL), _vec_spec(D_MODEL), _vec_spec(D_MODEL), _vec_spec(D_MODEL)],
        out_specs=_row_spec(tm, D_MODEL),
        out_shape=jax.ShapeDtypeStruct((s, D_MODEL), F32),
        scratch_shapes=[pltpu.VMEM((2, ATTN_HEADS, tm, HEAD_DIM), F32), pltpu.VMEM((tm, HD), BF16)],
        compiler_params=_cparams(("arbitrary",)),
        name="attn_out",
    )(outs[0], o1, o2, lses[0], l1, l2, w_bf, x, gate, g, b)


def _modulate_once(x_ref, sc_ref, sh_ref, u_scr):
    @pl.when(pl.program_id(1) == 0)
    def _():
        u_scr[...] = (x_ref[...] * (1.0 + sc_ref[...]) + sh_ref[...]).astype(BF16)


def _glu_in_kernel(x_ref, sc_ref, sh_ref, wa_ref, wg_ref, o_ref, u_scr):
    _modulate_once(x_ref, sc_ref, sh_ref, u_scr)
    u = u_scr[...]
    a = jnp.dot(u, wa_ref[...], preferred_element_type=F32)
    gt = jnp.dot(u, wg_ref[...], preferred_element_type=F32)
    o_ref[...] = a * jax.nn.sigmoid(gt)


def _sc_in_kernel(x_ref, sc_ref, sh_ref, wb_ref, wc_ref, wh_ref, b_ref, ch_ref, u_scr):
    _modulate_once(x_ref, sc_ref, sh_ref, u_scr)
    u = u_scr[...]
    b_ref[...] = jnp.dot(u, wb_ref[...], preferred_element_type=F32)
    c = jnp.dot(u, wc_ref[...], preferred_element_type=F32)
    h = jnp.dot(u, wh_ref[...], preferred_element_type=F32)
    ch_ref[...] = c * h


def _gated_in(kernel, n_parts, n_out, x, sc, sh, w_bf, name):
    s = x.shape[0]
    tm, tn = TM_PROJ, TN_GLU
    nj = D_MODEL // tn
    w_specs = [pl.BlockSpec((D_MODEL, tn), functools.partial(lambda i, j, p: (0, p * nj + j), p=p))
               for p in range(n_parts)]
    out_spec = pl.BlockSpec((tm, tn), lambda i, j: (i, j))
    out_shape = jax.ShapeDtypeStruct((s, D_MODEL), F32)
    return pl.pallas_call(
        kernel,
        grid=(s // tm, nj),
        in_specs=[pl.BlockSpec((tm, D_MODEL), lambda i, j: (i, 0)),
                  pl.BlockSpec((1, D_MODEL), lambda i, j: (0, 0)),
                  pl.BlockSpec((1, D_MODEL), lambda i, j: (0, 0))] + w_specs,
        out_specs=[out_spec] * n_out if n_out > 1 else out_spec,
        out_shape=[out_shape] * n_out if n_out > 1 else out_shape,
        scratch_shapes=[pltpu.VMEM((tm, D_MODEL), BF16)],
        compiler_params=_cparams(("arbitrary", "arbitrary")),
        name=name,
    )(x, sc, sh, *([w_bf] * n_parts))


def _fill_halo(hal_scr, prev_ref, cur_ref, halo):
    is_first = pl.program_id(0) == 0
    tm = cur_ref.shape[0]
    hal_scr[pl.ds(0, halo), :] = jnp.where(is_first, 0.0, prev_ref[...])
    hal_scr[pl.ds(halo, tm), :] = cur_ref[...]
    hal_scr[pl.ds(halo + tm, SUBLANES_V7X), :] = jnp.zeros((SUBLANES_V7X, D_MODEL), F32)


def _dwconv(hal_scr, w_ref, dst_scr, ph_scr, halo, ksize, tm):
    off = halo - (ksize - 1)
    win = CONV_ROWS + SUBLANES_V7X
    for r0 in range(0, tm, CONV_ROWS):
        for c0 in range(0, D_MODEL, CONV_COLS):
            cs = slice(c0, c0 + CONV_COLS)
            out = None
            for phase in range(SUBLANES_V7X):
                part = None
                for k in range(ksize):
                    if (off + k) % SUBLANES_V7X != phase:
                        continue
                    base = r0 + off + k - phase
                    term = w_ref[k:k + 1, cs] * hal_scr[base:base + win, cs]
                    part = term if part is None else part + term
                if part is None:
                    continue
                if phase == 0:
                    shifted = part[:CONV_ROWS]
                else:
                    ph_scr[phase] = part
                    shifted = ph_scr[phase, phase:phase + CONV_ROWS, :]
                out = shifted if out is None else out + shifted
            dst_scr[r0:r0 + CONV_ROWS, cs] = out


def _conv_out_kernel(cur_ref, prev_ref, wdw_ref, cg_ref, cb_ref, w_ref, x_ref, gate_ref, g_ref, b_ref,
                     out_ref, hal_scr, h_scr, ph_scr):
    tm = x_ref.shape[0]
    _fill_halo(hal_scr, prev_ref, cur_ref, HALO_CONV)
    _dwconv(hal_scr, wdw_ref, h_scr, ph_scr, HALO_CONV, CONV_KERNEL, tm)
    hn = _ln_rows(h_scr[...], cg_ref[...], cb_ref[...])
    a = (hn * jax.nn.sigmoid(hn)).astype(BF16)
    y = jnp.dot(a, w_ref[...], preferred_element_type=F32)
    out_ref[...] = _deepnorm(x_ref[...], y, gate_ref[...], g_ref[...], b_ref[...])


def _sc_out_kernel(cur_ref, prev_ref, bg_ref, wdw_ref, w_ref, x_ref, gate_ref, g_ref, b_ref,
                   out_ref, hal_scr, h_scr, ph_scr):
    tm = x_ref.shape[0]
    _fill_halo(hal_scr, prev_ref, cur_ref, HALO_SC)
    _dwconv(hal_scr, wdw_ref, h_scr, ph_scr, HALO_SC, SHORT_CONV_KERNEL, tm)
    a = (bg_ref[...] * h_scr[...]).astype(BF16)
    y = jnp.dot(a, w_ref[...], preferred_element_type=F32)
    out_ref[...] = _deepnorm(x_ref[...], y, gate_ref[...], g_ref[...], b_ref[...])


def _conv_phase_scratch():
    return pltpu.VMEM((SUBLANES_V7X, CONV_ROWS + SUBLANES_V7X, CONV_COLS), F32)


def _halo_spec(tm, halo):
    per = tm // halo
    return pl.BlockSpec((halo, D_MODEL), lambda i: (jnp.maximum(i * per - 1, 0), 0))


def _conv_out(glu, w_dw, cg, cb, w_bf, x, gate, g, b):
    s = x.shape[0]
    tm = TM_OUT
    full = lambda shape: pl.BlockSpec(shape, lambda i: (0, 0))
    return pl.pallas_call(
        _conv_out_kernel,
        grid=(s // tm,),
        in_specs=[_row_spec(tm, D_MODEL), _halo_spec(tm, HALO_CONV), full((CONV_KERNEL, D_MODEL)),
                  _vec_spec(D_MODEL), _vec_spec(D_MODEL), full((D_MODEL, D_MODEL)),
                  _row_spec(tm, D_MODEL), _vec_spec(D_MODEL), _vec_spec(D_MODEL), _vec_spec(D_MODEL)],
        out_specs=_row_spec(tm, D_MODEL),
        out_shape=jax.ShapeDtypeStruct((s, D_MODEL), F32),
        scratch_shapes=[pltpu.VMEM((tm + HALO_CONV + SUBLANES_V7X, D_MODEL), F32), pltpu.VMEM((tm, D_MODEL), F32),
                        _conv_phase_scratch()],
        compiler_params=_cparams(("arbitrary",)),
        name="conv_out",
    )(glu, glu, w_dw, cg, cb, w_bf, x, gate, g, b)


def _sc_out(ch, bgate, w_dw, w_bf, x, gate, g, b):
    s = x.shape[0]
    tm = TM_OUT
    full = lambda shape: pl.BlockSpec(shape, lambda i: (0, 0))
    return pl.pallas_call(
        _sc_out_kernel,
        grid=(s // tm,),
        in_specs=[_row_spec(tm, D_MODEL), _halo_spec(tm, HALO_SC), _row_spec(tm, D_MODEL),
                  full((SHORT_CONV_KERNEL, D_MODEL)), full((D_MODEL, D_MODEL)),
                  _row_spec(tm, D_MODEL), _vec_spec(D_MODEL), _vec_spec(D_MODEL), _vec_spec(D_MODEL)],
        out_specs=_row_spec(tm, D_MODEL),
        out_shape=jax.ShapeDtypeStruct((s, D_MODEL), F32),
        scratch_shapes=[pltpu.VMEM((tm + HALO_SC + SUBLANES_V7X, D_MODEL), F32), pltpu.VMEM((tm, D_MODEL), F32),
                        _conv_phase_scratch()],
        compiler_params=_cparams(("arbitrary",)),
        name="sc_out",
    )(ch, ch, bgate, w_dw, w_bf, x, gate, g, b)


def _argmax4_first(v):
    i01 = jnp.where(v[1] > v[0], 1, 0)
    m01 = jnp.maximum(v[0], v[1])
    i23 = jnp.where(v[3] > v[2], 3, 2)
    m23 = jnp.maximum(v[2], v[3])
    return jnp.where(m23 > m01, i23, i01), jnp.maximum(m01, m23)


def _select4(idx, v):
    return jnp.where(idx == 0, v[0], jnp.where(idx == 1, v[1], jnp.where(idx == 2, v[2], v[3])))


def _router_kernel(x_ref, sc_ref, sh_ref, wt_ref, rb_ref, e_ref, gt_ref, rk_ref, cnt_ref, carry_scr):
    i = pl.program_id(0)
    tm = x_ref.shape[0]

    @pl.when(i == 0)
    def _():
        carry_scr[...] = jnp.zeros_like(carry_scr)

    u = x_ref[...] * (1.0 + sc_ref[...]) + sh_ref[...]
    logits = lax.dot_general(wt_ref[...], u, (((1,), (1,)), ((), ())), precision=lax.Precision.HIGHEST,
                             preferred_element_type=F32) + rb_ref[...]
    ex = jnp.exp(logits - jnp.max(logits, axis=0, keepdims=True))
    probs = ex / jnp.sum(ex, axis=0, keepdims=True)

    top1_i, top1_v, top2_i, top2_v, score = [], [], [], [], []
    for grp in range(N_EXPERT_GROUPS):
        p = [probs[grp * EXPERTS_PER_GROUP + j:grp * EXPERTS_PER_GROUP + j + 1, :]
             for j in range(EXPERTS_PER_GROUP)]
        i1, v1 = _argmax4_first(p)
        rest = [jnp.where(i1 == j, -1.0, p[j]) for j in range(EXPERTS_PER_GROUP)]
        i2, v2 = _argmax4_first(rest)
        top1_i.append(i1)
        top1_v.append(v1)
        top2_i.append(i2)
        top2_v.append(v2)
        score.append(v1 + v2)
    gsel, _ = _argmax4_first(score)
    p1 = _select4(gsel, top1_v)
    p2 = _select4(gsel, top2_v)
    e1 = gsel * EXPERTS_PER_GROUP + _select4(gsel, top1_i)
    e2 = gsel * EXPERTS_PER_GROUP + _select4(gsel, top2_i)
    psum = p1 + p2
    e_ref[0:1, :] = e1
    e_ref[1:2, :] = e2
    gt_ref[0:1, :] = p1 / psum
    gt_ref[1:2, :] = p2 / psum

    eid = lax.broadcasted_iota(jnp.int32, (N_EXPERTS, tm), 0)
    earlier = (lax.broadcasted_iota(jnp.int32, (tm, tm), 0)
               < lax.broadcasted_iota(jnp.int32, (tm, tm), 1)).astype(BF16)
    base = carry_scr[:, 0:1]
    for slot, e_sel in enumerate((e1, e2)):
        onehot = (eid == e_sel).astype(F32)
        before = jnp.dot(onehot.astype(BF16), earlier, preferred_element_type=F32)
        rank = jnp.sum(onehot * (before + base), axis=0, keepdims=True)
        rk_ref[slot:slot + 1, :] = rank.astype(jnp.int32)
        base = base + jnp.sum(onehot, axis=1, keepdims=True)
    carry_scr[...] = jnp.broadcast_to(base, carry_scr.shape)
    cnt_ref[...] = carry_scr[...]


def _router(x, sc, sh, rw_t, rb_col):
    s = x.shape[0]
    tm = TM_ROUTE
    slot_spec = pl.BlockSpec((TOP_K, tm), lambda i: (0, i))
    full = lambda shape: pl.BlockSpec(shape, lambda i: (0, 0))
    return pl.pallas_call(
        _router_kernel,
        grid=(s // tm,),
        in_specs=[_row_spec(tm, D_MODEL), _vec_spec(D_MODEL), _vec_spec(D_MODEL),
                  full((N_EXPERTS, D_MODEL)), full((N_EXPERTS, 1))],
        out_specs=[slot_spec, slot_spec, slot_spec, full((N_EXPERTS, LANES_V7X))],
        out_shape=[jax.ShapeDtypeStruct((TOP_K, s), jnp.int32), jax.ShapeDtypeStruct((TOP_K, s), F32),
                   jax.ShapeDtypeStruct((TOP_K, s), jnp.int32),
                   jax.ShapeDtypeStruct((N_EXPERTS, LANES_V7X), F32)],
        scratch_shapes=[pltpu.VMEM((N_EXPERTS, LANES_V7X), F32)],
        compiler_params=_cparams(("arbitrary",)),
        name="router",
    )(x, sc, sh, rw_t, rb_col)


def _row_gather(src_hbm, idx_ref, base, n, dst, sem, priorities=(0, 1)):
    for r in range(n):
        tok = idx_ref[base + r]
        copy = pltpu.make_async_copy(src_hbm.at[pl.ds(tok, 1), :], dst.at[pl.ds(r, 1), :], sem)
        copy.start(priority=priorities[r % len(priorities)])


def _row_gather_wait(src_hbm, n, dst, sem):
    pltpu.make_async_copy(src_hbm.at[pl.ds(0, n), :], dst, sem).wait()


class _WeightStream:
    def __init__(self, layer, w_hbm, stage, wbuf, sem):
        self.layer, self.w_hbm, self.stage, self.wbuf, self.sem = layer, w_hbm, stage, wbuf, sem

    def _copy(self, mat, expert, q, slot):
        kind = 0 if mat < 2 else 1
        rows = self.stage[kind].shape[1]
        src = self.w_hbm[mat].at[self.layer, expert, pl.ds(pl.multiple_of(q * rows, rows), rows), :]
        return pltpu.make_async_copy(src, self.stage[kind].at[slot], self.sem.at[kind, slot])

    def _per_matrix(self, c, fn):
        mat, q, slot = c // W_CHUNKS, c % W_CHUNKS, c % 2
        for m in range(3):
            @pl.when(mat == m)
            def _(m=m):
                fn(m, q, slot)

    def start(self, c, expert):
        self._per_matrix(c, lambda m, q, slot: self._copy(m, expert, q, slot).start(priority=W_STREAM_PRIORITY))

    def finish(self, c, dst):
        def fn(m, q, slot):
            self._copy(m, 0, q, slot).wait()
            kind = 0 if m < 2 else 1
            rows = self.stage[kind].shape[1]
            self.wbuf[m][dst, pl.ds(pl.multiple_of(q * rows, rows), rows), :] = self.stage[kind][slot].astype(BF16)

        self._per_matrix(c, fn)

    def prefetch(self, expert, lo, hi):
        for t in range(2):
            @pl.when(lo + t < hi)
            def _(t=t):
                self.start(lo + t, expert)

    def drain(self, expert, dst, lo, upto, hi):
        def body(c, carry):
            self.finish(c, dst)

            @pl.when(c + 2 < hi)
            def _():
                self.start(c + 2, expert)

            return carry

        lax.fori_loop(lo, upto, body, 0)


def _ffn_kernel(rt_ref, be_ref, nbu_ref, ws_ref, nx_ref, lo_ref, hi_ref,
                x_hbm, sc_ref, sh_ref, wg_hbm, wu_hbm, wd_hbm, o_ref,
                xbuf, sem, wg_buf, wu_buf, wd_buf, stage_in, stage_dn, wsem, *, layer):
    i = pl.program_id(0)
    nbu = nbu_ref[0]
    tm = xbuf.shape[1]
    slot = i % 2
    stream = _WeightStream(layer, (wg_hbm, wu_hbm, wd_hbm), (stage_in, stage_dn), (wg_buf, wu_buf, wd_buf), wsem)
    n_chunks = 3 * W_CHUNKS

    @pl.when(i == 0)
    def _():
        _row_gather(x_hbm, rt_ref, 0, tm, xbuf.at[0], sem.at[0], ROW_GATHER_PRIORITY)
        stream.prefetch(be_ref[0], 0, n_chunks)
        stream.drain(be_ref[0], ws_ref[0], 0, n_chunks, n_chunks)

    @pl.when(i < nbu)
    def _():
        ws, nxt, lo, hi = ws_ref[i], nx_ref[i], lo_ref[i], hi_ref[i]
        stream.prefetch(nxt, lo, hi)
        _row_gather_wait(x_hbm, tm, xbuf.at[slot], sem.at[slot])
        _row_gather(x_hbm, rt_ref, (i + 1) * tm, tm, xbuf.at[1 - slot], sem.at[1 - slot], ROW_GATHER_PRIORITY)
        u = (xbuf[slot] * (1.0 + sc_ref[...]) + sh_ref[...]).astype(BF16)
        mid1, mid2 = jnp.minimum(lo + 2, hi), jnp.minimum(lo + 4, hi)
        gt = jnp.dot(u, wg_buf[ws], preferred_element_type=F32)
        stream.drain(nxt, 1 - ws, lo, mid1, hi)
        up = jnp.dot(u, wu_buf[ws], preferred_element_type=F32)
        stream.drain(nxt, 1 - ws, mid1, mid2, hi)
        h = (gt * jax.nn.sigmoid(gt) * up).astype(BF16)
        o_ref[...] = jnp.dot(h, wd_buf[ws], preferred_element_type=F32)
        stream.drain(nxt, 1 - ws, mid2, hi, hi)

    @pl.when(i >= nbu)
    def _():
        @pl.when(i == nbu)
        def _():
            _row_gather_wait(x_hbm, tm, xbuf.at[slot], sem.at[slot])

        o_ref[...] = jnp.zeros_like(o_ref)


def _moe_ffn(x, sc, sh, w_gate, w_up, w_down, layer, row_token, block_expert, nb_used, w_slot, next_expert,
             chunk_lo, chunk_hi):
    tm = TM_MOE
    p_rows = row_token.shape[0]
    nb = p_rows // tm
    vec = pl.BlockSpec((1, D_MODEL), lambda i, *_: (0, 0))
    hbm = pl.BlockSpec(memory_space=pl.ANY)
    return pl.pallas_call(
        functools.partial(_ffn_kernel, layer=layer),
        grid_spec=pltpu.PrefetchScalarGridSpec(
            num_scalar_prefetch=7,
            grid=(nb,),
            in_specs=[hbm, vec, vec, hbm, hbm, hbm],
            out_specs=pl.BlockSpec((tm, D_MODEL), lambda i, *_: (i, 0)),
            scratch_shapes=[pltpu.VMEM((2, tm, D_MODEL), F32), pltpu.SemaphoreType.DMA((2,)),
                            pltpu.VMEM((2, D_MODEL, D_EXPERT), BF16), pltpu.VMEM((2, D_MODEL, D_EXPERT), BF16),
                            pltpu.VMEM((2, D_EXPERT, D_MODEL), BF16),
                            pltpu.VMEM((2, D_MODEL // W_CHUNKS, D_EXPERT), F32),
                            pltpu.VMEM((2, D_EXPERT // W_CHUNKS, D_MODEL), F32),
                            pltpu.SemaphoreType.DMA((2, 2))]),
        out_shape=jax.ShapeDtypeStruct((p_rows, D_MODEL), F32),
        compiler_params=pltpu.CompilerParams(dimension_semantics=("arbitrary",),
                                             vmem_limit_bytes=VMEM_LIMIT_FFN_V7X),
        name="moe_ffn",
    )(row_token, block_expert, nb_used, w_slot, next_expert, chunk_lo, chunk_hi,
      x, sc, sh, w_gate, w_up, w_down)


def _combine_kernel(dest_ref, rows_hbm, gt_ref, x_ref, gate_ref, g_ref, b_ref, out_ref, ybuf, sem):
    i = pl.program_id(0)
    n = pl.num_programs(0)
    tm = x_ref.shape[0]
    s = tm * n
    slot = i % 2

    def start(tile, to):
        for k in range(TOP_K):
            _row_gather(rows_hbm, dest_ref, k * s + tile * tm, tm, ybuf.at[to, k], sem.at[to, k])

    def wait(at):
        for k in range(TOP_K):
            _row_gather_wait(rows_hbm, tm, ybuf.at[at, k], sem.at[at, k])

    @pl.when(i == 0)
    def _():
        start(0, 0)

    wait(slot)
    start(jnp.minimum(i + 1, n - 1), 1 - slot)
    gt = gt_ref[...]
    y = gt[:, 0:1] * ybuf[slot, 0] + gt[:, 1:2] * ybuf[slot, 1]
    out_ref[...] = _deepnorm(x_ref[...], y, gate_ref[...], g_ref[...], b_ref[...])

    @pl.when(i == n - 1)
    def _():
        wait(1 - slot)


def _moe_combine(dest_flat, out_rows, gates_t, x, gate, g, b):
    s = x.shape[0]
    tm = TM_COMB
    vec = pl.BlockSpec((1, D_MODEL), lambda i, dst: (0, 0))
    return pl.pallas_call(
        _combine_kernel,
        grid_spec=pltpu.PrefetchScalarGridSpec(
            num_scalar_prefetch=1,
            grid=(s // tm,),
            in_specs=[pl.BlockSpec(memory_space=pl.ANY),
                      pl.BlockSpec((tm, TOP_K), lambda i, dst: (i, 0)),
                      pl.BlockSpec((tm, D_MODEL), lambda i, dst: (i, 0)), vec, vec, vec],
            out_specs=pl.BlockSpec((tm, D_MODEL), lambda i, dst: (i, 0)),
            scratch_shapes=[pltpu.VMEM((2, TOP_K, tm, D_MODEL), F32), pltpu.SemaphoreType.DMA((2, TOP_K))]),
        out_shape=jax.ShapeDtypeStruct((s, D_MODEL), F32),
        compiler_params=_cparams(("arbitrary",)),
        name="moe_combine",
    )(dest_flat, out_rows, gates_t, x, gate, g, b)


def _moe_layer(x, sc, sh, gate, g, b, rw_t, rb_col, w_gate, w_up, w_down, layer):
    s = x.shape[0]
    tm = TM_MOE
    expert, gates, rank, cnt = _router(x, sc, sh, rw_t, rb_col)
    counts = cnt[:, 0].astype(jnp.int32)
    padded = (counts + tm - 1) // tm * tm
    pad_end = jnp.cumsum(padded)
    pad_start = pad_end - padded
    eids = jnp.arange(N_EXPERTS, dtype=jnp.int32)
    start_of = jnp.sum(jnp.where(expert[..., None] == eids, pad_start, 0), axis=-1)
    dest = (start_of + rank).reshape(-1)
    p_rows = TOP_K * s + N_EXPERTS * tm
    token = jnp.tile(jnp.arange(s, dtype=jnp.int32), TOP_K)
    row_token = jnp.zeros((p_rows,), jnp.int32).at[dest].set(token)
    nb = p_rows // tm
    block_row0 = jnp.arange(nb, dtype=jnp.int32) * tm
    block_expert = jnp.minimum(jnp.sum(block_row0[:, None] >= pad_end[None, :], axis=-1),
                               N_EXPERTS - 1).astype(jnp.int32)
    nb_used = (pad_end[-1:] // tm).astype(jnp.int32)
    has = padded > 0
    order = jnp.cumsum(has.astype(jnp.int32)) - 1
    later = jnp.logical_and(eids[None, :] > eids[:, None], has[None, :])
    next_of = jnp.min(jnp.where(later, eids[None, :], N_EXPERTS), axis=-1)
    of_block = lambda v: jnp.sum(jnp.where(block_expert[:, None] == eids[None, :], v[None, :], 0), axis=-1)
    run_len = jnp.maximum(of_block(padded) // tm, 1)
    run_pos = jnp.arange(nb, dtype=jnp.int32) - of_block(pad_start) // tm
    nxt = of_block(next_of)
    streams = jnp.logical_and(nxt < N_EXPERTS, jnp.arange(nb) < nb_used[0])
    n_chunks = 3 * W_CHUNKS
    chunk_lo = jnp.where(streams, n_chunks * run_pos // run_len, 0).astype(jnp.int32)
    chunk_hi = jnp.where(streams, n_chunks * (run_pos + 1) // run_len, 0).astype(jnp.int32)
    next_expert = jnp.where(streams, nxt, block_expert).astype(jnp.int32)
    w_slot = (of_block(order) % 2).astype(jnp.int32)
    out_rows = _moe_ffn(x, sc, sh, w_gate, w_up, w_down, layer, row_token, block_expert, nb_used,
                        w_slot, next_expert, chunk_lo, chunk_hi)
    return _moe_combine(dest.astype(jnp.int32), out_rows, gates.T, x, gate, g, b)


def kernel(x, c, positions, ada_w, ada_b, ln_g, ln_b, attn_w_qkv, attn_w_o, conv_w_pw1, conv_w_dw, conv_ln_g,
           conv_ln_b, conv_w_pw2, sc_w_in, sc_w_conv, sc_w_out, router_w, router_b, moe_w_gate, moe_w_up,
           moe_w_down):
    batch, s, d = x.shape
    assert batch == 1 and d == D_MODEL and s % (DILATIONS[-1] * Q_BLOCK) == 0
    xs = x.reshape(s, d)

    mod = _adaln(c.reshape(d, 1), ada_w, ada_b)
    half = HEAD_DIM // 2
    inv_freq = ROPE_THETA ** (-jnp.arange(half, dtype=F32) / half)
    freq_row = jnp.concatenate([inv_freq, inv_freq]).reshape(1, HEAD_DIM)
    cos, sin = _rope_tables(positions.reshape(s, 1), freq_row)

    rw_t = router_w.T
    rb_col = router_b.reshape(N_EXPERTS, 1)
    vec = lambda a: a.reshape(1, d)

    for i in range(DEPTH):
        sh1, sc1, g1, sh2, sc2, g2 = [mod[i, :, k * d:(k + 1) * d] for k in range(6)]
        m, j = i % N_MIXERS, i // N_MIXERS
        lg, lb = vec(ln_g[i, 0]), vec(ln_b[i, 0])
        if m == 0:
            outs, lses = [], []
            for grp in range(len(DILATIONS)):
                o_g, l_g = _attention(_qkv_proj(xs, sc1, sh1, attn_w_qkv, j, cos, sin, grp), grp)
                outs.append(o_g)
                lses.append(l_g)
            xs = _attn_out(outs, lses, attn_w_o[j].astype(BF16), xs, g1, lg, lb)
        elif m == 1:
            glu = _gated_in(_glu_in_kernel, 2, 1, xs, sc1, sh1, conv_w_pw1[j].astype(BF16), "conv_in")
            xs = _conv_out(glu, conv_w_dw[j], vec(conv_ln_g[j]), vec(conv_ln_b[j]),
                           conv_w_pw2[j].astype(BF16), xs, g1, lg, lb)
        else:
            bgate, ch = _gated_in(_sc_in_kernel, 3, 2, xs, sc1, sh1, sc_w_in[j].astype(BF16), "sc_in")
            xs = _sc_out(ch, bgate, sc_w_conv[j], sc_w_out[j].astype(BF16), xs, g1, lg, lb)
        xs = _moe_layer(xs, sc2, sh2, g2, vec(ln_g[i, 1]), vec(ln_b[i, 1]), rw_t, rb_col,
                        moe_w_gate, moe_w_up, moe_w_down, i)
    return xs.reshape(batch, s, d)
```

```python
import functools
import math

import jax
import jax.numpy as jnp
from jax import lax
from jax.experimental import pallas as pl
from jax.experimental.pallas import tpu as pltpu

F32 = jnp.float32
BF16 = jnp.bfloat16

D_MODEL = 2048
DEPTH = 4
N_MIXERS = 3
ATTN_HEADS = 16
HEAD_DIM = 128
HD = ATTN_HEADS * HEAD_DIM
DILATIONS = (1, 4, 16)
Q_BLOCK = 128
ROPE_THETA = 10000.0
CONV_KERNEL = 31
SHORT_CONV_KERNEL = 3
N_EXPERTS = 16
N_EXPERT_GROUPS = 4
EXPERTS_PER_GROUP = 4
TOP_K = 2
D_EXPERT = 1408
ALPHA = (2 * DEPTH) ** 0.25
LN_EPS = 1e-5
NEG_INF = -1e30

LANES_V7X = 128
SUBLANES_V7X = 8
MXU_COLS_V7X = 256
VMEM_LIMIT_V7X = 56 * 1024 * 1024
VMEM_LIMIT_FFN_V7X = 61 * 1024 * 1024
W_CHUNKS = 8
ROW_GATHER_PRIORITY = (0, 1)
W_STREAM_PRIORITY = 0
TM_GLU = 1024
TM_SC_IN = 512
TM_QKV = 1024
TN_PROJ = 1024
TN_GLU = 512
TM_OUT = 256
TM_ROUTE = 512
TM_MOE = 256
TM_COMB = 256
TN_ADA = 1024
HALO_CONV = 32
HALO_SC = 8
CONV_ROWS = 64
CONV_COLS = 256


def _cparams(sem):
    return pltpu.CompilerParams(dimension_semantics=sem, vmem_limit_bytes=VMEM_LIMIT_V7X)


def _ln_rows(z, g, b):
    mu = jnp.mean(z, axis=-1, keepdims=True)
    zc = z - mu
    var = jnp.mean(zc * zc, axis=-1, keepdims=True)
    return zc * lax.rsqrt(var + LN_EPS) * g + b


def _deepnorm(x, y, gate, g, b):
    return _ln_rows(ALPHA * x + (1.0 + gate) * y, g, b)


def _resident_bf16(w_ref, w_scr):
    @pl.when(pl.program_id(0) == 0)
    def _():
        w_scr[...] = w_ref[...].astype(BF16)

    return w_scr[...]


def _resident_spec(shape):
    return pl.BlockSpec(shape, lambda i: (0, 0), pipeline_mode=pl.Buffered(1))


def _row_spec(tm, width):
    return pl.BlockSpec((tm, width), lambda i: (i, 0))


def _vec_spec(width):
    return pl.BlockSpec((1, width), lambda i: (0, 0))


def _adaln_kernel(c_ref, w_ref, b_ref, o_ref):
    c = c_ref[...]
    ca = c * jax.nn.sigmoid(c)
    o_ref[...] = jnp.sum(w_ref[...] * ca, axis=0, keepdims=True) + b_ref[...]


def _adaln(c_col, ada_w, ada_b):
    depth, d, n = ada_w.shape
    return pl.pallas_call(
        _adaln_kernel,
        grid=(depth, n // TN_ADA),
        in_specs=[pl.BlockSpec((d, 1), lambda l, j: (0, 0)),
                  pl.BlockSpec((None, d, TN_ADA), lambda l, j: (l, 0, j)),
                  pl.BlockSpec((None, 1, TN_ADA), lambda l, j: (l, 0, j))],
        out_specs=pl.BlockSpec((None, 1, TN_ADA), lambda l, j: (l, 0, j)),
        out_shape=jax.ShapeDtypeStruct((depth, 1, n), F32),
        compiler_params=_cparams(("arbitrary", "arbitrary")),
        name="adaln",
    )(c_col, ada_w, ada_b.reshape(depth, 1, n))


def _rope_kernel(pos_ref, freq_ref, cos_ref, sin_ref):
    ang = pos_ref[...].astype(F32) * freq_ref[...]
    lane = lax.broadcasted_iota(jnp.int32, ang.shape, 1)
    s = jnp.sin(ang)
    cos_ref[...] = jnp.cos(ang)
    sin_ref[...] = jnp.where(lane < HEAD_DIM // 2, -s, s)


def _rope_tables(pos_col, freq_row):
    s = pos_col.shape[0]
    tm = 1024
    spec = pl.BlockSpec((tm, HEAD_DIM), lambda i: (i, 0))
    return pl.pallas_call(
        _rope_kernel,
        grid=(s // tm,),
        in_specs=[pl.BlockSpec((tm, 1), lambda i: (i, 0)), _vec_spec(HEAD_DIM)],
        out_specs=[spec, spec],
        out_shape=[jax.ShapeDtypeStruct((s, HEAD_DIM), F32)] * 2,
        compiler_params=_cparams(("arbitrary",)),
        name="rope_tables",
    )(pos_col, freq_row)


def _qkv_kernel(x_ref, sc_ref, sh_ref, w_ref, cos_ref, sin_ref, o_ref, u_scr, acc_scr, *, d, n_rope):
    j = pl.program_id(1)
    nh = acc_scr.shape[0]
    tm = x_ref.shape[0]
    skew = _residue_stride(d)

    @pl.when(j == 0)
    def _():
        u_scr[...] = (x_ref[...] * (1.0 + sc_ref[...]) + sh_ref[...]).astype(BF16)

    is_rope = j < n_rope
    cos = cos_ref[...]
    sin = sin_ref[...]
    u = u_scr[...]
    for c in range(nh // 2):
        w = w_ref[:, c * MXU_COLS_V7X:(c + 1) * MXU_COLS_V7X].astype(BF16)
        acc = jnp.dot(u, w, preferred_element_type=F32)
        for h in (2 * c, 2 * c + 1):
            a = acc[:, (h - 2 * c) * HEAD_DIM:(h - 2 * c + 1) * HEAD_DIM]
            a = jnp.where(is_rope, a * cos + pltpu.roll(a, HEAD_DIM // 2, 1) * sin, a)
            hs = slice(h * HEAD_DIM, (h + 1) * HEAD_DIM)
            if d == 1:
                o_ref[0, :, hs] = a.astype(BF16)
            else:
                if skew == d:
                    acc_scr[h] = a
                else:
                    for k in range(tm // d):
                        acc_scr[h, k * skew:k * skew + d] = a[k * d:(k + 1) * d]
                for r in range(d):
                    o_ref[r, :, hs] = acc_scr[h, pl.ds(r, tm // d, stride=skew), :].astype(BF16)


def _residue_stride(d):
    return d + 1 if d % SUBLANES_V7X == 0 else d


def _qkv_proj(x, sc, sh, w_all, layer, cos, sin, group):
    s = x.shape[0]
    d = DILATIONS[group]
    tm, tn = TM_QKV, TN_PROJ
    ncol = 3 * HD
    col0 = group * (ncol // tn)
    return pl.pallas_call(
        functools.partial(_qkv_kernel, d=d, n_rope=2 * HD // tn),
        grid=(s // tm, ncol // tn),
        in_specs=[pl.BlockSpec((tm, D_MODEL), lambda i, j: (i, 0)),
                  pl.BlockSpec((1, D_MODEL), lambda i, j: (0, 0)),
                  pl.BlockSpec((1, D_MODEL), lambda i, j: (0, 0)),
                  pl.BlockSpec((None, D_MODEL, tn), lambda i, j: (layer, 0, col0 + j)),
                  pl.BlockSpec((tm, HEAD_DIM), lambda i, j: (i, 0)),
                  pl.BlockSpec((tm, HEAD_DIM), lambda i, j: (i, 0))],
        out_specs=pl.BlockSpec((d, tm // d, tn), lambda i, j: (0, i, j)),
        out_shape=jax.ShapeDtypeStruct((d, s // d, ncol), BF16),
        scratch_shapes=[pltpu.VMEM((tm, D_MODEL), BF16),
                        pltpu.VMEM((tn // HEAD_DIM, tm // d * _residue_stride(d), HEAD_DIM), F32)],
        compiler_params=_cparams(("arbitrary", "arbitrary")),
        name=f"qkv_proj_g{group}",
    )(x, sc, sh, w_all, cos, sin)


def _attn_kernel(q_ref, kc_ref, vc_ref, o_ref, lse_ref, k_scr, v_scr, s_scr, p_scr, *, nblk):
    b = pl.program_id(0)
    not_first = (b % nblk) != 0

    @pl.when(b == 0)
    def _():
        k_scr[0:Q_BLOCK] = jnp.zeros((Q_BLOCK, HD), BF16)
        v_scr[0:Q_BLOCK] = jnp.zeros((Q_BLOCK, HD), BF16)

    k_scr[Q_BLOCK:] = kc_ref[...]
    v_scr[Q_BLOCK:] = vc_ref[...]
    nt = (((1,), (1,)), ((), ()))
    for h in range(ATTN_HEADS):
        sl = slice(h * HEAD_DIM, (h + 1) * HEAD_DIM)
        s_scr[h] = lax.dot_general(q_ref[:, sl], k_scr[:, sl], nt, preferred_element_type=F32)

    row = lax.broadcasted_iota(jnp.int32, (Q_BLOCK, 2 * Q_BLOCK), 0)
    col = lax.broadcasted_iota(jnp.int32, (Q_BLOCK, 2 * Q_BLOCK), 1)
    prev_ok = jnp.logical_and(jnp.logical_and(col < Q_BLOCK, col >= row), not_first)
    valid = jnp.logical_or(prev_ok, jnp.logical_and(col >= Q_BLOCK, col - Q_BLOCK <= row))
    scale = 1.0 / math.sqrt(HEAD_DIM)
    inv = []
    for h in range(ATTN_HEADS):
        s = jnp.where(valid, s_scr[h], NEG_INF)
        m = jnp.max(s, axis=1, keepdims=True)
        p = jnp.exp2((s - m) * (scale * math.log2(math.e)))
        den = jnp.sum(p, axis=1, keepdims=True)
        p_scr[h] = p.astype(BF16)
        inv.append(1.0 / den)
        lse_ref[:, h:h + 1] = m * scale + jnp.log(den)
    for h in range(ATTN_HEADS):
        sl = slice(h * HEAD_DIM, (h + 1) * HEAD_DIM)
        o = jnp.dot(p_scr[h], v_scr[:, sl], preferred_element_type=F32)
        o_ref[:, sl] = (o * inv[h]).astype(BF16)
    k_scr[0:Q_BLOCK] = kc_ref[...]
    v_scr[0:Q_BLOCK] = vc_ref[...]


def _attention(qkv, group):
    d, l, ncol = qkv.shape
    s = d * l
    flat = qkv.reshape(s, ncol)
    nblk = l // Q_BLOCK
    cur = lambda c: pl.BlockSpec((Q_BLOCK, HD), lambda b: (b, c))
    return pl.pallas_call(
        functools.partial(_attn_kernel, nblk=nblk),
        grid=(s // Q_BLOCK,),
        in_specs=[cur(0), cur(1), cur(2)],
        out_specs=[pl.BlockSpec((Q_BLOCK, HD), lambda b: (b, 0)),
                   pl.BlockSpec((Q_BLOCK, ATTN_HEADS), lambda b: (b, 0))],
        out_shape=[jax.ShapeDtypeStruct((s, HD), BF16), jax.ShapeDtypeStruct((s, ATTN_HEADS), F32)],
        scratch_shapes=[pltpu.VMEM((2 * Q_BLOCK, HD), BF16), pltpu.VMEM((2 * Q_BLOCK, HD), BF16),
                        pltpu.VMEM((ATTN_HEADS, Q_BLOCK, 2 * Q_BLOCK), F32),
                        pltpu.VMEM((ATTN_HEADS, Q_BLOCK, 2 * Q_BLOCK), BF16)],
        compiler_params=_cparams(("arbitrary",)),
        name=f"dil_attn_g{group}",
    )(flat, flat, flat)


def _attn_out_kernel(o0_ref, o1_ref, o2_ref, l0_ref, l1_ref, l2_ref, w_ref, x_ref, gate_ref, g_ref, b_ref,
                     out_ref, o_scr, a_scr, w_scr):
    tm = x_ref.shape[0]
    for gi, o_ref in enumerate((o1_ref, o2_ref)):
        d = DILATIONS[gi + 1]
        for r in range(d):
            for h in range(ATTN_HEADS):
                o_scr[gi, h, pl.ds(r, tm // d, stride=d), :] = (
                    o_ref[r, :, h * HEAD_DIM:(h + 1) * HEAD_DIM].astype(F32))
    l0 = l0_ref[...]
    l1 = l1_ref[...]
    l2 = l2_ref[...]
    m = jnp.maximum(jnp.maximum(l0, l1), l2)
    e0 = jnp.exp(l0 - m)
    e1 = jnp.exp(l1 - m)
    e2 = jnp.exp(l2 - m)
    z = e0 + e1 + e2
    w0 = e0 / z
    w1 = e1 / z
    w2 = e2 / z
    for h in range(ATTN_HEADS):
        sl = slice(h * HEAD_DIM, (h + 1) * HEAD_DIM)
        a = (w0[:, h:h + 1] * o0_ref[:, sl].astype(F32)
             + w1[:, h:h + 1] * o_scr[0, h]
             + w2[:, h:h + 1] * o_scr[1, h])
        a_scr[:, sl] = a.astype(BF16)
    y = jnp.dot(a_scr[...], _resident_bf16(w_ref, w_scr), preferred_element_type=F32)
    out_ref[...] = _deepnorm(x_ref[...], y, gate_ref[...], g_ref[...], b_ref[...])


def _attn_out(outs, lses, w, x, gate, g, b):
    s = x.shape[0]
    tm = TM_OUT
    d1, d2 = DILATIONS[1], DILATIONS[2]
    o1 = outs[1].reshape(d1, s // d1, HD)
    o2 = outs[2].reshape(d2, s // d2, HD)
    l1 = lses[1].reshape(d1, s // d1, ATTN_HEADS).transpose(1, 0, 2).reshape(s, ATTN_HEADS)
    l2 = lses[2].reshape(d2, s // d2, ATTN_HEADS).transpose(1, 0, 2).reshape(s, ATTN_HEADS)
    perm = lambda d, w: pl.BlockSpec((d, tm // d, w), lambda i: (0, i, 0))
    return pl.pallas_call(
        _attn_out_kernel,
        grid=(s // tm,),
        in_specs=[_row_spec(tm, HD), perm(d1, HD), perm(d2, HD),
                  _row_spec(tm, ATTN_HEADS), _row_spec(tm, ATTN_HEADS), _row_spec(tm, ATTN_HEADS),
                  _resident_spec((HD, D_MODEL)),
                  _row_spec(tm, D_MODEL), _vec_spec(D_MODEL), _vec_spec(D_MODEL), _vec_spec(D_MODEL)],
        out_specs=_row_spec(tm, D_MODEL),
        out_shape=jax.ShapeDtypeStruct((s, D_MODEL), F32),
        scratch_shapes=[pltpu.VMEM((2, ATTN_HEADS, tm, HEAD_DIM), F32), pltpu.VMEM((tm, HD), BF16),
                        pltpu.VMEM((HD, D_MODEL), BF16)],
        compiler_params=_cparams(("arbitrary",)),
        name="attn_out",
    )(outs[0], o1, o2, lses[0], l1, l2, w, x, gate, g, b)


def _modulate_once(x_ref, sc_ref, sh_ref, u_scr):
    @pl.when(pl.program_id(1) == 0)
    def _():
        u_scr[...] = (x_ref[...] * (1.0 + sc_ref[...]) + sh_ref[...]).astype(BF16)


def _glu_in_kernel(x_ref, sc_ref, sh_ref, wa_ref, wg_ref, o_ref, u_scr):
    _modulate_once(x_ref, sc_ref, sh_ref, u_scr)
    u = u_scr[...]
    a = jnp.dot(u, wa_ref[...].astype(BF16), preferred_element_type=F32)
    gt = jnp.dot(u, wg_ref[...].astype(BF16), preferred_element_type=F32)
    o_ref[...] = a * jax.nn.sigmoid(gt)


def _sc_in_kernel(x_ref, sc_ref, sh_ref, wb_ref, wc_ref, wh_ref, b_ref, ch_ref, u_scr):
    _modulate_once(x_ref, sc_ref, sh_ref, u_scr)
    u = u_scr[...]
    b_ref[...] = jnp.dot(u, wb_ref[...].astype(BF16), preferred_element_type=F32)
    c = jnp.dot(u, wc_ref[...].astype(BF16), preferred_element_type=F32)
    h = jnp.dot(u, wh_ref[...].astype(BF16), preferred_element_type=F32)
    ch_ref[...] = c * h


def _gated_in(kernel, n_parts, n_out, tm, x, sc, sh, w, name):
    s = x.shape[0]
    tn = TN_GLU
    nj = D_MODEL // tn
    w_specs = [pl.BlockSpec((D_MODEL, tn), functools.partial(lambda i, j, p: (0, p * nj + j), p=p))
               for p in range(n_parts)]
    out_spec = pl.BlockSpec((tm, tn), lambda i, j: (i, j))
    out_shape = jax.ShapeDtypeStruct((s, D_MODEL), F32)
    return pl.pallas_call(
        kernel,
        grid=(s // tm, nj),
        in_specs=[pl.BlockSpec((tm, D_MODEL), lambda i, j: (i, 0)),
                  pl.BlockSpec((1, D_MODEL), lambda i, j: (0, 0)),
                  pl.BlockSpec((1, D_MODEL), lambda i, j: (0, 0))] + w_specs,
        out_specs=[out_spec] * n_out if n_out > 1 else out_spec,
        out_shape=[out_shape] * n_out if n_out > 1 else out_shape,
        scratch_shapes=[pltpu.VMEM((tm, D_MODEL), BF16)],
        compiler_params=_cparams(("arbitrary", "arbitrary")),
        name=name,
    )(x, sc, sh, *([w] * n_parts))


def _fill_halo(hal_scr, prev_ref, cur_ref, halo):
    is_first = pl.program_id(0) == 0
    tm = cur_ref.shape[0]
    hal_scr[pl.ds(0, halo), :] = jnp.where(is_first, 0.0, prev_ref[...])
    hal_scr[pl.ds(halo, tm), :] = cur_ref[...]
    hal_scr[pl.ds(halo + tm, SUBLANES_V7X), :] = jnp.zeros((SUBLANES_V7X, D_MODEL), F32)


def _dwconv(hal_scr, w_ref, dst_scr, ph_scr, halo, ksize, tm):
    off = halo - (ksize - 1)
    win = CONV_ROWS + SUBLANES_V7X
    for r0 in range(0, tm, CONV_ROWS):
        for c0 in range(0, D_MODEL, CONV_COLS):
            cs = slice(c0, c0 + CONV_COLS)
            out = None
            for phase in range(SUBLANES_V7X):
                part = None
                for k in range(ksize):
                    if (off + k) % SUBLANES_V7X != phase:
                        continue
                    base = r0 + off + k - phase
                    term = w_ref[k:k + 1, cs] * hal_scr[base:base + win, cs]
                    part = term if part is None else part + term
                if part is None:
                    continue
                if phase == 0:
                    shifted = part[:CONV_ROWS]
                else:
                    ph_scr[phase] = part
                    shifted = ph_scr[phase, phase:phase + CONV_ROWS, :]
                out = shifted if out is None else out + shifted
            dst_scr[r0:r0 + CONV_ROWS, cs] = out


def _conv_out_kernel(cur_ref, prev_ref, wdw_ref, cg_ref, cb_ref, w_ref, x_ref, gate_ref, g_ref, b_ref,
                     out_ref, hal_scr, h_scr, ph_scr, w_scr):
    tm = x_ref.shape[0]
    _fill_halo(hal_scr, prev_ref, cur_ref, HALO_CONV)
    _dwconv(hal_scr, wdw_ref, h_scr, ph_scr, HALO_CONV, CONV_KERNEL, tm)
    hn = _ln_rows(h_scr[...], cg_ref[...], cb_ref[...])
    a = (hn * jax.nn.sigmoid(hn)).astype(BF16)
    y = jnp.dot(a, _resident_bf16(w_ref, w_scr), preferred_element_type=F32)
    out_ref[...] = _deepnorm(x_ref[...], y, gate_ref[...], g_ref[...], b_ref[...])


def _sc_out_kernel(cur_ref, prev_ref, bg_ref, wdw_ref, w_ref, x_ref, gate_ref, g_ref, b_ref,
                   out_ref, hal_scr, h_scr, ph_scr, w_scr):
    tm = x_ref.shape[0]
    _fill_halo(hal_scr, prev_ref, cur_ref, HALO_SC)
    _dwconv(hal_scr, wdw_ref, h_scr, ph_scr, HALO_SC, SHORT_CONV_KERNEL, tm)
    a = (bg_ref[...] * h_scr[...]).astype(BF16)
    y = jnp.dot(a, _resident_bf16(w_ref, w_scr), preferred_element_type=F32)
    out_ref[...] = _deepnorm(x_ref[...], y, gate_ref[...], g_ref[...], b_ref[...])


def _conv_phase_scratch():
    return pltpu.VMEM((SUBLANES_V7X, CONV_ROWS + SUBLANES_V7X, CONV_COLS), F32)


def _halo_spec(tm, halo):
    per = tm // halo
    return pl.BlockSpec((halo, D_MODEL), lambda i: (jnp.maximum(i * per - 1, 0), 0))


def _conv_out(glu, w_dw, cg, cb, w, x, gate, g, b):
    s = x.shape[0]
    tm = TM_OUT
    full = lambda shape: pl.BlockSpec(shape, lambda i: (0, 0))
    return pl.pallas_call(
        _conv_out_kernel,
        grid=(s // tm,),
        in_specs=[_row_spec(tm, D_MODEL), _halo_spec(tm, HALO_CONV), full((CONV_KERNEL, D_MODEL)),
                  _vec_spec(D_MODEL), _vec_spec(D_MODEL), _resident_spec((D_MODEL, D_MODEL)),
                  _row_spec(tm, D_MODEL), _vec_spec(D_MODEL), _vec_spec(D_MODEL), _vec_spec(D_MODEL)],
        out_specs=_row_spec(tm, D_MODEL),
        out_shape=jax.ShapeDtypeStruct((s, D_MODEL), F32),
        scratch_shapes=[pltpu.VMEM((tm + HALO_CONV + SUBLANES_V7X, D_MODEL), F32), pltpu.VMEM((tm, D_MODEL), F32),
                        _conv_phase_scratch(), pltpu.VMEM((D_MODEL, D_MODEL), BF16)],
        compiler_params=_cparams(("arbitrary",)),
        name="conv_out",
    )(glu, glu, w_dw, cg, cb, w, x, gate, g, b)


def _sc_out(ch, bgate, w_dw, w, x, gate, g, b):
    s = x.shape[0]
    tm = TM_OUT
    full = lambda shape: pl.BlockSpec(shape, lambda i: (0, 0))
    return pl.pallas_call(
        _sc_out_kernel,
        grid=(s // tm,),
        in_specs=[_row_spec(tm, D_MODEL), _halo_spec(tm, HALO_SC), _row_spec(tm, D_MODEL),
                  full((SHORT_CONV_KERNEL, D_MODEL)), _resident_spec((D_MODEL, D_MODEL)),
                  _row_spec(tm, D_MODEL), _vec_spec(D_MODEL), _vec_spec(D_MODEL), _vec_spec(D_MODEL)],
        out_specs=_row_spec(tm, D_MODEL),
        out_shape=jax.ShapeDtypeStruct((s, D_MODEL), F32),
        scratch_shapes=[pltpu.VMEM((tm + HALO_SC + SUBLANES_V7X, D_MODEL), F32), pltpu.VMEM((tm, D_MODEL), F32),
                        _conv_phase_scratch(), pltpu.VMEM((D_MODEL, D_MODEL), BF16)],
        compiler_params=_cparams(("arbitrary",)),
        name="sc_out",
    )(ch, ch, bgate, w_dw, w, x, gate, g, b)


def _argmax4_first(v):
    i01 = jnp.where(v[1] > v[0], 1, 0)
    m01 = jnp.maximum(v[0], v[1])
    i23 = jnp.where(v[3] > v[2], 3, 2)
    m23 = jnp.maximum(v[2], v[3])
    return jnp.where(m23 > m01, i23, i01), jnp.maximum(m01, m23)


def _select4(idx, v):
    return jnp.where(idx == 0, v[0], jnp.where(idx == 1, v[1], jnp.where(idx == 2, v[2], v[3])))


def _router_kernel(x_ref, sc_ref, sh_ref, wt_ref, rb_ref, e_ref, gt_ref, rk_ref, cnt_ref, carry_scr):
    i = pl.program_id(0)
    tm = x_ref.shape[0]

    @pl.when(i == 0)
    def _():
        carry_scr[...] = jnp.zeros_like(carry_scr)

    u = x_ref[...] * (1.0 + sc_ref[...]) + sh_ref[...]
    logits = lax.dot_general(wt_ref[...], u, (((1,), (1,)), ((), ())), precision=lax.Precision.HIGHEST,
                             preferred_element_type=F32) + rb_ref[...]
    ex = jnp.exp(logits - jnp.max(logits, axis=0, keepdims=True))
    probs = ex / jnp.sum(ex, axis=0, keepdims=True)

    top1_i, top1_v, top2_i, top2_v, score = [], [], [], [], []
    for grp in range(N_EXPERT_GROUPS):
        p = [probs[grp * EXPERTS_PER_GROUP + j:grp * EXPERTS_PER_GROUP + j + 1, :]
             for j in range(EXPERTS_PER_GROUP)]
        i1, v1 = _argmax4_first(p)
        rest = [jnp.where(i1 == j, -1.0, p[j]) for j in range(EXPERTS_PER_GROUP)]
        i2, v2 = _argmax4_first(rest)
        top1_i.append(i1)
        top1_v.append(v1)
        top2_i.append(i2)
        top2_v.append(v2)
        score.append(v1 + v2)
    gsel, _ = _argmax4_first(score)
    p1 = _select4(gsel, top1_v)
    p2 = _select4(gsel, top2_v)
    e1 = gsel * EXPERTS_PER_GROUP + _select4(gsel, top1_i)
    e2 = gsel * EXPERTS_PER_GROUP + _select4(gsel, top2_i)
    psum = p1 + p2
    e_ref[0:1, :] = e1
    e_ref[1:2, :] = e2
    gt_ref[0:1, :] = p1 / psum
    gt_ref[1:2, :] = p2 / psum

    eid = lax.broadcasted_iota(jnp.int32, (N_EXPERTS, tm), 0)
    earlier = (lax.broadcasted_iota(jnp.int32, (tm, tm), 0)
               < lax.broadcasted_iota(jnp.int32, (tm, tm), 1)).astype(BF16)
    base = carry_scr[:, 0:1]
    for slot, e_sel in enumerate((e1, e2)):
        onehot = (eid == e_sel).astype(F32)
        before = jnp.dot(onehot.astype(BF16), earlier, preferred_element_type=F32)
        rank = jnp.sum(onehot * (before + base), axis=0, keepdims=True)
        rk_ref[slot:slot + 1, :] = rank.astype(jnp.int32)
        base = base + jnp.sum(onehot, axis=1, keepdims=True)
    carry_scr[...] = jnp.broadcast_to(base, carry_scr.shape)
    cnt_ref[...] = carry_scr[...]


def _router(x, sc, sh, rw_t, rb_col):
    s = x.shape[0]
    tm = TM_ROUTE
    slot_spec = pl.BlockSpec((TOP_K, tm), lambda i: (0, i))
    full = lambda shape: pl.BlockSpec(shape, lambda i: (0, 0))
    return pl.pallas_call(
        _router_kernel,
        grid=(s // tm,),
        in_specs=[_row_spec(tm, D_MODEL), _vec_spec(D_MODEL), _vec_spec(D_MODEL),
                  full((N_EXPERTS, D_MODEL)), full((N_EXPERTS, 1))],
        out_specs=[slot_spec, slot_spec, slot_spec, full((N_EXPERTS, LANES_V7X))],
        out_shape=[jax.ShapeDtypeStruct((TOP_K, s), jnp.int32), jax.ShapeDtypeStruct((TOP_K, s), F32),
                   jax.ShapeDtypeStruct((TOP_K, s), jnp.int32),
                   jax.ShapeDtypeStruct((N_EXPERTS, LANES_V7X), F32)],
        scratch_shapes=[pltpu.VMEM((N_EXPERTS, LANES_V7X), F32)],
        compiler_params=_cparams(("arbitrary",)),
        name="router",
    )(x, sc, sh, rw_t, rb_col)


def _row_gather(src_hbm, idx_ref, base, n, dst, sem, priorities=(0, 1)):
    for r in range(n):
        tok = idx_ref[base + r]
        copy = pltpu.make_async_copy(src_hbm.at[pl.ds(tok, 1), :], dst.at[pl.ds(r, 1), :], sem)
        copy.start(priority=priorities[r % len(priorities)])


def _row_gather_wait(src_hbm, n, dst, sem):
    pltpu.make_async_copy(src_hbm.at[pl.ds(0, n), :], dst, sem).wait()


class _WeightStream:
    def __init__(self, layer, w_hbm, stage, wbuf, sem):
        self.layer, self.w_hbm, self.stage, self.wbuf, self.sem = layer, w_hbm, stage, wbuf, sem

    def _copy(self, mat, expert, q, slot):
        kind = 0 if mat < 2 else 1
        rows = self.stage[kind].shape[1]
        src = self.w_hbm[mat].at[self.layer, expert, pl.ds(pl.multiple_of(q * rows, rows), rows), :]
        return pltpu.make_async_copy(src, self.stage[kind].at[slot], self.sem.at[kind, slot])

    def _per_matrix(self, c, fn):
        mat, q, slot = c // W_CHUNKS, c % W_CHUNKS, c % 2
        for m in range(3):
            @pl.when(mat == m)
            def _(m=m):
                fn(m, q, slot)

    def start(self, c, expert):
        self._per_matrix(c, lambda m, q, slot: self._copy(m, expert, q, slot).start(priority=W_STREAM_PRIORITY))

    def finish(self, c, dst):
        def fn(m, q, slot):
            self._copy(m, 0, q, slot).wait()
            kind = 0 if m < 2 else 1
            rows = self.stage[kind].shape[1]
            self.wbuf[m][dst, pl.ds(pl.multiple_of(q * rows, rows), rows), :] = self.stage[kind][slot].astype(BF16)

        self._per_matrix(c, fn)

    def prefetch(self, expert, lo, hi):
        for t in range(2):
            @pl.when(lo + t < hi)
            def _(t=t):
                self.start(lo + t, expert)

    def drain(self, expert, dst, lo, upto, hi):
        def body(c, carry):
            self.finish(c, dst)

            @pl.when(c + 2 < hi)
            def _():
                self.start(c + 2, expert)

            return carry

        lax.fori_loop(lo, upto, body, 0)


def _ffn_kernel(rt_ref, be_ref, nbu_ref, ws_ref, nx_ref, lo_ref, hi_ref,
                x_hbm, sc_ref, sh_ref, wg_hbm, wu_hbm, wd_hbm, o_ref,
                xbuf, sem, wg_buf, wu_buf, wd_buf, stage_in, stage_dn, wsem, *, layer):
    i = pl.program_id(0)
    nbu = nbu_ref[0]
    tm = xbuf.shape[1]
    slot = i % 2
    stream = _WeightStream(layer, (wg_hbm, wu_hbm, wd_hbm), (stage_in, stage_dn), (wg_buf, wu_buf, wd_buf), wsem)
    n_chunks = 3 * W_CHUNKS

    @pl.when(i == 0)
    def _():
        _row_gather(x_hbm, rt_ref, 0, tm, xbuf.at[0], sem.at[0], ROW_GATHER_PRIORITY)
        stream.prefetch(be_ref[0], 0, n_chunks)
        stream.drain(be_ref[0], ws_ref[0], 0, n_chunks, n_chunks)

    @pl.when(i < nbu)
    def _():
        ws, nxt, lo, hi = ws_ref[i], nx_ref[i], lo_ref[i], hi_ref[i]
        stream.prefetch(nxt, lo, hi)
        _row_gather_wait(x_hbm, tm, xbuf.at[slot], sem.at[slot])
        _row_gather(x_hbm, rt_ref, (i + 1) * tm, tm, xbuf.at[1 - slot], sem.at[1 - slot], ROW_GATHER_PRIORITY)
        u = (xbuf[slot] * (1.0 + sc_ref[...]) + sh_ref[...]).astype(BF16)
        mid1, mid2 = jnp.minimum(lo + 2, hi), jnp.minimum(lo + 4, hi)
        gt = jnp.dot(u, wg_buf[ws], preferred_element_type=F32)
        stream.drain(nxt, 1 - ws, lo, mid1, hi)
        up = jnp.dot(u, wu_buf[ws], preferred_element_type=F32)
        stream.drain(nxt, 1 - ws, mid1, mid2, hi)
        h = (gt * jax.nn.sigmoid(gt) * up).astype(BF16)
        o_ref[...] = jnp.dot(h, wd_buf[ws], preferred_element_type=F32)
        stream.drain(nxt, 1 - ws, mid2, hi, hi)

    @pl.when(i >= nbu)
    def _():
        @pl.when(i == nbu)
        def _():
            _row_gather_wait(x_hbm, tm, xbuf.at[slot], sem.at[slot])

        o_ref[...] = jnp.zeros_like(o_ref)


def _moe_ffn(x, sc, sh, w_gate, w_up, w_down, layer, row_token, block_expert, nb_used, w_slot, next_expert,
             chunk_lo, chunk_hi):
    tm = TM_MOE
    p_rows = row_token.shape[0]
    nb = p_rows // tm
    vec = pl.BlockSpec((1, D_MODEL), lambda i, *_: (0, 0))
    hbm = pl.BlockSpec(memory_space=pl.ANY)
    return pl.pallas_call(
        functools.partial(_ffn_kernel, layer=layer),
        grid_spec=pltpu.PrefetchScalarGridSpec(
            num_scalar_prefetch=7,
            grid=(nb,),
            in_specs=[hbm, vec, vec, hbm, hbm, hbm],
            out_specs=pl.BlockSpec((tm, D_MODEL), lambda i, *_: (i, 0)),
            scratch_shapes=[pltpu.VMEM((2, tm, D_MODEL), F32), pltpu.SemaphoreType.DMA((2,)),
                            pltpu.VMEM((2, D_MODEL, D_EXPERT), BF16), pltpu.VMEM((2, D_MODEL, D_EXPERT), BF16),
                            pltpu.VMEM((2, D_EXPERT, D_MODEL), BF16),
                            pltpu.VMEM((2, D_MODEL // W_CHUNKS, D_EXPERT), F32),
                            pltpu.VMEM((2, D_EXPERT // W_CHUNKS, D_MODEL), F32),
                            pltpu.SemaphoreType.DMA((2, 2))]),
        out_shape=jax.ShapeDtypeStruct((p_rows, D_MODEL), F32),
        compiler_params=pltpu.CompilerParams(dimension_semantics=("arbitrary",),
                                             vmem_limit_bytes=VMEM_LIMIT_FFN_V7X),
        name="moe_ffn",
    )(row_token, block_expert, nb_used, w_slot, next_expert, chunk_lo, chunk_hi,
      x, sc, sh, w_gate, w_up, w_down)


def _combine_kernel(dest_ref, rows_hbm, gt_ref, x_ref, gate_ref, g_ref, b_ref, out_ref, ybuf, sem):
    i = pl.program_id(0)
    n = pl.num_programs(0)
    tm = x_ref.shape[0]
    s = tm * n
    slot = i % 2

    def start(tile, to):
        for k in range(TOP_K):
            _row_gather(rows_hbm, dest_ref, k * s + tile * tm, tm, ybuf.at[to, k], sem.at[to, k])

    def wait(at):
        for k in range(TOP_K):
            _row_gather_wait(rows_hbm, tm, ybuf.at[at, k], sem.at[at, k])

    @pl.when(i == 0)
    def _():
        start(0, 0)

    wait(slot)
    start(jnp.minimum(i + 1, n - 1), 1 - slot)
    gt = gt_ref[...]
    y = gt[:, 0:1] * ybuf[slot, 0] + gt[:, 1:2] * ybuf[slot, 1]
    out_ref[...] = _deepnorm(x_ref[...], y, gate_ref[...], g_ref[...], b_ref[...])

    @pl.when(i == n - 1)
    def _():
        wait(1 - slot)


def _moe_combine(dest_flat, out_rows, gates_t, x, gate, g, b):
    s = x.shape[0]
    tm = TM_COMB
    vec = pl.BlockSpec((1, D_MODEL), lambda i, dst: (0, 0))
    return pl.pallas_call(
        _combine_kernel,
        grid_spec=pltpu.PrefetchScalarGridSpec(
            num_scalar_prefetch=1,
            grid=(s // tm,),
            in_specs=[pl.BlockSpec(memory_space=pl.ANY),
                      pl.BlockSpec((tm, TOP_K), lambda i, dst: (i, 0)),
                      pl.BlockSpec((tm, D_MODEL), lambda i, dst: (i, 0)), vec, vec, vec],
            out_specs=pl.BlockSpec((tm, D_MODEL), lambda i, dst: (i, 0)),
            scratch_shapes=[pltpu.VMEM((2, TOP_K, tm, D_MODEL), F32), pltpu.SemaphoreType.DMA((2, TOP_K))]),
        out_shape=jax.ShapeDtypeStruct((s, D_MODEL), F32),
        compiler_params=_cparams(("arbitrary",)),
        name="moe_combine",
    )(dest_flat, out_rows, gates_t, x, gate, g, b)


def _moe_layer(x, sc, sh, gate, g, b, rw_t, rb_col, w_gate, w_up, w_down, layer):
    s = x.shape[0]
    tm = TM_MOE
    expert, gates, rank, cnt = _router(x, sc, sh, rw_t, rb_col)
    counts = cnt[:, 0].astype(jnp.int32)
    padded = (counts + tm - 1) // tm * tm
    pad_end = jnp.cumsum(padded)
    pad_start = pad_end - padded
    eids = jnp.arange(N_EXPERTS, dtype=jnp.int32)
    start_of = jnp.sum(jnp.where(expert[..., None] == eids, pad_start, 0), axis=-1)
    dest = (start_of + rank).reshape(-1)
    p_rows = TOP_K * s + N_EXPERTS * tm
    token = jnp.tile(jnp.arange(s, dtype=jnp.int32), TOP_K)
    row_token = jnp.zeros((p_rows,), jnp.int32).at[dest].set(token)
    nb = p_rows // tm
    block_row0 = jnp.arange(nb, dtype=jnp.int32) * tm
    block_expert = jnp.minimum(jnp.sum(block_row0[:, None] >= pad_end[None, :], axis=-1),
                               N_EXPERTS - 1).astype(jnp.int32)
    nb_used = (pad_end[-1:] // tm).astype(jnp.int32)
    has = padded > 0
    order = jnp.cumsum(has.astype(jnp.int32)) - 1
    later = jnp.logical_and(eids[None, :] > eids[:, None], has[None, :])
    next_of = jnp.min(jnp.where(later, eids[None, :], N_EXPERTS), axis=-1)
    of_block = lambda v: jnp.sum(jnp.where(block_expert[:, None] == eids[None, :], v[None, :], 0), axis=-1)
    run_len = jnp.maximum(of_block(padded) // tm, 1)
    run_pos = jnp.arange(nb, dtype=jnp.int32) - of_block(pad_start) // tm
    nxt = of_block(next_of)
    streams = jnp.logical_and(nxt < N_EXPERTS, jnp.arange(nb) < nb_used[0])
    n_chunks = 3 * W_CHUNKS
    chunk_lo = jnp.where(streams, n_chunks * run_pos // run_len, 0).astype(jnp.int32)
    chunk_hi = jnp.where(streams, n_chunks * (run_pos + 1) // run_len, 0).astype(jnp.int32)
    next_expert = jnp.where(streams, nxt, block_expert).astype(jnp.int32)
    w_slot = (of_block(order) % 2).astype(jnp.int32)
    out_rows = _moe_ffn(x, sc, sh, w_gate, w_up, w_down, layer, row_token, block_expert, nb_used,
                        w_slot, next_expert, chunk_lo, chunk_hi)
    return _moe_combine(dest.astype(jnp.int32), out_rows, gates.T, x, gate, g, b)


def kernel(x, c, positions, ada_w, ada_b, ln_g, ln_b, attn_w_qkv, attn_w_o, conv_w_pw1, conv_w_dw, conv_ln_g,
           conv_ln_b, conv_w_pw2, sc_w_in, sc_w_conv, sc_w_out, router_w, router_b, moe_w_gate, moe_w_up,
           moe_w_down):
    batch, s, d = x.shape
    assert batch == 1 and d == D_MODEL and s % (DILATIONS[-1] * Q_BLOCK) == 0
    xs = x.reshape(s, d)

    mod = _adaln(c.reshape(d, 1), ada_w, ada_b)
    half = HEAD_DIM // 2
    inv_freq = ROPE_THETA ** (-jnp.arange(half, dtype=F32) / half)
    freq_row = jnp.concatenate([inv_freq, inv_freq]).reshape(1, HEAD_DIM)
    cos, sin = _rope_tables(positions.reshape(s, 1), freq_row)

    rw_t = router_w.T
    rb_col = router_b.reshape(N_EXPERTS, 1)
    vec = lambda a: a.reshape(1, d)

    for i in range(DEPTH):
        sh1, sc1, g1, sh2, sc2, g2 = [mod[i, :, k * d:(k + 1) * d] for k in range(6)]
        m, j = i % N_MIXERS, i // N_MIXERS
        lg, lb = vec(ln_g[i, 0]), vec(ln_b[i, 0])
        if m == 0:
            outs, lses = [], []
            for grp in range(len(DILATIONS)):
                o_g, l_g = _attention(_qkv_proj(xs, sc1, sh1, attn_w_qkv, j, cos, sin, grp), grp)
                outs.append(o_g)
                lses.append(l_g)
            xs = _attn_out(outs, lses, attn_w_o[j], xs, g1, lg, lb)
        elif m == 1:
            glu = _gated_in(_glu_in_kernel, 2, 1, TM_GLU, xs, sc1, sh1, conv_w_pw1[j], "conv_in")
            xs = _conv_out(glu, conv_w_dw[j], vec(conv_ln_g[j]), vec(conv_ln_b[j]),
                           conv_w_pw2[j], xs, g1, lg, lb)
        else:
            bgate, ch = _gated_in(_sc_in_kernel, 3, 2, TM_SC_IN, xs, sc1, sh1, sc_w_in[j], "sc_in")
            xs = _sc_out(ch, bgate, sc_w_conv[j], sc_w_out[j], xs, g1, lg, lb)
        xs = _moe_layer(xs, sc2, sh2, g2, vec(ln_g[i, 1]), vec(ln_b[i, 1]), rw_t, rb_col,
                        moe_w_gate, moe_w_up, moe_w_down, i)
    return xs.reshape(batch, s, d)
```

```python
import functools
import math

import jax
import jax.numpy as jnp
from jax import lax
from jax.experimental import pallas as pl
from jax.experimental.pallas import tpu as pltpu

F32 = jnp.float32
BF16 = jnp.bfloat16

D_MODEL = 2048
DEPTH = 4
N_MIXERS = 3
ATTN_HEADS = 16
HEAD_DIM = 128
HD = ATTN_HEADS * HEAD_DIM
DILATIONS = (1, 4, 16)
Q_BLOCK = 128
ROPE_THETA = 10000.0
CONV_KERNEL = 31
SHORT_CONV_KERNEL = 3
N_EXPERTS = 16
N_EXPERT_GROUPS = 4
EXPERTS_PER_GROUP = 4
TOP_K = 2
D_EXPERT = 1408
ALPHA = (2 * DEPTH) ** 0.25
LN_EPS = 1e-5
NEG_INF = -1e30

LANES_V7X = 128
SUBLANES_V7X = 8
MXU_COLS_V7X = 256
VMEM_LIMIT_V7X = 56 * 1024 * 1024
VMEM_LIMIT_FFN_V7X = 61 * 1024 * 1024
W_CHUNKS = 8
ROW_GATHER_PRIORITY = (0, 1)
W_STREAM_PRIORITY = 0
TM_PROJ = 1024
TM_QKV = 1024
TN_PROJ = 1024
TN_GLU = 512
TM_OUT = 256
TM_ROUTE = 256
TM_MOE = 256
TM_COMB = 256
TN_ADA = 1024
HALO_CONV = 32
HALO_SC = 8
CONV_ROWS = 64
CONV_COLS = 256


def _cparams(sem):
    return pltpu.CompilerParams(dimension_semantics=sem, vmem_limit_bytes=VMEM_LIMIT_V7X)


def _ln_rows(z, g, b):
    mu = jnp.mean(z, axis=-1, keepdims=True)
    zc = z - mu
    var = jnp.mean(zc * zc, axis=-1, keepdims=True)
    return zc * lax.rsqrt(var + LN_EPS) * g + b


def _deepnorm(x, y, gate, g, b):
    return _ln_rows(ALPHA * x + (1.0 + gate) * y, g, b)


def _row_spec(tm, width):
    return pl.BlockSpec((tm, width), lambda i: (i, 0))


def _vec_spec(width):
    return pl.BlockSpec((1, width), lambda i: (0, 0))


def _adaln_kernel(c_ref, w_ref, b_ref, o_ref):
    c = c_ref[...]
    ca = c * jax.nn.sigmoid(c)
    o_ref[...] = jnp.sum(w_ref[...] * ca, axis=0, keepdims=True) + b_ref[...]


def _adaln(c_col, ada_w, ada_b):
    depth, d, n = ada_w.shape
    return pl.pallas_call(
        _adaln_kernel,
        grid=(depth, n // TN_ADA),
        in_specs=[pl.BlockSpec((d, 1), lambda l, j: (0, 0)),
                  pl.BlockSpec((None, d, TN_ADA), lambda l, j: (l, 0, j)),
                  pl.BlockSpec((None, 1, TN_ADA), lambda l, j: (l, 0, j))],
        out_specs=pl.BlockSpec((None, 1, TN_ADA), lambda l, j: (l, 0, j)),
        out_shape=jax.ShapeDtypeStruct((depth, 1, n), F32),
        compiler_params=_cparams(("arbitrary", "arbitrary")),
        name="adaln",
    )(c_col, ada_w, ada_b.reshape(depth, 1, n))


def _rope_kernel(pos_ref, freq_ref, cos_ref, sin_ref):
    ang = pos_ref[...].astype(F32) * freq_ref[...]
    lane = lax.broadcasted_iota(jnp.int32, ang.shape, 1)
    s = jnp.sin(ang)
    cos_ref[...] = jnp.cos(ang)
    sin_ref[...] = jnp.where(lane < HEAD_DIM // 2, -s, s)


def _rope_tables(pos_col, freq_row):
    s = pos_col.shape[0]
    tm = 1024
    spec = pl.BlockSpec((tm, HEAD_DIM), lambda i: (i, 0))
    return pl.pallas_call(
        _rope_kernel,
        grid=(s // tm,),
        in_specs=[pl.BlockSpec((tm, 1), lambda i: (i, 0)), _vec_spec(HEAD_DIM)],
        out_specs=[spec, spec],
        out_shape=[jax.ShapeDtypeStruct((s, HEAD_DIM), F32)] * 2,
        compiler_params=_cparams(("arbitrary",)),
        name="rope_tables",
    )(pos_col, freq_row)


def _qkv_kernel(x_ref, sc_ref, sh_ref, w_ref, cos_ref, sin_ref, o_ref, u_scr, acc_scr, *, d, n_rope):
    j = pl.program_id(1)
    nh = acc_scr.shape[0]
    tm = x_ref.shape[0]
    skew = _residue_stride(d)

    @pl.when(j == 0)
    def _():
        u_scr[...] = (x_ref[...] * (1.0 + sc_ref[...]) + sh_ref[...]).astype(BF16)

    is_rope = j < n_rope
    cos = cos_ref[...]
    sin = sin_ref[...]
    u = u_scr[...]
    for c in range(nh // 2):
        w = w_ref[:, c * MXU_COLS_V7X:(c + 1) * MXU_COLS_V7X].astype(BF16)
        acc = jnp.dot(u, w, preferred_element_type=F32)
        for h in (2 * c, 2 * c + 1):
            a = acc[:, (h - 2 * c) * HEAD_DIM:(h - 2 * c + 1) * HEAD_DIM]
            a = jnp.where(is_rope, a * cos + pltpu.roll(a, HEAD_DIM // 2, 1) * sin, a)
            hs = slice(h * HEAD_DIM, (h + 1) * HEAD_DIM)
            if d == 1:
                o_ref[0, :, hs] = a.astype(BF16)
            else:
                if skew == d:
                    acc_scr[h] = a
                else:
                    for k in range(tm // d):
                        acc_scr[h, k * skew:k * skew + d] = a[k * d:(k + 1) * d]
                for r in range(d):
                    o_ref[r, :, hs] = acc_scr[h, pl.ds(r, tm // d, stride=skew), :].astype(BF16)


def _residue_stride(d):
    return d + 1 if d % SUBLANES_V7X == 0 else d


def _qkv_proj(x, sc, sh, w_all, layer, cos, sin, group):
    s = x.shape[0]
    d = DILATIONS[group]
    tm, tn = TM_QKV, TN_PROJ
    ncol = 3 * HD
    col0 = group * (ncol // tn)
    return pl.pallas_call(
        functools.partial(_qkv_kernel, d=d, n_rope=2 * HD // tn),
        grid=(s // tm, ncol // tn),
        in_specs=[pl.BlockSpec((tm, D_MODEL), lambda i, j: (i, 0)),
                  pl.BlockSpec((1, D_MODEL), lambda i, j: (0, 0)),
                  pl.BlockSpec((1, D_MODEL), lambda i, j: (0, 0)),
                  pl.BlockSpec((None, D_MODEL, tn), lambda i, j: (layer, 0, col0 + j)),
                  pl.BlockSpec((tm, HEAD_DIM), lambda i, j: (i, 0)),
                  pl.BlockSpec((tm, HEAD_DIM), lambda i, j: (i, 0))],
        out_specs=pl.BlockSpec((d, tm // d, tn), lambda i, j: (0, i, j)),
        out_shape=jax.ShapeDtypeStruct((d, s // d, ncol), BF16),
        scratch_shapes=[pltpu.VMEM((tm, D_MODEL), BF16),
                        pltpu.VMEM((tn // HEAD_DIM, tm // d * _residue_stride(d), HEAD_DIM), F32)],
        compiler_params=_cparams(("arbitrary", "arbitrary")),
        name=f"qkv_proj_g{group}",
    )(x, sc, sh, w_all, cos, sin)


def _attn_kernel(q_ref, kc_ref, vc_ref, o_ref, lse_ref, k_scr, v_scr, s_scr, p_scr, *, nblk):
    b = pl.program_id(0)
    not_first = (b % nblk) != 0

    @pl.when(b == 0)
    def _():
        k_scr[0:Q_BLOCK] = jnp.zeros((Q_BLOCK, HD), BF16)
        v_scr[0:Q_BLOCK] = jnp.zeros((Q_BLOCK, HD), BF16)

    k_scr[Q_BLOCK:] = kc_ref[...]
    v_scr[Q_BLOCK:] = vc_ref[...]
    nt = (((1,), (1,)), ((), ()))
    for h in range(ATTN_HEADS):
        sl = slice(h * HEAD_DIM, (h + 1) * HEAD_DIM)
        s_scr[h] = lax.dot_general(q_ref[:, sl], k_scr[:, sl], nt, preferred_element_type=F32)

    row = lax.broadcasted_iota(jnp.int32, (Q_BLOCK, 2 * Q_BLOCK), 0)
    col = lax.broadcasted_iota(jnp.int32, (Q_BLOCK, 2 * Q_BLOCK), 1)
    prev_ok = jnp.logical_and(jnp.logical_and(col < Q_BLOCK, col >= row), not_first)
    valid = jnp.logical_or(prev_ok, jnp.logical_and(col >= Q_BLOCK, col - Q_BLOCK <= row))
    scale = 1.0 / math.sqrt(HEAD_DIM)
    inv = []
    for h in range(ATTN_HEADS):
        s = jnp.where(valid, s_scr[h], NEG_INF)
        m = jnp.max(s, axis=1, keepdims=True)
        p = jnp.exp2((s - m) * (scale * math.log2(math.e)))
        den = jnp.sum(p, axis=1, keepdims=True)
        p_scr[h] = p.astype(BF16)
        inv.append(1.0 / den)
        lse_ref[:, h:h + 1] = m * scale + jnp.log(den)
    for h in range(ATTN_HEADS):
        sl = slice(h * HEAD_DIM, (h + 1) * HEAD_DIM)
        o = jnp.dot(p_scr[h], v_scr[:, sl], preferred_element_type=F32)
        o_ref[:, sl] = (o * inv[h]).astype(BF16)
    k_scr[0:Q_BLOCK] = kc_ref[...]
    v_scr[0:Q_BLOCK] = vc_ref[...]


def _attention(qkv, group):
    d, l, ncol = qkv.shape
    s = d * l
    flat = qkv.reshape(s, ncol)
    nblk = l // Q_BLOCK
    cur = lambda c: pl.BlockSpec((Q_BLOCK, HD), lambda b: (b, c))
    return pl.pallas_call(
        functools.partial(_attn_kernel, nblk=nblk),
        grid=(s // Q_BLOCK,),
        in_specs=[cur(0), cur(1), cur(2)],
        out_specs=[pl.BlockSpec((Q_BLOCK, HD), lambda b: (b, 0)),
                   pl.BlockSpec((Q_BLOCK, ATTN_HEADS), lambda b: (b, 0))],
        out_shape=[jax.ShapeDtypeStruct((s, HD), BF16), jax.ShapeDtypeStruct((s, ATTN_HEADS), F32)],
        scratch_shapes=[pltpu.VMEM((2 * Q_BLOCK, HD), BF16), pltpu.VMEM((2 * Q_BLOCK, HD), BF16),
                        pltpu.VMEM((ATTN_HEADS, Q_BLOCK, 2 * Q_BLOCK), F32),
                        pltpu.VMEM((ATTN_HEADS, Q_BLOCK, 2 * Q_BLOCK), BF16)],
        compiler_params=_cparams(("arbitrary",)),
        name=f"dil_attn_g{group}",
    )(flat, flat, flat)


def _attn_out_kernel(o0_ref, o1_ref, o2_ref, l0_ref, l1_ref, l2_ref, w_ref, x_ref, gate_ref, g_ref, b_ref,
                     out_ref, o_scr, a_scr):
    tm = x_ref.shape[0]
    for gi, o_ref in enumerate((o1_ref, o2_ref)):
        d = DILATIONS[gi + 1]
        for r in range(d):
            for h in range(ATTN_HEADS):
                o_scr[gi, h, pl.ds(r, tm // d, stride=d), :] = (
                    o_ref[r, :, h * HEAD_DIM:(h + 1) * HEAD_DIM].astype(F32))
    l0 = l0_ref[...]
    l1 = l1_ref[...]
    l2 = l2_ref[...]
    m = jnp.maximum(jnp.maximum(l0, l1), l2)
    e0 = jnp.exp(l0 - m)
    e1 = jnp.exp(l1 - m)
    e2 = jnp.exp(l2 - m)
    z = e0 + e1 + e2
    w0 = e0 / z
    w1 = e1 / z
    w2 = e2 / z
    for h in range(ATTN_HEADS):
        sl = slice(h * HEAD_DIM, (h + 1) * HEAD_DIM)
        a = (w0[:, h:h + 1] * o0_ref[:, sl].astype(F32)
             + w1[:, h:h + 1] * o_scr[0, h]
             + w2[:, h:h + 1] * o_scr[1, h])
        a_scr[:, sl] = a.astype(BF16)
    y = jnp.dot(a_scr[...], w_ref[...], preferred_element_type=F32)
    out_ref[...] = _deepnorm(x_ref[...], y, gate_ref[...], g_ref[...], b_ref[...])


def _attn_out(outs, lses, w_bf, x, gate, g, b):
    s = x.shape[0]
    tm = TM_OUT
    d1, d2 = DILATIONS[1], DILATIONS[2]
    o1 = outs[1].reshape(d1, s // d1, HD)
    o2 = outs[2].reshape(d2, s // d2, HD)
    l1 = lses[1].reshape(d1, s // d1, ATTN_HEADS).transpose(1, 0, 2).reshape(s, ATTN_HEADS)
    l2 = lses[2].reshape(d2, s // d2, ATTN_HEADS).transpose(1, 0, 2).reshape(s, ATTN_HEADS)
    perm = lambda d, w: pl.BlockSpec((d, tm // d, w), lambda i: (0, i, 0))
    return pl.pallas_call(
        _attn_out_kernel,
        grid=(s // tm,),
        in_specs=[_row_spec(tm, HD), perm(d1, HD), perm(d2, HD),
                  _row_spec(tm, ATTN_HEADS), _row_spec(tm, ATTN_HEADS), _row_spec(tm, ATTN_HEADS),
                  pl.BlockSpec((HD, D_MODEL), lambda i: (0, 0)),
                  _row_spec(tm, D_MODEL), _vec_spec(D_MODEL), _vec_spec(D_MODEL), _vec_spec(D_MODEL)],
        out_specs=_row_spec(tm, D_MODEL),
        out_shape=jax.ShapeDtypeStruct((s, D_MODEL), F32),
        scratch_shapes=[pltpu.VMEM((2, ATTN_HEADS, tm, HEAD_DIM), F32), pltpu.VMEM((tm, HD), BF16)],
        compiler_params=_cparams(("arbitrary",)),
        name="attn_out",
    )(outs[0], o1, o2, lses[0], l1, l2, w_bf, x, gate, g, b)


def _modulate_once(x_ref, sc_ref, sh_ref, u_scr):
    @pl.when(pl.program_id(1) == 0)
    def _():
        u_scr[...] = (x_ref[...] * (1.0 + sc_ref[...]) + sh_ref[...]).astype(BF16)


def _glu_in_kernel(x_ref, sc_ref, sh_ref, wa_ref, wg_ref, o_ref, u_scr):
    _modulate_once(x_ref, sc_ref, sh_ref, u_scr)
    u = u_scr[...]
    a = jnp.dot(u, wa_ref[...], preferred_element_type=F32)
    gt = jnp.dot(u, wg_ref[...], preferred_element_type=F32)
    o_ref[...] = a * jax.nn.sigmoid(gt)


def _sc_in_kernel(x_ref, sc_ref, sh_ref, wb_ref, wc_ref, wh_ref, b_ref, ch_ref, u_scr):
    _modulate_once(x_ref, sc_ref, sh_ref, u_scr)
    u = u_scr[...]
    b_ref[...] = jnp.dot(u, wb_ref[...], preferred_element_type=F32)
    c = jnp.dot(u, wc_ref[...], preferred_element_type=F32)
    h = jnp.dot(u, wh_ref[...], preferred_element_type=F32)
    ch_ref[...] = c * h


def _gated_in(kernel, n_parts, n_out, x, sc, sh, w_bf, name):
    s = x.shape[0]
    tm, tn = TM_PROJ, TN_GLU
    nj = D_MODEL // tn
    w_specs = [pl.BlockSpec((D_MODEL, tn), functools.partial(lambda i, j, p: (0, p * nj + j), p=p))
               for p in range(n_parts)]
    out_spec = pl.BlockSpec((tm, tn), lambda i, j: (i, j))
    out_shape = jax.ShapeDtypeStruct((s, D_MODEL), F32)
    return pl.pallas_call(
        kernel,
        grid=(s // tm, nj),
        in_specs=[pl.BlockSpec((tm, D_MODEL), lambda i, j: (i, 0)),
                  pl.BlockSpec((1, D_MODEL), lambda i, j: (0, 0)),
                  pl.BlockSpec((1, D_MODEL), lambda i, j: (0, 0))] + w_specs,
        out_specs=[out_spec] * n_out if n_out > 1 else out_spec,
        out_shape=[out_shape] * n_out if n_out > 1 else out_shape,
        scratch_shapes=[pltpu.VMEM((tm, D_MODEL), BF16)],
        compiler_params=_cparams(("arbitrary", "arbitrary")),
        name=name,
    )(x, sc, sh, *([w_bf] * n_parts))


def _fill_halo(hal_scr, prev_ref, cur_ref, halo):
    is_first = pl.program_id(0) == 0
    tm = cur_ref.shape[0]
    hal_scr[pl.ds(0, halo), :] = jnp.where(is_first, 0.0, prev_ref[...])
    hal_scr[pl.ds(halo, tm), :] = cur_ref[...]
    hal_scr[pl.ds(halo + tm, SUBLANES_V7X), :] = jnp.zeros((SUBLANES_V7X, D_MODEL), F32)


def _dwconv(hal_scr, w_ref, dst_scr, ph_scr, halo, ksize, tm):
    off = halo - (ksize - 1)
    win = CONV_ROWS + SUBLANES_V7X
    for r0 in range(0, tm, CONV_ROWS):
        for c0 in range(0, D_MODEL, CONV_COLS):
            cs = slice(c0, c0 + CONV_COLS)
            out = None
            for phase in range(SUBLANES_V7X):
                part = None
                for k in range(ksize):
                    if (off + k) % SUBLANES_V7X != phase:
                        continue
                    base = r0 + off + k - phase
                    term = w_ref[k:k + 1, cs] * hal_scr[base:base + win, cs]
                    part = term if part is None else part + term
                if part is None:
                    continue
                if phase == 0:
                    shifted = part[:CONV_ROWS]
                else:
                    ph_scr[phase] = part
                    shifted = ph_scr[phase, phase:phase + CONV_ROWS, :]
                out = shifted if out is None else out + shifted
            dst_scr[r0:r0 + CONV_ROWS, cs] = out


def _conv_out_kernel(cur_ref, prev_ref, wdw_ref, cg_ref, cb_ref, w_ref, x_ref, gate_ref, g_ref, b_ref,
                     out_ref, hal_scr, h_scr, ph_scr):
    tm = x_ref.shape[0]
    _fill_halo(hal_scr, prev_ref, cur_ref, HALO_CONV)
    _dwconv(hal_scr, wdw_ref, h_scr, ph_scr, HALO_CONV, CONV_KERNEL, tm)
    hn = _ln_rows(h_scr[...], cg_ref[...], cb_ref[...])
    a = (hn * jax.nn.sigmoid(hn)).astype(BF16)
    y = jnp.dot(a, w_ref[...], preferred_element_type=F32)
    out_ref[...] = _deepnorm(x_ref[...], y, gate_ref[...], g_ref[...], b_ref[...])


def _sc_out_kernel(cur_ref, prev_ref, bg_ref, wdw_ref, w_ref, x_ref, gate_ref, g_ref, b_ref,
                   out_ref, hal_scr, h_scr, ph_scr):
    tm = x_ref.shape[0]
    _fill_halo(hal_scr, prev_ref, cur_ref, HALO_SC)
    _dwconv(hal_scr, wdw_ref, h_scr, ph_scr, HALO_SC, SHORT_CONV_KERNEL, tm)
    a = (bg_ref[...] * h_scr[...]).astype(BF16)
    y = jnp.dot(a, w_ref[...], preferred_element_type=F32)
    out_ref[...] = _deepnorm(x_ref[...], y, gate_ref[...], g_ref[...], b_ref[...])


def _conv_phase_scratch():
    return pltpu.VMEM((SUBLANES_V7X, CONV_ROWS + SUBLANES_V7X, CONV_COLS), F32)


def _halo_spec(tm, halo):
    per = tm // halo
    return pl.BlockSpec((halo, D_MODEL), lambda i: (jnp.maximum(i * per - 1, 0), 0))


def _conv_out(glu, w_dw, cg, cb, w_bf, x, gate, g, b):
    s = x.shape[0]
    tm = TM_OUT
    full = lambda shape: pl.BlockSpec(shape, lambda i: (0, 0))
    return pl.pallas_call(
        _conv_out_kernel,
        grid=(s // tm,),
        in_specs=[_row_spec(tm, D_MODEL), _halo_spec(tm, HALO_CONV), full((CONV_KERNEL, D_MODEL)),
                  _vec_spec(D_MODEL), _vec_spec(D_MODEL), full((D_MODEL, D_MODEL)),
                  _row_spec(tm, D_MODEL), _vec_spec(D_MODEL), _vec_spec(D_MODEL), _vec_spec(D_MODEL)],
        out_specs=_row_spec(tm, D_MODEL),
        out_shape=jax.ShapeDtypeStruct((s, D_MODEL), F32),
        scratch_shapes=[pltpu.VMEM((tm + HALO_CONV + SUBLANES_V7X, D_MODEL), F32), pltpu.VMEM((tm, D_MODEL), F32),
                        _conv_phase_scratch()],
        compiler_params=_cparams(("arbitrary",)),
        name="conv_out",
    )(glu, glu, w_dw, cg, cb, w_bf, x, gate, g, b)


def _sc_out(ch, bgate, w_dw, w_bf, x, gate, g, b):
    s = x.shape[0]
    tm = TM_OUT
    full = lambda shape: pl.BlockSpec(shape, lambda i: (0, 0))
    return pl.pallas_call(
        _sc_out_kernel,
        grid=(s // tm,),
        in_specs=[_row_spec(tm, D_MODEL), _halo_spec(tm, HALO_SC), _row_spec(tm, D_MODEL),
                  full((SHORT_CONV_KERNEL, D_MODEL)), full((D_MODEL, D_MODEL)),
                  _row_spec(tm, D_MODEL), _vec_spec(D_MODEL), _vec_spec(D_MODEL), _vec_spec(D_MODEL)],
        out_specs=_row_spec(tm, D_MODEL),
        out_shape=jax.ShapeDtypeStruct((s, D_MODEL), F32),
        scratch_shapes=[pltpu.VMEM((tm + HALO_SC + SUBLANES_V7X, D_MODEL), F32), pltpu.VMEM((tm, D_MODEL), F32),
                        _conv_phase_scratch()],
        compiler_params=_cparams(("arbitrary",)),
        name="sc_out",
    )(ch, ch, bgate, w_dw, w_bf, x, gate, g, b)


def _argmax4_first(v):
    i01 = jnp.where(v[1] > v[0], 1, 0)
    m01 = jnp.maximum(v[0], v[1])
    i23 = jnp.where(v[3] > v[2], 3, 2)
    m23 = jnp.maximum(v[2], v[3])
    return jnp.where(m23 > m01, i23, i01), jnp.maximum(m01, m23)


def _select4(idx, v):
    return jnp.where(idx == 0, v[0], jnp.where(idx == 1, v[1], jnp.where(idx == 2, v[2], v[3])))


def _router_kernel(x_ref, sc_ref, sh_ref, wt_ref, rb_ref, e_ref, gt_ref, rk_ref, cnt_ref, carry_scr):
    i = pl.program_id(0)
    tm = x_ref.shape[0]

    @pl.when(i == 0)
    def _():
        carry_scr[...] = jnp.zeros_like(carry_scr)

    u = x_ref[...] * (1.0 + sc_ref[...]) + sh_ref[...]
    logits = lax.dot_general(wt_ref[...], u, (((1,), (1,)), ((), ())), precision=lax.Precision.HIGHEST,
                             preferred_element_type=F32) + rb_ref[...]
    ex = jnp.exp(logits - jnp.max(logits, axis=0, keepdims=True))
    probs = ex / jnp.sum(ex, axis=0, keepdims=True)

    top1_i, top1_v, top2_i, top2_v, score = [], [], [], [], []
    for grp in range(N_EXPERT_GROUPS):
        p = [probs[grp * EXPERTS_PER_GROUP + j:grp * EXPERTS_PER_GROUP + j + 1, :]
             for j in range(EXPERTS_PER_GROUP)]
        i1, v1 = _argmax4_first(p)
        rest = [jnp.where(i1 == j, -1.0, p[j]) for j in range(EXPERTS_PER_GROUP)]
        i2, v2 = _argmax4_first(rest)
        top1_i.append(i1)
        top1_v.append(v1)
        top2_i.append(i2)
        top2_v.append(v2)
        score.append(v1 + v2)
    gsel, _ = _argmax4_first(score)
    p1 = _select4(gsel, top1_v)
    p2 = _select4(gsel, top2_v)
    e1 = gsel * EXPERTS_PER_GROUP + _select4(gsel, top1_i)
    e2 = gsel * EXPERTS_PER_GROUP + _select4(gsel, top2_i)
    psum = p1 + p2
    e_ref[0:1, :] = e1
    e_ref[1:2, :] = e2
    gt_ref[0:1, :] = p1 / psum
    gt_ref[1:2, :] = p2 / psum

    eid = lax.broadcasted_iota(jnp.int32, (N_EXPERTS, tm), 0)
    earlier = (lax.broadcasted_iota(jnp.int32, (tm, tm), 0)
               < lax.broadcasted_iota(jnp.int32, (tm, tm), 1)).astype(BF16)
    base = carry_scr[:, 0:1]
    for slot, e_sel in enumerate((e1, e2)):
        onehot = (eid == e_sel).astype(F32)
        before = jnp.dot(onehot.astype(BF16), earlier, preferred_element_type=F32)
        rank = jnp.sum(onehot * (before + base), axis=0, keepdims=True)
        rk_ref[slot:slot + 1, :] = rank.astype(jnp.int32)
        base = base + jnp.sum(onehot, axis=1, keepdims=True)
    carry_scr[...] = jnp.broadcast_to(base, carry_scr.shape)
    cnt_ref[...] = carry_scr[...]


def _router(x, sc, sh, rw_t, rb_col):
    s = x.shape[0]
    tm = TM_ROUTE
    slot_spec = pl.BlockSpec((TOP_K, tm), lambda i: (0, i))
    full = lambda shape: pl.BlockSpec(shape, lambda i: (0, 0))
    return pl.pallas_call(
        _router_kernel,
        grid=(s // tm,),
        in_specs=[_row_spec(tm, D_MODEL), _vec_spec(D_MODEL), _vec_spec(D_MODEL),
                  full((N_EXPERTS, D_MODEL)), full((N_EXPERTS, 1))],
        out_specs=[slot_spec, slot_spec, slot_spec, full((N_EXPERTS, LANES_V7X))],
        out_shape=[jax.ShapeDtypeStruct((TOP_K, s), jnp.int32), jax.ShapeDtypeStruct((TOP_K, s), F32),
                   jax.ShapeDtypeStruct((TOP_K, s), jnp.int32),
                   jax.ShapeDtypeStruct((N_EXPERTS, LANES_V7X), F32)],
        scratch_shapes=[pltpu.VMEM((N_EXPERTS, LANES_V7X), F32)],
        compiler_params=_cparams(("arbitrary",)),
        name="router",
    )(x, sc, sh, rw_t, rb_col)


def _row_gather(src_hbm, idx_ref, base, n, dst, sem, priorities=(0, 1)):
    for r in range(n):
        tok = idx_ref[base + r]
        copy = pltpu.make_async_copy(src_hbm.at[pl.ds(tok, 1), :], dst.at[pl.ds(r, 1), :], sem)
        copy.start(priority=priorities[r % len(priorities)])


def _row_gather_wait(src_hbm, n, dst, sem):
    pltpu.make_async_copy(src_hbm.at[pl.ds(0, n), :], dst, sem).wait()


class _WeightStream:
    def __init__(self, layer, w_hbm, stage, wbuf, sem):
        self.layer, self.w_hbm, self.stage, self.wbuf, self.sem = layer, w_hbm, stage, wbuf, sem

    def _copy(self, mat, expert, q, slot):
        kind = 0 if mat < 2 else 1
        rows = self.stage[kind].shape[1]
        src = self.w_hbm[mat].at[self.layer, expert, pl.ds(pl.multiple_of(q * rows, rows), rows), :]
        return pltpu.make_async_copy(src, self.stage[kind].at[slot], self.sem.at[kind, slot])

    def _per_matrix(self, c, fn):
        mat, q, slot = c // W_CHUNKS, c % W_CHUNKS, c % 2
        for m in range(3):
            @pl.when(mat == m)
            def _(m=m):
                fn(m, q, slot)

    def start(self, c, expert):
        self._per_matrix(c, lambda m, q, slot: self._copy(m, expert, q, slot).start(priority=W_STREAM_PRIORITY))

    def finish(self, c, dst):
        def fn(m, q, slot):
            self._copy(m, 0, q, slot).wait()
            kind = 0 if m < 2 else 1
            rows = self.stage[kind].shape[1]
            self.wbuf[m][dst, pl.ds(pl.multiple_of(q * rows, rows), rows), :] = self.stage[kind][slot].astype(BF16)

        self._per_matrix(c, fn)

    def prefetch(self, expert, lo, hi):
        for t in range(2):
            @pl.when(lo + t < hi)
            def _(t=t):
                self.start(lo + t, expert)

    def drain(self, expert, dst, lo, upto, hi):
        def body(c, carry):
            self.finish(c, dst)

            @pl.when(c + 2 < hi)
            def _():
                self.start(c + 2, expert)

            return carry

        lax.fori_loop(lo, upto, body, 0)


def _ffn_kernel(rt_ref, be_ref, nbu_ref, ws_ref, nx_ref, lo_ref, hi_ref,
                x_hbm, sc_ref, sh_ref, wg_hbm, wu_hbm, wd_hbm, o_ref,
                xbuf, sem, wg_buf, wu_buf, wd_buf, stage_in, stage_dn, wsem, *, layer):
    i = pl.program_id(0)
    nbu = nbu_ref[0]
    tm = xbuf.shape[1]
    slot = i % 2
    stream = _WeightStream(layer, (wg_hbm, wu_hbm, wd_hbm), (stage_in, stage_dn), (wg_buf, wu_buf, wd_buf), wsem)
    n_chunks = 3 * W_CHUNKS

    @pl.when(i == 0)
    def _():
        _row_gather(x_hbm, rt_ref, 0, tm, xbuf.at[0], sem.at[0], ROW_GATHER_PRIORITY)
        stream.prefetch(be_ref[0], 0, n_chunks)
        stream.drain(be_ref[0], ws_ref[0], 0, n_chunks, n_chunks)

    @pl.when(i < nbu)
    def _():
        ws, nxt, lo, hi = ws_ref[i], nx_ref[i], lo_ref[i], hi_ref[i]
        stream.prefetch(nxt, lo, hi)
        _row_gather_wait(x_hbm, tm, xbuf.at[slot], sem.at[slot])
        _row_gather(x_hbm, rt_ref, (i + 1) * tm, tm, xbuf.at[1 - slot], sem.at[1 - slot], ROW_GATHER_PRIORITY)
        u = (xbuf[slot] * (1.0 + sc_ref[...]) + sh_ref[...]).astype(BF16)
        mid1, mid2 = jnp.minimum(lo + 2, hi), jnp.minimum(lo + 4, hi)
        gt = jnp.dot(u, wg_buf[ws], preferred_element_type=F32)
        stream.drain(nxt, 1 - ws, lo, mid1, hi)
        up = jnp.dot(u, wu_buf[ws], preferred_element_type=F32)
        stream.drain(nxt, 1 - ws, mid1, mid2, hi)
        h = (gt * jax.nn.sigmoid(gt) * up).astype(BF16)
        o_ref[...] = jnp.dot(h, wd_buf[ws], preferred_element_type=F32)
        stream.drain(nxt, 1 - ws, mid2, hi, hi)

    @pl.when(i >= nbu)
    def _():
        @pl.when(i == nbu)
        def _():
            _row_gather_wait(x_hbm, tm, xbuf.at[slot], sem.at[slot])

        o_ref[...] = jnp.zeros_like(o_ref)


def _moe_ffn(x, sc, sh, w_gate, w_up, w_down, layer, row_token, block_expert, nb_used, w_slot, next_expert,
             chunk_lo, chunk_hi):
    tm = TM_MOE
    p_rows = row_token.shape[0]
    nb = p_rows // tm
    vec = pl.BlockSpec((1, D_MODEL), lambda i, *_: (0, 0))
    hbm = pl.BlockSpec(memory_space=pl.ANY)
    return pl.pallas_call(
        functools.partial(_ffn_kernel, layer=layer),
        grid_spec=pltpu.PrefetchScalarGridSpec(
            num_scalar_prefetch=7,
            grid=(nb,),
            in_specs=[hbm, vec, vec, hbm, hbm, hbm],
            out_specs=pl.BlockSpec((tm, D_MODEL), lambda i, *_: (i, 0)),
            scratch_shapes=[pltpu.VMEM((2, tm, D_MODEL), F32), pltpu.SemaphoreType.DMA((2,)),
                            pltpu.VMEM((2, D_MODEL, D_EXPERT), BF16), pltpu.VMEM((2, D_MODEL, D_EXPERT), BF16),
                            pltpu.VMEM((2, D_EXPERT, D_MODEL), BF16),
                            pltpu.VMEM((2, D_MODEL // W_CHUNKS, D_EXPERT), F32),
                            pltpu.VMEM((2, D_EXPERT // W_CHUNKS, D_MODEL), F32),
                            pltpu.SemaphoreType.DMA((2, 2))]),
        out_shape=jax.ShapeDtypeStruct((p_rows, D_MODEL), F32),
        compiler_params=pltpu.CompilerParams(dimension_semantics=("arbitrary",),
                                             vmem_limit_bytes=VMEM_LIMIT_FFN_V7X),
        name="moe_ffn",
    )(row_token, block_expert, nb_used, w_slot, next_expert, chunk_lo, chunk_hi,
      x, sc, sh, w_gate, w_up, w_down)


def _combine_kernel(dest_ref, rows_hbm, gt_ref, x_ref, gate_ref, g_ref, b_ref, out_ref, ybuf, sem):
    i = pl.program_id(0)
    n = pl.num_programs(0)
    tm = x_ref.shape[0]
    s = tm * n
    slot = i % 2

    def start(tile, to):
        for k in range(TOP_K):
            _row_gather(rows_hbm, dest_ref, k * s + tile * tm, tm, ybuf.at[to, k], sem.at[to, k])

    def wait(at):
        for k in range(TOP_K):
            _row_gather_wait(rows_hbm, tm, ybuf.at[at, k], sem.at[at, k])

    @pl.when(i == 0)
    def _():
        start(0, 0)

    wait(slot)
    start(jnp.minimum(i + 1, n - 1), 1 - slot)
    gt = gt_ref[...]
    y = gt[:, 0:1] * ybuf[slot, 0] + gt[:, 1:2] * ybuf[slot, 1]
    out_ref[...] = _deepnorm(x_ref[...], y, gate_ref[...], g_ref[...], b_ref[...])

    @pl.when(i == n - 1)
    def _():
        wait(1 - slot)


def _moe_combine(dest_flat, out_rows, gates_t, x, gate, g, b):
    s = x.shape[0]
    tm = TM_COMB
    vec = pl.BlockSpec((1, D_MODEL), lambda i, dst: (0, 0))
    return pl.pallas_call(
        _combine_kernel,
        grid_spec=pltpu.PrefetchScalarGridSpec(
            num_scalar_prefetch=1,
            grid=(s // tm,),
            in_specs=[pl.BlockSpec(memory_space=pl.ANY),
                      pl.BlockSpec((tm, TOP_K), lambda i, dst: (i, 0)),
                      pl.BlockSpec((tm, D_MODEL), lambda i, dst: (i, 0)), vec, vec, vec],
            out_specs=pl.BlockSpec((tm, D_MODEL), lambda i, dst: (i, 0)),
            scratch_shapes=[pltpu.VMEM((2, TOP_K, tm, D_MODEL), F32), pltpu.SemaphoreType.DMA((2, TOP_K))]),
        out_shape=jax.ShapeDtypeStruct((s, D_MODEL), F32),
        compiler_params=_cparams(("arbitrary",)),
        name="moe_combine",
    )(dest_flat, out_rows, gates_t, x, gate, g, b)


def _moe_layer(x, sc, sh, gate, g, b, rw_t, rb_col, w_gate, w_up, w_down, layer):
    s = x.shape[0]
    tm = TM_MOE
    expert, gates, rank, cnt = _router(x, sc, sh, rw_t, rb_col)
    counts = cnt[:, 0].astype(jnp.int32)
    padded = (counts + tm - 1) // tm * tm
    pad_end = jnp.cumsum(padded)
    pad_start = pad_end - padded
    eids = jnp.arange(N_EXPERTS, dtype=jnp.int32)
    start_of = jnp.sum(jnp.where(expert[..., None] == eids, pad_start, 0), axis=-1)
    dest = (start_of + rank).reshape(-1)
    p_rows = TOP_K * s + N_EXPERTS * tm
    token = jnp.tile(jnp.arange(s, dtype=jnp.int32), TOP_K)
    row_token = jnp.zeros((p_rows,), jnp.int32).at[dest].set(token)
    nb = p_rows // tm
    block_row0 = jnp.arange(nb, dtype=jnp.int32) * tm
    block_expert = jnp.minimum(jnp.sum(block_row0[:, None] >= pad_end[None, :], axis=-1),
                               N_EXPERTS - 1).astype(jnp.int32)
    nb_used = (pad_end[-1:] // tm).astype(jnp.int32)
    has = padded > 0
    order = jnp.cumsum(has.astype(jnp.int32)) - 1
    later = jnp.logical_and(eids[None, :] > eids[:, None], has[None, :])
    next_of = jnp.min(jnp.where(later, eids[None, :], N_EXPERTS), axis=-1)
    of_block = lambda v: jnp.sum(jnp.where(block_expert[:, None] == eids[None, :], v[None, :], 0), axis=-1)
    run_len = jnp.maximum(of_block(padded) // tm, 1)
    run_pos = jnp.arange(nb, dtype=jnp.int32) - of_block(pad_start) // tm
    nxt = of_block(next_of)
    streams = jnp.logical_and(nxt < N_EXPERTS, jnp.arange(nb) < nb_used[0])
    n_chunks = 3 * W_CHUNKS
    chunk_lo = jnp.where(streams, n_chunks * run_pos // run_len, 0).astype(jnp.int32)
    chunk_hi = jnp.where(streams, n_chunks * (run_pos + 1) // run_len, 0).astype(jnp.int32)
    next_expert = jnp.where(streams, nxt, block_expert).astype(jnp.int32)
    w_slot = (of_block(order) % 2).astype(jnp.int32)
    out_rows = _moe_ffn(x, sc, sh, w_gate, w_up, w_down, layer, row_token, block_expert, nb_used,
                        w_slot, next_expert, chunk_lo, chunk_hi)
    return _moe_combine(dest.astype(jnp.int32), out_rows, gates.T, x, gate, g, b)


def kernel(x, c, positions, ada_w, ada_b, ln_g, ln_b, attn_w_qkv, attn_w_o, conv_w_pw1, conv_w_dw, conv_ln_g,
           conv_ln_b, conv_w_pw2, sc_w_in, sc_w_conv, sc_w_out, router_w, router_b, moe_w_gate, moe_w_up,
           moe_w_down):
    batch, s, d = x.shape
    assert batch == 1 and d == D_MODEL and s % (DILATIONS[-1] * Q_BLOCK) == 0
    xs = x.reshape(s, d)

    mod = _adaln(c.reshape(d, 1), ada_w, ada_b)
    half = HEAD_DIM // 2
    inv_freq = ROPE_THETA ** (-jnp.arange(half, dtype=F32) / half)
    freq_row = jnp.concatenate([inv_freq, inv_freq]).reshape(1, HEAD_DIM)
    cos, sin = _rope_tables(positions.reshape(s, 1), freq_row)

    rw_t = router_w.T
    rb_col = router_b.reshape(N_EXPERTS, 1)
    vec = lambda a: a.reshape(1, d)

    for i in range(DEPTH):
        sh1, sc1, g1, sh2, sc2, g2 = [mod[i, :, k * d:(k + 1) * d] for k in range(6)]
        m, j = i % N_MIXERS, i // N_MIXERS
        lg, lb = vec(ln_g[i, 0]), vec(ln_b[i, 0])
        if m == 0:
            outs, lses = [], []
            for grp in range(len(DILATIONS)):
                o_g, l_g = _attention(_qkv_proj(xs, sc1, sh1, attn_w_qkv, j, cos, sin, grp), grp)
                outs.append(o_g)
                lses.append(l_g)
            xs = _attn_out(outs, lses, attn_w_o[j].astype(BF16), xs, g1, lg, lb)
        elif m == 1:
            glu = _gated_in(_glu_in_kernel, 2, 1, xs, sc1, sh1, conv_w_pw1[j].astype(BF16), "conv_in")
            xs = _conv_out(glu, conv_w_dw[j], vec(conv_ln_g[j]), vec(conv_ln_b[j]),
                           conv_w_pw2[j].astype(BF16), xs, g1, lg, lb)
        else:
            bgate, ch = _gated_in(_sc_in_kernel, 3, 2, xs, sc1, sh1, sc_w_in[j].astype(BF16), "sc_in")
            xs = _sc_out(ch, bgate, sc_w_conv[j], sc_w_out[j].astype(BF16), xs, g1, lg, lb)
        xs = _moe_layer(xs, sc2, sh2, g2, vec(ln_g[i, 1]), vec(ln_b[i, 1]), rw_t, rb_col,
                        moe_w_gate, moe_w_up, moe_w_down, i)
    return xs.reshape(batch, s, d)
```

```python
import functools
import math

import jax
import jax.numpy as jnp
from jax import lax
from jax.experimental import pallas as pl
from jax.experimental.pallas import tpu as pltpu

F32 = jnp.float32
BF16 = jnp.bfloat16

D_MODEL = 2048
DEPTH = 4
N_MIXERS = 3
ATTN_HEADS = 16
HEAD_DIM = 128
HD = ATTN_HEADS * HEAD_DIM
DILATIONS = (1, 4, 16)
Q_BLOCK = 128
ROPE_THETA = 10000.0
CONV_KERNEL = 31
SHORT_CONV_KERNEL = 3
N_EXPERTS = 16
N_EXPERT_GROUPS = 4
EXPERTS_PER_GROUP = 4
TOP_K = 2
D_EXPERT = 1408
ALPHA = (2 * DEPTH) ** 0.25
LN_EPS = 1e-5
NEG_INF = -1e30

LANES_V7X = 128
SUBLANES_V7X = 8
MXU_COLS_V7X = 256
VMEM_LIMIT_V7X = 56 * 1024 * 1024
VMEM_LIMIT_FFN_V7X = 61 * 1024 * 1024
W_CHUNKS = 8
ROW_GATHER_PRIORITY = (0, 1)
W_STREAM_PRIORITY = 0
TM_PROJ = 1024
TM_QKV = 1024
TN_PROJ = 1024
TN_GLU = 512
TM_OUT = 256
ATTN_Q_PER_STEP = 2
TM_ROUTE = 512
TM_MOE = 256
TM_COMB = 256
TN_ADA = 1024
HALO_CONV = 32
HALO_SC = 8
CONV_ROWS = 64
CONV_COLS = 256


def _cparams(sem):
    return pltpu.CompilerParams(dimension_semantics=sem, vmem_limit_bytes=VMEM_LIMIT_V7X)


def _ln_rows(z, g, b):
    mu = jnp.mean(z, axis=-1, keepdims=True)
    zc = z - mu
    var = jnp.mean(zc * zc, axis=-1, keepdims=True)
    return zc * lax.rsqrt(var + LN_EPS) * g + b


def _deepnorm(x, y, gate, g, b):
    return _ln_rows(ALPHA * x + (1.0 + gate) * y, g, b)


def _row_spec(tm, width):
    return pl.BlockSpec((tm, width), lambda i: (i, 0))


def _vec_spec(width):
    return pl.BlockSpec((1, width), lambda i: (0, 0))


def _adaln_kernel(c_ref, w_ref, b_ref, o_ref):
    c = c_ref[...]
    ca = c * jax.nn.sigmoid(c)
    o_ref[...] = jnp.sum(w_ref[...] * ca, axis=0, keepdims=True) + b_ref[...]


def _adaln(c_col, ada_w, ada_b):
    depth, d, n = ada_w.shape
    return pl.pallas_call(
        _adaln_kernel,
        grid=(depth, n // TN_ADA),
        in_specs=[pl.BlockSpec((d, 1), lambda l, j: (0, 0)),
                  pl.BlockSpec((None, d, TN_ADA), lambda l, j: (l, 0, j)),
                  pl.BlockSpec((None, 1, TN_ADA), lambda l, j: (l, 0, j))],
        out_specs=pl.BlockSpec((None, 1, TN_ADA), lambda l, j: (l, 0, j)),
        out_shape=jax.ShapeDtypeStruct((depth, 1, n), F32),
        compiler_params=_cparams(("arbitrary", "arbitrary")),
        name="adaln",
    )(c_col, ada_w, ada_b.reshape(depth, 1, n))


def _rope_kernel(pos_ref, freq_ref, cos_ref, sin_ref):
    ang = pos_ref[...].astype(F32) * freq_ref[...]
    lane = lax.broadcasted_iota(jnp.int32, ang.shape, 1)
    s = jnp.sin(ang)
    cos_ref[...] = jnp.cos(ang)
    sin_ref[...] = jnp.where(lane < HEAD_DIM // 2, -s, s)


def _rope_tables(pos_col, freq_row):
    s = pos_col.shape[0]
    tm = 1024
    spec = pl.BlockSpec((tm, HEAD_DIM), lambda i: (i, 0))
    return pl.pallas_call(
        _rope_kernel,
        grid=(s // tm,),
        in_specs=[pl.BlockSpec((tm, 1), lambda i: (i, 0)), _vec_spec(HEAD_DIM)],
        out_specs=[spec, spec],
        out_shape=[jax.ShapeDtypeStruct((s, HEAD_DIM), F32)] * 2,
        compiler_params=_cparams(("arbitrary",)),
        name="rope_tables",
    )(pos_col, freq_row)


def _qkv_kernel(x_ref, sc_ref, sh_ref, w_ref, cos_ref, sin_ref, o_ref, u_scr, acc_scr, *, d, n_rope):
    j = pl.program_id(1)
    nh = acc_scr.shape[0]
    tm = x_ref.shape[0]
    skew = _residue_stride(d)

    @pl.when(j == 0)
    def _():
        u_scr[...] = (x_ref[...] * (1.0 + sc_ref[...]) + sh_ref[...]).astype(BF16)

    is_rope = j < n_rope
    cos = cos_ref[...]
    sin = sin_ref[...]
    u = u_scr[...]
    for c in range(nh // 2):
        w = w_ref[:, c * MXU_COLS_V7X:(c + 1) * MXU_COLS_V7X].astype(BF16)
        acc = jnp.dot(u, w, preferred_element_type=F32)
        for h in (2 * c, 2 * c + 1):
            a = acc[:, (h - 2 * c) * HEAD_DIM:(h - 2 * c + 1) * HEAD_DIM]
            a = jnp.where(is_rope, a * cos + pltpu.roll(a, HEAD_DIM // 2, 1) * sin, a)
            hs = slice(h * HEAD_DIM, (h + 1) * HEAD_DIM)
            if d == 1:
                o_ref[0, :, hs] = a.astype(BF16)
            else:
                if skew == d:
                    acc_scr[h] = a
                else:
                    for k in range(tm // d):
                        acc_scr[h, k * skew:k * skew + d] = a[k * d:(k + 1) * d]
                for r in range(d):
                    o_ref[r, :, hs] = acc_scr[h, pl.ds(r, tm // d, stride=skew), :].astype(BF16)


def _residue_stride(d):
    return d + 1 if d % SUBLANES_V7X == 0 else d


def _qkv_proj(x, sc, sh, w_all, layer, cos, sin, group):
    s = x.shape[0]
    d = DILATIONS[group]
    tm, tn = TM_QKV, TN_PROJ
    ncol = 3 * HD
    col0 = group * (ncol // tn)
    return pl.pallas_call(
        functools.partial(_qkv_kernel, d=d, n_rope=2 * HD // tn),
        grid=(s // tm, ncol // tn),
        in_specs=[pl.BlockSpec((tm, D_MODEL), lambda i, j: (i, 0)),
                  pl.BlockSpec((1, D_MODEL), lambda i, j: (0, 0)),
                  pl.BlockSpec((1, D_MODEL), lambda i, j: (0, 0)),
                  pl.BlockSpec((None, D_MODEL, tn), lambda i, j: (layer, 0, col0 + j)),
                  pl.BlockSpec((tm, HEAD_DIM), lambda i, j: (i, 0)),
                  pl.BlockSpec((tm, HEAD_DIM), lambda i, j: (i, 0))],
        out_specs=pl.BlockSpec((d, tm // d, tn), lambda i, j: (0, i, j)),
        out_shape=jax.ShapeDtypeStruct((d, s // d, ncol), BF16),
        scratch_shapes=[pltpu.VMEM((tm, D_MODEL), BF16),
                        pltpu.VMEM((tn // HEAD_DIM, tm // d * _residue_stride(d), HEAD_DIM), F32)],
        compiler_params=_cparams(("arbitrary", "arbitrary")),
        name=f"qkv_proj_g{group}",
    )(x, sc, sh, w_all, cos, sin)


def _attn_kernel(q_ref, kc_ref, vc_ref, o_ref, lse_ref, k_scr, v_scr, s_scr, p_scr, *, nblk):
    b = pl.program_id(0)
    nq = q_ref.shape[0] // Q_BLOCK

    @pl.when(b == 0)
    def _():
        k_scr[0:Q_BLOCK] = jnp.zeros((Q_BLOCK, HD), BF16)
        v_scr[0:Q_BLOCK] = jnp.zeros((Q_BLOCK, HD), BF16)

    k_scr[Q_BLOCK:] = kc_ref[...]
    v_scr[Q_BLOCK:] = vc_ref[...]
    nt = (((1,), (1,)), ((), ()))
    row = lax.broadcasted_iota(jnp.int32, (Q_BLOCK, 2 * Q_BLOCK), 0)
    col = lax.broadcasted_iota(jnp.int32, (Q_BLOCK, 2 * Q_BLOCK), 1)
    scale = 1.0 / math.sqrt(HEAD_DIM)
    for part in range(nq):
        qs = slice(part * Q_BLOCK, (part + 1) * Q_BLOCK)
        ks = slice(part * Q_BLOCK, (part + 2) * Q_BLOCK)
        not_first = ((b * nq) % nblk) != 0 if part == 0 else True
        for h in range(ATTN_HEADS):
            sl = slice(h * HEAD_DIM, (h + 1) * HEAD_DIM)
            s_scr[h] = lax.dot_general(q_ref[qs, sl], k_scr[ks, sl], nt, preferred_element_type=F32)

        prev_ok = jnp.logical_and(jnp.logical_and(col < Q_BLOCK, col >= row), not_first)
        valid = jnp.logical_or(prev_ok, jnp.logical_and(col >= Q_BLOCK, col - Q_BLOCK <= row))
        inv = []
        for h in range(ATTN_HEADS):
            s = jnp.where(valid, s_scr[h], NEG_INF)
            m = jnp.max(s, axis=1, keepdims=True)
            p = jnp.exp2((s - m) * (scale * math.log2(math.e)))
            den = jnp.sum(p, axis=1, keepdims=True)
            p_scr[h] = p.astype(BF16)
            inv.append(1.0 / den)
            lse_ref[qs, h:h + 1] = m * scale + jnp.log(den)
        for h in range(ATTN_HEADS):
            sl = slice(h * HEAD_DIM, (h + 1) * HEAD_DIM)
            o = jnp.dot(p_scr[h], v_scr[ks, sl], preferred_element_type=F32)
            o_ref[qs, sl] = (o * inv[h]).astype(BF16)
    last = slice((nq - 1) * Q_BLOCK, nq * Q_BLOCK)
    k_scr[0:Q_BLOCK] = kc_ref[last, :]
    v_scr[0:Q_BLOCK] = vc_ref[last, :]


def _attention(qkv, group):
    d, l, ncol = qkv.shape
    s = d * l
    flat = qkv.reshape(s, ncol)
    nblk = l // Q_BLOCK
    rows = ATTN_Q_PER_STEP * Q_BLOCK
    assert nblk % ATTN_Q_PER_STEP == 0
    cur = lambda c: pl.BlockSpec((rows, HD), lambda b: (b, c))
    return pl.pallas_call(
        functools.partial(_attn_kernel, nblk=nblk),
        grid=(s // rows,),
        in_specs=[cur(0), cur(1), cur(2)],
        out_specs=[pl.BlockSpec((rows, HD), lambda b: (b, 0)),
                   pl.BlockSpec((rows, ATTN_HEADS), lambda b: (b, 0))],
        out_shape=[jax.ShapeDtypeStruct((s, HD), BF16), jax.ShapeDtypeStruct((s, ATTN_HEADS), F32)],
        scratch_shapes=[pltpu.VMEM((rows + Q_BLOCK, HD), BF16), pltpu.VMEM((rows + Q_BLOCK, HD), BF16),
                        pltpu.VMEM((ATTN_HEADS, Q_BLOCK, 2 * Q_BLOCK), F32),
                        pltpu.VMEM((ATTN_HEADS, Q_BLOCK, 2 * Q_BLOCK), BF16)],
        compiler_params=_cparams(("arbitrary",)),
        name=f"dil_attn_g{group}",
    )(flat, flat, flat)


def _attn_out_kernel(o0_ref, o1_ref, o2_ref, l0_ref, l1_ref, l2_ref, w_ref, x_ref, gate_ref, g_ref, b_ref,
                     out_ref, o_scr, a_scr):
    tm = x_ref.shape[0]
    for gi, o_ref in enumerate((o1_ref, o2_ref)):
        d = DILATIONS[gi + 1]
        for r in range(d):
            for h in range(ATTN_HEADS):
                o_scr[gi, h, pl.ds(r, tm // d, stride=d), :] = (
                    o_ref[r, :, h * HEAD_DIM:(h + 1) * HEAD_DIM].astype(F32))
    l0 = l0_ref[...]
    l1 = l1_ref[...]
    l2 = l2_ref[...]
    m = jnp.maximum(jnp.maximum(l0, l1), l2)
    e0 = jnp.exp(l0 - m)
    e1 = jnp.exp(l1 - m)
    e2 = jnp.exp(l2 - m)
    z = e0 + e1 + e2
    w0 = e0 / z
    w1 = e1 / z
    w2 = e2 / z
    for h in range(ATTN_HEADS):
        sl = slice(h * HEAD_DIM, (h + 1) * HEAD_DIM)
        a = (w0[:, h:h + 1] * o0_ref[:, sl].astype(F32)
             + w1[:, h:h + 1] * o_scr[0, h]
             + w2[:, h:h + 1] * o_scr[1, h])
        a_scr[:, sl] = a.astype(BF16)
    y = jnp.dot(a_scr[...], w_ref[...], preferred_element_type=F32)
    out_ref[...] = _deepnorm(x_ref[...], y, gate_ref[...], g_ref[...], b_ref[...])


def _attn_out(outs, lses, w_bf, x, gate, g, b):
    s = x.shape[0]
    tm = TM_OUT
    d1, d2 = DILATIONS[1], DILATIONS[2]
    o1 = outs[1].reshape(d1, s // d1, HD)
    o2 = outs[2].reshape(d2, s // d2, HD)
    l1 = lses[1].reshape(d1, s // d1, ATTN_HEADS).transpose(1, 0, 2).reshape(s, ATTN_HEADS)
    l2 = lses[2].reshape(d2, s // d2, ATTN_HEADS).transpose(1, 0, 2).reshape(s, ATTN_HEADS)
    perm = lambda d, w: pl.BlockSpec((d, tm // d, w), lambda i: (0, i, 0))
    return pl.pallas_call(
        _attn_out_kernel,
        grid=(s // tm,),
        in_specs=[_row_spec(tm, HD), perm(d1, HD), perm(d2, HD),
                  _row_spec(tm, ATTN_HEADS), _row_spec(tm, ATTN_HEADS), _row_spec(tm, ATTN_HEADS),
                  pl.BlockSpec((HD, D_MODEL), lambda i: (0, 0)),
                  _row_spec(tm, D_MODEL), _vec_spec(D_MODEL), _vec_spec(D_MODEL), _vec_spec(D_MODEL)],
        out_specs=_row_spec(tm, D_MODEL),
        out_shape=jax.ShapeDtypeStruct((s, D_MODEL), F32),
        scratch_shapes=[pltpu.VMEM((2, ATTN_HEADS, tm, HEAD_DIM), F32), pltpu.VMEM((tm, HD), BF16)],
        compiler_params=_cparams(("arbitrary",)),
        name="attn_out",
    )(outs[0], o1, o2, lses[0], l1, l2, w_bf, x, gate, g, b)


def _modulate_once(x_ref, sc_ref, sh_ref, u_scr):
    @pl.when(pl.program_id(1) == 0)
    def _():
        u_scr[...] = (x_ref[...] * (1.0 + sc_ref[...]) + sh_ref[...]).astype(BF16)


def _glu_in_kernel(x_ref, sc_ref, sh_ref, wa_ref, wg_ref, o_ref, u_scr):
    _modulate_once(x_ref, sc_ref, sh_ref, u_scr)
    u = u_scr[...]
    a = jnp.dot(u, wa_ref[...], preferred_element_type=F32)
    gt = jnp.dot(u, wg_ref[...], preferred_element_type=F32)
    o_ref[...] = a * jax.nn.sigmoid(gt)


def _sc_in_kernel(x_ref, sc_ref, sh_ref, wb_ref, wc_ref, wh_ref, b_ref, ch_ref, u_scr):
    _modulate_once(x_ref, sc_ref, sh_ref, u_scr)
    u = u_scr[...]
    b_ref[...] = jnp.dot(u, wb_ref[...], preferred_element_type=F32)
    c = jnp.dot(u, wc_ref[...], preferred_element_type=F32)
    h = jnp.dot(u, wh_ref[...], preferred_element_type=F32)
    ch_ref[...] = c * h


def _gated_in(kernel, n_parts, n_out, x, sc, sh, w_bf, name):
    s = x.shape[0]
    tm, tn = TM_PROJ, TN_GLU
    nj = D_MODEL // tn
    w_specs = [pl.BlockSpec((D_MODEL, tn), functools.partial(lambda i, j, p: (0, p * nj + j), p=p))
               for p in range(n_parts)]
    out_spec = pl.BlockSpec((tm, tn), lambda i, j: (i, j))
    out_shape = jax.ShapeDtypeStruct((s, D_MODEL), F32)
    return pl.pallas_call(
        kernel,
        grid=(s // tm, nj),
        in_specs=[pl.BlockSpec((tm, D_MODEL), lambda i, j: (i, 0)),
                  pl.BlockSpec((1, D_MODEL), lambda i, j: (0, 0)),
                  pl.BlockSpec((1, D_MODEL), lambda i, j: (0, 0))] + w_specs,
        out_specs=[out_spec] * n_out if n_out > 1 else out_spec,
        out_shape=[out_shape] * n_out if n_out > 1 else out_shape,
        scratch_shapes=[pltpu.VMEM((tm, D_MODEL), BF16)],
        compiler_params=_cparams(("arbitrary", "arbitrary")),
        name=name,
    )(x, sc, sh, *([w_bf] * n_parts))


def _fill_halo(hal_scr, prev_ref, cur_ref, halo):
    is_first = pl.program_id(0) == 0
    tm = cur_ref.shape[0]
    hal_scr[pl.ds(0, halo), :] = jnp.where(is_first, 0.0, prev_ref[...])
    hal_scr[pl.ds(halo, tm), :] = cur_ref[...]
    hal_scr[pl.ds(halo + tm, SUBLANES_V7X), :] = jnp.zeros((SUBLANES_V7X, D_MODEL), F32)


def _dwconv(hal_scr, w_ref, dst_scr, ph_scr, halo, ksize, tm):
    off = halo - (ksize - 1)
    win = CONV_ROWS + SUBLANES_V7X
    for r0 in range(0, tm, CONV_ROWS):
        for c0 in range(0, D_MODEL, CONV_COLS):
            cs = slice(c0, c0 + CONV_COLS)
            out = None
            for phase in range(SUBLANES_V7X):
                part = None
                for k in range(ksize):
                    if (off + k) % SUBLANES_V7X != phase:
                        continue
                    base = r0 + off + k - phase
                    term = w_ref[k:k + 1, cs] * hal_scr[base:base + win, cs]
                    part = term if part is None else part + term
                if part is None:
                    continue
                if phase == 0:
                    shifted = part[:CONV_ROWS]
                else:
                    ph_scr[phase] = part
                    shifted = ph_scr[phase, phase:phase + CONV_ROWS, :]
                out = shifted if out is None else out + shifted
            dst_scr[r0:r0 + CONV_ROWS, cs] = out


def _conv_out_kernel(cur_ref, prev_ref, wdw_ref, cg_ref, cb_ref, w_ref, x_ref, gate_ref, g_ref, b_ref,
                     out_ref, hal_scr, h_scr, ph_scr):
    tm = x_ref.shape[0]
    _fill_halo(hal_scr, prev_ref, cur_ref, HALO_CONV)
    _dwconv(hal_scr, wdw_ref, h_scr, ph_scr, HALO_CONV, CONV_KERNEL, tm)
    hn = _ln_rows(h_scr[...], cg_ref[...], cb_ref[...])
    a = (hn * jax.nn.sigmoid(hn)).astype(BF16)
    y = jnp.dot(a, w_ref[...], preferred_element_type=F32)
    out_ref[...] = _deepnorm(x_ref[...], y, gate_ref[...], g_ref[...], b_ref[...])


def _sc_out_kernel(cur_ref, prev_ref, bg_ref, wdw_ref, w_ref, x_ref, gate_ref, g_ref, b_ref,
                   out_ref, hal_scr, h_scr, ph_scr):
    tm = x_ref.shape[0]
    _fill_halo(hal_scr, prev_ref, cur_ref, HALO_SC)
    _dwconv(hal_scr, wdw_ref, h_scr, ph_scr, HALO_SC, SHORT_CONV_KERNEL, tm)
    a = (bg_ref[...] * h_scr[...]).astype(BF16)
    y = jnp.dot(a, w_ref[...], preferred_element_type=F32)
    out_ref[...] = _deepnorm(x_ref[...], y, gate_ref[...], g_ref[...], b_ref[...])


def _conv_phase_scratch():
    return pltpu.VMEM((SUBLANES_V7X, CONV_ROWS + SUBLANES_V7X, CONV_COLS), F32)


def _halo_spec(tm, halo):
    per = tm // halo
    return pl.BlockSpec((halo, D_MODEL), lambda i: (jnp.maximum(i * per - 1, 0), 0))


def _conv_out(glu, w_dw, cg, cb, w_bf, x, gate, g, b):
    s = x.shape[0]
    tm = TM_OUT
    full = lambda shape: pl.BlockSpec(shape, lambda i: (0, 0))
    return pl.pallas_call(
        _conv_out_kernel,
        grid=(s // tm,),
        in_specs=[_row_spec(tm, D_MODEL), _halo_spec(tm, HALO_CONV), full((CONV_KERNEL, D_MODEL)),
                  _vec_spec(D_MODEL), _vec_spec(D_MODEL), full((D_MODEL, D_MODEL)),
                  _row_spec(tm, D_MODEL), _vec_spec(D_MODEL), _vec_spec(D_MODEL), _vec_spec(D_MODEL)],
        out_specs=_row_spec(tm, D_MODEL),
        out_shape=jax.ShapeDtypeStruct((s, D_MODEL), F32),
        scratch_shapes=[pltpu.VMEM((tm + HALO_CONV + SUBLANES_V7X, D_MODEL), F32), pltpu.VMEM((tm, D_MODEL), F32),
                        _conv_phase_scratch()],
        compiler_params=_cparams(("arbitrary",)),
        name="conv_out",
    )(glu, glu, w_dw, cg, cb, w_bf, x, gate, g, b)


def _sc_out(ch, bgate, w_dw, w_bf, x, gate, g, b):
    s = x.shape[0]
    tm = TM_OUT
    full = lambda shape: pl.BlockSpec(shape, lambda i: (0, 0))
    return pl.pallas_call(
        _sc_out_kernel,
        grid=(s // tm,),
        in_specs=[_row_spec(tm, D_MODEL), _halo_spec(tm, HALO_SC), _row_spec(tm, D_MODEL),
                  full((SHORT_CONV_KERNEL, D_MODEL)), full((D_MODEL, D_MODEL)),
                  _row_spec(tm, D_MODEL), _vec_spec(D_MODEL), _vec_spec(D_MODEL), _vec_spec(D_MODEL)],
        out_specs=_row_spec(tm, D_MODEL),
        out_shape=jax.ShapeDtypeStruct((s, D_MODEL), F32),
        scratch_shapes=[pltpu.VMEM((tm + HALO_SC + SUBLANES_V7X, D_MODEL), F32), pltpu.VMEM((tm, D_MODEL), F32),
                        _conv_phase_scratch()],
        compiler_params=_cparams(("arbitrary",)),
        name="sc_out",
    )(ch, ch, bgate, w_dw, w_bf, x, gate, g, b)


def _argmax4_first(v):
    i01 = jnp.where(v[1] > v[0], 1, 0)
    m01 = jnp.maximum(v[0], v[1])
    i23 = jnp.where(v[3] > v[2], 3, 2)
    m23 = jnp.maximum(v[2], v[3])
    return jnp.where(m23 > m01, i23, i01), jnp.maximum(m01, m23)


def _select4(idx, v):
    return jnp.where(idx == 0, v[0], jnp.where(idx == 1, v[1], jnp.where(idx == 2, v[2], v[3])))


def _router_kernel(x_ref, sc_ref, sh_ref, wt_ref, rb_ref, e_ref, gt_ref, rk_ref, cnt_ref, carry_scr):
    i = pl.program_id(0)
    tm = x_ref.shape[0]

    @pl.when(i == 0)
    def _():
        carry_scr[...] = jnp.zeros_like(carry_scr)

    u = x_ref[...] * (1.0 + sc_ref[...]) + sh_ref[...]
    logits = lax.dot_general(wt_ref[...], u, (((1,), (1,)), ((), ())), precision=lax.Precision.HIGHEST,
                             preferred_element_type=F32) + rb_ref[...]
    ex = jnp.exp(logits - jnp.max(logits, axis=0, keepdims=True))
    probs = ex / jnp.sum(ex, axis=0, keepdims=True)

    top1_i, top1_v, top2_i, top2_v, score = [], [], [], [], []
    for grp in range(N_EXPERT_GROUPS):
        p = [probs[grp * EXPERTS_PER_GROUP + j:grp * EXPERTS_PER_GROUP + j + 1, :]
             for j in range(EXPERTS_PER_GROUP)]
        i1, v1 = _argmax4_first(p)
        rest = [jnp.where(i1 == j, -1.0, p[j]) for j in range(EXPERTS_PER_GROUP)]
        i2, v2 = _argmax4_first(rest)
        top1_i.append(i1)
        top1_v.append(v1)
        top2_i.append(i2)
        top2_v.append(v2)
        score.append(v1 + v2)
    gsel, _ = _argmax4_first(score)
    p1 = _select4(gsel, top1_v)
    p2 = _select4(gsel, top2_v)
    e1 = gsel * EXPERTS_PER_GROUP + _select4(gsel, top1_i)
    e2 = gsel * EXPERTS_PER_GROUP + _select4(gsel, top2_i)
    psum = p1 + p2
    e_ref[0:1, :] = e1
    e_ref[1:2, :] = e2
    gt_ref[0:1, :] = p1 / psum
    gt_ref[1:2, :] = p2 / psum

    eid = lax.broadcasted_iota(jnp.int32, (N_EXPERTS, tm), 0)
    earlier = (lax.broadcasted_iota(jnp.int32, (tm, tm), 0)
               < lax.broadcasted_iota(jnp.int32, (tm, tm), 1)).astype(BF16)
    base = carry_scr[:, 0:1]
    for slot, e_sel in enumerate((e1, e2)):
        onehot = (eid == e_sel).astype(F32)
        before = jnp.dot(onehot.astype(BF16), earlier, preferred_element_type=F32)
        rank = jnp.sum(onehot * (before + base), axis=0, keepdims=True)
        rk_ref[slot:slot + 1, :] = rank.astype(jnp.int32)
        base = base + jnp.sum(onehot, axis=1, keepdims=True)
    carry_scr[...] = jnp.broadcast_to(base, carry_scr.shape)
    cnt_ref[...] = carry_scr[...]


def _router(x, sc, sh, rw_t, rb_col):
    s = x.shape[0]
    tm = TM_ROUTE
    slot_spec = pl.BlockSpec((TOP_K, tm), lambda i: (0, i))
    full = lambda shape: pl.BlockSpec(shape, lambda i: (0, 0))
    return pl.pallas_call(
        _router_kernel,
        grid=(s // tm,),
        in_specs=[_row_spec(tm, D_MODEL), _vec_spec(D_MODEL), _vec_spec(D_MODEL),
                  full((N_EXPERTS, D_MODEL)), full((N_EXPERTS, 1))],
        out_specs=[slot_spec, slot_spec, slot_spec, full((N_EXPERTS, LANES_V7X))],
        out_shape=[jax.ShapeDtypeStruct((TOP_K, s), jnp.int32), jax.ShapeDtypeStruct((TOP_K, s), F32),
                   jax.ShapeDtypeStruct((TOP_K, s), jnp.int32),
                   jax.ShapeDtypeStruct((N_EXPERTS, LANES_V7X), F32)],
        scratch_shapes=[pltpu.VMEM((N_EXPERTS, LANES_V7X), F32)],
        compiler_params=_cparams(("arbitrary",)),
        name="router",
    )(x, sc, sh, rw_t, rb_col)


def _row_gather(src_hbm, idx_ref, base, n, dst, sem, priorities=(0, 1)):
    for r in range(n):
        tok = idx_ref[base + r]
        copy = pltpu.make_async_copy(src_hbm.at[pl.ds(tok, 1), :], dst.at[pl.ds(r, 1), :], sem)
        copy.start(priority=priorities[r % len(priorities)])


def _row_gather_wait(src_hbm, n, dst, sem):
    pltpu.make_async_copy(src_hbm.at[pl.ds(0, n), :], dst, sem).wait()


class _WeightStream:
    def __init__(self, layer, w_hbm, stage, wbuf, sem):
        self.layer, self.w_hbm, self.stage, self.wbuf, self.sem = layer, w_hbm, stage, wbuf, sem

    def _copy(self, mat, expert, q, slot):
        kind = 0 if mat < 2 else 1
        rows = self.stage[kind].shape[1]
        src = self.w_hbm[mat].at[self.layer, expert, pl.ds(pl.multiple_of(q * rows, rows), rows), :]
        return pltpu.make_async_copy(src, self.stage[kind].at[slot], self.sem.at[kind, slot])

    def _per_matrix(self, c, fn):
        mat, q, slot = c // W_CHUNKS, c % W_CHUNKS, c % 2
        for m in range(3):
            @pl.when(mat == m)
            def _(m=m):
                fn(m, q, slot)

    def start(self, c, expert):
        self._per_matrix(c, lambda m, q, slot: self._copy(m, expert, q, slot).start(priority=W_STREAM_PRIORITY))

    def finish(self, c, dst):
        def fn(m, q, slot):
            self._copy(m, 0, q, slot).wait()
            kind = 0 if m < 2 else 1
            rows = self.stage[kind].shape[1]
            self.wbuf[m][dst, pl.ds(pl.multiple_of(q * rows, rows), rows), :] = self.stage[kind][slot].astype(BF16)

        self._per_matrix(c, fn)

    def prefetch(self, expert, lo, hi):
        for t in range(2):
            @pl.when(lo + t < hi)
            def _(t=t):
                self.start(lo + t, expert)

    def drain(self, expert, dst, lo, upto, hi):
        def body(c, carry):
            self.finish(c, dst)

            @pl.when(c + 2 < hi)
            def _():
                self.start(c + 2, expert)

            return carry

        lax.fori_loop(lo, upto, body, 0)


def _ffn_kernel(rt_ref, be_ref, nbu_ref, ws_ref, nx_ref, lo_ref, hi_ref,
                x_hbm, sc_ref, sh_ref, wg_hbm, wu_hbm, wd_hbm, o_ref,
                xbuf, sem, wg_buf, wu_buf, wd_buf, stage_in, stage_dn, wsem, *, layer):
    i = pl.program_id(0)
    nbu = nbu_ref[0]
    tm = xbuf.shape[1]
    slot = i % 2
    stream = _WeightStream(layer, (wg_hbm, wu_hbm, wd_hbm), (stage_in, stage_dn), (wg_buf, wu_buf, wd_buf), wsem)
    n_chunks = 3 * W_CHUNKS

    @pl.when(i == 0)
    def _():
        _row_gather(x_hbm, rt_ref, 0, tm, xbuf.at[0], sem.at[0], ROW_GATHER_PRIORITY)
        stream.prefetch(be_ref[0], 0, n_chunks)
        stream.drain(be_ref[0], ws_ref[0], 0, n_chunks, n_chunks)

    @pl.when(i < nbu)
    def _():
        ws, nxt, lo, hi = ws_ref[i], nx_ref[i], lo_ref[i], hi_ref[i]
        stream.prefetch(nxt, lo, hi)
        _row_gather_wait(x_hbm, tm, xbuf.at[slot], sem.at[slot])
        _row_gather(x_hbm, rt_ref, (i + 1) * tm, tm, xbuf.at[1 - slot], sem.at[1 - slot], ROW_GATHER_PRIORITY)
        u = (xbuf[slot] * (1.0 + sc_ref[...]) + sh_ref[...]).astype(BF16)
        mid1, mid2 = jnp.minimum(lo + 2, hi), jnp.minimum(lo + 4, hi)
        gt = jnp.dot(u, wg_buf[ws], preferred_element_type=F32)
        stream.drain(nxt, 1 - ws, lo, mid1, hi)
        up = jnp.dot(u, wu_buf[ws], preferred_element_type=F32)
        stream.drain(nxt, 1 - ws, mid1, mid2, hi)
        h = (gt * jax.nn.sigmoid(gt) * up).astype(BF16)
        o_ref[...] = jnp.dot(h, wd_buf[ws], preferred_element_type=F32)
        stream.drain(nxt, 1 - ws, mid2, hi, hi)

    @pl.when(i >= nbu)
    def _():
        @pl.when(i == nbu)
        def _():
            _row_gather_wait(x_hbm, tm, xbuf.at[slot], sem.at[slot])

        o_ref[...] = jnp.zeros_like(o_ref)


def _moe_ffn(x, sc, sh, w_gate, w_up, w_down, layer, row_token, block_expert, nb_used, w_slot, next_expert,
             chunk_lo, chunk_hi):
    tm = TM_MOE
    p_rows = row_token.shape[0]
    nb = p_rows // tm
    vec = pl.BlockSpec((1, D_MODEL), lambda i, *_: (0, 0))
    hbm = pl.BlockSpec(memory_space=pl.ANY)
    return pl.pallas_call(
        functools.partial(_ffn_kernel, layer=layer),
        grid_spec=pltpu.PrefetchScalarGridSpec(
            num_scalar_prefetch=7,
            grid=(nb,),
            in_specs=[hbm, vec, vec, hbm, hbm, hbm],
            out_specs=pl.BlockSpec((tm, D_MODEL), lambda i, *_: (i, 0)),
            scratch_shapes=[pltpu.VMEM((2, tm, D_MODEL), F32), pltpu.SemaphoreType.DMA((2,)),
                            pltpu.VMEM((2, D_MODEL, D_EXPERT), BF16), pltpu.VMEM((2, D_MODEL, D_EXPERT), BF16),
                            pltpu.VMEM((2, D_EXPERT, D_MODEL), BF16),
                            pltpu.VMEM((2, D_MODEL // W_CHUNKS, D_EXPERT), F32),
                            pltpu.VMEM((2, D_EXPERT // W_CHUNKS, D_MODEL), F32),
                            pltpu.SemaphoreType.DMA((2, 2))]),
        out_shape=jax.ShapeDtypeStruct((p_rows, D_MODEL), F32),
        compiler_params=pltpu.CompilerParams(dimension_semantics=("arbitrary",),
                                             vmem_limit_bytes=VMEM_LIMIT_FFN_V7X),
        name="moe_ffn",
    )(row_token, block_expert, nb_used, w_slot, next_expert, chunk_lo, chunk_hi,
      x, sc, sh, w_gate, w_up, w_down)


def _combine_kernel(dest_ref, rows_hbm, gt_ref, x_ref, gate_ref, g_ref, b_ref, out_ref, ybuf, sem):
    i = pl.program_id(0)
    n = pl.num_programs(0)
    tm = x_ref.shape[0]
    s = tm * n
    slot = i % 2

    def start(tile, to):
        for k in range(TOP_K):
            _row_gather(rows_hbm, dest_ref, k * s + tile * tm, tm, ybuf.at[to, k], sem.at[to, k])

    def wait(at):
        for k in range(TOP_K):
            _row_gather_wait(rows_hbm, tm, ybuf.at[at, k], sem.at[at, k])

    @pl.when(i == 0)
    def _():
        start(0, 0)

    wait(slot)
    start(jnp.minimum(i + 1, n - 1), 1 - slot)
    gt = gt_ref[...]
    y = gt[:, 0:1] * ybuf[slot, 0] + gt[:, 1:2] * ybuf[slot, 1]
    out_ref[...] = _deepnorm(x_ref[...], y, gate_ref[...], g_ref[...], b_ref[...])

    @pl.when(i == n - 1)
    def _():
        wait(1 - slot)


def _moe_combine(dest_flat, out_rows, gates_t, x, gate, g, b):
    s = x.shape[0]
    tm = TM_COMB
    vec = pl.BlockSpec((1, D_MODEL), lambda i, dst: (0, 0))
    return pl.pallas_call(
        _combine_kernel,
        grid_spec=pltpu.PrefetchScalarGridSpec(
            num_scalar_prefetch=1,
            grid=(s // tm,),
            in_specs=[pl.BlockSpec(memory_space=pl.ANY),
                      pl.BlockSpec((tm, TOP_K), lambda i, dst: (i, 0)),
                      pl.BlockSpec((tm, D_MODEL), lambda i, dst: (i, 0)), vec, vec, vec],
            out_specs=pl.BlockSpec((tm, D_MODEL), lambda i, dst: (i, 0)),
            scratch_shapes=[pltpu.VMEM((2, TOP_K, tm, D_MODEL), F32), pltpu.SemaphoreType.DMA((2, TOP_K))]),
        out_shape=jax.ShapeDtypeStruct((s, D_MODEL), F32),
        compiler_params=_cparams(("arbitrary",)),
        name="moe_combine",
    )(dest_flat, out_rows, gates_t, x, gate, g, b)


def _moe_layer(x, sc, sh, gate, g, b, rw_t, rb_col, w_gate, w_up, w_down, layer):
    s = x.shape[0]
    tm = TM_MOE
    expert, gates, rank, cnt = _router(x, sc, sh, rw_t, rb_col)
    counts = cnt[:, 0].astype(jnp.int32)
    padded = (counts + tm - 1) // tm * tm
    pad_end = jnp.cumsum(padded)
    pad_start = pad_end - padded
    eids = jnp.arange(N_EXPERTS, dtype=jnp.int32)
    start_of = jnp.sum(jnp.where(expert[..., None] == eids, pad_start, 0), axis=-1)
    dest = (start_of + rank).reshape(-1)
    p_rows = TOP_K * s + N_EXPERTS * tm
    token = jnp.tile(jnp.arange(s, dtype=jnp.int32), TOP_K)
    row_token = jnp.zeros((p_rows,), jnp.int32).at[dest].set(token)
    nb = p_rows // tm
    block_row0 = jnp.arange(nb, dtype=jnp.int32) * tm
    block_expert = jnp.minimum(jnp.sum(block_row0[:, None] >= pad_end[None, :], axis=-1),
                               N_EXPERTS - 1).astype(jnp.int32)
    nb_used = (pad_end[-1:] // tm).astype(jnp.int32)
    has = padded > 0
    order = jnp.cumsum(has.astype(jnp.int32)) - 1
    later = jnp.logical_and(eids[None, :] > eids[:, None], has[None, :])
    next_of = jnp.min(jnp.where(later, eids[None, :], N_EXPERTS), axis=-1)
    of_block = lambda v: jnp.sum(jnp.where(block_expert[:, None] == eids[None, :], v[None, :], 0), axis=-1)
    run_len = jnp.maximum(of_block(padded) // tm, 1)
    run_pos = jnp.arange(nb, dtype=jnp.int32) - of_block(pad_start) // tm
    nxt = of_block(next_of)
    streams = jnp.logical_and(nxt < N_EXPERTS, jnp.arange(nb) < nb_used[0])
    n_chunks = 3 * W_CHUNKS
    chunk_lo = jnp.where(streams, n_chunks * run_pos // run_len, 0).astype(jnp.int32)
    chunk_hi = jnp.where(streams, n_chunks * (run_pos + 1) // run_len, 0).astype(jnp.int32)
    next_expert = jnp.where(streams, nxt, block_expert).astype(jnp.int32)
    w_slot = (of_block(order) % 2).astype(jnp.int32)
    out_rows = _moe_ffn(x, sc, sh, w_gate, w_up, w_down, layer, row_token, block_expert, nb_used,
                        w_slot, next_expert, chunk_lo, chunk_hi)
    return _moe_combine(dest.astype(jnp.int32), out_rows, gates.T, x, gate, g, b)


def kernel(x, c, positions, ada_w, ada_b, ln_g, ln_b, attn_w_qkv, attn_w_o, conv_w_pw1, conv_w_dw, conv_ln_g,
           conv_ln_b, conv_w_pw2, sc_w_in, sc_w_conv, sc_w_out, router_w, router_b, moe_w_gate, moe_w_up,
           moe_w_down):
    batch, s, d = x.shape
    assert batch == 1 and d == D_MODEL and s % (DILATIONS[-1] * Q_BLOCK) == 0
    xs = x.reshape(s, d)

    mod = _adaln(c.reshape(d, 1), ada_w, ada_b)
    half = HEAD_DIM // 2
    inv_freq = ROPE_THETA ** (-jnp.arange(half, dtype=F32) / half)
    freq_row = jnp.concatenate([inv_freq, inv_freq]).reshape(1, HEAD_DIM)
    cos, sin = _rope_tables(positions.reshape(s, 1), freq_row)

    rw_t = router_w.T
    rb_col = router_b.reshape(N_EXPERTS, 1)
    vec = lambda a: a.reshape(1, d)

    for i in range(DEPTH):
        sh1, sc1, g1, sh2, sc2, g2 = [mod[i, :, k * d:(k + 1) * d] for k in range(6)]
        m, j = i % N_MIXERS, i // N_MIXERS
        lg, lb = vec(ln_g[i, 0]), vec(ln_b[i, 0])
        if m == 0:
            outs, lses = [], []
            for grp in range(len(DILATIONS)):
                o_g, l_g = _attention(_qkv_proj(xs, sc1, sh1, attn_w_qkv, j, cos, sin, grp), grp)
                outs.append(o_g)
                lses.append(l_g)
            xs = _attn_out(outs, lses, attn_w_o[j].astype(BF16), xs, g1, lg, lb)
        elif m == 1:
            glu = _gated_in(_glu_in_kernel, 2, 1, xs, sc1, sh1, conv_w_pw1[j].astype(BF16), "conv_in")
            xs = _conv_out(glu, conv_w_dw[j], vec(conv_ln_g[j]), vec(conv_ln_b[j]),
                           conv_w_pw2[j].astype(BF16), xs, g1, lg, lb)
        else:
            bgate, ch = _gated_in(_sc_in_kernel, 3, 2, xs, sc1, sh1, sc_w_in[j].astype(BF16), "sc_in")
            xs = _sc_out(ch, bgate, sc_w_conv[j], sc_w_out[j].astype(BF16), xs, g1, lg, lb)
        xs = _moe_layer(xs, sc2, sh2, g2, vec(ln_g[i, 1]), vec(ln_b[i, 1]), rw_t, rb_col,
                        moe_w_gate, moe_w_up, moe_w_down, i)
    return xs.reshape(batch, s, d)
```

```python
import functools
import math

import jax
import jax.numpy as jnp
from jax import lax
from jax.experimental import pallas as pl
from jax.experimental.pallas import tpu as pltpu

F32 = jnp.float32
BF16 = jnp.bfloat16

D_MODEL = 2048
DEPTH = 4
N_MIXERS = 3
ATTN_HEADS = 16
HEAD_DIM = 128
HD = ATTN_HEADS * HEAD_DIM
DILATIONS = (1, 4, 16)
Q_BLOCK = 128
ROPE_THETA = 10000.0
CONV_KERNEL = 31
SHORT_CONV_KERNEL = 3
N_EXPERTS = 16
N_EXPERT_GROUPS = 4
EXPERTS_PER_GROUP = 4
TOP_K = 2
D_EXPERT = 1408
ALPHA = (2 * DEPTH) ** 0.25
LN_EPS = 1e-5
NEG_INF = -1e30

LANES_V7X = 128
SUBLANES_V7X = 8
MXU_COLS_V7X = 256
VMEM_LIMIT_V7X = 56 * 1024 * 1024
VMEM_LIMIT_FFN_V7X = 61 * 1024 * 1024
W_CHUNKS = 8
ROW_GATHER_PRIORITY = (0, 1)
W_STREAM_PRIORITY = 0
TM_PROJ = 1024
TM_QKV = 1024
TN_PROJ = 1024
TN_GLU = 512
TM_OUT = 256
ATTN_Q_PER_STEP = 4
TM_ROUTE = 512
TM_MOE = 256
TM_COMB = 256
TN_ADA = 1024
HALO_CONV = 32
HALO_SC = 8
CONV_ROWS = 64
CONV_COLS = 256


def _cparams(sem):
    return pltpu.CompilerParams(dimension_semantics=sem, vmem_limit_bytes=VMEM_LIMIT_V7X)


def _ln_rows(z, g, b):
    mu = jnp.mean(z, axis=-1, keepdims=True)
    zc = z - mu
    var = jnp.mean(zc * zc, axis=-1, keepdims=True)
    return zc * lax.rsqrt(var + LN_EPS) * g + b


def _deepnorm(x, y, gate, g, b):
    return _ln_rows(ALPHA * x + (1.0 + gate) * y, g, b)


def _row_spec(tm, width):
    return pl.BlockSpec((tm, width), lambda i: (i, 0))


def _vec_spec(width):
    return pl.BlockSpec((1, width), lambda i: (0, 0))


def _adaln_kernel(c_ref, w_ref, b_ref, o_ref):
    c = c_ref[...]
    ca = c * jax.nn.sigmoid(c)
    o_ref[...] = jnp.sum(w_ref[...] * ca, axis=0, keepdims=True) + b_ref[...]


def _adaln(c_col, ada_w, ada_b):
    depth, d, n = ada_w.shape
    return pl.pallas_call(
        _adaln_kernel,
        grid=(depth, n // TN_ADA),
        in_specs=[pl.BlockSpec((d, 1), lambda l, j: (0, 0)),
                  pl.BlockSpec((None, d, TN_ADA), lambda l, j: (l, 0, j)),
                  pl.BlockSpec((None, 1, TN_ADA), lambda l, j: (l, 0, j))],
        out_specs=pl.BlockSpec((None, 1, TN_ADA), lambda l, j: (l, 0, j)),
        out_shape=jax.ShapeDtypeStruct((depth, 1, n), F32),
        compiler_params=_cparams(("arbitrary", "arbitrary")),
        name="adaln",
    )(c_col, ada_w, ada_b.reshape(depth, 1, n))


def _rope_kernel(pos_ref, freq_ref, cos_ref, sin_ref):
    ang = pos_ref[...].astype(F32) * freq_ref[...]
    lane = lax.broadcasted_iota(jnp.int32, ang.shape, 1)
    s = jnp.sin(ang)
    cos_ref[...] = jnp.cos(ang)
    sin_ref[...] = jnp.where(lane < HEAD_DIM // 2, -s, s)


def _rope_tables(pos_col, freq_row):
    s = pos_col.shape[0]
    tm = 1024
    spec = pl.BlockSpec((tm, HEAD_DIM), lambda i: (i, 0))
    return pl.pallas_call(
        _rope_kernel,
        grid=(s // tm,),
        in_specs=[pl.BlockSpec((tm, 1), lambda i: (i, 0)), _vec_spec(HEAD_DIM)],
        out_specs=[spec, spec],
        out_shape=[jax.ShapeDtypeStruct((s, HEAD_DIM), F32)] * 2,
        compiler_params=_cparams(("arbitrary",)),
        name="rope_tables",
    )(pos_col, freq_row)


def _qkv_kernel(x_ref, sc_ref, sh_ref, w_ref, cos_ref, sin_ref, o_ref, u_scr, acc_scr, *, d, n_rope):
    j = pl.program_id(1)
    nh = acc_scr.shape[0]
    tm = x_ref.shape[0]
    skew = _residue_stride(d)

    @pl.when(j == 0)
    def _():
        u_scr[...] = (x_ref[...] * (1.0 + sc_ref[...]) + sh_ref[...]).astype(BF16)

    is_rope = j < n_rope
    cos = cos_ref[...]
    sin = sin_ref[...]
    u = u_scr[...]
    for c in range(nh // 2):
        w = w_ref[:, c * MXU_COLS_V7X:(c + 1) * MXU_COLS_V7X].astype(BF16)
        acc = jnp.dot(u, w, preferred_element_type=F32)
        for h in (2 * c, 2 * c + 1):
            a = acc[:, (h - 2 * c) * HEAD_DIM:(h - 2 * c + 1) * HEAD_DIM]
            a = jnp.where(is_rope, a * cos + pltpu.roll(a, HEAD_DIM // 2, 1) * sin, a)
            hs = slice(h * HEAD_DIM, (h + 1) * HEAD_DIM)
            if d == 1:
                o_ref[0, :, hs] = a.astype(BF16)
            else:
                if skew == d:
                    acc_scr[h] = a
                else:
                    for k in range(tm // d):
                        acc_scr[h, k * skew:k * skew + d] = a[k * d:(k + 1) * d]
                for r in range(d):
                    o_ref[r, :, hs] = acc_scr[h, pl.ds(r, tm // d, stride=skew), :].astype(BF16)


def _residue_stride(d):
    return d + 1 if d % SUBLANES_V7X == 0 else d


def _qkv_proj(x, sc, sh, w_all, layer, cos, sin, group):
    s = x.shape[0]
    d = DILATIONS[group]
    tm, tn = TM_QKV, TN_PROJ
    ncol = 3 * HD
    col0 = group * (ncol // tn)
    return pl.pallas_call(
        functools.partial(_qkv_kernel, d=d, n_rope=2 * HD // tn),
        grid=(s // tm, ncol // tn),
        in_specs=[pl.BlockSpec((tm, D_MODEL), lambda i, j: (i, 0)),
                  pl.BlockSpec((1, D_MODEL), lambda i, j: (0, 0)),
                  pl.BlockSpec((1, D_MODEL), lambda i, j: (0, 0)),
                  pl.BlockSpec((None, D_MODEL, tn), lambda i, j: (layer, 0, col0 + j)),
                  pl.BlockSpec((tm, HEAD_DIM), lambda i, j: (i, 0)),
                  pl.BlockSpec((tm, HEAD_DIM), lambda i, j: (i, 0))],
        out_specs=pl.BlockSpec((d, tm // d, tn), lambda i, j: (0, i, j)),
        out_shape=jax.ShapeDtypeStruct((d, s // d, ncol), BF16),
        scratch_shapes=[pltpu.VMEM((tm, D_MODEL), BF16),
                        pltpu.VMEM((tn // HEAD_DIM, tm // d * _residue_stride(d), HEAD_DIM), F32)],
        compiler_params=_cparams(("arbitrary", "arbitrary")),
        name=f"qkv_proj_g{group}",
    )(x, sc, sh, w_all, cos, sin)


def _attn_kernel(q_ref, kc_ref, vc_ref, o_ref, lse_ref, k_scr, v_scr, s_scr, p_scr, *, nblk):
    b = pl.program_id(0)
    nq = q_ref.shape[0] // Q_BLOCK

    @pl.when(b == 0)
    def _():
        k_scr[0:Q_BLOCK] = jnp.zeros((Q_BLOCK, HD), BF16)
        v_scr[0:Q_BLOCK] = jnp.zeros((Q_BLOCK, HD), BF16)

    k_scr[Q_BLOCK:] = kc_ref[...]
    v_scr[Q_BLOCK:] = vc_ref[...]
    nt = (((1,), (1,)), ((), ()))
    row = lax.broadcasted_iota(jnp.int32, (Q_BLOCK, 2 * Q_BLOCK), 0)
    col = lax.broadcasted_iota(jnp.int32, (Q_BLOCK, 2 * Q_BLOCK), 1)
    scale = 1.0 / math.sqrt(HEAD_DIM)
    for part in range(nq):
        qs = slice(part * Q_BLOCK, (part + 1) * Q_BLOCK)
        ks = slice(part * Q_BLOCK, (part + 2) * Q_BLOCK)
        not_first = ((b * nq) % nblk) != 0 if part == 0 else True
        for h in range(ATTN_HEADS):
            sl = slice(h * HEAD_DIM, (h + 1) * HEAD_DIM)
            s_scr[h] = lax.dot_general(q_ref[qs, sl], k_scr[ks, sl], nt, preferred_element_type=F32)

        prev_ok = jnp.logical_and(jnp.logical_and(col < Q_BLOCK, col >= row), not_first)
        valid = jnp.logical_or(prev_ok, jnp.logical_and(col >= Q_BLOCK, col - Q_BLOCK <= row))
        inv = []
        for h in range(ATTN_HEADS):
            s = jnp.where(valid, s_scr[h], NEG_INF)
            m = jnp.max(s, axis=1, keepdims=True)
            p = jnp.exp2((s - m) * (scale * math.log2(math.e)))
            den = jnp.sum(p, axis=1, keepdims=True)
            p_scr[h] = p.astype(BF16)
            inv.append(1.0 / den)
            lse_ref[qs, h:h + 1] = m * scale + jnp.log(den)
        for h in range(ATTN_HEADS):
            sl = slice(h * HEAD_DIM, (h + 1) * HEAD_DIM)
            o = jnp.dot(p_scr[h], v_scr[ks, sl], preferred_element_type=F32)
            o_ref[qs, sl] = (o * inv[h]).astype(BF16)
    last = slice((nq - 1) * Q_BLOCK, nq * Q_BLOCK)
    k_scr[0:Q_BLOCK] = kc_ref[last, :]
    v_scr[0:Q_BLOCK] = vc_ref[last, :]


def _attention(qkv, group):
    d, l, ncol = qkv.shape
    s = d * l
    flat = qkv.reshape(s, ncol)
    nblk = l // Q_BLOCK
    rows = ATTN_Q_PER_STEP * Q_BLOCK
    assert nblk % ATTN_Q_PER_STEP == 0
    cur = lambda c: pl.BlockSpec((rows, HD), lambda b: (b, c))
    return pl.pallas_call(
        functools.partial(_attn_kernel, nblk=nblk),
        grid=(s // rows,),
        in_specs=[cur(0), cur(1), cur(2)],
        out_specs=[pl.BlockSpec((rows, HD), lambda b: (b, 0)),
                   pl.BlockSpec((rows, ATTN_HEADS), lambda b: (b, 0))],
        out_shape=[jax.ShapeDtypeStruct((s, HD), BF16), jax.ShapeDtypeStruct((s, ATTN_HEADS), F32)],
        scratch_shapes=[pltpu.VMEM((rows + Q_BLOCK, HD), BF16), pltpu.VMEM((rows + Q_BLOCK, HD), BF16),
                        pltpu.VMEM((ATTN_HEADS, Q_BLOCK, 2 * Q_BLOCK), F32),
                        pltpu.VMEM((ATTN_HEADS, Q_BLOCK, 2 * Q_BLOCK), BF16)],
        compiler_params=_cparams(("arbitrary",)),
        name=f"dil_attn_g{group}",
    )(flat, flat, flat)


def _attn_out_kernel(o0_ref, o1_ref, o2_ref, l0_ref, l1_ref, l2_ref, w_ref, x_ref, gate_ref, g_ref, b_ref,
                     out_ref, o_scr, a_scr):
    tm = x_ref.shape[0]
    for gi, o_ref in enumerate((o1_ref, o2_ref)):
        d = DILATIONS[gi + 1]
        for r in range(d):
            for h in range(ATTN_HEADS):
                o_scr[gi, h, pl.ds(r, tm // d, stride=d), :] = (
                    o_ref[r, :, h * HEAD_DIM:(h + 1) * HEAD_DIM].astype(F32))
    l0 = l0_ref[...]
    l1 = l1_ref[...]
    l2 = l2_ref[...]
    m = jnp.maximum(jnp.maximum(l0, l1), l2)
    e0 = jnp.exp(l0 - m)
    e1 = jnp.exp(l1 - m)
    e2 = jnp.exp(l2 - m)
    z = e0 + e1 + e2
    w0 = e0 / z
    w1 = e1 / z
    w2 = e2 / z
    for h in range(ATTN_HEADS):
        sl = slice(h * HEAD_DIM, (h + 1) * HEAD_DIM)
        a = (w0[:, h:h + 1] * o0_ref[:, sl].astype(F32)
             + w1[:, h:h + 1] * o_scr[0, h]
             + w2[:, h:h + 1] * o_scr[1, h])
        a_scr[:, sl] = a.astype(BF16)
    y = jnp.dot(a_scr[...], w_ref[...], preferred_element_type=F32)
    out_ref[...] = _deepnorm(x_ref[...], y, gate_ref[...], g_ref[...], b_ref[...])


def _attn_out(outs, lses, w_bf, x, gate, g, b):
    s = x.shape[0]
    tm = TM_OUT
    d1, d2 = DILATIONS[1], DILATIONS[2]
    o1 = outs[1].reshape(d1, s // d1, HD)
    o2 = outs[2].reshape(d2, s // d2, HD)
    l1 = lses[1].reshape(d1, s // d1, ATTN_HEADS).transpose(1, 0, 2).reshape(s, ATTN_HEADS)
    l2 = lses[2].reshape(d2, s // d2, ATTN_HEADS).transpose(1, 0, 2).reshape(s, ATTN_HEADS)
    perm = lambda d, w: pl.BlockSpec((d, tm // d, w), lambda i: (0, i, 0))
    return pl.pallas_call(
        _attn_out_kernel,
        grid=(s // tm,),
        in_specs=[_row_spec(tm, HD), perm(d1, HD), perm(d2, HD),
                  _row_spec(tm, ATTN_HEADS), _row_spec(tm, ATTN_HEADS), _row_spec(tm, ATTN_HEADS),
                  pl.BlockSpec((HD, D_MODEL), lambda i: (0, 0)),
                  _row_spec(tm, D_MODEL), _vec_spec(D_MODEL), _vec_spec(D_MODEL), _vec_spec(D_MODEL)],
        out_specs=_row_spec(tm, D_MODEL),
        out_shape=jax.ShapeDtypeStruct((s, D_MODEL), F32),
        scratch_shapes=[pltpu.VMEM((2, ATTN_HEADS, tm, HEAD_DIM), F32), pltpu.VMEM((tm, HD), BF16)],
        compiler_params=_cparams(("arbitrary",)),
        name="attn_out",
    )(outs[0], o1, o2, lses[0], l1, l2, w_bf, x, gate, g, b)


def _modulate_once(x_ref, sc_ref, sh_ref, u_scr):
    @pl.when(pl.program_id(1) == 0)
    def _():
        u_scr[...] = (x_ref[...] * (1.0 + sc_ref[...]) + sh_ref[...]).astype(BF16)


def _glu_in_kernel(x_ref, sc_ref, sh_ref, wa_ref, wg_ref, o_ref, u_scr):
    _modulate_once(x_ref, sc_ref, sh_ref, u_scr)
    u = u_scr[...]
    a = jnp.dot(u, wa_ref[...], preferred_element_type=F32)
    gt = jnp.dot(u, wg_ref[...], preferred_element_type=F32)
    o_ref[...] = a * jax.nn.sigmoid(gt)


def _sc_in_kernel(x_ref, sc_ref, sh_ref, wb_ref, wc_ref, wh_ref, b_ref, ch_ref, u_scr):
    _modulate_once(x_ref, sc_ref, sh_ref, u_scr)
    u = u_scr[...]
    b_ref[...] = jnp.dot(u, wb_ref[...], preferred_element_type=F32)
    c = jnp.dot(u, wc_ref[...], preferred_element_type=F32)
    h = jnp.dot(u, wh_ref[...], preferred_element_type=F32)
    ch_ref[...] = c * h


def _gated_in(kernel, n_parts, n_out, x, sc, sh, w_bf, name):
    s = x.shape[0]
    tm, tn = TM_PROJ, TN_GLU
    nj = D_MODEL // tn
    w_specs = [pl.BlockSpec((D_MODEL, tn), functools.partial(lambda i, j, p: (0, p * nj + j), p=p))
               for p in range(n_parts)]
    out_spec = pl.BlockSpec((tm, tn), lambda i, j: (i, j))
    out_shape = jax.ShapeDtypeStruct((s, D_MODEL), F32)
    return pl.pallas_call(
        kernel,
        grid=(s // tm, nj),
        in_specs=[pl.BlockSpec((tm, D_MODEL), lambda i, j: (i, 0)),
                  pl.BlockSpec((1, D_MODEL), lambda i, j: (0, 0)),
                  pl.BlockSpec((1, D_MODEL), lambda i, j: (0, 0))] + w_specs,
        out_specs=[out_spec] * n_out if n_out > 1 else out_spec,
        out_shape=[out_shape] * n_out if n_out > 1 else out_shape,
        scratch_shapes=[pltpu.VMEM((tm, D_MODEL), BF16)],
        compiler_params=_cparams(("arbitrary", "arbitrary")),
        name=name,
    )(x, sc, sh, *([w_bf] * n_parts))


def _fill_halo(hal_scr, prev_ref, cur_ref, halo):
    is_first = pl.program_id(0) == 0
    tm = cur_ref.shape[0]
    hal_scr[pl.ds(0, halo), :] = jnp.where(is_first, 0.0, prev_ref[...])
    hal_scr[pl.ds(halo, tm), :] = cur_ref[...]
    hal_scr[pl.ds(halo + tm, SUBLANES_V7X), :] = jnp.zeros((SUBLANES_V7X, D_MODEL), F32)


def _dwconv(hal_scr, w_ref, dst_scr, ph_scr, halo, ksize, tm):
    off = halo - (ksize - 1)
    win = CONV_ROWS + SUBLANES_V7X
    for r0 in range(0, tm, CONV_ROWS):
        for c0 in range(0, D_MODEL, CONV_COLS):
            cs = slice(c0, c0 + CONV_COLS)
            out = None
            for phase in range(SUBLANES_V7X):
                part = None
                for k in range(ksize):
                    if (off + k) % SUBLANES_V7X != phase:
                        continue
                    base = r0 + off + k - phase
                    term = w_ref[k:k + 1, cs] * hal_scr[base:base + win, cs]
                    part = term if part is None else part + term
                if part is None:
                    continue
                if phase == 0:
                    shifted = part[:CONV_ROWS]
                else:
                    ph_scr[phase] = part
                    shifted = ph_scr[phase, phase:phase + CONV_ROWS, :]
                out = shifted if out is None else out + shifted
            dst_scr[r0:r0 + CONV_ROWS, cs] = out


def _conv_out_kernel(cur_ref, prev_ref, wdw_ref, cg_ref, cb_ref, w_ref, x_ref, gate_ref, g_ref, b_ref,
                     out_ref, hal_scr, h_scr, ph_scr):
    tm = x_ref.shape[0]
    _fill_halo(hal_scr, prev_ref, cur_ref, HALO_CONV)
    _dwconv(hal_scr, wdw_ref, h_scr, ph_scr, HALO_CONV, CONV_KERNEL, tm)
    hn = _ln_rows(h_scr[...], cg_ref[...], cb_ref[...])
    a = (hn * jax.nn.sigmoid(hn)).astype(BF16)
    y = jnp.dot(a, w_ref[...], preferred_element_type=F32)
    out_ref[...] = _deepnorm(x_ref[...], y, gate_ref[...], g_ref[...], b_ref[...])


def _sc_out_kernel(cur_ref, prev_ref, bg_ref, wdw_ref, w_ref, x_ref, gate_ref, g_ref, b_ref,
                   out_ref, hal_scr, h_scr, ph_scr):
    tm = x_ref.shape[0]
    _fill_halo(hal_scr, prev_ref, cur_ref, HALO_SC)
    _dwconv(hal_scr, wdw_ref, h_scr, ph_scr, HALO_SC, SHORT_CONV_KERNEL, tm)
    a = (bg_ref[...] * h_scr[...]).astype(BF16)
    y = jnp.dot(a, w_ref[...], preferred_element_type=F32)
    out_ref[...] = _deepnorm(x_ref[...], y, gate_ref[...], g_ref[...], b_ref[...])


def _conv_phase_scratch():
    return pltpu.VMEM((SUBLANES_V7X, CONV_ROWS + SUBLANES_V7X, CONV_COLS), F32)


def _halo_spec(tm, halo):
    per = tm // halo
    return pl.BlockSpec((halo, D_MODEL), lambda i: (jnp.maximum(i * per - 1, 0), 0))


def _conv_out(glu, w_dw, cg, cb, w_bf, x, gate, g, b):
    s = x.shape[0]
    tm = TM_OUT
    full = lambda shape: pl.BlockSpec(shape, lambda i: (0, 0))
    return pl.pallas_call(
        _conv_out_kernel,
        grid=(s // tm,),
        in_specs=[_row_spec(tm, D_MODEL), _halo_spec(tm, HALO_CONV), full((CONV_KERNEL, D_MODEL)),
                  _vec_spec(D_MODEL), _vec_spec(D_MODEL), full((D_MODEL, D_MODEL)),
                  _row_spec(tm, D_MODEL), _vec_spec(D_MODEL), _vec_spec(D_MODEL), _vec_spec(D_MODEL)],
        out_specs=_row_spec(tm, D_MODEL),
        out_shape=jax.ShapeDtypeStruct((s, D_MODEL), F32),
        scratch_shapes=[pltpu.VMEM((tm + HALO_CONV + SUBLANES_V7X, D_MODEL), F32), pltpu.VMEM((tm, D_MODEL), F32),
                        _conv_phase_scratch()],
        compiler_params=_cparams(("arbitrary",)),
        name="conv_out",
    )(glu, glu, w_dw, cg, cb, w_bf, x, gate, g, b)


def _sc_out(ch, bgate, w_dw, w_bf, x, gate, g, b):
    s = x.shape[0]
    tm = TM_OUT
    full = lambda shape: pl.BlockSpec(shape, lambda i: (0, 0))
    return pl.pallas_call(
        _sc_out_kernel,
        grid=(s // tm,),
        in_specs=[_row_spec(tm, D_MODEL), _halo_spec(tm, HALO_SC), _row_spec(tm, D_MODEL),
                  full((SHORT_CONV_KERNEL, D_MODEL)), full((D_MODEL, D_MODEL)),
                  _row_spec(tm, D_MODEL), _vec_spec(D_MODEL), _vec_spec(D_MODEL), _vec_spec(D_MODEL)],
        out_specs=_row_spec(tm, D_MODEL),
        out_shape=jax.ShapeDtypeStruct((s, D_MODEL), F32),
        scratch_shapes=[pltpu.VMEM((tm + HALO_SC + SUBLANES_V7X, D_MODEL), F32), pltpu.VMEM((tm, D_MODEL), F32),
                        _conv_phase_scratch()],
        compiler_params=_cparams(("arbitrary",)),
        name="sc_out",
    )(ch, ch, bgate, w_dw, w_bf, x, gate, g, b)


def _argmax4_first(v):
    i01 = jnp.where(v[1] > v[0], 1, 0)
    m01 = jnp.maximum(v[0], v[1])
    i23 = jnp.where(v[3] > v[2], 3, 2)
    m23 = jnp.maximum(v[2], v[3])
    return jnp.where(m23 > m01, i23, i01), jnp.maximum(m01, m23)


def _select4(idx, v):
    return jnp.where(idx == 0, v[0], jnp.where(idx == 1, v[1], jnp.where(idx == 2, v[2], v[3])))


def _router_kernel(x_ref, sc_ref, sh_ref, wt_ref, rb_ref, e_ref, gt_ref, rk_ref, cnt_ref, carry_scr):
    i = pl.program_id(0)
    tm = x_ref.shape[0]

    @pl.when(i == 0)
    def _():
        carry_scr[...] = jnp.zeros_like(carry_scr)

    u = x_ref[...] * (1.0 + sc_ref[...]) + sh_ref[...]
    logits = lax.dot_general(wt_ref[...], u, (((1,), (1,)), ((), ())), precision=lax.Precision.HIGHEST,
                             preferred_element_type=F32) + rb_ref[...]
    ex = jnp.exp(logits - jnp.max(logits, axis=0, keepdims=True))
    probs = ex / jnp.sum(ex, axis=0, keepdims=True)

    top1_i, top1_v, top2_i, top2_v, score = [], [], [], [], []
    for grp in range(N_EXPERT_GROUPS):
        p = [probs[grp * EXPERTS_PER_GROUP + j:grp * EXPERTS_PER_GROUP + j + 1, :]
             for j in range(EXPERTS_PER_GROUP)]
        i1, v1 = _argmax4_first(p)
        rest = [jnp.where(i1 == j, -1.0, p[j]) for j in range(EXPERTS_PER_GROUP)]
        i2, v2 = _argmax4_first(rest)
        top1_i.append(i1)
        top1_v.append(v1)
        top2_i.append(i2)
        top2_v.append(v2)
        score.append(v1 + v2)
    gsel, _ = _argmax4_first(score)
    p1 = _select4(gsel, top1_v)
    p2 = _select4(gsel, top2_v)
    e1 = gsel * EXPERTS_PER_GROUP + _select4(gsel, top1_i)
    e2 = gsel * EXPERTS_PER_GROUP + _select4(gsel, top2_i)
    psum = p1 + p2
    e_ref[0:1, :] = e1
    e_ref[1:2, :] = e2
    gt_ref[0:1, :] = p1 / psum
    gt_ref[1:2, :] = p2 / psum

    eid = lax.broadcasted_iota(jnp.int32, (N_EXPERTS, tm), 0)
    earlier = (lax.broadcasted_iota(jnp.int32, (tm, tm), 0)
               < lax.broadcasted_iota(jnp.int32, (tm, tm), 1)).astype(BF16)
    base = carry_scr[:, 0:1]
    for slot, e_sel in enumerate((e1, e2)):
        onehot = (eid == e_sel).astype(F32)
        before = jnp.dot(onehot.astype(BF16), earlier, preferred_element_type=F32)
        rank = jnp.sum(onehot * (before + base), axis=0, keepdims=True)
        rk_ref[slot:slot + 1, :] = rank.astype(jnp.int32)
        base = base + jnp.sum(onehot, axis=1, keepdims=True)
    carry_scr[...] = jnp.broadcast_to(base, carry_scr.shape)
    cnt_ref[...] = carry_scr[...]


def _router(x, sc, sh, rw_t, rb_col):
    s = x.shape[0]
    tm = TM_ROUTE
    slot_spec = pl.BlockSpec((TOP_K, tm), lambda i: (0, i))
    full = lambda shape: pl.BlockSpec(shape, lambda i: (0, 0))
    return pl.pallas_call(
        _router_kernel,
        grid=(s // tm,),
        in_specs=[_row_spec(tm, D_MODEL), _vec_spec(D_MODEL), _vec_spec(D_MODEL),
                  full((N_EXPERTS, D_MODEL)), full((N_EXPERTS, 1))],
        out_specs=[slot_spec, slot_spec, slot_spec, full((N_EXPERTS, LANES_V7X))],
        out_shape=[jax.ShapeDtypeStruct((TOP_K, s), jnp.int32), jax.ShapeDtypeStruct((TOP_K, s), F32),
                   jax.ShapeDtypeStruct((TOP_K, s), jnp.int32),
                   jax.ShapeDtypeStruct((N_EXPERTS, LANES_V7X), F32)],
        scratch_shapes=[pltpu.VMEM((N_EXPERTS, LANES_V7X), F32)],
        compiler_params=_cparams(("arbitrary",)),
        name="router",
    )(x, sc, sh, rw_t, rb_col)


def _row_gather(src_hbm, idx_ref, base, n, dst, sem, priorities=(0, 1)):
    for r in range(n):
        tok = idx_ref[base + r]
        copy = pltpu.make_async_copy(src_hbm.at[pl.ds(tok, 1), :], dst.at[pl.ds(r, 1), :], sem)
        copy.start(priority=priorities[r % len(priorities)])


def _row_gather_wait(src_hbm, n, dst, sem):
    pltpu.make_async_copy(src_hbm.at[pl.ds(0, n), :], dst, sem).wait()


class _WeightStream:
    def __init__(self, layer, w_hbm, stage, wbuf, sem):
        self.layer, self.w_hbm, self.stage, self.wbuf, self.sem = layer, w_hbm, stage, wbuf, sem

    def _copy(self, mat, expert, q, slot):
        kind = 0 if mat < 2 else 1
        rows = self.stage[kind].shape[1]
        src = self.w_hbm[mat].at[self.layer, expert, pl.ds(pl.multiple_of(q * rows, rows), rows), :]
        return pltpu.make_async_copy(src, self.stage[kind].at[slot], self.sem.at[kind, slot])

    def _per_matrix(self, c, fn):
        mat, q, slot = c // W_CHUNKS, c % W_CHUNKS, c % 2
        for m in range(3):
            @pl.when(mat == m)
            def _(m=m):
                fn(m, q, slot)

    def start(self, c, expert):
        self._per_matrix(c, lambda m, q, slot: self._copy(m, expert, q, slot).start(priority=W_STREAM_PRIORITY))

    def finish(self, c, dst):
        def fn(m, q, slot):
            self._copy(m, 0, q, slot).wait()
            kind = 0 if m < 2 else 1
            rows = self.stage[kind].shape[1]
            self.wbuf[m][dst, pl.ds(pl.multiple_of(q * rows, rows), rows), :] = self.stage[kind][slot].astype(BF16)

        self._per_matrix(c, fn)

    def prefetch(self, expert, lo, hi):
        for t in range(2):
            @pl.when(lo + t < hi)
            def _(t=t):
                self.start(lo + t, expert)

    def drain(self, expert, dst, lo, upto, hi):
        def body(c, carry):
            self.finish(c, dst)

            @pl.when(c + 2 < hi)
            def _():
                self.start(c + 2, expert)

            return carry

        lax.fori_loop(lo, upto, body, 0)


def _ffn_kernel(rt_ref, be_ref, nbu_ref, ws_ref, nx_ref, lo_ref, hi_ref,
                x_hbm, sc_ref, sh_ref, wg_hbm, wu_hbm, wd_hbm, o_ref,
                xbuf, sem, wg_buf, wu_buf, wd_buf, stage_in, stage_dn, wsem, *, layer):
    i = pl.program_id(0)
    nbu = nbu_ref[0]
    tm = xbuf.shape[1]
    slot = i % 2
    stream = _WeightStream(layer, (wg_hbm, wu_hbm, wd_hbm), (stage_in, stage_dn), (wg_buf, wu_buf, wd_buf), wsem)
    n_chunks = 3 * W_CHUNKS

    @pl.when(i == 0)
    def _():
        _row_gather(x_hbm, rt_ref, 0, tm, xbuf.at[0], sem.at[0], ROW_GATHER_PRIORITY)
        stream.prefetch(be_ref[0], 0, n_chunks)
        stream.drain(be_ref[0], ws_ref[0], 0, n_chunks, n_chunks)

    @pl.when(i < nbu)
    def _():
        ws, nxt, lo, hi = ws_ref[i], nx_ref[i], lo_ref[i], hi_ref[i]
        stream.prefetch(nxt, lo, hi)
        _row_gather_wait(x_hbm, tm, xbuf.at[slot], sem.at[slot])
        _row_gather(x_hbm, rt_ref, (i + 1) * tm, tm, xbuf.at[1 - slot], sem.at[1 - slot], ROW_GATHER_PRIORITY)
        u = (xbuf[slot] * (1.0 + sc_ref[...]) + sh_ref[...]).astype(BF16)
        mid1, mid2 = jnp.minimum(lo + 2, hi), jnp.minimum(lo + 4, hi)
        gt = jnp.dot(u, wg_buf[ws], preferred_element_type=F32)
        stream.drain(nxt, 1 - ws, lo, mid1, hi)
        up = jnp.dot(u, wu_buf[ws], preferred_element_type=F32)
        stream.drain(nxt, 1 - ws, mid1, mid2, hi)
        h = (gt * jax.nn.sigmoid(gt) * up).astype(BF16)
        o_ref[...] = jnp.dot(h, wd_buf[ws], preferred_element_type=F32)
        stream.drain(nxt, 1 - ws, mid2, hi, hi)

    @pl.when(i >= nbu)
    def _():
        @pl.when(i == nbu)
        def _():
            _row_gather_wait(x_hbm, tm, xbuf.at[slot], sem.at[slot])

        o_ref[...] = jnp.zeros_like(o_ref)


def _moe_ffn(x, sc, sh, w_gate, w_up, w_down, layer, row_token, block_expert, nb_used, w_slot, next_expert,
             chunk_lo, chunk_hi):
    tm = TM_MOE
    p_rows = row_token.shape[0]
    nb = p_rows // tm
    vec = pl.BlockSpec((1, D_MODEL), lambda i, *_: (0, 0))
    hbm = pl.BlockSpec(memory_space=pl.ANY)
    return pl.pallas_call(
        functools.partial(_ffn_kernel, layer=layer),
        grid_spec=pltpu.PrefetchScalarGridSpec(
            num_scalar_prefetch=7,
            grid=(nb,),
            in_specs=[hbm, vec, vec, hbm, hbm, hbm],
            out_specs=pl.BlockSpec((tm, D_MODEL), lambda i, *_: (i, 0)),
            scratch_shapes=[pltpu.VMEM((2, tm, D_MODEL), F32), pltpu.SemaphoreType.DMA((2,)),
                            pltpu.VMEM((2, D_MODEL, D_EXPERT), BF16), pltpu.VMEM((2, D_MODEL, D_EXPERT), BF16),
                            pltpu.VMEM((2, D_EXPERT, D_MODEL), BF16),
                            pltpu.VMEM((2, D_MODEL // W_CHUNKS, D_EXPERT), F32),
                            pltpu.VMEM((2, D_EXPERT // W_CHUNKS, D_MODEL), F32),
                            pltpu.SemaphoreType.DMA((2, 2))]),
        out_shape=jax.ShapeDtypeStruct((p_rows, D_MODEL), F32),
        compiler_params=pltpu.CompilerParams(dimension_semantics=("arbitrary",),
                                             vmem_limit_bytes=VMEM_LIMIT_FFN_V7X),
        name="moe_ffn",
    )(row_token, block_expert, nb_used, w_slot, next_expert, chunk_lo, chunk_hi,
      x, sc, sh, w_gate, w_up, w_down)


def _combine_kernel(dest_ref, rows_hbm, gt_ref, x_ref, gate_ref, g_ref, b_ref, out_ref, ybuf, sem):
    i = pl.program_id(0)
    n = pl.num_programs(0)
    tm = x_ref.shape[0]
    s = tm * n
    slot = i % 2

    def start(tile, to):
        for k in range(TOP_K):
            _row_gather(rows_hbm, dest_ref, k * s + tile * tm, tm, ybuf.at[to, k], sem.at[to, k])

    def wait(at):
        for k in range(TOP_K):
            _row_gather_wait(rows_hbm, tm, ybuf.at[at, k], sem.at[at, k])

    @pl.when(i == 0)
    def _():
        start(0, 0)

    wait(slot)
    start(jnp.minimum(i + 1, n - 1), 1 - slot)
    gt = gt_ref[...]
    y = gt[:, 0:1] * ybuf[slot, 0] + gt[:, 1:2] * ybuf[slot, 1]
    out_ref[...] = _deepnorm(x_ref[...], y, gate_ref[...], g_ref[...], b_ref[...])

    @pl.when(i == n - 1)
    def _():
        wait(1 - slot)


def _moe_combine(dest_flat, out_rows, gates_t, x, gate, g, b):
    s = x.shape[0]
    tm = TM_COMB
    vec = pl.BlockSpec((1, D_MODEL), lambda i, dst: (0, 0))
    return pl.pallas_call(
        _combine_kernel,
        grid_spec=pltpu.PrefetchScalarGridSpec(
            num_scalar_prefetch=1,
            grid=(s // tm,),
            in_specs=[pl.BlockSpec(memory_space=pl.ANY),
                      pl.BlockSpec((tm, TOP_K), lambda i, dst: (i, 0)),
                      pl.BlockSpec((tm, D_MODEL), lambda i, dst: (i, 0)), vec, vec, vec],
            out_specs=pl.BlockSpec((tm, D_MODEL), lambda i, dst: (i, 0)),
            scratch_shapes=[pltpu.VMEM((2, TOP_K, tm, D_MODEL), F32), pltpu.SemaphoreType.DMA((2, TOP_K))]),
        out_shape=jax.ShapeDtypeStruct((s, D_MODEL), F32),
        compiler_params=_cparams(("arbitrary",)),
        name="moe_combine",
    )(dest_flat, out_rows, gates_t, x, gate, g, b)


def _moe_layer(x, sc, sh, gate, g, b, rw_t, rb_col, w_gate, w_up, w_down, layer):
    s = x.shape[0]
    tm = TM_MOE
    expert, gates, rank, cnt = _router(x, sc, sh, rw_t, rb_col)
    counts = cnt[:, 0].astype(jnp.int32)
    padded = (counts + tm - 1) // tm * tm
    pad_end = jnp.cumsum(padded)
    pad_start = pad_end - padded
    eids = jnp.arange(N_EXPERTS, dtype=jnp.int32)
    start_of = jnp.sum(jnp.where(expert[..., None] == eids, pad_start, 0), axis=-1)
    dest = (start_of + rank).reshape(-1)
    p_rows = TOP_K * s + N_EXPERTS * tm
    token = jnp.tile(jnp.arange(s, dtype=jnp.int32), TOP_K)
    row_token = jnp.zeros((p_rows,), jnp.int32).at[dest].set(token)
    nb = p_rows // tm
    block_row0 = jnp.arange(nb, dtype=jnp.int32) * tm
    block_expert = jnp.minimum(jnp.sum(block_row0[:, None] >= pad_end[None, :], axis=-1),
                               N_EXPERTS - 1).astype(jnp.int32)
    nb_used = (pad_end[-1:] // tm).astype(jnp.int32)
    has = padded > 0
    order = jnp.cumsum(has.astype(jnp.int32)) - 1
    later = jnp.logical_and(eids[None, :] > eids[:, None], has[None, :])
    next_of = jnp.min(jnp.where(later, eids[None, :], N_EXPERTS), axis=-1)
    of_block = lambda v: jnp.sum(jnp.where(block_expert[:, None] == eids[None, :], v[None, :], 0), axis=-1)
    run_len = jnp.maximum(of_block(padded) // tm, 1)
    run_pos = jnp.arange(nb, dtype=jnp.int32) - of_block(pad_start) // tm
    nxt = of_block(next_of)
    streams = jnp.logical_and(nxt < N_EXPERTS, jnp.arange(nb) < nb_used[0])
    n_chunks = 3 * W_CHUNKS
    chunk_lo = jnp.where(streams, n_chunks * run_pos // run_len, 0).astype(jnp.int32)
    chunk_hi = jnp.where(streams, n_chunks * (run_pos + 1) // run_len, 0).astype(jnp.int32)
    next_expert = jnp.where(streams, nxt, block_expert).astype(jnp.int32)
    w_slot = (of_block(order) % 2).astype(jnp.int32)
    out_rows = _moe_ffn(x, sc, sh, w_gate, w_up, w_down, layer, row_token, block_expert, nb_used,
                        w_slot, next_expert, chunk_lo, chunk_hi)
    return _moe_combine(dest.astype(jnp.int32), out_rows, gates.T, x, gate, g, b)


def kernel(x, c, positions, ada_w, ada_b, ln_g, ln_b, attn_w_qkv, attn_w_o, conv_w_pw1, conv_w_dw, conv_ln_g,
           conv_ln_b, conv_w_pw2, sc_w_in, sc_w_conv, sc_w_out, router_w, router_b, moe_w_gate, moe_w_up,
           moe_w_down):
    batch, s, d = x.shape
    assert batch == 1 and d == D_MODEL and s % (DILATIONS[-1] * Q_BLOCK) == 0
    xs = x.reshape(s, d)

    mod = _adaln(c.reshape(d, 1), ada_w, ada_b)
    half = HEAD_DIM // 2
    inv_freq = ROPE_THETA ** (-jnp.arange(half, dtype=F32) / half)
    freq_row = jnp.concatenate([inv_freq, inv_freq]).reshape(1, HEAD_DIM)
    cos, sin = _rope_tables(positions.reshape(s, 1), freq_row)

    rw_t = router_w.T
    rb_col = router_b.reshape(N_EXPERTS, 1)
    vec = lambda a: a.reshape(1, d)

    for i in range(DEPTH):
        sh1, sc1, g1, sh2, sc2, g2 = [mod[i, :, k * d:(k + 1) * d] for k in range(6)]
        m, j = i % N_MIXERS, i // N_MIXERS
        lg, lb = vec(ln_g[i, 0]), vec(ln_b[i, 0])
        if m == 0:
            outs, lses = [], []
            for grp in range(len(DILATIONS)):
                o_g, l_g = _attention(_qkv_proj(xs, sc1, sh1, attn_w_qkv, j, cos, sin, grp), grp)
                outs.append(o_g)
                lses.append(l_g)
            xs = _attn_out(outs, lses, attn_w_o[j].astype(BF16), xs, g1, lg, lb)
        elif m == 1:
            glu = _gated_in(_glu_in_kernel, 2, 1, xs, sc1, sh1, conv_w_pw1[j].astype(BF16), "conv_in")
            xs = _conv_out(glu, conv_w_dw[j], vec(conv_ln_g[j]), vec(conv_ln_b[j]),
                           conv_w_pw2[j].astype(BF16), xs, g1, lg, lb)
        else:
            bgate, ch = _gated_in(_sc_in_kernel, 3, 2, xs, sc1, sh1, sc_w_in[j].astype(BF16), "sc_in")
            xs = _sc_out(ch, bgate, sc_w_conv[j], sc_w_out[j].astype(BF16), xs, g1, lg, lb)
        xs = _moe_layer(xs, sc2, sh2, g2, vec(ln_g[i, 1]), vec(ln_b[i, 1]), rw_t, rb_col,
                        moe_w_gate, moe_w_up, moe_w_down, i)
    return xs.reshape(batch, s, d)
```
